```python
import jax, jax.numpy as jnp
from jax import lax
import numpy as np

D_MODEL = 1024
BATCH = 8
SEQ = 4096
DEPTH = 4

N_MIXERS = 2
SB_HEADS = 16
SB_HEAD_DIM = D_MODEL // SB_HEADS
Q_BLOCK = 128
RET_HEADS = 8
RET_QK_DIM = D_MODEL // RET_HEADS
RET_V_DIM = 2 * RET_QK_DIM
RET_CHUNK = 128
ROPE_BASE = 10000.0
N_EXPERTS = 16
N_GROUPS = 4
EXPERTS_PER_GROUP = N_EXPERTS // N_GROUPS
TOP_K = 2
D_EXPERT = D_MODEL // 2
N_SB = (DEPTH + 1) // 2
N_RET = DEPTH // 2
DEEPNORM_ALPHA = (2 * DEPTH) ** 0.25
DEEPNORM_BETA = (8 * DEPTH) ** -0.25
LN_EPS = 1e-5

kernel_name = "hybrid_stickbreak_retention_groupmoe"


def layer_norm(x, g, b):
    xf = x.astype(jnp.float32)
    mu = jnp.mean(xf, axis=-1, keepdims=True)
    var = jnp.mean(jnp.square(xf - mu), axis=-1, keepdims=True)
    y = (xf - mu) * lax.rsqrt(var + LN_EPS) * g.astype(jnp.float32) + b.astype(jnp.float32)
    return y.astype(x.dtype)


def adaln_params(c, w, b):
    m = jax.nn.silu(c) @ w + b
    shift, scale, gate = jnp.split(m, 3, axis=-1)
    return shift[:, None, :], scale[:, None, :], gate[:, None, :]


def stick_breaking_attention(h, w_in, w_out):
    B, S, _ = h.shape
    qkv = (h @ w_in).reshape(B, S, 3, SB_HEADS, SB_HEAD_DIM)
    q = qkv[:, :, 0].transpose(0, 2, 1, 3)
    k = qkv[:, :, 1].transpose(0, 2, 1, 3)
    v = qkv[:, :, 2].transpose(0, 2, 1, 3)
    n_blocks = S // Q_BLOCK
    q_blocks = q.reshape(B, SB_HEADS, n_blocks, Q_BLOCK, SB_HEAD_DIM).transpose(2, 0, 1, 3, 4)
    key_pos = jnp.arange(S)
    scale = SB_HEAD_DIM ** -0.5

    def block(args):
        qb, i = args
        z = jnp.einsum('bhqd,bhkd->bhqk', qb, k).astype(jnp.float32) * scale
        q_pos = i * Q_BLOCK + jnp.arange(Q_BLOCK)
        causal = key_pos[None, :] < q_pos[:, None]
        log_keep = jnp.where(causal, jax.nn.log_sigmoid(-z), 0.0)
        suffix = lax.cumsum(log_keep, axis=3, reverse=True) - log_keep
        a = jnp.where(causal, jnp.exp(jax.nn.log_sigmoid(z) + suffix), 0.0)
        return jnp.einsum('bhqk,bhkd->bhqd', a.astype(v.dtype), v)

    o = lax.map(block, (q_blocks, jnp.arange(n_blocks)))
    o = o.transpose(1, 0, 3, 2, 4).reshape(B, S, D_MODEL)
    return o @ w_out


def rotary(x, cos, sin):
    half = x.shape[-1] // 2
    x1, x2 = x[..., :half], x[..., half:]
    return jnp.concatenate([x1 * cos - x2 * sin, x2 * cos + x1 * sin], axis=-1)


def retention(h, positions, w_in, gn_g, w_out):
    B, S, _ = h.shape
    f32 = jnp.float32
    proj = h @ w_in
    q, k, v, g = jnp.split(proj, [D_MODEL, 2 * D_MODEL, 4 * D_MODEL], axis=-1)
    q = q.reshape(B, S, RET_HEADS, RET_QK_DIM).astype(f32)
    k = k.reshape(B, S, RET_HEADS, RET_QK_DIM).astype(f32) * (RET_QK_DIM ** -0.5)
    v = v.reshape(B, S, RET_HEADS, RET_V_DIM).astype(f32)
    inv_freq = 1.0 / (ROPE_BASE ** (jnp.arange(0, RET_QK_DIM, 2, dtype=f32) / RET_QK_DIM))
    ang = positions.astype(f32)[..., None] * inv_freq
    cos = jnp.cos(ang)[:, :, None, :]
    sin = jnp.sin(ang)[:, :, None, :]
    q = rotary(q, cos, sin)
    k = rotary(k, cos, sin)

    log_gamma = jnp.log(1.0 - jnp.exp2(-5.0 - jnp.arange(RET_HEADS, dtype=f32)))
    n = jnp.arange(RET_CHUNK, dtype=f32)
    diff = n[:, None] - n[None, :]
    dmask = jnp.where(diff >= 0, jnp.exp(jnp.maximum(diff, 0.0) * log_gamma[:, None, None]), 0.0)
    xi = jnp.exp((n + 1.0) * log_gamma[:, None])
    zeta = jnp.exp((RET_CHUNK - 1.0 - n) * log_gamma[:, None])
    chunk_decay = jnp.exp(RET_CHUNK * log_gamma)
    n_chunks = S // RET_CHUNK

    def chunks(t):
        return t.reshape(B, n_chunks, RET_CHUNK, RET_HEADS, t.shape[-1]).transpose(1, 0, 3, 2, 4)

    def step(state, xs):
        qc, kc, vc = xs
        qk = jnp.einsum('bhnd,bhmd->bhnm', qc, kc) * dmask
        inner = jnp.einsum('bhnm,bhme->bhne', qk, vc)
        cross = jnp.einsum('bhnd,bhde->bhne', qc, state) * xi[None, :, :, None]
        state = state * chunk_decay[None, :, None, None] + jnp.einsum(
            'bhmd,bhme->bhde', kc * zeta[None, :, :, None], vc)
        return state, inner + cross

    state0 = jnp.zeros((B, RET_HEADS, RET_QK_DIM, RET_V_DIM), f32)
    _, o = lax.scan(step, state0, (chunks(q), chunks(k), chunks(v)))
    o = o.transpose(1, 0, 3, 2, 4).reshape(B, S, RET_HEADS, RET_V_DIM)
    mu = jnp.mean(o, axis=-1, keepdims=True)
    var = jnp.mean(jnp.square(o - mu), axis=-1, keepdims=True)
    o = ((o - mu) * lax.rsqrt(var + LN_EPS)).reshape(B, S, 2 * D_MODEL) * gn_g.astype(f32)
    return (jax.nn.silu(g) * o.astype(h.dtype)) @ w_out


def grouped_moe(h, router_w, router_b, w_gate, w_up, w_down):
    B, S, Dm = h.shape
    t = h.reshape(B * S, Dm)
    logits = t.astype(jnp.float32) @ router_w.astype(jnp.float32) + router_b.astype(jnp.float32)
    probs = jax.nn.softmax(logits, axis=-1)
    grouped = probs.reshape(-1, N_GROUPS, EXPERTS_PER_GROUP)
    group_score = jnp.sum(lax.top_k(grouped, TOP_K)[0], axis=-1)
    sel_group = jnp.argmax(group_score, axis=-1)
    expert_group = jnp.arange(N_EXPERTS) // EXPERTS_PER_GROUP
    in_group = expert_group[None, :] == sel_group[:, None]
    masked = jnp.where(in_group, logits, -jnp.inf)
    top_val, top_idx = lax.top_k(masked, TOP_K)
    top_w = jax.nn.softmax(top_val, axis=-1)
    combine = jnp.sum(jax.nn.one_hot(top_idx, N_EXPERTS, dtype=jnp.float32) * top_w[..., None], axis=1)
    out = jnp.zeros(t.shape, jnp.float32)
    for e in range(N_EXPERTS):
        hidden = jax.nn.silu(t @ w_gate[e]) * (t @ w_up[e])
        out = out + combine[:, e:e + 1] * (hidden @ w_down[e]).astype(jnp.float32)
    return out.astype(h.dtype).reshape(B, S, Dm)


def setup_inputs(seed: int = 0) -> dict:
    key = jax.random.key(seed)
    ks = jax.random.split(key, 20)
    D, F = D_MODEL, D_EXPERT
    nrm = jax.random.normal
    x = nrm(ks[0], (BATCH, SEQ, D), jnp.float32)
    c = nrm(ks[1], (BATCH, D), jnp.float32)
    positions = jnp.broadcast_to(jnp.arange(SEQ, dtype=jnp.int32)[None, :], (BATCH, SEQ))
    ada_w = nrm(ks[2], (DEPTH, 2, D, 3 * D), jnp.float32) * (0.5 * D ** -0.5)
    ada_b = nrm(ks[3], (DEPTH, 2, 3 * D), jnp.float32) * 0.02
    ln_g = 1.0 + 0.02 * nrm(ks[4], (DEPTH, 2, D), jnp.float32)
    ln_b = 0.02 * nrm(ks[5], (DEPTH, 2, D), jnp.float32)
    sb_col = jnp.concatenate([jnp.ones((2 * D,), jnp.float32), jnp.full((D,), DEEPNORM_BETA, jnp.float32)])
    sb_w_in = nrm(ks[6], (N_SB, D, 3 * D), jnp.float32) * (D ** -0.5) * sb_col
    sb_w_out = nrm(ks[7], (N_SB, D, D), jnp.float32) * (D ** -0.5) * DEEPNORM_BETA
    ret_col = jnp.concatenate([jnp.ones((2 * D,), jnp.float32), jnp.full((2 * D,), DEEPNORM_BETA, jnp.float32),
                               jnp.ones((2 * D,), jnp.float32)])
    ret_w_in = nrm(ks[8], (N_RET, D, 6 * D), jnp.float32) * (D ** -0.5) * ret_col
    ret_gn_g = 1.0 + 0.02 * nrm(ks[9], (N_RET, 2 * D), jnp.float32)
    ret_w_out = nrm(ks[10], (N_RET, 2 * D, D), jnp.float32) * ((2 * D) ** -0.5) * DEEPNORM_BETA
    router_w = nrm(ks[11], (D, N_EXPERTS), jnp.float32) * (D ** -0.5)
    router_b = nrm(ks[12], (N_EXPERTS,), jnp.float32) * 0.01
    moe_w_gate = nrm(ks[13], (DEPTH, N_EXPERTS, D, F), jnp.float32) * (D ** -0.5)
    moe_w_up = nrm(ks[14], (DEPTH, N_EXPERTS, D, F), jnp.float32) * (D ** -0.5) * DEEPNORM_BETA
    moe_w_down = nrm(ks[15], (DEPTH, N_EXPERTS, F, D), jnp.float32) * (F ** -0.5) * DEEPNORM_BETA
    return {"x": x, "c": c, "positions": positions, "ada_w": ada_w, "ada_b": ada_b,
            "ln_g": ln_g, "ln_b": ln_b, "sb_w_in": sb_w_in, "sb_w_out": sb_w_out,
            "ret_w_in": ret_w_in, "ret_gn_g": ret_gn_g, "ret_w_out": ret_w_out,
            "router_w": router_w, "router_b": router_b, "moe_w_gate": moe_w_gate,
            "moe_w_up": moe_w_up, "moe_w_down": moe_w_down}


def reference(x, c, positions, ada_w, ada_b, ln_g, ln_b, sb_w_in, sb_w_out, ret_w_in, ret_gn_g,
              ret_w_out, router_w, router_b, moe_w_gate, moe_w_up, moe_w_down):
    for i in range(DEPTH):
        shift, scale, gate = adaln_params(c, ada_w[i, 0], ada_b[i, 0])
        h = x * (1.0 + scale) + shift
        if i % N_MIXERS == 0:
            y = stick_breaking_attention(h, sb_w_in[i // 2], sb_w_out[i // 2])
        else:
            y = retention(h, positions, ret_w_in[i // 2], ret_gn_g[i // 2], ret_w_out[i // 2])
        x = layer_norm(DEEPNORM_ALPHA * x + (1.0 + gate) * y, ln_g[i, 0], ln_b[i, 0])
        shift, scale, gate = adaln_params(c, ada_w[i, 1], ada_b[i, 1])
        h = x * (1.0 + scale) + shift
        y = grouped_moe(h, router_w, router_b, moe_w_gate[i], moe_w_up[i], moe_w_down[i])
        x = layer_norm(DEEPNORM_ALPHA * x + (1.0 + gate) * y, ln_g[i, 1], ln_b[i, 1])
    return x
```

```python
import functools

import jax
import jax.numpy as jnp
from jax import lax
from jax.experimental import pallas as pl
from jax.experimental.pallas import tpu as pltpu

F32 = jnp.float32
BF16 = jnp.bfloat16

D_MODEL = 1024
DEPTH = 4
SB_HEADS = 16
SB_HEAD_DIM = D_MODEL // SB_HEADS
RET_HEADS = 8
RET_QK_DIM = D_MODEL // RET_HEADS
RET_V_DIM = 2 * RET_QK_DIM
ROPE_BASE = 10000.0
N_EXPERTS = 16
N_GROUPS = 4
EXPERTS_PER_GROUP = N_EXPERTS // N_GROUPS
D_EXPERT = D_MODEL // 2
DEEPNORM_ALPHA = (2 * DEPTH) ** 0.25
LN_EPS = 1e-5

LANES = 128
SUBLANES = 8
MXU_DIM = 256
VMEM_LIMIT_BYTES = 56 * 1024 * 1024

PROJ_TM = 512
SB_TQ = MXU_DIM
SB_TK = MXU_DIM
RET_CHUNK = 256
ROUTE_TM = 512
MOE_TM = 256
PAIRS_PER_GROUP = 6
N_BUCKETS = N_GROUPS * PAIRS_PER_GROUP
BUCKET_ROWS = 32
ROW_SLAB = D_MODEL // LANES
DMA_CHUNK = 16
PERM_ROWS = 512
assert ROW_SLAB == SUBLANES


def _cparams(semantics):
    return pltpu.CompilerParams(dimension_semantics=semantics, vmem_limit_bytes=VMEM_LIMIT_BYTES)


def _split_bf16(a):
    hi = a.astype(BF16)
    lo = (a - hi.astype(F32)).astype(BF16)
    return hi, lo


def _ada_kernel(c_ref, w_ref, b_ref, o_ref):
    c = c_ref[...]
    s = c * jax.nn.sigmoid(c)
    sh, sl = _split_bf16(s)
    wh, wl = _split_bf16(w_ref[0])
    acc = jnp.dot(sh, wh, preferred_element_type=F32)
    acc += jnp.dot(sl, wh, preferred_element_type=F32)
    acc += jnp.dot(sh, wl, preferred_element_type=F32)
    o_ref[0] = acc + b_ref[0]


def _ada_params(c, ada_w, ada_b):
    B, D = c.shape
    n_sub = ada_w.shape[0] * ada_w.shape[1]
    N = ada_w.shape[-1]
    tn = N // 2
    w = ada_w.reshape(n_sub, D, N)
    b = ada_b.reshape(n_sub, 1, N)
    out = pl.pallas_call(
        _ada_kernel,
        grid=(n_sub, N // tn),
        in_specs=[
            pl.BlockSpec((B, D), lambda k, j: (0, 0)),
            pl.BlockSpec((1, D, tn), lambda k, j: (k, 0, j)),
            pl.BlockSpec((1, 1, tn), lambda k, j: (k, 0, j)),
        ],
        out_specs=pl.BlockSpec((1, B, tn), lambda k, j: (k, 0, j)),
        out_shape=jax.ShapeDtypeStruct((n_sub, B, N), F32),
        compiler_params=_cparams(("arbitrary", "arbitrary")),
        name="ada_params",
    )(c, w, b)
    return out.reshape(n_sub * B * 3, 1, D)


def _mod_row(k, b, which, B):
    return (k * B + b) * 3 + which


SHIFT, SCALE, GATE = 0, 1, 2


def _modmm_kernel(x_ref, sc_ref, sh_ref, w_ref, o_ref):
    h = x_ref[0] * (1.0 + sc_ref[0]) + sh_ref[0]
    o_ref[0] = jnp.dot(h.astype(BF16), w_ref[...], preferred_element_type=F32).astype(o_ref.dtype)


def _mod_matmul(x, mods, k, w, tn):
    B, S, D = x.shape
    N = w.shape[1]
    tm = PROJ_TM
    return pl.pallas_call(
        _modmm_kernel,
        grid=(N // tn, B, S // tm),
        in_specs=[
            pl.BlockSpec((1, tm, D), lambda n, b, i: (b, i, 0)),
            pl.BlockSpec((1, 1, D), lambda n, b, i: (_mod_row(k, b, SCALE, B), 0, 0)),
            pl.BlockSpec((1, 1, D), lambda n, b, i: (_mod_row(k, b, SHIFT, B), 0, 0)),
            pl.BlockSpec((D, tn), lambda n, b, i: (0, n)),
        ],
        out_specs=pl.BlockSpec((1, tm, tn), lambda n, b, i: (b, i, n)),
        out_shape=jax.ShapeDtypeStruct((B, S, N), BF16),
        compiler_params=_cparams(("arbitrary", "arbitrary", "arbitrary")),
        name="mod_matmul",
    )(x, mods, mods, w)


def _sb_block(k_blk, qT_h, vT_h, u_ref, acc, run, valid):
    z = jnp.dot(k_blk, qT_h, preferred_element_type=F32)
    sp = jnp.maximum(z, 0.0) + jnp.log(1.0 + jnp.exp(-jnp.abs(z)))
    lk = -sp
    if valid is not None:
        lk = jnp.where(valid, lk, 0.0)
    suf = jnp.dot(u_ref[...], lk.astype(BF16), preferred_element_type=F32)
    tot = suf + run
    a = jnp.exp((z - sp) + tot)
    if valid is not None:
        a = jnp.where(valid, a, 0.0)
    acc = acc + jnp.dot(vT_h, a.astype(BF16), preferred_element_type=F32)
    run = tot[0:1, :] + lk[0:1, :]
    return acc, run


def _sb_kernel(q_ref, k_ref, v_ref, u_ref, o_ref, vT_ref):
    S = q_ref.shape[1]
    tq, tk, dh = SB_TQ, SB_TK, SB_HEAD_DIM
    nq = S // tq

    for c in range(S // tk):
        v_c = v_ref[0, c * tk:(c + 1) * tk, :].astype(F32)
        vT_ref[c] = v_c.T.astype(BF16)

    scale = dh ** -0.5

    def q_block(i, carry):
        rows = lax.broadcasted_iota(jnp.int32, (tk, tq), 0)
        cols = lax.broadcasted_iota(jnp.int32, (tk, tq), 1)
        valid = rows < cols
        head_rows = lax.broadcasted_iota(jnp.int32, (2 * dh, tq), 0)
        q0 = pl.multiple_of(i * tq, tq)
        q_i = q_ref[0, pl.ds(q0, tq), :].astype(F32) * scale
        qT = q_i.T
        qT0 = jnp.where(head_rows < dh, qT, 0.0).astype(BF16)
        qT1 = jnp.where(head_rows >= dh, qT, 0.0).astype(BF16)

        def tile(j, state, mask):
            acc0, acc1, run0, run1 = state
            k0 = pl.multiple_of(j * tk, tk)
            k_blk = k_ref[0, pl.ds(k0, tk), :]
            vT = vT_ref[j]
            acc0, run0 = _sb_block(k_blk, qT0, vT[0:dh, :], u_ref, acc0, run0, mask)
            acc1, run1 = _sb_block(k_blk, qT1, vT[dh:2 * dh, :], u_ref, acc1, run1, mask)
            return acc0, acc1, run0, run1

        zero_acc = jnp.zeros((dh, tq), F32)
        zero_run = jnp.zeros((1, tq), F32)
        state = tile(i, (zero_acc, zero_acc, zero_run, zero_run), valid)
        state = lax.fori_loop(0, i, lambda jj, st: tile(i - 1 - jj, st, None), state)
        oT = jnp.concatenate([state[0], state[1]], axis=0)
        o_ref[0, pl.ds(q0, tq), :] = oT.T.astype(o_ref.dtype)
        return carry

    lax.fori_loop(0, nq, q_block, 0)


def _sb_attention(qkv, u):
    B, S, _ = qkv.shape
    n_pairs = SB_HEADS // 2
    blk = (1, S, 2 * SB_HEAD_DIM)
    return pl.pallas_call(
        _sb_kernel,
        grid=(B, n_pairs),
        in_specs=[
            pl.BlockSpec(blk, lambda b, p: (b, 0, p)),
            pl.BlockSpec(blk, lambda b, p: (b, 0, n_pairs + p)),
            pl.BlockSpec(blk, lambda b, p: (b, 0, 2 * n_pairs + p)),
            pl.BlockSpec((SB_TK, SB_TK), lambda b, p: (0, 0)),
        ],
        out_specs=pl.BlockSpec(blk, lambda b, p: (b, 0, p)),
        out_shape=jax.ShapeDtypeStruct((B, S, D_MODEL), BF16),
        scratch_shapes=[pltpu.VMEM((S // SB_TK, 2 * SB_HEAD_DIM, SB_TK), BF16)],
        compiler_params=_cparams(("arbitrary", "arbitrary")),
        name="sb_attention",
    )(qkv, qkv, qkv, u)


def _layer_norm_rows(u, g, b):
    mu = jnp.mean(u, axis=-1, keepdims=True)
    d = u - mu
    var = jnp.mean(d * d, axis=-1, keepdims=True)
    return d * lax.rsqrt(var + LN_EPS) * g + b


def _outln_kernel(a_ref, w_ref, x_ref, gate_ref, g_ref, b_ref, o_ref):
    y = jnp.dot(a_ref[0], w_ref[...], preferred_element_type=F32)
    u = DEEPNORM_ALPHA * x_ref[0] + (1.0 + gate_ref[0]) * y
    o_ref[0] = _layer_norm_rows(u, g_ref[...], b_ref[...])


def _out_proj_ln(a, w, x, mods, k, g, b):
    B, S, D = x.shape
    K = a.shape[-1]
    tm = PROJ_TM
    return pl.pallas_call(
        _outln_kernel,
        grid=(B, S // tm),
        in_specs=[
            pl.BlockSpec((1, tm, K), lambda bb, i: (bb, i, 0)),
            pl.BlockSpec((K, D), lambda bb, i: (0, 0)),
            pl.BlockSpec((1, tm, D), lambda bb, i: (bb, i, 0)),
            pl.BlockSpec((1, 1, D), lambda bb, i: (_mod_row(k, bb, GATE, B), 0, 0)),
            pl.BlockSpec((1, D), lambda bb, i: (0, 0)),
            pl.BlockSpec((1, D), lambda bb, i: (0, 0)),
        ],
        out_specs=pl.BlockSpec((1, tm, D), lambda bb, i: (bb, i, 0)),
        out_shape=jax.ShapeDtypeStruct((B, S, D), F32),
        compiler_params=_cparams(("arbitrary", "arbitrary")),
        name="out_proj_ln",
    )(a, w, x, mods, g.reshape(1, D), b.reshape(1, D))


def _rope_kernel(pos_ref, invf_ref, sign_ref, cc_ref, ss_ref):
    ang = pos_ref[0].astype(F32) * invf_ref[...]
    cc_ref[0] = jnp.cos(ang)
    ss_ref[0] = jnp.sin(ang) * sign_ref[...]


def _rope_tables(positions):
    B, S = positions.shape
    half = RET_QK_DIM // 2
    inv_freq = 1.0 / (ROPE_BASE ** (jnp.arange(0, RET_QK_DIM, 2, dtype=F32) / RET_QK_DIM))
    invf2 = jnp.concatenate([inv_freq, inv_freq]).reshape(1, RET_QK_DIM)
    sign = jnp.concatenate([-jnp.ones((half,), F32), jnp.ones((half,), F32)]).reshape(1, RET_QK_DIM)
    ts = PROJ_TM
    shp = jax.ShapeDtypeStruct((B, S, RET_QK_DIM), F32)
    return pl.pallas_call(
        _rope_kernel,
        grid=(B, S // ts),
        in_specs=[
            pl.BlockSpec((1, ts, 1), lambda b, i: (b, i, 0)),
            pl.BlockSpec((1, RET_QK_DIM), lambda b, i: (0, 0)),
            pl.BlockSpec((1, RET_QK_DIM), lambda b, i: (0, 0)),
        ],
        out_specs=[pl.BlockSpec((1, ts, RET_QK_DIM), lambda b, i: (b, i, 0))] * 2,
        out_shape=[shp, shp],
        compiler_params=_cparams(("arbitrary", "arbitrary")),
        name="rope_tables",
    )(positions.reshape(B, S, 1), invf2, sign)


def _retention_decay_tables():
    C = RET_CHUNK
    kscale = RET_QK_DIM ** -0.5
    log_gamma = jnp.log(1.0 - jnp.exp2(-5.0 - jnp.arange(RET_HEADS, dtype=F32)))
    n = jnp.arange(C, dtype=F32)
    diff = n[:, None] - n[None, :]
    dmask = jnp.where(diff >= 0, jnp.exp(jnp.maximum(diff, 0.0) * log_gamma[:, None, None]), 0.0)
    xi = jnp.exp((n + 1.0) * log_gamma[:, None])
    zeta = jnp.exp((C - 1.0 - n) * log_gamma[:, None])
    chunk_decay = jnp.exp(C * log_gamma)
    dm = dmask * kscale
    xi_t = jnp.broadcast_to(xi[:, :, None], (RET_HEADS, C, RET_V_DIM))
    zeta_t = jnp.broadcast_to(zeta[:, :, None] * kscale, (RET_HEADS, C, RET_QK_DIM))
    decay_t = jnp.broadcast_to(chunk_decay[:, None, None], (RET_HEADS, RET_QK_DIM, RET_V_DIM))
    return dm, xi_t, zeta_t, decay_t


def _ret_kernel(q_ref, k_ref, v_ref, g_ref, cc_ref, ss_ref, dm_ref, xi_ref, zeta_ref, decay_ref,
                gn_ref, o_ref, state_ref):
    S = q_ref.shape[1]
    C = RET_CHUNK
    half = RET_QK_DIM // 2
    state_ref[...] = jnp.zeros_like(state_ref)

    def chunk(c, carry):
        r0 = pl.multiple_of(c * C, C)
        rows = pl.ds(r0, C)
        cc = cc_ref[0, rows, :]
        ss = ss_ref[0, rows, :]
        q = q_ref[0, rows, :].astype(F32)
        k = k_ref[0, rows, :].astype(F32)
        qr = q * cc + pltpu.roll(q, half, 1) * ss
        kr = k * cc + pltpu.roll(k, half, 1) * ss
        qb = qr.astype(BF16)
        v = v_ref[0, rows, :]
        state = state_ref[...]
        qk = lax.dot_general(qb, kr.astype(BF16), (((1,), (1,)), ((), ())),
                             preferred_element_type=F32) * dm_ref[0]
        inner = jnp.dot(qk.astype(BF16), v, preferred_element_type=F32)
        cross = jnp.dot(qb, state.astype(BF16), preferred_element_type=F32) * xi_ref[0]
        kzT = (kr * zeta_ref[0]).T.astype(BF16)
        state_ref[...] = state * decay_ref[0] + jnp.dot(kzT, v, preferred_element_type=F32)
        o = inner + cross
        mu = jnp.mean(o, axis=-1, keepdims=True)
        d = o - mu
        var = jnp.mean(d * d, axis=-1, keepdims=True)
        on = d * lax.rsqrt(var + LN_EPS) * gn_ref[...]
        g = g_ref[0, rows, :].astype(F32)
        o_ref[0, rows, :] = (g * jax.nn.sigmoid(g) * on).astype(o_ref.dtype)
        return carry

    lax.fori_loop(0, S // C, chunk, 0)


def _retention(proj, cc, ss, tables, gn_g):
    B, S, _ = proj.shape
    dk, dv, H, C = RET_QK_DIM, RET_V_DIM, RET_HEADS, RET_CHUNK
    dm, xi_t, zeta_t, decay_t = tables
    return pl.pallas_call(
        _ret_kernel,
        grid=(B, H),
        in_specs=[
            pl.BlockSpec((1, S, dk), lambda b, h: (b, 0, h)),
            pl.BlockSpec((1, S, dk), lambda b, h: (b, 0, H + h)),
            pl.BlockSpec((1, S, dv), lambda b, h: (b, 0, H + h)),
            pl.BlockSpec((1, S, dv), lambda b, h: (b, 0, 2 * H + h)),
            pl.BlockSpec((1, S, dk), lambda b, h: (b, 0, 0)),
            pl.BlockSpec((1, S, dk), lambda b, h: (b, 0, 0)),
            pl.BlockSpec((1, C, C), lambda b, h: (h, 0, 0)),
            pl.BlockSpec((1, C, dv), lambda b, h: (h, 0, 0)),
            pl.BlockSpec((1, C, dk), lambda b, h: (h, 0, 0)),
            pl.BlockSpec((1, dk, dv), lambda b, h: (h, 0, 0)),
            pl.BlockSpec((1, dv), lambda b, h: (0, h)),
        ],
        out_specs=pl.BlockSpec((1, S, dv), lambda b, h: (b, 0, h)),
        out_shape=jax.ShapeDtypeStruct((B, S, H * dv), BF16),
        scratch_shapes=[pltpu.VMEM((dk, dv), F32)],
        compiler_params=_cparams(("arbitrary", "arbitrary")),
        name="retention",
    )(proj, proj, proj, proj, cc, ss, dm, xi_t, zeta_t, decay_t, gn_g.reshape(1, H * dv))


def _store_slabs(ref, val):
    tm = val.shape[0]
    for j in range(ROW_SLAB):
        ref[pl.ds(j, tm, stride=ROW_SLAB), :] = val[:, j * LANES:(j + 1) * LANES]


def _load_slab_chunks(ref, tm):
    return [ref[pl.ds(j, tm, stride=ROW_SLAB), :] for j in range(ROW_SLAB)]


def _top2_flags(vals):
    n = len(vals)
    rank = []
    for k in range(n):
        r = jnp.zeros_like(vals[0])
        for j in range(n):
            if j < k:
                r = r + (vals[j] >= vals[k]).astype(F32)
            elif j > k:
                r = r + (vals[j] > vals[k]).astype(F32)
        rank.append(r)
    is1 = [r == 0.0 for r in rank]
    is2 = [r == 1.0 for r in rank]
    m1 = sum(jnp.where(f, v, 0.0) for f, v in zip(is1, vals))
    m2 = sum(jnp.where(f, v, 0.0) for f, v in zip(is2, vals))
    return is1, is2, m1, m2


def _router_kernel(x_ref, sc_ref, sh_ref, rw_ref, rb_ref, lt_ref,
                   hrow_ref, bucket_ref, rank_ref, counts_ref, carry_ref):
    i = pl.program_id(0)
    tm = x_ref.shape[0]

    @pl.when(i == 0)
    def _():
        carry_ref[...] = jnp.zeros_like(carry_ref)

    h = x_ref[...] * (1.0 + sc_ref[0]) + sh_ref[0]
    _store_slabs(hrow_ref, h)

    hh, hl = _split_bf16(h)
    wh, wl = _split_bf16(rw_ref[...])
    nt = (((1,), (1,)), ((), ()))
    logits = lax.dot_general(wh, hh, nt, preferred_element_type=F32)
    logits += lax.dot_general(wl, hh, nt, preferred_element_type=F32)
    logits += lax.dot_general(wh, hl, nt, preferred_element_type=F32)
    logits = logits + rb_ref[:, 0:1]

    rows = [logits[e:e + 1, :] for e in range(N_EXPERTS)]
    gmax = functools.reduce(jnp.maximum, rows)
    groups = []
    for g in range(N_GROUPS):
        vals = rows[g * EXPERTS_PER_GROUP:(g + 1) * EXPERTS_PER_GROUP]
        is1, is2, m1, m2 = _top2_flags(vals)
        score = jnp.exp(m1 - gmax) + jnp.exp(m2 - gmax)
        groups.append(([jnp.logical_or(a, b) for a, b in zip(is1, is2)], score))
    best = groups[0][1]
    gsel = jnp.zeros_like(best)
    for g in range(1, N_GROUPS):
        better = groups[g][1] > best
        best = jnp.where(better, groups[g][1], best)
        gsel = jnp.where(better, float(g), gsel)
    sel = [jnp.zeros_like(best) > 1.0] * EXPERTS_PER_GROUP
    for g in range(N_GROUPS):
        here = gsel == float(g)
        sel = [jnp.logical_or(s, jnp.logical_and(here, f)) for s, f in zip(sel, groups[g][0])]
    pair = jnp.where(sel[0],
                     jnp.where(sel[1], 0.0, jnp.where(sel[2], 1.0, 2.0)),
                     jnp.where(sel[1], jnp.where(sel[2], 3.0, 4.0), 5.0))
    bucket = (gsel * float(PAIRS_PER_GROUP) + pair).astype(jnp.int32)

    brow = lax.broadcasted_iota(jnp.int32, (BUCKET_ROWS, tm), 0)
    onehot = (brow == bucket).astype(F32)
    before = jnp.dot(onehot.astype(BF16), lt_ref[...], preferred_element_type=F32)
    carry = carry_ref[...]
    in_tile = jnp.sum(onehot * before, axis=0, keepdims=True)
    base = jnp.sum(onehot * carry[:, 0:1], axis=0, keepdims=True)
    rank_ref[...] = (base + in_tile).astype(jnp.int32)
    bucket_ref[...] = bucket
    new_carry = carry + jnp.sum(onehot, axis=1, keepdims=True)
    carry_ref[...] = new_carry
    counts_ref[...] = new_carry.astype(jnp.int32)


def _router(x2, mods, k, B, S, router_wT, router_bt, lt):
    T, D = x2.shape
    tm = ROUTE_TM
    per_b = S // tm
    return pl.pallas_call(
        _router_kernel,
        grid=(T // tm,),
        in_specs=[
            pl.BlockSpec((tm, D), lambda i: (i, 0)),
            pl.BlockSpec((1, 1, D), lambda i: (_mod_row(k, i // per_b, SCALE, B), 0, 0)),
            pl.BlockSpec((1, 1, D), lambda i: (_mod_row(k, i // per_b, SHIFT, B), 0, 0)),
            pl.BlockSpec((N_EXPERTS, D), lambda i: (0, 0)),
            pl.BlockSpec((N_EXPERTS, LANES), lambda i: (0, 0)),
            pl.BlockSpec((tm, tm), lambda i: (0, 0)),
        ],
        out_specs=[
            pl.BlockSpec((tm * ROW_SLAB, LANES), lambda i: (i, 0)),
            pl.BlockSpec((1, tm), lambda i: (0, i)),
            pl.BlockSpec((1, tm), lambda i: (0, i)),
            pl.BlockSpec((BUCKET_ROWS, LANES), lambda i: (0, 0)),
        ],
        out_shape=[
            jax.ShapeDtypeStruct((T * ROW_SLAB, LANES), F32),
            jax.ShapeDtypeStruct((1, T), jnp.int32),
            jax.ShapeDtypeStruct((1, T), jnp.int32),
            jax.ShapeDtypeStruct((BUCKET_ROWS, LANES), jnp.int32),
        ],
        scratch_shapes=[pltpu.VMEM((BUCKET_ROWS, LANES), F32)],
        compiler_params=_cparams(("arbitrary",)),
        name="router",
    )(x2, mods, mods, router_wT, router_bt, lt)


def _row_copy(src_ref, s, dst_ref, d, sem):
    s0 = pl.multiple_of(s * ROW_SLAB, ROW_SLAB)
    d0 = pl.multiple_of(d * ROW_SLAB, ROW_SLAB)
    return pltpu.make_async_copy(src_ref.at[pl.ds(s0, ROW_SLAB)], dst_ref.at[pl.ds(d0, ROW_SLAB)], sem)


def _permute_kernel(idx_ref, pad_lo_ref, pad_hi_ref, src_ref, dst_ref, sem, *, scatter):
    step = pl.program_id(0)
    last = pl.num_programs(0) - 1

    def wait_rows(n):
        for _ in range(n):
            _row_copy(src_ref, 0, dst_ref, 0, sem).wait()

    def wait_step():
        lax.fori_loop(0, PERM_ROWS // DMA_CHUNK, lambda c, carry: (wait_rows(DMA_CHUNK), carry)[1], 0)

    def issue(c, carry):
        for u in range(DMA_CHUNK):
            t = step * PERM_ROWS + c * DMA_CHUNK + u
            if scatter:
                _row_copy(src_ref, t, dst_ref, idx_ref[t], sem).start()
            else:
                _row_copy(src_ref, idx_ref[t], dst_ref, t, sem).start()
        return carry

    lax.fori_loop(0, PERM_ROWS // DMA_CHUNK, issue, 0)

    @pl.when(step > 0)
    def _():
        wait_step()

    @pl.when(step == last)
    def _():
        wait_step()
        if scatter:
            def pad_bucket(b, carry):
                def pad_row(r, cc):
                    cp = _row_copy(src_ref, 0, dst_ref, r, sem)
                    cp.start()
                    cp.wait()
                    return cc
                lax.fori_loop(pad_lo_ref[b], pad_hi_ref[b], pad_row, 0)
                return carry
            lax.fori_loop(0, N_BUCKETS, pad_bucket, 0)


def _permute_rows(src, idx, pad_lo, pad_hi, n_out, scatter):
    n_rows = idx.shape[0]
    kern = functools.partial(_permute_kernel, scatter=scatter)
    return pl.pallas_call(
        kern,
        grid_spec=pltpu.PrefetchScalarGridSpec(
            num_scalar_prefetch=3,
            grid=(n_rows // PERM_ROWS,),
            in_specs=[pl.BlockSpec(memory_space=pl.ANY)],
            out_specs=pl.BlockSpec(memory_space=pl.ANY),
            scratch_shapes=[pltpu.SemaphoreType.DMA(())],
        ),
        out_shape=jax.ShapeDtypeStruct((n_out * ROW_SLAB, LANES), src.dtype),
        compiler_params=_cparams(("arbitrary",)),
        name="scatter_rows" if scatter else "gather_rows",
    )(idx, pad_lo, pad_hi, src)


def _expert(x, wg_ref, wu_ref, wd_ref):
    gte = jnp.dot(x, wg_ref[0], preferred_element_type=F32)
    up = jnp.dot(x, wu_ref[0], preferred_element_type=F32)
    hid = gte * jax.nn.sigmoid(gte) * up
    return jnp.dot(hid.astype(BF16), wd_ref[0], preferred_element_type=F32)


def _router_logit(chunks, rw_ref, rb_ref, e):
    w_row = rw_ref[pl.ds(e, 1), :]
    acc = chunks[0] * w_row[:, 0:LANES]
    for j in range(1, ROW_SLAB):
        acc = acc + chunks[j] * w_row[:, j * LANES:(j + 1) * LANES]
    return jnp.sum(acc, axis=-1, keepdims=True) + rb_ref[pl.ds(e, 1), 0:1]


def _moe_kernel(src_ref, ea_ref, eb_ref, valid_ref,
                xs_ref, rw_ref, rb_ref, wga_ref, wua_ref, wda_ref, wgb_ref, wub_ref, wdb_ref, o_ref):
    i = pl.program_id(0)

    @pl.when(valid_ref[i] == 1)
    def _():
        chunks = _load_slab_chunks(xs_ref, MOE_TM)
        l_a = _router_logit(chunks, rw_ref, rb_ref, ea_ref[i])
        l_b = _router_logit(chunks, rw_ref, rb_ref, eb_ref[i])
        w_a = jax.nn.sigmoid(l_a - l_b)
        w_b = jax.nn.sigmoid(l_b - l_a)
        x = jnp.concatenate([c.astype(BF16) for c in chunks], axis=-1)
        y = w_a * _expert(x, wga_ref, wua_ref, wda_ref) + w_b * _expert(x, wgb_ref, wub_ref, wdb_ref)
        _store_slabs(o_ref, y)


def _grouped_experts(xs, plan, router_wT, router_bt, w_gate, w_up, w_down):
    n_tiles = xs.shape[0] // (MOE_TM * ROW_SLAB)
    D, Fd = D_MODEL, D_EXPERT
    wspec_a = lambda shape: pl.BlockSpec(shape, lambda i, src, ea, eb, va: (ea[i], 0, 0))
    wspec_b = lambda shape: pl.BlockSpec(shape, lambda i, src, ea, eb, va: (eb[i], 0, 0))
    row_spec = pl.BlockSpec((MOE_TM * ROW_SLAB, LANES), lambda i, src, ea, eb, va: (src[i], 0))
    return pl.pallas_call(
        _moe_kernel,
        grid_spec=pltpu.PrefetchScalarGridSpec(
            num_scalar_prefetch=4,
            grid=(n_tiles,),
            in_specs=[
                row_spec,
                pl.BlockSpec((N_EXPERTS, D), lambda i, src, ea, eb, va: (0, 0)),
                pl.BlockSpec((N_EXPERTS, LANES), lambda i, src, ea, eb, va: (0, 0)),
                wspec_a((1, D, Fd)), wspec_a((1, D, Fd)), wspec_a((1, Fd, D)),
                wspec_b((1, D, Fd)), wspec_b((1, D, Fd)), wspec_b((1, Fd, D)),
            ],
            out_specs=row_spec,
        ),
        out_shape=jax.ShapeDtypeStruct(xs.shape, F32),
        compiler_params=_cparams(("arbitrary",)),
        name="grouped_experts",
    )(plan["tile_src"], plan["tile_ea"], plan["tile_eb"], plan["tile_valid"],
      xs, router_wT, router_bt, w_gate, w_up, w_down, w_gate, w_up, w_down)


def _moe_plan(bucket, rank, counts, n_tiles):
    cnt = counts[:N_BUCKETS, 0]
    tiles_b = (cnt + MOE_TM - 1) // MOE_TM
    tile_end = jnp.cumsum(tiles_b)
    tile_start = tile_end - tiles_b
    row_start = tile_start * MOE_TM
    dest = jnp.take(row_start, bucket[0]) + rank[0]
    n_used = tile_end[-1]
    t = jnp.arange(n_tiles, dtype=jnp.int32)
    t_c = jnp.minimum(t, n_used - 1)
    tb = jnp.sum((t_c[:, None] >= tile_end[None, :]).astype(jnp.int32), axis=1)
    pair_lo = jnp.array([0, 0, 0, 1, 1, 2], jnp.int32)
    pair_hi = jnp.array([1, 2, 3, 2, 3, 3], jnp.int32)
    grp = tb // PAIRS_PER_GROUP
    pr = tb % PAIRS_PER_GROUP
    return {
        "dest": dest.astype(jnp.int32),
        "pad_lo": (row_start + cnt).astype(jnp.int32),
        "pad_hi": (tile_end * MOE_TM).astype(jnp.int32),
        "tile_src": t_c.astype(jnp.int32),
        "tile_ea": (grp * EXPERTS_PER_GROUP + jnp.take(pair_lo, pr)).astype(jnp.int32),
        "tile_eb": (grp * EXPERTS_PER_GROUP + jnp.take(pair_hi, pr)).astype(jnp.int32),
        "tile_valid": (t < n_used).astype(jnp.int32),
    }


def _resln_kernel(y_ref, x_ref, gate_ref, g_ref, b_ref, o_ref):
    tm = x_ref.shape[0]
    y = jnp.concatenate(_load_slab_chunks(y_ref, tm), axis=-1)
    u = DEEPNORM_ALPHA * x_ref[...] + (1.0 + gate_ref[0]) * y
    o_ref[...] = _layer_norm_rows(u, g_ref[...], b_ref[...])


def _residual_ln(yrow, x2, mods, k, B, S, g, b):
    T, D = x2.shape
    tm = PROJ_TM
    per_b = S // tm
    return pl.pallas_call(
        _resln_kernel,
        grid=(T // tm,),
        in_specs=[
            pl.BlockSpec((tm * ROW_SLAB, LANES), lambda i: (i, 0)),
            pl.BlockSpec((tm, D), lambda i: (i, 0)),
            pl.BlockSpec((1, 1, D), lambda i: (_mod_row(k, i // per_b, GATE, B), 0, 0)),
            pl.BlockSpec((1, D), lambda i: (0, 0)),
            pl.BlockSpec((1, D), lambda i: (0, 0)),
        ],
        out_specs=pl.BlockSpec((tm, D), lambda i: (i, 0)),
        out_shape=jax.ShapeDtypeStruct((T, D), F32),
        compiler_params=_cparams(("arbitrary",)),
        name="residual_ln",
    )(yrow, x2, mods, g.reshape(1, D), b.reshape(1, D))


def _moe_sublayer(x, mods, k, router_wT, router_bt, lt, w_gate, w_up, w_down, g, b):
    B, S, D = x.shape
    T = B * S
    x2 = x.reshape(T, D)
    hrow, bucket, rank, counts = _router(x2, mods, k, B, S, router_wT, router_bt, lt)
    n_tiles = T // MOE_TM + N_BUCKETS
    plan = _moe_plan(bucket, rank, counts, n_tiles)
    xs = _permute_rows(hrow, plan["dest"], plan["pad_lo"], plan["pad_hi"], n_tiles * MOE_TM, scatter=True)
    ys = _grouped_experts(xs, plan, router_wT, router_bt, w_gate, w_up, w_down)
    yrow = _permute_rows(ys, plan["dest"], plan["pad_lo"], plan["pad_hi"], T, scatter=False)
    return _residual_ln(yrow, x2, mods, k, B, S, g, b).reshape(B, S, D)


def kernel(x, c, positions, ada_w, ada_b, ln_g, ln_b, sb_w_in, sb_w_out, ret_w_in, ret_gn_g, ret_w_out,
           router_w, router_b, moe_w_gate, moe_w_up, moe_w_down):
    mods = _ada_params(c, ada_w, ada_b)
    cc, ss = _rope_tables(positions)
    tables = _retention_decay_tables()
    u = (jnp.arange(SB_TK)[:, None] < jnp.arange(SB_TK)[None, :]).astype(BF16)
    lt = (jnp.arange(ROUTE_TM)[:, None] < jnp.arange(ROUTE_TM)[None, :]).astype(BF16)
    router_wT = router_w.T
    router_bt = jnp.broadcast_to(router_b[:, None], (N_EXPERTS, LANES))
    for i in range(DEPTH):
        k_mix, k_moe = 2 * i, 2 * i + 1
        if i % 2 == 0:
            qkv = _mod_matmul(x, mods, k_mix, sb_w_in[i // 2].astype(BF16), tn=3 * D_MODEL // 2)
            a = _sb_attention(qkv, u)
            w_out = sb_w_out[i // 2]
        else:
            proj = _mod_matmul(x, mods, k_mix, ret_w_in[i // 2].astype(BF16), tn=2 * D_MODEL)
            a = _retention(proj, cc, ss, tables, ret_gn_g[i // 2])
            w_out = ret_w_out[i // 2]
        x = _out_proj_ln(a, w_out.astype(BF16), x, mods, k_mix, ln_g[i, 0], ln_b[i, 0])
        x = _moe_sublayer(x, mods, k_moe, router_wT, router_bt, lt,
                          moe_w_gate[i].astype(BF16), moe_w_up[i].astype(BF16), moe_w_down[i].astype(BF16),
                          ln_g[i, 1], ln_b[i, 1])
    return x
```

```python
import functools

import jax
import jax.numpy as jnp
from jax import lax
from jax.experimental import pallas as pl
from jax.experimental.pallas import tpu as pltpu

F32 = jnp.float32
BF16 = jnp.bfloat16

D_MODEL = 1024
DEPTH = 4
SB_HEADS = 16
SB_HEAD_DIM = D_MODEL // SB_HEADS
RET_HEADS = 8
RET_QK_DIM = D_MODEL // RET_HEADS
RET_V_DIM = 2 * RET_QK_DIM
ROPE_BASE = 10000.0
N_EXPERTS = 16
N_GROUPS = 4
EXPERTS_PER_GROUP = N_EXPERTS // N_GROUPS
D_EXPERT = D_MODEL // 2
DEEPNORM_ALPHA = (2 * DEPTH) ** 0.25
LN_EPS = 1e-5
LOG2_E = 1.4426950408889634

LANES = 128
SUBLANES = 8
MXU_DIM = 256
VMEM_LIMIT_BYTES = 56 * 1024 * 1024

PROJ_TM = 512
SB_TQ = MXU_DIM
SB_TK = MXU_DIM
RET_CHUNK = 256
ROUTE_TM = 512
MOE_TM = 256
PAIRS_PER_GROUP = 6
N_BUCKETS = N_GROUPS * PAIRS_PER_GROUP
BUCKET_ROWS = 32
ROW_SLAB = D_MODEL // LANES
DMA_CHUNK = 16
assert ROW_SLAB == SUBLANES


def _cparams(semantics):
    return pltpu.CompilerParams(dimension_semantics=semantics, vmem_limit_bytes=VMEM_LIMIT_BYTES)


def _split_bf16(a):
    hi = a.astype(BF16)
    lo = (a - hi.astype(F32)).astype(BF16)
    return hi, lo


def _ada_kernel(c_ref, w_ref, b_ref, o_ref):
    c = c_ref[...]
    s = c * jax.nn.sigmoid(c)
    sh, sl = _split_bf16(s)
    wh, wl = _split_bf16(w_ref[0])
    acc = jnp.dot(sh, wh, preferred_element_type=F32)
    acc += jnp.dot(sl, wh, preferred_element_type=F32)
    acc += jnp.dot(sh, wl, preferred_element_type=F32)
    o_ref[0] = acc + b_ref[0]


def _ada_params(c, ada_w, ada_b):
    B, D = c.shape
    n_sub = ada_w.shape[0] * ada_w.shape[1]
    N = ada_w.shape[-1]
    tn = N // 2
    w = ada_w.reshape(n_sub, D, N)
    b = ada_b.reshape(n_sub, 1, N)
    out = pl.pallas_call(
        _ada_kernel,
        grid=(n_sub, N // tn),
        in_specs=[
            pl.BlockSpec((B, D), lambda k, j: (0, 0)),
            pl.BlockSpec((1, D, tn), lambda k, j: (k, 0, j)),
            pl.BlockSpec((1, 1, tn), lambda k, j: (k, 0, j)),
        ],
        out_specs=pl.BlockSpec((1, B, tn), lambda k, j: (k, 0, j)),
        out_shape=jax.ShapeDtypeStruct((n_sub, B, N), F32),
        compiler_params=_cparams(("arbitrary", "arbitrary")),
        name="ada_params",
    )(c, w, b)
    return out.reshape(n_sub * B * 3, 1, D)


def _mod_row(k, b, which, B):
    return (k * B + b) * 3 + which


SHIFT, SCALE, GATE = 0, 1, 2


def _modmm_kernel(x_ref, sc_ref, sh_ref, w_ref, o_ref):
    h = x_ref[0] * (1.0 + sc_ref[0]) + sh_ref[0]
    o_ref[0] = jnp.dot(h.astype(BF16), w_ref[...], preferred_element_type=F32).astype(o_ref.dtype)


def _mod_matmul(x, mods, k, w, tn):
    B, S, D = x.shape
    N = w.shape[1]
    tm = PROJ_TM
    return pl.pallas_call(
        _modmm_kernel,
        grid=(N // tn, B, S // tm),
        in_specs=[
            pl.BlockSpec((1, tm, D), lambda n, b, i: (b, i, 0)),
            pl.BlockSpec((1, 1, D), lambda n, b, i: (_mod_row(k, b, SCALE, B), 0, 0)),
            pl.BlockSpec((1, 1, D), lambda n, b, i: (_mod_row(k, b, SHIFT, B), 0, 0)),
            pl.BlockSpec((D, tn), lambda n, b, i: (0, n)),
        ],
        out_specs=pl.BlockSpec((1, tm, tn), lambda n, b, i: (b, i, n)),
        out_shape=jax.ShapeDtypeStruct((B, S, N), BF16),
        compiler_params=_cparams(("arbitrary", "arbitrary", "arbitrary")),
        name="mod_matmul",
    )(x, mods, mods, w)


def _sb_scores(k_ref, j, qT0, qT1):
    k0 = pl.multiple_of(j * SB_TK, SB_TK)
    k_blk = k_ref[0, pl.ds(k0, SB_TK), :]
    return (jnp.dot(k_blk, qT0, preferred_element_type=F32),
            jnp.dot(k_blk, qT1, preferred_element_type=F32))


def _sb_stage_a(z, valid, sp_ref, d_ref, h):
    sp = jnp.maximum(z, 0.0) + jnp.log2(1.0 + jnp.exp2(-jnp.abs(z)))
    d = z - sp
    if valid is not None:
        sp = jnp.where(valid, sp, 0.0)
    sp_ref[h] = sp.astype(BF16)
    d_ref[h] = d


def _sb_stage_b(suf, run, sp_ref, d_ref, h, valid):
    tot = suf + run
    a = jnp.exp2(d_ref[h] - tot)
    if valid is not None:
        a = jnp.where(valid, a, 0.0)
    return a.astype(BF16), tot[0:1, :] + sp_ref[h, 0:1, :].astype(F32)


def _sb_kernel(q_ref, k_ref, v_ref, u_ref, o_ref, vT_ref, sp_ref, d_ref):
    S = q_ref.shape[1]
    tq, tk, dh = SB_TQ, SB_TK, SB_HEAD_DIM
    nq = S // tq

    for c in range(S // tk):
        v_c = v_ref[0, c * tk:(c + 1) * tk, :].astype(F32)
        vT_ref[c] = v_c.T.astype(BF16)

    scale = dh ** -0.5 * LOG2_E

    def q_block(i, carry):
        rows = lax.broadcasted_iota(jnp.int32, (tk, tq), 0)
        cols = lax.broadcasted_iota(jnp.int32, (tk, tq), 1)
        valid = rows < cols
        head_rows = lax.broadcasted_iota(jnp.int32, (2 * dh, tq), 0)
        q0 = pl.multiple_of(i * tq, tq)
        q_i = q_ref[0, pl.ds(q0, tq), :].astype(F32) * scale
        qT = q_i.T
        qT0 = jnp.where(head_rows < dh, qT, 0.0).astype(BF16)
        qT1 = jnp.where(head_rows >= dh, qT, 0.0).astype(BF16)

        def step(j, j_next, state, mask):
            acc0, acc1, run0, run1 = state
            u = u_ref[...]
            suf0 = jnp.dot(u, sp_ref[0], preferred_element_type=F32)
            suf1 = jnp.dot(u, sp_ref[1], preferred_element_type=F32)
            if j_next is not None:
                z0, z1 = _sb_scores(k_ref, j_next, qT0, qT1)
            a0, run0 = _sb_stage_b(suf0, run0, sp_ref, d_ref, 0, mask)
            a1, run1 = _sb_stage_b(suf1, run1, sp_ref, d_ref, 1, mask)
            vT = vT_ref[j]
            acc0 = acc0 + jnp.dot(vT[0:dh, :], a0, preferred_element_type=F32)
            acc1 = acc1 + jnp.dot(vT[dh:2 * dh, :], a1, preferred_element_type=F32)
            if j_next is not None:
                _sb_stage_a(z0, None, sp_ref, d_ref, 0)
                _sb_stage_a(z1, None, sp_ref, d_ref, 1)
            return acc0, acc1, run0, run1

        z0, z1 = _sb_scores(k_ref, i, qT0, qT1)
        _sb_stage_a(z0, valid, sp_ref, d_ref, 0)
        _sb_stage_a(z1, valid, sp_ref, d_ref, 1)
        zero_acc = jnp.zeros((dh, tq), F32)
        zero_run = jnp.zeros((1, tq), F32)
        state = (zero_acc, zero_acc, zero_run, zero_run)
        state = step(i, jnp.maximum(i - 1, 0), state, valid)
        state = lax.fori_loop(0, jnp.maximum(i - 1, 0), lambda t, st: step(i - 1 - t, i - 2 - t, st, None), state)
        state = lax.fori_loop(0, jnp.minimum(i, 1), lambda t, st: step(0, None, st, None), state)
        oT = jnp.concatenate([state[0], state[1]], axis=0)
        o_ref[0, pl.ds(q0, tq), :] = oT.T.astype(o_ref.dtype)
        return carry

    lax.fori_loop(0, nq, q_block, 0)


def _sb_attention(qkv, u):
    B, S, _ = qkv.shape
    n_pairs = SB_HEADS // 2
    blk = (1, S, 2 * SB_HEAD_DIM)
    return pl.pallas_call(
        _sb_kernel,
        grid=(B, n_pairs),
        in_specs=[
            pl.BlockSpec(blk, lambda b, p: (b, 0, p)),
            pl.BlockSpec(blk, lambda b, p: (b, 0, n_pairs + p)),
            pl.BlockSpec(blk, lambda b, p: (b, 0, 2 * n_pairs + p)),
            pl.BlockSpec((SB_TK, SB_TK), lambda b, p: (0, 0)),
        ],
        out_specs=pl.BlockSpec(blk, lambda b, p: (b, 0, p)),
        out_shape=jax.ShapeDtypeStruct((B, S, D_MODEL), BF16),
        scratch_shapes=[pltpu.VMEM((S // SB_TK, 2 * SB_HEAD_DIM, SB_TK), BF16),
                        pltpu.VMEM((2, SB_TK, SB_TQ), BF16),
                        pltpu.VMEM((2, SB_TK, SB_TQ), F32)],
        compiler_params=_cparams(("arbitrary", "arbitrary")),
        name="sb_attention",
    )(qkv, qkv, qkv, u)


def _layer_norm_rows(u, g, b):
    mu = jnp.mean(u, axis=-1, keepdims=True)
    d = u - mu
    var = jnp.mean(d * d, axis=-1, keepdims=True)
    return d * lax.rsqrt(var + LN_EPS) * g + b


def _outln_kernel(a_ref, w_ref, x_ref, gate_ref, g_ref, b_ref, o_ref):
    y = jnp.dot(a_ref[0], w_ref[...], preferred_element_type=F32)
    u = DEEPNORM_ALPHA * x_ref[0] + (1.0 + gate_ref[0]) * y
    o_ref[0] = _layer_norm_rows(u, g_ref[...], b_ref[...])


def _out_proj_ln(a, w, x, mods, k, g, b):
    B, S, D = x.shape
    K = a.shape[-1]
    tm = PROJ_TM
    return pl.pallas_call(
        _outln_kernel,
        grid=(B, S // tm),
        in_specs=[
            pl.BlockSpec((1, tm, K), lambda bb, i: (bb, i, 0)),
            pl.BlockSpec((K, D), lambda bb, i: (0, 0)),
            pl.BlockSpec((1, tm, D), lambda bb, i: (bb, i, 0)),
            pl.BlockSpec((1, 1, D), lambda bb, i: (_mod_row(k, bb, GATE, B), 0, 0)),
            pl.BlockSpec((1, D), lambda bb, i: (0, 0)),
            pl.BlockSpec((1, D), lambda bb, i: (0, 0)),
        ],
        out_specs=pl.BlockSpec((1, tm, D), lambda bb, i: (bb, i, 0)),
        out_shape=jax.ShapeDtypeStruct((B, S, D), F32),
        compiler_params=_cparams(("arbitrary", "arbitrary")),
        name="out_proj_ln",
    )(a, w, x, mods, g.reshape(1, D), b.reshape(1, D))


def _rope_kernel(pos_ref, invf_ref, sign_ref, cc_ref, ss_ref):
    ang = pos_ref[0].astype(F32) * invf_ref[...]
    cc_ref[0] = jnp.cos(ang)
    ss_ref[0] = jnp.sin(ang) * sign_ref[...]


def _rope_tables(positions):
    B, S = positions.shape
    half = RET_QK_DIM // 2
    inv_freq = 1.0 / (ROPE_BASE ** (jnp.arange(0, RET_QK_DIM, 2, dtype=F32) / RET_QK_DIM))
    invf2 = jnp.concatenate([inv_freq, inv_freq]).reshape(1, RET_QK_DIM)
    sign = jnp.concatenate([-jnp.ones((half,), F32), jnp.ones((half,), F32)]).reshape(1, RET_QK_DIM)
    ts = PROJ_TM
    shp = jax.ShapeDtypeStruct((B, S, RET_QK_DIM), F32)
    return pl.pallas_call(
        _rope_kernel,
        grid=(B, S // ts),
        in_specs=[
            pl.BlockSpec((1, ts, 1), lambda b, i: (b, i, 0)),
            pl.BlockSpec((1, RET_QK_DIM), lambda b, i: (0, 0)),
            pl.BlockSpec((1, RET_QK_DIM), lambda b, i: (0, 0)),
        ],
        out_specs=[pl.BlockSpec((1, ts, RET_QK_DIM), lambda b, i: (b, i, 0))] * 2,
        out_shape=[shp, shp],
        compiler_params=_cparams(("arbitrary", "arbitrary")),
        name="rope_tables",
    )(positions.reshape(B, S, 1), invf2, sign)


def _retention_decay_tables():
    C = RET_CHUNK
    kscale = RET_QK_DIM ** -0.5
    log_gamma = jnp.log(1.0 - jnp.exp2(-5.0 - jnp.arange(RET_HEADS, dtype=F32)))
    n = jnp.arange(C, dtype=F32)
    diff = n[:, None] - n[None, :]
    dmask = jnp.where(diff >= 0, jnp.exp(jnp.maximum(diff, 0.0) * log_gamma[:, None, None]), 0.0)
    xi = jnp.exp((n + 1.0) * log_gamma[:, None])
    zeta = jnp.exp((C - 1.0 - n) * log_gamma[:, None])
    chunk_decay = jnp.exp(C * log_gamma)
    dm = dmask * kscale
    xi_t = jnp.broadcast_to(xi[:, :, None], (RET_HEADS, C, RET_V_DIM))
    zeta_t = jnp.broadcast_to(zeta[:, :, None] * kscale, (RET_HEADS, C, RET_QK_DIM))
    decay_t = jnp.broadcast_to(chunk_decay[:, None, None], (RET_HEADS, RET_QK_DIM, RET_V_DIM))
    return dm, xi_t, zeta_t, decay_t


def _ret_kernel(q_ref, k_ref, v_ref, g_ref, cc_ref, ss_ref, dm_ref, xi_ref, zeta_ref, decay_ref,
                gn_ref, o_ref, state_ref):
    S = q_ref.shape[1]
    C = RET_CHUNK
    half = RET_QK_DIM // 2
    state_ref[...] = jnp.zeros_like(state_ref)

    def chunk(c, carry):
        r0 = pl.multiple_of(c * C, C)
        rows = pl.ds(r0, C)
        cc = cc_ref[0, rows, :]
        ss = ss_ref[0, rows, :]
        q = q_ref[0, rows, :].astype(F32)
        k = k_ref[0, rows, :].astype(F32)
        qr = q * cc + pltpu.roll(q, half, 1) * ss
        kr = k * cc + pltpu.roll(k, half, 1) * ss
        qb = qr.astype(BF16)
        v = v_ref[0, rows, :]
        state = state_ref[...]
        qk = lax.dot_general(qb, kr.astype(BF16), (((1,), (1,)), ((), ())),
                             preferred_element_type=F32) * dm_ref[0]
        inner = jnp.dot(qk.astype(BF16), v, preferred_element_type=F32)
        cross = jnp.dot(qb, state.astype(BF16), preferred_element_type=F32) * xi_ref[0]
        kzT = (kr * zeta_ref[0]).T.astype(BF16)
        state_ref[...] = state * decay_ref[0] + jnp.dot(kzT, v, preferred_element_type=F32)
        o = inner + cross
        mu = jnp.mean(o, axis=-1, keepdims=True)
        d = o - mu
        var = jnp.mean(d * d, axis=-1, keepdims=True)
        on = d * lax.rsqrt(var + LN_EPS) * gn_ref[...]
        g = g_ref[0, rows, :].astype(F32)
        o_ref[0, rows, :] = (g * jax.nn.sigmoid(g) * on).astype(o_ref.dtype)
        return carry

    lax.fori_loop(0, S // C, chunk, 0)


def _retention(proj, cc, ss, tables, gn_g):
    B, S, _ = proj.shape
    dk, dv, H, C = RET_QK_DIM, RET_V_DIM, RET_HEADS, RET_CHUNK
    dm, xi_t, zeta_t, decay_t = tables
    return pl.pallas_call(
        _ret_kernel,
        grid=(B, H),
        in_specs=[
            pl.BlockSpec((1, S, dk), lambda b, h: (b, 0, h)),
            pl.BlockSpec((1, S, dk), lambda b, h: (b, 0, H + h)),
            pl.BlockSpec((1, S, dv), lambda b, h: (b, 0, H + h)),
            pl.BlockSpec((1, S, dv), lambda b, h: (b, 0, 2 * H + h)),
            pl.BlockSpec((1, S, dk), lambda b, h: (b, 0, 0)),
            pl.BlockSpec((1, S, dk), lambda b, h: (b, 0, 0)),
            pl.BlockSpec((1, C, C), lambda b, h: (h, 0, 0)),
            pl.BlockSpec((1, C, dv), lambda b, h: (h, 0, 0)),
            pl.BlockSpec((1, C, dk), lambda b, h: (h, 0, 0)),
            pl.BlockSpec((1, dk, dv), lambda b, h: (h, 0, 0)),
            pl.BlockSpec((1, dv), lambda b, h: (0, h)),
        ],
        out_specs=pl.BlockSpec((1, S, dv), lambda b, h: (b, 0, h)),
        out_shape=jax.ShapeDtypeStruct((B, S, H * dv), BF16),
        scratch_shapes=[pltpu.VMEM((dk, dv), F32)],
        compiler_params=_cparams(("arbitrary", "arbitrary")),
        name="retention",
    )(proj, proj, proj, proj, cc, ss, dm, xi_t, zeta_t, decay_t, gn_g.reshape(1, H * dv))


def _store_slabs(ref, val):
    tm = val.shape[0]
    for j in range(ROW_SLAB):
        ref[pl.ds(j, tm, stride=ROW_SLAB), :] = val[:, j * LANES:(j + 1) * LANES]


def _load_slab_chunks(ref, tm):
    return [ref[pl.ds(j, tm, stride=ROW_SLAB), :] for j in range(ROW_SLAB)]


def _top2_flags(vals):
    n = len(vals)
    rank = []
    for k in range(n):
        r = jnp.zeros_like(vals[0])
        for j in range(n):
            if j < k:
                r = r + (vals[j] >= vals[k]).astype(F32)
            elif j > k:
                r = r + (vals[j] > vals[k]).astype(F32)
        rank.append(r)
    is1 = [r == 0.0 for r in rank]
    is2 = [r == 1.0 for r in rank]
    m1 = sum(jnp.where(f, v, 0.0) for f, v in zip(is1, vals))
    m2 = sum(jnp.where(f, v, 0.0) for f, v in zip(is2, vals))
    return is1, is2, m1, m2


def _router_kernel(x_ref, sc_ref, sh_ref, rw_ref, rb_ref, lt_ref,
                   bucket_ref, rank_ref, counts_ref, carry_ref):
    i = pl.program_id(0)
    tm = x_ref.shape[0]

    @pl.when(i == 0)
    def _():
        carry_ref[...] = jnp.zeros_like(carry_ref)

    h = x_ref[...] * (1.0 + sc_ref[0]) + sh_ref[0]
    hh, hl = _split_bf16(h)
    wh, wl = _split_bf16(rw_ref[...])
    nt = (((1,), (1,)), ((), ()))
    logits = lax.dot_general(wh, hh, nt, preferred_element_type=F32)
    logits += lax.dot_general(wl, hh, nt, preferred_element_type=F32)
    logits += lax.dot_general(wh, hl, nt, preferred_element_type=F32)
    logits = logits + rb_ref[:, 0:1]

    rows = [logits[e:e + 1, :] for e in range(N_EXPERTS)]
    gmax = functools.reduce(jnp.maximum, rows)
    groups = []
    for g in range(N_GROUPS):
        vals = rows[g * EXPERTS_PER_GROUP:(g + 1) * EXPERTS_PER_GROUP]
        is1, is2, m1, m2 = _top2_flags(vals)
        score = jnp.exp(m1 - gmax) + jnp.exp(m2 - gmax)
        groups.append(([jnp.logical_or(a, b) for a, b in zip(is1, is2)], score))
    best = groups[0][1]
    gsel = jnp.zeros_like(best)
    for g in range(1, N_GROUPS):
        better = groups[g][1] > best
        best = jnp.where(better, groups[g][1], best)
        gsel = jnp.where(better, float(g), gsel)
    sel = [jnp.zeros_like(best) > 1.0] * EXPERTS_PER_GROUP
    for g in range(N_GROUPS):
        here = gsel == float(g)
        sel = [jnp.logical_or(s, jnp.logical_and(here, f)) for s, f in zip(sel, groups[g][0])]
    pair = jnp.where(sel[0],
                     jnp.where(sel[1], 0.0, jnp.where(sel[2], 1.0, 2.0)),
                     jnp.where(sel[1], jnp.where(sel[2], 3.0, 4.0), 5.0))
    bucket = (gsel * float(PAIRS_PER_GROUP) + pair).astype(jnp.int32)

    brow = lax.broadcasted_iota(jnp.int32, (BUCKET_ROWS, tm), 0)
    onehot = (brow == bucket).astype(F32)
    before = jnp.dot(onehot.astype(BF16), lt_ref[...], preferred_element_type=F32)
    carry = carry_ref[...]
    in_tile = jnp.sum(onehot * before, axis=0, keepdims=True)
    base = jnp.sum(onehot * carry[:, 0:1], axis=0, keepdims=True)
    rank_ref[...] = (base + in_tile).astype(jnp.int32)
    bucket_ref[...] = bucket
    new_carry = carry + jnp.sum(onehot, axis=1, keepdims=True)
    carry_ref[...] = new_carry
    counts_ref[...] = new_carry.astype(jnp.int32)


def _router(x2, mods, k, B, S, router_wT, router_bt, lt):
    T, D = x2.shape
    tm = ROUTE_TM
    per_b = S // tm
    return pl.pallas_call(
        _router_kernel,
        grid=(T // tm,),
        in_specs=[
            pl.BlockSpec((tm, D), lambda i: (i, 0)),
            pl.BlockSpec((1, 1, D), lambda i: (_mod_row(k, i // per_b, SCALE, B), 0, 0)),
            pl.BlockSpec((1, 1, D), lambda i: (_mod_row(k, i // per_b, SHIFT, B), 0, 0)),
            pl.BlockSpec((N_EXPERTS, D), lambda i: (0, 0)),
            pl.BlockSpec((N_EXPERTS, LANES), lambda i: (0, 0)),
            pl.BlockSpec((tm, tm), lambda i: (0, 0)),
        ],
        out_specs=[
            pl.BlockSpec((1, tm), lambda i: (0, i)),
            pl.BlockSpec((1, tm), lambda i: (0, i)),
            pl.BlockSpec((BUCKET_ROWS, LANES), lambda i: (0, 0)),
        ],
        out_shape=[
            jax.ShapeDtypeStruct((1, T), jnp.int32),
            jax.ShapeDtypeStruct((1, T), jnp.int32),
            jax.ShapeDtypeStruct((BUCKET_ROWS, LANES), jnp.int32),
        ],
        scratch_shapes=[pltpu.VMEM((BUCKET_ROWS, LANES), F32)],
        compiler_params=_cparams(("arbitrary",)),
        name="router",
    )(x2, mods, mods, router_wT, router_bt, lt)


def _slab(ref, r):
    return ref.at[pl.ds(pl.multiple_of(r * ROW_SLAB, ROW_SLAB), ROW_SLAB)]


def _wait_rows(src_ref, dst_ref, sem, n):
    def body(c, carry):
        for _ in range(DMA_CHUNK):
            pltpu.make_async_copy(_slab(src_ref, 0), _slab(dst_ref, 0), sem).wait()
        return carry
    lax.fori_loop(0, n // DMA_CHUNK, body, 0)


def _scatter_kernel(dest_ref, pad_lo_ref, pad_hi_ref, x_ref, sc_ref, sh_ref, xs_ref, hbuf, sems):
    i = pl.program_id(0)
    last = pl.num_programs(0) - 1
    tm = x_ref.shape[0]
    h = x_ref[...] * (1.0 + sc_ref[0]) + sh_ref[0]

    for s in range(2):
        @pl.when(lax.rem(i, 2) == s)
        def _(s=s):
            buf = hbuf.at[s]
            _store_slabs(buf, h)

            def issue(c, carry):
                for u in range(DMA_CHUNK):
                    r = c * DMA_CHUNK + u
                    pltpu.make_async_copy(_slab(buf, r), _slab(xs_ref, dest_ref[i * tm + r]), sems.at[s]).start()
                return carry
            lax.fori_loop(0, tm // DMA_CHUNK, issue, 0)

            @pl.when(i > 0)
            def _():
                _wait_rows(hbuf.at[1 - s], xs_ref, sems.at[1 - s], tm)

            @pl.when(i == last)
            def _():
                _wait_rows(buf, xs_ref, sems.at[s], tm)

                def filler(row0, n_rows, start):
                    cp = pltpu.make_async_copy(
                        buf.at[pl.ds(0, n_rows * ROW_SLAB)],
                        xs_ref.at[pl.ds(pl.multiple_of(row0 * ROW_SLAB, ROW_SLAB), n_rows * ROW_SLAB)],
                        sems.at[s])
                    if start:
                        cp.start()
                    else:
                        cp.wait()

                def fill(start):
                    def bucket(b, carry):
                        pos = pad_lo_ref[b]
                        n = pad_hi_ref[b] - pos
                        for bit in reversed(range(MOE_TM.bit_length() - 1)):
                            take = lax.rem(lax.shift_right_logical(n, bit), 2)

                            @pl.when(take == 1)
                            def _(pos=pos, bit=bit):
                                filler(pos, 1 << bit, start)
                            pos = pos + take * (1 << bit)
                        return carry
                    lax.fori_loop(0, N_BUCKETS, bucket, 0)

                    def tail(t, carry):
                        filler(t * MOE_TM, MOE_TM, start)
                        return carry
                    n_tiles = xs_ref.shape[0] // (MOE_TM * ROW_SLAB)
                    lax.fori_loop(pad_hi_ref[N_BUCKETS - 1] // MOE_TM, n_tiles, tail, 0)
                fill(True)
                fill(False)


def _scatter_rows(x2, mods, k, B, S, plan, n_out):
    T, D = x2.shape
    tm = PROJ_TM
    per_b = S // tm
    return pl.pallas_call(
        _scatter_kernel,
        grid_spec=pltpu.PrefetchScalarGridSpec(
            num_scalar_prefetch=3,
            grid=(T // tm,),
            in_specs=[
                pl.BlockSpec((tm, D), lambda i, *_: (i, 0)),
                pl.BlockSpec((1, 1, D), lambda i, *_: (_mod_row(k, i // per_b, SCALE, B), 0, 0)),
                pl.BlockSpec((1, 1, D), lambda i, *_: (_mod_row(k, i // per_b, SHIFT, B), 0, 0)),
            ],
            out_specs=pl.BlockSpec(memory_space=pl.ANY),
            scratch_shapes=[pltpu.VMEM((2, tm * ROW_SLAB, LANES), F32), pltpu.SemaphoreType.DMA((2,))],
        ),
        out_shape=jax.ShapeDtypeStruct((n_out * ROW_SLAB, LANES), F32),
        compiler_params=_cparams(("arbitrary",)),
        name="scatter_rows",
    )(plan["dest"], plan["pad_lo"], plan["pad_hi"], x2, mods, mods)


def _expert(x, wg_ref, wu_ref, wd_ref):
    gte = jnp.dot(x, wg_ref[0], preferred_element_type=F32)
    up = jnp.dot(x, wu_ref[0], preferred_element_type=F32)
    hid = gte * jax.nn.sigmoid(gte) * up
    return jnp.dot(hid.astype(BF16), wd_ref[0], preferred_element_type=F32)


def _router_logit(chunks, rw_ref, rb_ref, e):
    w_row = rw_ref[pl.ds(e, 1), :]
    acc = chunks[0] * w_row[:, 0:LANES]
    for j in range(1, ROW_SLAB):
        acc = acc + chunks[j] * w_row[:, j * LANES:(j + 1) * LANES]
    return jnp.sum(acc, axis=-1, keepdims=True) + rb_ref[pl.ds(e, 1), 0:1]


def _moe_kernel(src_ref, ea_ref, eb_ref, valid_ref,
                xs_ref, rw_ref, rb_ref, wga_ref, wua_ref, wda_ref, wgb_ref, wub_ref, wdb_ref, o_ref):
    i = pl.program_id(0)

    @pl.when(valid_ref[i] == 1)
    def _():
        chunks = _load_slab_chunks(xs_ref, MOE_TM)
        l_a = _router_logit(chunks, rw_ref, rb_ref, ea_ref[i])
        l_b = _router_logit(chunks, rw_ref, rb_ref, eb_ref[i])
        w_a = jax.nn.sigmoid(l_a - l_b)
        w_b = jax.nn.sigmoid(l_b - l_a)
        x = jnp.concatenate([c.astype(BF16) for c in chunks], axis=-1)
        y = w_a * _expert(x, wga_ref, wua_ref, wda_ref) + w_b * _expert(x, wgb_ref, wub_ref, wdb_ref)
        _store_slabs(o_ref, y)

    @pl.when(valid_ref[i] == 0)
    def _():
        o_ref[...] = jnp.zeros_like(o_ref)


def _grouped_experts(xs, plan, router_wT, router_bt, w_gate, w_up, w_down):
    n_tiles = xs.shape[0] // (MOE_TM * ROW_SLAB)
    D, Fd = D_MODEL, D_EXPERT
    wspec_a = lambda shape: pl.BlockSpec(shape, lambda i, src, ea, eb, va: (ea[i], 0, 0))
    wspec_b = lambda shape: pl.BlockSpec(shape, lambda i, src, ea, eb, va: (eb[i], 0, 0))
    row_spec = pl.BlockSpec((MOE_TM * ROW_SLAB, LANES), lambda i, src, ea, eb, va: (src[i], 0))
    return pl.pallas_call(
        _moe_kernel,
        grid_spec=pltpu.PrefetchScalarGridSpec(
            num_scalar_prefetch=4,
            grid=(n_tiles,),
            in_specs=[
                row_spec,
                pl.BlockSpec((N_EXPERTS, D), lambda i, src, ea, eb, va: (0, 0)),
                pl.BlockSpec((N_EXPERTS, LANES), lambda i, src, ea, eb, va: (0, 0)),
                wspec_a((1, D, Fd)), wspec_a((1, D, Fd)), wspec_a((1, Fd, D)),
                wspec_b((1, D, Fd)), wspec_b((1, D, Fd)), wspec_b((1, Fd, D)),
            ],
            out_specs=pl.BlockSpec((MOE_TM * ROW_SLAB, LANES), lambda i, src, ea, eb, va: (i, 0)),
        ),
        out_shape=jax.ShapeDtypeStruct(xs.shape, F32),
        compiler_params=_cparams(("arbitrary",)),
        name="grouped_experts",
    )(plan["tile_src"], plan["tile_ea"], plan["tile_eb"], plan["tile_valid"],
      xs, router_wT, router_bt, w_gate, w_up, w_down, w_gate, w_up, w_down)


def _moe_plan(bucket, rank, counts, n_tiles):
    cnt = counts[:N_BUCKETS, 0]
    tiles_b = (cnt + MOE_TM - 1) // MOE_TM
    tile_end = jnp.cumsum(tiles_b)
    tile_start = tile_end - tiles_b
    row_start = tile_start * MOE_TM
    dest = jnp.take(row_start, bucket[0]) + rank[0]
    n_used = tile_end[-1]
    t = jnp.arange(n_tiles, dtype=jnp.int32)
    t_c = jnp.minimum(t, n_used - 1)
    tb = jnp.sum((t_c[:, None] >= tile_end[None, :]).astype(jnp.int32), axis=1)
    pair_lo = jnp.array([0, 0, 0, 1, 1, 2], jnp.int32)
    pair_hi = jnp.array([1, 2, 3, 2, 3, 3], jnp.int32)
    grp = tb // PAIRS_PER_GROUP
    pr = tb % PAIRS_PER_GROUP
    return {
        "dest": dest.astype(jnp.int32),
        "pad_lo": (row_start + cnt).astype(jnp.int32),
        "pad_hi": (tile_end * MOE_TM).astype(jnp.int32),
        "tile_src": t_c.astype(jnp.int32),
        "tile_ea": (grp * EXPERTS_PER_GROUP + jnp.take(pair_lo, pr)).astype(jnp.int32),
        "tile_eb": (grp * EXPERTS_PER_GROUP + jnp.take(pair_hi, pr)).astype(jnp.int32),
        "tile_valid": (t < n_used).astype(jnp.int32),
    }


def _resln_kernel(dest_ref, ys_ref, x_ref, gate_ref, g_ref, b_ref, o_ref, ybuf, sems):
    i = pl.program_id(0)
    n = pl.num_programs(0)
    tm = x_ref.shape[0]

    def issue(tile, s):
        def body(c, carry):
            for u in range(DMA_CHUNK):
                r = c * DMA_CHUNK + u
                pltpu.make_async_copy(_slab(ys_ref, dest_ref[tile * tm + r]), _slab(ybuf.at[s], r), sems.at[s]).start()
            return carry
        lax.fori_loop(0, tm // DMA_CHUNK, body, 0)

    @pl.when(i == 0)
    def _():
        issue(0, 0)

    for s in range(2):
        @pl.when(lax.rem(i, 2) == s)
        def _(s=s):
            @pl.when(i + 1 < n)
            def _():
                issue(i + 1, 1 - s)

            _wait_rows(ys_ref, ybuf.at[s], sems.at[s], tm)
            y = jnp.concatenate(_load_slab_chunks(ybuf.at[s], tm), axis=-1)
            u = DEEPNORM_ALPHA * x_ref[...] + (1.0 + gate_ref[0]) * y
            o_ref[...] = _layer_norm_rows(u, g_ref[...], b_ref[...])


def _gather_residual_ln(ys, plan, x2, mods, k, B, S, g, b):
    T, D = x2.shape
    tm = PROJ_TM
    per_b = S // tm
    return pl.pallas_call(
        _resln_kernel,
        grid_spec=pltpu.PrefetchScalarGridSpec(
            num_scalar_prefetch=1,
            grid=(T // tm,),
            in_specs=[
                pl.BlockSpec(memory_space=pl.ANY),
                pl.BlockSpec((tm, D), lambda i, *_: (i, 0)),
                pl.BlockSpec((1, 1, D), lambda i, *_: (_mod_row(k, i // per_b, GATE, B), 0, 0)),
                pl.BlockSpec((1, D), lambda i, *_: (0, 0)),
                pl.BlockSpec((1, D), lambda i, *_: (0, 0)),
            ],
            out_specs=pl.BlockSpec((tm, D), lambda i, *_: (i, 0)),
            scratch_shapes=[pltpu.VMEM((2, tm * ROW_SLAB, LANES), F32), pltpu.SemaphoreType.DMA((2,))],
        ),
        out_shape=jax.ShapeDtypeStruct((T, D), F32),
        compiler_params=_cparams(("arbitrary",)),
        name="gather_residual_ln",
    )(plan["dest"], ys, x2, mods, g.reshape(1, D), b.reshape(1, D))


def _moe_sublayer(x, mods, k, router_wT, router_bt, lt, w_gate, w_up, w_down, g, b):
    B, S, D = x.shape
    T = B * S
    x2 = x.reshape(T, D)
    bucket, rank, counts = _router(x2, mods, k, B, S, router_wT, router_bt, lt)
    n_tiles = T // MOE_TM + N_BUCKETS
    plan = _moe_plan(bucket, rank, counts, n_tiles)
    xs = _scatter_rows(x2, mods, k, B, S, plan, n_tiles * MOE_TM)
    ys = _grouped_experts(xs, plan, router_wT, router_bt, w_gate, w_up, w_down)
    return _gather_residual_ln(ys, plan, x2, mods, k, B, S, g, b).reshape(B, S, D)


def kernel(x, c, positions, ada_w, ada_b, ln_g, ln_b, sb_w_in, sb_w_out, ret_w_in, ret_gn_g, ret_w_out,
           router_w, router_b, moe_w_gate, moe_w_up, moe_w_down):
    mods = _ada_params(c, ada_w, ada_b)
    cc, ss = _rope_tables(positions)
    tables = _retention_decay_tables()
    u = (jnp.arange(SB_TK)[:, None] < jnp.arange(SB_TK)[None, :]).astype(BF16)
    lt = (jnp.arange(ROUTE_TM)[:, None] < jnp.arange(ROUTE_TM)[None, :]).astype(BF16)
    router_wT = router_w.T
    router_bt = jnp.broadcast_to(router_b[:, None], (N_EXPERTS, LANES))
    for i in range(DEPTH):
        k_mix, k_moe = 2 * i, 2 * i + 1
        if i % 2 == 0:
            qkv = _mod_matmul(x, mods, k_mix, sb_w_in[i // 2].astype(BF16), tn=3 * D_MODEL // 2)
            a = _sb_attention(qkv, u)
            w_out = sb_w_out[i // 2]
        else:
            proj = _mod_matmul(x, mods, k_mix, ret_w_in[i // 2].astype(BF16), tn=2 * D_MODEL)
            a = _retention(proj, cc, ss, tables, ret_gn_g[i // 2])
            w_out = ret_w_out[i // 2]
        x = _out_proj_ln(a, w_out.astype(BF16), x, mods, k_mix, ln_g[i, 0], ln_b[i, 0])
        x = _moe_sublayer(x, mods, k_moe, router_wT, router_bt, lt,
                          moe_w_gate[i].astype(BF16), moe_w_up[i].astype(BF16), moe_w_down[i].astype(BF16),
                          ln_g[i, 1], ln_b[i, 1])
    return x
```

```python
import functools

import jax
import jax.numpy as jnp
from jax import lax
from jax.experimental import pallas as pl
from jax.experimental.pallas import tpu as pltpu

F32 = jnp.float32
BF16 = jnp.bfloat16

D_MODEL = 1024
DEPTH = 4
SB_HEADS = 16
SB_HEAD_DIM = D_MODEL // SB_HEADS
RET_HEADS = 8
RET_QK_DIM = D_MODEL // RET_HEADS
RET_V_DIM = 2 * RET_QK_DIM
ROPE_BASE = 10000.0
N_EXPERTS = 16
N_GROUPS = 4
EXPERTS_PER_GROUP = N_EXPERTS // N_GROUPS
D_EXPERT = D_MODEL // 2
DEEPNORM_ALPHA = (2 * DEPTH) ** 0.25
LN_EPS = 1e-5
LOG2_E = 1.4426950408889634

LANES = 128
SUBLANES = 8
MXU_DIM = 256
VMEM_LIMIT_BYTES = 56 * 1024 * 1024

PROJ_TM = 512
SB_TQ = MXU_DIM
SB_TK = MXU_DIM
RET_CHUNK = 256
ROUTE_TM = 512
MOE_TM = 256
PAIRS_PER_GROUP = 6
N_BUCKETS = N_GROUPS * PAIRS_PER_GROUP
BUCKET_ROWS = 32
ROW_SLAB = D_MODEL // LANES
DMA_CHUNK = 16
assert ROW_SLAB == SUBLANES


def _cparams(semantics):
    return pltpu.CompilerParams(dimension_semantics=semantics, vmem_limit_bytes=VMEM_LIMIT_BYTES)


def _split_bf16(a):
    hi = a.astype(BF16)
    lo = (a - hi.astype(F32)).astype(BF16)
    return hi, lo


def _ada_kernel(c_ref, w_ref, b_ref, o_ref):
    c = c_ref[...]
    s = c * jax.nn.sigmoid(c)
    sh, sl = _split_bf16(s)
    wh, wl = _split_bf16(w_ref[0])
    acc = jnp.dot(sh, wh, preferred_element_type=F32)
    acc += jnp.dot(sl, wh, preferred_element_type=F32)
    acc += jnp.dot(sh, wl, preferred_element_type=F32)
    o_ref[0] = acc + b_ref[0]


def _ada_params(c, ada_w, ada_b):
    B, D = c.shape
    n_sub = ada_w.shape[0] * ada_w.shape[1]
    N = ada_w.shape[-1]
    tn = N // 2
    w = ada_w.reshape(n_sub, D, N)
    b = ada_b.reshape(n_sub, 1, N)
    out = pl.pallas_call(
        _ada_kernel,
        grid=(n_sub, N // tn),
        in_specs=[
            pl.BlockSpec((B, D), lambda k, j: (0, 0)),
            pl.BlockSpec((1, D, tn), lambda k, j: (k, 0, j)),
            pl.BlockSpec((1, 1, tn), lambda k, j: (k, 0, j)),
        ],
        out_specs=pl.BlockSpec((1, B, tn), lambda k, j: (k, 0, j)),
        out_shape=jax.ShapeDtypeStruct((n_sub, B, N), F32),
        compiler_params=_cparams(("arbitrary", "arbitrary")),
        name="ada_params",
    )(c, w, b)
    return out.reshape(n_sub * B * 3, 1, D)


def _mod_row(k, b, which, B):
    return (k * B + b) * 3 + which


SHIFT, SCALE, GATE = 0, 1, 2


def _modmm_kernel(x_ref, sc_ref, sh_ref, w_ref, o_ref):
    h = x_ref[0] * (1.0 + sc_ref[0]) + sh_ref[0]
    o_ref[0] = jnp.dot(h.astype(BF16), w_ref[...], preferred_element_type=F32).astype(o_ref.dtype)


def _mod_matmul(x, mods, k, w, tn):
    B, S, D = x.shape
    N = w.shape[1]
    tm = PROJ_TM
    return pl.pallas_call(
        _modmm_kernel,
        grid=(N // tn, B, S // tm),
        in_specs=[
            pl.BlockSpec((1, tm, D), lambda n, b, i: (b, i, 0)),
            pl.BlockSpec((1, 1, D), lambda n, b, i: (_mod_row(k, b, SCALE, B), 0, 0)),
            pl.BlockSpec((1, 1, D), lambda n, b, i: (_mod_row(k, b, SHIFT, B), 0, 0)),
            pl.BlockSpec((D, tn), lambda n, b, i: (0, n)),
        ],
        out_specs=pl.BlockSpec((1, tm, tn), lambda n, b, i: (b, i, n)),
        out_shape=jax.ShapeDtypeStruct((B, S, N), BF16),
        compiler_params=_cparams(("arbitrary", "arbitrary", "arbitrary")),
        name="mod_matmul",
    )(x, mods, mods, w)


SB_PIPE = 4
SB_UNROLL = 4


def _sb_tile_tables(nq):
    diag = [(i, i) for i in range(nq)]
    off = [(i, j) for j in range(nq - 2, -1, -1) for i in range(j + 1, nq)]
    ti, tj, spans = [], [], []
    for tiles in (diag, off):
        n_iter = len(tiles) + SB_PIPE
        n_iter += -n_iter % SB_UNROLL
        padded = [(nq, 0)] * SB_PIPE + tiles + [(nq, 0)] * (n_iter - len(tiles))
        spans.append((len(ti), n_iter))
        ti += [t[0] for t in padded]
        tj += [t[1] for t in padded]
    return ti, tj, spans


def _sb_kernel(ti_ref, tj_ref, q_ref, k_ref, v_ref, u_ref, o_ref,
               vT_ref, qT_ref, z_ref, sp_ref, d_ref, a_ref, c_ref, run_ref, acc_ref, *, spans):
    S = q_ref.shape[1]
    tq, tk, dh = SB_TQ, SB_TK, SB_HEAD_DIM
    nq = S // tq

    for c in range(S // tk):
        v_c = v_ref[0, c * tk:(c + 1) * tk, :].astype(F32)
        vT_ref[c] = v_c.T.astype(BF16)

    scale = dh ** -0.5 * LOG2_E
    head_rows = lax.broadcasted_iota(jnp.int32, (2 * dh, tq), 0)
    for i in range(nq):
        qT = (q_ref[0, i * tq:(i + 1) * tq, :].astype(F32) * scale).T
        qT_ref[i, 0] = jnp.where(head_rows < dh, qT, 0.0).astype(BF16)
        qT_ref[i, 1] = jnp.where(head_rows >= dh, qT, 0.0).astype(BF16)
    qT_ref[nq] = jnp.zeros_like(qT_ref[nq])
    z_ref[...] = jnp.zeros_like(z_ref)
    sp_ref[...] = jnp.zeros_like(sp_ref)
    d_ref[...] = jnp.zeros_like(d_ref)
    a_ref[...] = jnp.zeros_like(a_ref)
    c_ref[...] = jnp.zeros_like(c_ref)

    def iteration(base, it, p, masked):
        q = 1 - p
        t_scores = base + it + SB_PIPE
        i1, j1 = ti_ref[t_scores], tj_ref[t_scores]
        i3 = ti_ref[t_scores - 3]
        i4, j4 = ti_ref[t_scores - 4], tj_ref[t_scores - 4]
        if masked:
            rows = lax.broadcasted_iota(jnp.int32, (tk, tq), 0)
            cols = lax.broadcasted_iota(jnp.int32, (tk, tq), 1)
            valid = rows < cols
        u = u_ref[...]
        suf = [jnp.dot(u, sp_ref[q, h], preferred_element_type=F32) for h in range(2)]
        vT = vT_ref[j4]
        pv = [jnp.dot(vT[h * dh:(h + 1) * dh, :], a_ref[p, h], preferred_element_type=F32) for h in range(2)]
        k_blk = k_ref[0, pl.ds(pl.multiple_of(j1 * tk, tk), tk), :]
        z_new = [jnp.dot(k_blk, qT_ref[i1, h], preferred_element_type=F32) for h in range(2)]
        for h in range(2):
            z = z_ref[p, h]
            sp = jnp.maximum(z, 0.0) + jnp.log2(1.0 + jnp.exp2(-jnp.abs(z)))
            d_ref[p, h] = z - sp
            if masked:
                sp = jnp.where(valid, sp, 0.0)
            sp_ref[p, h] = sp.astype(BF16)
        for h in range(2):
            a = jnp.exp2(d_ref[q, h] - suf[h])
            if masked:
                a = jnp.where(valid, a, 0.0)
            a_ref[q, h] = a.astype(BF16)
            tile_mass = suf[h][0:1, :] + sp_ref[q, h, 0:1, :].astype(F32)
            if masked:
                run_ref[i3, h, 0:1, :] = tile_mass
            else:
                run = run_ref[i3, h, 0:1, :]
                c_ref[q, h, 0:1, :] = jnp.exp2(-run)
                run_ref[i3, h, 0:1, :] = run + tile_mass
        for h in range(2):
            if masked:
                acc_ref[i4, h] = pv[h]
            else:
                acc_ref[i4, h] += pv[h] * c_ref[p, h, 0:1, :]
            z_ref[p, h] = z_new[h]

    for (base, n_iter), masked in zip(spans, (True, False)):
        def body(t, carry, base=base, masked=masked):
            for r in range(SB_UNROLL):
                iteration(base, SB_UNROLL * t + r, r % 2, masked)
            return carry
        lax.fori_loop(0, n_iter // SB_UNROLL, body, 0)

    for i in range(nq):
        oT = jnp.concatenate([acc_ref[i, 0], acc_ref[i, 1]], axis=0)
        o_ref[0, i * tq:(i + 1) * tq, :] = oT.T.astype(o_ref.dtype)


def _sb_attention(qkv, u):
    B, S, _ = qkv.shape
    n_pairs = SB_HEADS // 2
    nq = S // SB_TQ
    blk = (1, S, 2 * SB_HEAD_DIM)
    ti, tj, spans = _sb_tile_tables(nq)
    tile = (SB_TK, SB_TQ)
    return pl.pallas_call(
        functools.partial(_sb_kernel, spans=spans),
        grid_spec=pltpu.PrefetchScalarGridSpec(
            num_scalar_prefetch=2,
            grid=(B, n_pairs),
            in_specs=[
                pl.BlockSpec(blk, lambda b, p, *_: (b, 0, p)),
                pl.BlockSpec(blk, lambda b, p, *_: (b, 0, n_pairs + p)),
                pl.BlockSpec(blk, lambda b, p, *_: (b, 0, 2 * n_pairs + p)),
                pl.BlockSpec((SB_TK, SB_TK), lambda b, p, *_: (0, 0)),
            ],
            out_specs=pl.BlockSpec(blk, lambda b, p, *_: (b, 0, p)),
            scratch_shapes=[
                pltpu.VMEM((S // SB_TK, 2 * SB_HEAD_DIM, SB_TK), BF16),
                pltpu.VMEM((nq + 1, 2, 2 * SB_HEAD_DIM, SB_TQ), BF16),
                pltpu.VMEM((2, 2) + tile, F32),
                pltpu.VMEM((2, 2) + tile, BF16),
                pltpu.VMEM((2, 2) + tile, F32),
                pltpu.VMEM((2, 2) + tile, BF16),
                pltpu.VMEM((2, 2, SUBLANES, SB_TQ), F32),
                pltpu.VMEM((nq + 1, 2, SUBLANES, SB_TQ), F32),
                pltpu.VMEM((nq + 1, 2, SB_HEAD_DIM, SB_TQ), F32),
            ],
        ),
        out_shape=jax.ShapeDtypeStruct((B, S, D_MODEL), BF16),
        compiler_params=_cparams(("arbitrary", "arbitrary")),
        name="sb_attention",
    )(jnp.asarray(ti, jnp.int32), jnp.asarray(tj, jnp.int32), qkv, qkv, qkv, u)


def _layer_norm_rows(u, g, b):
    mu = jnp.mean(u, axis=-1, keepdims=True)
    d = u - mu
    var = jnp.mean(d * d, axis=-1, keepdims=True)
    return d * lax.rsqrt(var + LN_EPS) * g + b


def _outln_kernel(a_ref, w_ref, x_ref, gate_ref, g_ref, b_ref, o_ref):
    y = jnp.dot(a_ref[0], w_ref[...], preferred_element_type=F32)
    u = DEEPNORM_ALPHA * x_ref[0] + (1.0 + gate_ref[0]) * y
    o_ref[0] = _layer_norm_rows(u, g_ref[...], b_ref[...])


def _out_proj_ln(a, w, x, mods, k, g, b):
    B, S, D = x.shape
    K = a.shape[-1]
    tm = PROJ_TM
    return pl.pallas_call(
        _outln_kernel,
        grid=(B, S // tm),
        in_specs=[
            pl.BlockSpec((1, tm, K), lambda bb, i: (bb, i, 0)),
            pl.BlockSpec((K, D), lambda bb, i: (0, 0)),
            pl.BlockSpec((1, tm, D), lambda bb, i: (bb, i, 0)),
            pl.BlockSpec((1, 1, D), lambda bb, i: (_mod_row(k, bb, GATE, B), 0, 0)),
            pl.BlockSpec((1, D), lambda bb, i: (0, 0)),
            pl.BlockSpec((1, D), lambda bb, i: (0, 0)),
        ],
        out_specs=pl.BlockSpec((1, tm, D), lambda bb, i: (bb, i, 0)),
        out_shape=jax.ShapeDtypeStruct((B, S, D), F32),
        compiler_params=_cparams(("arbitrary", "arbitrary")),
        name="out_proj_ln",
    )(a, w, x, mods, g.reshape(1, D), b.reshape(1, D))


def _rope_kernel(pos_ref, invf_ref, sign_ref, cc_ref, ss_ref):
    ang = pos_ref[0].astype(F32) * invf_ref[...]
    cc_ref[0] = jnp.cos(ang)
    ss_ref[0] = jnp.sin(ang) * sign_ref[...]


def _rope_tables(positions):
    B, S = positions.shape
    half = RET_QK_DIM // 2
    inv_freq = 1.0 / (ROPE_BASE ** (jnp.arange(0, RET_QK_DIM, 2, dtype=F32) / RET_QK_DIM))
    invf2 = jnp.concatenate([inv_freq, inv_freq]).reshape(1, RET_QK_DIM)
    sign = jnp.concatenate([-jnp.ones((half,), F32), jnp.ones((half,), F32)]).reshape(1, RET_QK_DIM)
    ts = PROJ_TM
    shp = jax.ShapeDtypeStruct((B, S, RET_QK_DIM), F32)
    return pl.pallas_call(
        _rope_kernel,
        grid=(B, S // ts),
        in_specs=[
            pl.BlockSpec((1, ts, 1), lambda b, i: (b, i, 0)),
            pl.BlockSpec((1, RET_QK_DIM), lambda b, i: (0, 0)),
            pl.BlockSpec((1, RET_QK_DIM), lambda b, i: (0, 0)),
        ],
        out_specs=[pl.BlockSpec((1, ts, RET_QK_DIM), lambda b, i: (b, i, 0))] * 2,
        out_shape=[shp, shp],
        compiler_params=_cparams(("arbitrary", "arbitrary")),
        name="rope_tables",
    )(positions.reshape(B, S, 1), invf2, sign)


def _retention_decay_tables():
    C = RET_CHUNK
    kscale = RET_QK_DIM ** -0.5
    log_gamma = jnp.log(1.0 - jnp.exp2(-5.0 - jnp.arange(RET_HEADS, dtype=F32)))
    n = jnp.arange(C, dtype=F32)
    diff = n[:, None] - n[None, :]
    dmask = jnp.where(diff >= 0, jnp.exp(jnp.maximum(diff, 0.0) * log_gamma[:, None, None]), 0.0)
    xi = jnp.exp((n + 1.0) * log_gamma[:, None])
    zeta = jnp.exp((C - 1.0 - n) * log_gamma[:, None])
    chunk_decay = jnp.exp(C * log_gamma)
    dm = dmask * kscale
    xi_t = jnp.broadcast_to(xi[:, :, None], (RET_HEADS, C, RET_V_DIM))
    zeta_t = jnp.broadcast_to(zeta[:, :, None] * kscale, (RET_HEADS, C, RET_QK_DIM))
    decay_t = jnp.broadcast_to(chunk_decay[:, None, None], (RET_HEADS, RET_QK_DIM, RET_V_DIM))
    return dm, xi_t, zeta_t, decay_t


def _ret_kernel(q_ref, k_ref, v_ref, g_ref, cc_ref, ss_ref, dm_ref, xi_ref, zeta_ref, decay_ref,
                gn_ref, o_ref, state_ref):
    S = q_ref.shape[1]
    C = RET_CHUNK
    half = RET_QK_DIM // 2
    state_ref[...] = jnp.zeros_like(state_ref)

    def prep(c):
        r0 = pl.multiple_of(c * C, C)
        rows = pl.ds(r0, C)
        cc = cc_ref[0, rows, :]
        ss = ss_ref[0, rows, :]
        q = q_ref[0, rows, :].astype(F32)
        k = k_ref[0, rows, :].astype(F32)
        qr = q * cc + pltpu.roll(q, half, 1) * ss
        kr = k * cc + pltpu.roll(k, half, 1) * ss
        kzT = (kr * zeta_ref[0]).T.astype(BF16)
        return rows, qr.astype(BF16), kr.astype(BF16), v_ref[0, rows, :], kzT

    def finish(rows, inner, cross):
        o = inner + cross * xi_ref[0]
        mu = jnp.mean(o, axis=-1, keepdims=True)
        d = o - mu
        var = jnp.mean(d * d, axis=-1, keepdims=True)
        on = d * lax.rsqrt(var + LN_EPS) * gn_ref[...]
        g = g_ref[0, rows, :].astype(F32)
        o_ref[0, rows, :] = (g * jax.nn.sigmoid(g) * on).astype(o_ref.dtype)

    def chunk_pair(t, carry):
        nt = (((1,), (1,)), ((), ()))
        pa = prep(2 * t)
        pb = prep(2 * t + 1)
        qk_a = lax.dot_general(pa[1], pa[2], nt, preferred_element_type=F32)
        qk_b = lax.dot_general(pb[1], pb[2], nt, preferred_element_type=F32)
        ds_a = jnp.dot(pa[4], pa[3], preferred_element_type=F32)
        ds_b = jnp.dot(pb[4], pb[3], preferred_element_type=F32)
        state0 = state_ref[...]
        cross_a = jnp.dot(pa[1], state0.astype(BF16), preferred_element_type=F32)
        state1 = state0 * decay_ref[0] + ds_a
        cross_b = jnp.dot(pb[1], state1.astype(BF16), preferred_element_type=F32)
        state_ref[...] = state1 * decay_ref[0] + ds_b
        inner_a = jnp.dot((qk_a * dm_ref[0]).astype(BF16), pa[3], preferred_element_type=F32)
        inner_b = jnp.dot((qk_b * dm_ref[0]).astype(BF16), pb[3], preferred_element_type=F32)
        finish(pa[0], inner_a, cross_a)
        finish(pb[0], inner_b, cross_b)
        return carry

    lax.fori_loop(0, S // (2 * C), chunk_pair, 0)


def _retention(proj, cc, ss, tables, gn_g):
    B, S, _ = proj.shape
    dk, dv, H, C = RET_QK_DIM, RET_V_DIM, RET_HEADS, RET_CHUNK
    dm, xi_t, zeta_t, decay_t = tables
    return pl.pallas_call(
        _ret_kernel,
        grid=(B, H),
        in_specs=[
            pl.BlockSpec((1, S, dk), lambda b, h: (b, 0, h)),
            pl.BlockSpec((1, S, dk), lambda b, h: (b, 0, H + h)),
            pl.BlockSpec((1, S, dv), lambda b, h: (b, 0, H + h)),
            pl.BlockSpec((1, S, dv), lambda b, h: (b, 0, 2 * H + h)),
            pl.BlockSpec((1, S, dk), lambda b, h: (b, 0, 0)),
            pl.BlockSpec((1, S, dk), lambda b, h: (b, 0, 0)),
            pl.BlockSpec((1, C, C), lambda b, h: (h, 0, 0)),
            pl.BlockSpec((1, C, dv), lambda b, h: (h, 0, 0)),
            pl.BlockSpec((1, C, dk), lambda b, h: (h, 0, 0)),
            pl.BlockSpec((1, dk, dv), lambda b, h: (h, 0, 0)),
            pl.BlockSpec((1, dv), lambda b, h: (0, h)),
        ],
        out_specs=pl.BlockSpec((1, S, dv), lambda b, h: (b, 0, h)),
        out_shape=jax.ShapeDtypeStruct((B, S, H * dv), BF16),
        scratch_shapes=[pltpu.VMEM((dk, dv), F32)],
        compiler_params=_cparams(("arbitrary", "arbitrary")),
        name="retention",
    )(proj, proj, proj, proj, cc, ss, dm, xi_t, zeta_t, decay_t, gn_g.reshape(1, H * dv))


def _store_slabs(ref, val):
    tm = val.shape[0]
    for j in range(ROW_SLAB):
        ref[pl.ds(j, tm, stride=ROW_SLAB), :] = val[:, j * LANES:(j + 1) * LANES]


def _load_slab_chunks(ref, tm):
    return [ref[pl.ds(j, tm, stride=ROW_SLAB), :] for j in range(ROW_SLAB)]


def _top2_flags(vals):
    n = len(vals)
    rank = []
    for k in range(n):
        r = jnp.zeros_like(vals[0])
        for j in range(n):
            if j < k:
                r = r + (vals[j] >= vals[k]).astype(F32)
            elif j > k:
                r = r + (vals[j] > vals[k]).astype(F32)
        rank.append(r)
    is1 = [r == 0.0 for r in rank]
    is2 = [r == 1.0 for r in rank]
    m1 = sum(jnp.where(f, v, 0.0) for f, v in zip(is1, vals))
    m2 = sum(jnp.where(f, v, 0.0) for f, v in zip(is2, vals))
    return is1, is2, m1, m2


def _router_kernel(x_ref, sc_ref, sh_ref, rw_ref, rb_ref, lt_ref,
                   bucket_ref, rank_ref, counts_ref, carry_ref):
    i = pl.program_id(0)
    tm = x_ref.shape[0]

    @pl.when(i == 0)
    def _():
        carry_ref[...] = jnp.zeros_like(carry_ref)

    h = x_ref[...] * (1.0 + sc_ref[0]) + sh_ref[0]
    hh, hl = _split_bf16(h)
    wh, wl = _split_bf16(rw_ref[...])
    nt = (((1,), (1,)), ((), ()))
    logits = lax.dot_general(wh, hh, nt, preferred_element_type=F32)
    logits += lax.dot_general(wl, hh, nt, preferred_element_type=F32)
    logits += lax.dot_general(wh, hl, nt, preferred_element_type=F32)
    logits = logits + rb_ref[:, 0:1]

    rows = [logits[e:e + 1, :] for e in range(N_EXPERTS)]
    gmax = functools.reduce(jnp.maximum, rows)
    groups = []
    for g in range(N_GROUPS):
        vals = rows[g * EXPERTS_PER_GROUP:(g + 1) * EXPERTS_PER_GROUP]
        is1, is2, m1, m2 = _top2_flags(vals)
        score = jnp.exp(m1 - gmax) + jnp.exp(m2 - gmax)
        groups.append(([jnp.logical_or(a, b) for a, b in zip(is1, is2)], score))
    best = groups[0][1]
    gsel = jnp.zeros_like(best)
    for g in range(1, N_GROUPS):
        better = groups[g][1] > best
        best = jnp.where(better, groups[g][1], best)
        gsel = jnp.where(better, float(g), gsel)
    sel = [jnp.zeros_like(best) > 1.0] * EXPERTS_PER_GROUP
    for g in range(N_GROUPS):
        here = gsel == float(g)
        sel = [jnp.logical_or(s, jnp.logical_and(here, f)) for s, f in zip(sel, groups[g][0])]
    pair = jnp.where(sel[0],
                     jnp.where(sel[1], 0.0, jnp.where(sel[2], 1.0, 2.0)),
                     jnp.where(sel[1], jnp.where(sel[2], 3.0, 4.0), 5.0))
    bucket = (gsel * float(PAIRS_PER_GROUP) + pair).astype(jnp.int32)

    brow = lax.broadcasted_iota(jnp.int32, (BUCKET_ROWS, tm), 0)
    onehot = (brow == bucket).astype(F32)
    before = jnp.dot(onehot.astype(BF16), lt_ref[...], preferred_element_type=F32)
    carry = carry_ref[...]
    in_tile = jnp.sum(onehot * before, axis=0, keepdims=True)
    base = jnp.sum(onehot * carry[:, 0:1], axis=0, keepdims=True)
    rank_ref[...] = (base + in_tile).astype(jnp.int32)
    bucket_ref[...] = bucket
    new_carry = carry + jnp.sum(onehot, axis=1, keepdims=True)
    carry_ref[...] = new_carry
    counts_ref[...] = new_carry.astype(jnp.int32)


def _router(x2, mods, k, B, S, router_wT, router_bt, lt):
    T, D = x2.shape
    tm = ROUTE_TM
    per_b = S // tm
    return pl.pallas_call(
        _router_kernel,
        grid=(T // tm,),
        in_specs=[
            pl.BlockSpec((tm, D), lambda i: (i, 0)),
            pl.BlockSpec((1, 1, D), lambda i: (_mod_row(k, i // per_b, SCALE, B), 0, 0)),
            pl.BlockSpec((1, 1, D), lambda i: (_mod_row(k, i // per_b, SHIFT, B), 0, 0)),
            pl.BlockSpec((N_EXPERTS, D), lambda i: (0, 0)),
            pl.BlockSpec((N_EXPERTS, LANES), lambda i: (0, 0)),
            pl.BlockSpec((tm, tm), lambda i: (0, 0)),
        ],
        out_specs=[
            pl.BlockSpec((1, tm), lambda i: (0, i)),
            pl.BlockSpec((1, tm), lambda i: (0, i)),
            pl.BlockSpec((BUCKET_ROWS, LANES), lambda i: (0, 0)),
        ],
        out_shape=[
            jax.ShapeDtypeStruct((1, T), jnp.int32),
            jax.ShapeDtypeStruct((1, T), jnp.int32),
            jax.ShapeDtypeStruct((BUCKET_ROWS, LANES), jnp.int32),
        ],
        scratch_shapes=[pltpu.VMEM((BUCKET_ROWS, LANES), F32)],
        compiler_params=_cparams(("arbitrary",)),
        name="router",
    )(x2, mods, mods, router_wT, router_bt, lt)


def _slab(ref, r):
    return ref.at[pl.ds(pl.multiple_of(r * ROW_SLAB, ROW_SLAB), ROW_SLAB)]


def _wait_rows(src_ref, dst_ref, sem, n):
    def body(c, carry):
        for _ in range(DMA_CHUNK):
            pltpu.make_async_copy(_slab(src_ref, 0), _slab(dst_ref, 0), sem).wait()
        return carry
    lax.fori_loop(0, n // DMA_CHUNK, body, 0)


def _scatter_kernel(dest_ref, pad_lo_ref, pad_hi_ref, x_ref, sc_ref, sh_ref, xs_ref, hbuf, sems):
    i = pl.program_id(0)
    last = pl.num_programs(0) - 1
    tm = x_ref.shape[0]
    h = x_ref[...] * (1.0 + sc_ref[0]) + sh_ref[0]

    for s in range(2):
        @pl.when(lax.rem(i, 2) == s)
        def _(s=s):
            buf = hbuf.at[s]
            _store_slabs(buf, h)

            def issue(c, carry):
                for u in range(DMA_CHUNK):
                    r = c * DMA_CHUNK + u
                    pltpu.make_async_copy(_slab(buf, r), _slab(xs_ref, dest_ref[i * tm + r]), sems.at[s]).start(priority=u % 2)
                return carry
            lax.fori_loop(0, tm // DMA_CHUNK, issue, 0)

            @pl.when(i > 0)
            def _():
                _wait_rows(hbuf.at[1 - s], xs_ref, sems.at[1 - s], tm)

            @pl.when(i == last)
            def _():
                _wait_rows(buf, xs_ref, sems.at[s], tm)

                def filler(row0, n_rows, start):
                    cp = pltpu.make_async_copy(
                        buf.at[pl.ds(0, n_rows * ROW_SLAB)],
                        xs_ref.at[pl.ds(pl.multiple_of(row0 * ROW_SLAB, ROW_SLAB), n_rows * ROW_SLAB)],
                        sems.at[s])
                    if start:
                        cp.start()
                    else:
                        cp.wait()

                def fill(start):
                    def bucket(b, carry):
                        pos = pad_lo_ref[b]
                        n = pad_hi_ref[b] - pos
                        for bit in reversed(range(MOE_TM.bit_length() - 1)):
                            take = lax.rem(lax.shift_right_logical(n, bit), 2)

                            @pl.when(take == 1)
                            def _(pos=pos, bit=bit):
                                filler(pos, 1 << bit, start)
                            pos = pos + take * (1 << bit)
                        return carry
                    lax.fori_loop(0, N_BUCKETS, bucket, 0)

                    def tail(t, carry):
                        filler(t * MOE_TM, MOE_TM, start)
                        return carry
                    n_tiles = xs_ref.shape[0] // (MOE_TM * ROW_SLAB)
                    lax.fori_loop(pad_hi_ref[N_BUCKETS - 1] // MOE_TM, n_tiles, tail, 0)
                fill(True)
                fill(False)


def _scatter_rows(x2, mods, k, B, S, plan, n_out):
    T, D = x2.shape
    tm = PROJ_TM
    per_b = S // tm
    return pl.pallas_call(
        _scatter_kernel,
        grid_spec=pltpu.PrefetchScalarGridSpec(
            num_scalar_prefetch=3,
            grid=(T // tm,),
            in_specs=[
                pl.BlockSpec((tm, D), lambda i, *_: (i, 0)),
                pl.BlockSpec((1, 1, D), lambda i, *_: (_mod_row(k, i // per_b, SCALE, B), 0, 0)),
                pl.BlockSpec((1, 1, D), lambda i, *_: (_mod_row(k, i // per_b, SHIFT, B), 0, 0)),
            ],
            out_specs=pl.BlockSpec(memory_space=pl.ANY),
            scratch_shapes=[pltpu.VMEM((2, tm * ROW_SLAB, LANES), F32), pltpu.SemaphoreType.DMA((2,))],
        ),
        out_shape=jax.ShapeDtypeStruct((n_out * ROW_SLAB, LANES), F32),
        compiler_params=_cparams(("arbitrary",)),
        name="scatter_rows",
    )(plan["dest"], plan["pad_lo"], plan["pad_hi"], x2, mods, mods)


def _expert(x, wg_ref, wu_ref, wd_ref):
    gte = jnp.dot(x, wg_ref[0], preferred_element_type=F32)
    up = jnp.dot(x, wu_ref[0], preferred_element_type=F32)
    hid = gte * jax.nn.sigmoid(gte) * up
    return jnp.dot(hid.astype(BF16), wd_ref[0], preferred_element_type=F32)


def _router_logit(chunks, rw_ref, rb_ref, e):
    w_row = rw_ref[pl.ds(e, 1), :]
    acc = chunks[0] * w_row[:, 0:LANES]
    for j in range(1, ROW_SLAB):
        acc = acc + chunks[j] * w_row[:, j * LANES:(j + 1) * LANES]
    return jnp.sum(acc, axis=-1, keepdims=True) + rb_ref[pl.ds(e, 1), 0:1]


def _moe_kernel(src_ref, ea_ref, eb_ref, valid_ref,
                xs_ref, rw_ref, rb_ref, wga_ref, wua_ref, wda_ref, wgb_ref, wub_ref, wdb_ref, o_ref):
    i = pl.program_id(0)

    @pl.when(valid_ref[i] == 1)
    def _():
        chunks = _load_slab_chunks(xs_ref, MOE_TM)
        l_a = _router_logit(chunks, rw_ref, rb_ref, ea_ref[i])
        l_b = _router_logit(chunks, rw_ref, rb_ref, eb_ref[i])
        w_a = jax.nn.sigmoid(l_a - l_b)
        w_b = jax.nn.sigmoid(l_b - l_a)
        x = jnp.concatenate([c.astype(BF16) for c in chunks], axis=-1)
        y = w_a * _expert(x, wga_ref, wua_ref, wda_ref) + w_b * _expert(x, wgb_ref, wub_ref, wdb_ref)
        _store_slabs(o_ref, y)

    @pl.when(valid_ref[i] == 0)
    def _():
        o_ref[...] = jnp.zeros_like(o_ref)


def _grouped_experts(xs, plan, router_wT, router_bt, w_gate, w_up, w_down):
    n_tiles = xs.shape[0] // (MOE_TM * ROW_SLAB)
    D, Fd = D_MODEL, D_EXPERT
    wspec_a = lambda shape: pl.BlockSpec(shape, lambda i, src, ea, eb, va: (ea[i], 0, 0))
    wspec_b = lambda shape: pl.BlockSpec(shape, lambda i, src, ea, eb, va: (eb[i], 0, 0))
    row_spec = pl.BlockSpec((MOE_TM * ROW_SLAB, LANES), lambda i, src, ea, eb, va: (src[i], 0))
    return pl.pallas_call(
        _moe_kernel,
        grid_spec=pltpu.PrefetchScalarGridSpec(
            num_scalar_prefetch=4,
            grid=(n_tiles,),
            in_specs=[
                row_spec,
                pl.BlockSpec((N_EXPERTS, D), lambda i, src, ea, eb, va: (0, 0)),
                pl.BlockSpec((N_EXPERTS, LANES), lambda i, src, ea, eb, va: (0, 0)),
                wspec_a((1, D, Fd)), wspec_a((1, D, Fd)), wspec_a((1, Fd, D)),
                wspec_b((1, D, Fd)), wspec_b((1, D, Fd)), wspec_b((1, Fd, D)),
            ],
            out_specs=pl.BlockSpec((MOE_TM * ROW_SLAB, LANES), lambda i, src, ea, eb, va: (i, 0)),
        ),
        out_shape=jax.ShapeDtypeStruct(xs.shape, F32),
        compiler_params=_cparams(("arbitrary",)),
        name="grouped_experts",
    )(plan["tile_src"], plan["tile_ea"], plan["tile_eb"], plan["tile_valid"],
      xs, router_wT, router_bt, w_gate, w_up, w_down, w_gate, w_up, w_down)


def _moe_plan(bucket, rank, counts, n_tiles):
    cnt = counts[:N_BUCKETS, 0]
    tiles_b = (cnt + MOE_TM - 1) // MOE_TM
    tile_end = jnp.cumsum(tiles_b)
    tile_start = tile_end - tiles_b
    row_start = tile_start * MOE_TM
    dest = jnp.take(row_start, bucket[0]) + rank[0]
    n_used = tile_end[-1]
    t = jnp.arange(n_tiles, dtype=jnp.int32)
    t_c = jnp.minimum(t, n_used - 1)
    tb = jnp.sum((t_c[:, None] >= tile_end[None, :]).astype(jnp.int32), axis=1)
    pair_lo = jnp.array([0, 0, 0, 1, 1, 2], jnp.int32)
    pair_hi = jnp.array([1, 2, 3, 2, 3, 3], jnp.int32)
    grp = tb // PAIRS_PER_GROUP
    pr = tb % PAIRS_PER_GROUP
    return {
        "dest": dest.astype(jnp.int32),
        "pad_lo": (row_start + cnt).astype(jnp.int32),
        "pad_hi": (tile_end * MOE_TM).astype(jnp.int32),
        "tile_src": t_c.astype(jnp.int32),
        "tile_ea": (grp * EXPERTS_PER_GROUP + jnp.take(pair_lo, pr)).astype(jnp.int32),
        "tile_eb": (grp * EXPERTS_PER_GROUP + jnp.take(pair_hi, pr)).astype(jnp.int32),
        "tile_valid": (t < n_used).astype(jnp.int32),
    }


def _resln_kernel(dest_ref, ys_ref, x_ref, gate_ref, g_ref, b_ref, o_ref, ybuf, sems):
    i = pl.program_id(0)
    n = pl.num_programs(0)
    tm = x_ref.shape[0]

    def issue(tile, s):
        def body(c, carry):
            for u in range(DMA_CHUNK):
                r = c * DMA_CHUNK + u
                pltpu.make_async_copy(_slab(ys_ref, dest_ref[tile * tm + r]), _slab(ybuf.at[s], r), sems.at[s]).start(priority=u % 2)
            return carry
        lax.fori_loop(0, tm // DMA_CHUNK, body, 0)

    @pl.when(i == 0)
    def _():
        issue(0, 0)

    for s in range(2):
        @pl.when(lax.rem(i, 2) == s)
        def _(s=s):
            @pl.when(i + 1 < n)
            def _():
                issue(i + 1, 1 - s)

            _wait_rows(ys_ref, ybuf.at[s], sems.at[s], tm)
            y = jnp.concatenate(_load_slab_chunks(ybuf.at[s], tm), axis=-1)
            u = DEEPNORM_ALPHA * x_ref[...] + (1.0 + gate_ref[0]) * y
            o_ref[...] = _layer_norm_rows(u, g_ref[...], b_ref[...])


def _gather_residual_ln(ys, plan, x2, mods, k, B, S, g, b):
    T, D = x2.shape
    tm = PROJ_TM
    per_b = S // tm
    return pl.pallas_call(
        _resln_kernel,
        grid_spec=pltpu.PrefetchScalarGridSpec(
            num_scalar_prefetch=1,
            grid=(T // tm,),
            in_specs=[
                pl.BlockSpec(memory_space=pl.ANY),
                pl.BlockSpec((tm, D), lambda i, *_: (i, 0)),
                pl.BlockSpec((1, 1, D), lambda i, *_: (_mod_row(k, i // per_b, GATE, B), 0, 0)),
                pl.BlockSpec((1, D), lambda i, *_: (0, 0)),
                pl.BlockSpec((1, D), lambda i, *_: (0, 0)),
            ],
            out_specs=pl.BlockSpec((tm, D), lambda i, *_: (i, 0)),
            scratch_shapes=[pltpu.VMEM((2, tm * ROW_SLAB, LANES), F32), pltpu.SemaphoreType.DMA((2,))],
        ),
        out_shape=jax.ShapeDtypeStruct((T, D), F32),
        compiler_params=_cparams(("arbitrary",)),
        name="gather_residual_ln",
    )(plan["dest"], ys, x2, mods, g.reshape(1, D), b.reshape(1, D))


def _moe_sublayer(x, mods, k, router_wT, router_bt, lt, w_gate, w_up, w_down, g, b):
    B, S, D = x.shape
    T = B * S
    x2 = x.reshape(T, D)
    bucket, rank, counts = _router(x2, mods, k, B, S, router_wT, router_bt, lt)
    n_tiles = T // MOE_TM + N_BUCKETS
    plan = _moe_plan(bucket, rank, counts, n_tiles)
    xs = _scatter_rows(x2, mods, k, B, S, plan, n_tiles * MOE_TM)
    ys = _grouped_experts(xs, plan, router_wT, router_bt, w_gate, w_up, w_down)
    return _gather_residual_ln(ys, plan, x2, mods, k, B, S, g, b).reshape(B, S, D)


def kernel(x, c, positions, ada_w, ada_b, ln_g, ln_b, sb_w_in, sb_w_out, ret_w_in, ret_gn_g, ret_w_out,
           router_w, router_b, moe_w_gate, moe_w_up, moe_w_down):
    mods = _ada_params(c, ada_w, ada_b)
    cc, ss = _rope_tables(positions)
    tables = _retention_decay_tables()
    u = (jnp.arange(SB_TK)[:, None] < jnp.arange(SB_TK)[None, :]).astype(BF16)
    lt = (jnp.arange(ROUTE_TM)[:, None] < jnp.arange(ROUTE_TM)[None, :]).astype(BF16)
    router_wT = router_w.T
    router_bt = jnp.broadcast_to(router_b[:, None], (N_EXPERTS, LANES))
    for i in range(DEPTH):
        k_mix, k_moe = 2 * i, 2 * i + 1
        if i % 2 == 0:
            qkv = _mod_matmul(x, mods, k_mix, sb_w_in[i // 2].astype(BF16), tn=3 * D_MODEL // 2)
            a = _sb_attention(qkv, u)
            w_out = sb_w_out[i // 2]
        else:
            proj = _mod_matmul(x, mods, k_mix, ret_w_in[i // 2].astype(BF16), tn=2 * D_MODEL)
            a = _retention(proj, cc, ss, tables, ret_gn_g[i // 2])
            w_out = ret_w_out[i // 2]
        x = _out_proj_ln(a, w_out.astype(BF16), x, mods, k_mix, ln_g[i, 0], ln_b[i, 0])
        x = _moe_sublayer(x, mods, k_moe, router_wT, router_bt, lt,
                          moe_w_gate[i].astype(BF16), moe_w_up[i].astype(BF16), moe_w_down[i].astype(BF16),
                          ln_g[i, 1], ln_b[i, 1])
    return x
```

```python
import functools

import jax
import jax.numpy as jnp
from jax import lax
from jax.experimental import pallas as pl
from jax.experimental.pallas import tpu as pltpu

F32 = jnp.float32
BF16 = jnp.bfloat16

D_MODEL = 1024
DEPTH = 4
SB_HEADS = 16
SB_HEAD_DIM = D_MODEL // SB_HEADS
RET_HEADS = 8
RET_QK_DIM = D_MODEL // RET_HEADS
RET_V_DIM = 2 * RET_QK_DIM
ROPE_BASE = 10000.0
N_EXPERTS = 16
N_GROUPS = 4
EXPERTS_PER_GROUP = N_EXPERTS // N_GROUPS
D_EXPERT = D_MODEL // 2
DEEPNORM_ALPHA = (2 * DEPTH) ** 0.25
LN_EPS = 1e-5
LOG2_E = 1.4426950408889634

LANES = 128
SUBLANES = 8
MXU_DIM = 256
VMEM_LIMIT_BYTES = 56 * 1024 * 1024

PROJ_TM = 512
SB_TQ = MXU_DIM
SB_TK = MXU_DIM
RET_CHUNK = 256
ROUTE_TM = 512
MOE_TM = 256
PAIRS_PER_GROUP = 6
N_BUCKETS = N_GROUPS * PAIRS_PER_GROUP
BUCKET_ROWS = 32
ROW_SLAB = D_MODEL // LANES
DMA_CHUNK = 16
assert ROW_SLAB == SUBLANES


def _cparams(semantics):
    return pltpu.CompilerParams(dimension_semantics=semantics, vmem_limit_bytes=VMEM_LIMIT_BYTES)


def _split_bf16(a):
    hi = a.astype(BF16)
    lo = (a - hi.astype(F32)).astype(BF16)
    return hi, lo


def _ada_kernel(c_ref, w_ref, b_ref, o_ref):
    c = c_ref[...]
    s = c * jax.nn.sigmoid(c)
    sh, sl = _split_bf16(s)
    wh, wl = _split_bf16(w_ref[0])
    acc = jnp.dot(sh, wh, preferred_element_type=F32)
    acc += jnp.dot(sl, wh, preferred_element_type=F32)
    acc += jnp.dot(sh, wl, preferred_element_type=F32)
    o_ref[0] = acc + b_ref[0]


def _ada_params(c, ada_w, ada_b):
    B, D = c.shape
    n_sub = ada_w.shape[0] * ada_w.shape[1]
    N = ada_w.shape[-1]
    tn = N // 2
    w = ada_w.reshape(n_sub, D, N)
    b = ada_b.reshape(n_sub, 1, N)
    out = pl.pallas_call(
        _ada_kernel,
        grid=(n_sub, N // tn),
        in_specs=[
            pl.BlockSpec((B, D), lambda k, j: (0, 0)),
            pl.BlockSpec((1, D, tn), lambda k, j: (k, 0, j)),
            pl.BlockSpec((1, 1, tn), lambda k, j: (k, 0, j)),
        ],
        out_specs=pl.BlockSpec((1, B, tn), lambda k, j: (k, 0, j)),
        out_shape=jax.ShapeDtypeStruct((n_sub, B, N), F32),
        compiler_params=_cparams(("arbitrary", "arbitrary")),
        name="ada_params",
    )(c, w, b)
    return out.reshape(n_sub * B * 3, 1, D)


def _mod_row(k, b, which, B):
    return (k * B + b) * 3 + which


SHIFT, SCALE, GATE = 0, 1, 2


def _modmm_kernel(x_ref, sc_ref, sh_ref, w_ref, o_ref):
    h = x_ref[0] * (1.0 + sc_ref[0]) + sh_ref[0]
    o_ref[0] = jnp.dot(h.astype(BF16), w_ref[...], preferred_element_type=F32).astype(o_ref.dtype)


def _mod_matmul(x, mods, k, w, tn):
    B, S, D = x.shape
    N = w.shape[1]
    tm = PROJ_TM
    return pl.pallas_call(
        _modmm_kernel,
        grid=(N // tn, B, S // tm),
        in_specs=[
            pl.BlockSpec((1, tm, D), lambda n, b, i: (b, i, 0)),
            pl.BlockSpec((1, 1, D), lambda n, b, i: (_mod_row(k, b, SCALE, B), 0, 0)),
            pl.BlockSpec((1, 1, D), lambda n, b, i: (_mod_row(k, b, SHIFT, B), 0, 0)),
            pl.BlockSpec((D, tn), lambda n, b, i: (0, n)),
        ],
        out_specs=pl.BlockSpec((1, tm, tn), lambda n, b, i: (b, i, n)),
        out_shape=jax.ShapeDtypeStruct((B, S, N), BF16),
        compiler_params=_cparams(("arbitrary", "arbitrary", "arbitrary")),
        name="mod_matmul",
    )(x, mods, mods, w)


SB_PIPE = 4
SB_UNROLL = 4


def _sb_tile_tables(nq):
    diag = [(i, i) for i in range(nq)]
    off = [(i, i - o) for o in range(1, nq) for i in range(o, nq)]
    ti, tj, spans = [], [], []
    for tiles in (diag, off):
        padded = [(nq, 0)] * SB_PIPE + tiles + [(nq, 0)] * (SB_PIPE + SB_UNROLL)
        spans.append((len(ti), len(tiles)))
        ti += [t[0] for t in padded]
        tj += [t[1] for t in padded]
    return ti, tj, spans


def _sb_kernel(ti_ref, tj_ref, q_ref, k_ref, v_ref, u_ref, o_ref,
               vT_ref, qT_ref, z_ref, sp_ref, d_ref, a_ref, c_ref, run_ref, acc_ref, alive_ref, *, spans):
    S = q_ref.shape[1]
    tq, tk, dh = SB_TQ, SB_TK, SB_HEAD_DIM
    nq = S // tq

    for c in range(S // tk):
        v_c = v_ref[0, c * tk:(c + 1) * tk, :].astype(F32)
        vT_ref[c] = v_c.T.astype(BF16)

    scale = dh ** -0.5 * LOG2_E
    head_rows = lax.broadcasted_iota(jnp.int32, (2 * dh, tq), 0)
    for i in range(nq):
        qT = (q_ref[0, i * tq:(i + 1) * tq, :].astype(F32) * scale).T
        qT_ref[i, 0] = jnp.where(head_rows < dh, qT, 0.0).astype(BF16)
        qT_ref[i, 1] = jnp.where(head_rows >= dh, qT, 0.0).astype(BF16)
    qT_ref[nq] = jnp.zeros_like(qT_ref[nq])
    z_ref[...] = jnp.zeros_like(z_ref)
    sp_ref[...] = jnp.zeros_like(sp_ref)
    d_ref[...] = jnp.zeros_like(d_ref)
    a_ref[...] = jnp.zeros_like(a_ref)
    c_ref[...] = jnp.zeros_like(c_ref)
    for i in range(nq + 1):
        alive_ref[i] = jnp.int32(1)

    def iteration(t1, t3, t4, p, masked):
        q = 1 - p
        i1, j1 = ti_ref[t1], tj_ref[t1]
        i3 = ti_ref[t3]
        i4, j4 = ti_ref[t4], tj_ref[t4]
        if masked:
            rows = lax.broadcasted_iota(jnp.int32, (tk, tq), 0)
            cols = lax.broadcasted_iota(jnp.int32, (tk, tq), 1)
            valid = rows < cols
        u = u_ref[...]
        suf = [jnp.dot(u, sp_ref[q, h], preferred_element_type=F32) for h in range(2)]
        vT = vT_ref[j4]
        pv = [jnp.dot(vT[h * dh:(h + 1) * dh, :], a_ref[p, h], preferred_element_type=F32) for h in range(2)]
        k_blk = k_ref[0, pl.ds(pl.multiple_of(j1 * tk, tk), tk), :]
        z_new = [jnp.dot(k_blk, qT_ref[i1, h], preferred_element_type=F32) for h in range(2)]
        for h in range(2):
            z = z_ref[p, h]
            sp = jnp.maximum(z, 0.0) + jnp.log2(1.0 + jnp.exp2(-jnp.abs(z)))
            d_ref[p, h] = z - sp
            if masked:
                sp = jnp.where(valid, sp, 0.0)
            sp_ref[p, h] = sp.astype(BF16)
        for h in range(2):
            a = jnp.exp2(d_ref[q, h] - suf[h])
            if masked:
                a = jnp.where(valid, a, 0.0)
            a_ref[q, h] = a.astype(BF16)
            tile_mass = suf[h][0:1, :] + sp_ref[q, h, 0:1, :].astype(F32)
            if masked:
                run_ref[i3, h, 0:1, :] = tile_mass
            else:
                run = run_ref[i3, h, 0:1, :]
                c = jnp.exp2(-run)
                c_ref[q, h, 0:1, :] = c
                run_ref[i3, h, 0:1, :] = run + tile_mass
                c_max = c if h == 0 else jnp.maximum(c_max, c)
        if not masked:
            live = jnp.logical_or(jnp.max(c_max) > 0.0, i3 == nq)
            alive_ref[i3] = live.astype(jnp.int32)
        for h in range(2):
            if masked:
                acc_ref[i4, h] = pv[h]
            else:
                acc_ref[i4, h] += pv[h] * c_ref[p, h, 0:1, :]
            z_ref[p, h] = z_new[h]

    base, n_real = spans[0]
    n_iter = n_real + SB_PIPE
    n_iter += -n_iter % SB_UNROLL

    def diag_body(t, carry):
        for r in range(SB_UNROLL):
            t1 = base + SB_PIPE + SB_UNROLL * t + r
            iteration(t1, t1 - 3, t1 - 4, r % 2, True)
        return carry
    lax.fori_loop(0, n_iter // SB_UNROLL, diag_body, 0)

    base, n_real = spans[1]
    first = base + SB_PIPE
    end = first + n_real

    def next_live(t):
        return lax.while_loop(lambda u: alive_ref[ti_ref[u]] == 0, lambda u: u + 1, t)

    def off_body(carry):
        ptr, h1, h2, h3, h4 = carry
        picks = []
        for r in range(SB_UNROLL):
            t1 = next_live(ptr)
            picks.append(t1)
            ptr = t1 + 1
        for r, t1 in enumerate(picks):
            iteration(t1, h3, h4, r % 2, False)
            h1, h2, h3, h4 = t1, h1, h2, h3
        return ptr, h1, h2, h3, h4

    lax.while_loop(lambda carry: carry[0] < end + SB_PIPE, off_body,
                   (jnp.int32(first), jnp.int32(first - 1), jnp.int32(first - 2), jnp.int32(first - 3), jnp.int32(first - 4)))

    for i in range(nq):
        oT = jnp.concatenate([acc_ref[i, 0], acc_ref[i, 1]], axis=0)
        o_ref[0, i * tq:(i + 1) * tq, :] = oT.T.astype(o_ref.dtype)


def _sb_attention(qkv, u):
    B, S, _ = qkv.shape
    n_pairs = SB_HEADS // 2
    nq = S // SB_TQ
    blk = (1, S, 2 * SB_HEAD_DIM)
    ti, tj, spans = _sb_tile_tables(nq)
    tile = (SB_TK, SB_TQ)
    return pl.pallas_call(
        functools.partial(_sb_kernel, spans=spans),
        grid_spec=pltpu.PrefetchScalarGridSpec(
            num_scalar_prefetch=2,
            grid=(B, n_pairs),
            in_specs=[
                pl.BlockSpec(blk, lambda b, p, *_: (b, 0, p)),
                pl.BlockSpec(blk, lambda b, p, *_: (b, 0, n_pairs + p)),
                pl.BlockSpec(blk, lambda b, p, *_: (b, 0, 2 * n_pairs + p)),
                pl.BlockSpec((SB_TK, SB_TK), lambda b, p, *_: (0, 0)),
            ],
            out_specs=pl.BlockSpec(blk, lambda b, p, *_: (b, 0, p)),
            scratch_shapes=[
                pltpu.VMEM((S // SB_TK, 2 * SB_HEAD_DIM, SB_TK), BF16),
                pltpu.VMEM((nq + 1, 2, 2 * SB_HEAD_DIM, SB_TQ), BF16),
                pltpu.VMEM((2, 2) + tile, F32),
                pltpu.VMEM((2, 2) + tile, BF16),
                pltpu.VMEM((2, 2) + tile, F32),
                pltpu.VMEM((2, 2) + tile, BF16),
                pltpu.VMEM((2, 2, SUBLANES, SB_TQ), F32),
                pltpu.VMEM((nq + 1, 2, SUBLANES, SB_TQ), F32),
                pltpu.VMEM((nq + 1, 2, SB_HEAD_DIM, SB_TQ), F32),
                pltpu.SMEM((nq + 1,), jnp.int32),
            ],
        ),
        out_shape=jax.ShapeDtypeStruct((B, S, D_MODEL), BF16),
        compiler_params=_cparams(("arbitrary", "arbitrary")),
        name="sb_attention",
    )(jnp.asarray(ti, jnp.int32), jnp.asarray(tj, jnp.int32), qkv, qkv, qkv, u)


def _layer_norm_rows(u, g, b):
    mu = jnp.mean(u, axis=-1, keepdims=True)
    d = u - mu
    var = jnp.mean(d * d, axis=-1, keepdims=True)
    return d * lax.rsqrt(var + LN_EPS) * g + b


def _outln_kernel(a_ref, w_ref, x_ref, gate_ref, g_ref, b_ref, o_ref):
    y = jnp.dot(a_ref[0], w_ref[...], preferred_element_type=F32)
    u = DEEPNORM_ALPHA * x_ref[0] + (1.0 + gate_ref[0]) * y
    o_ref[0] = _layer_norm_rows(u, g_ref[...], b_ref[...])


def _out_proj_ln(a, w, x, mods, k, g, b):
    B, S, D = x.shape
    K = a.shape[-1]
    tm = PROJ_TM
    return pl.pallas_call(
        _outln_kernel,
        grid=(B, S // tm),
        in_specs=[
            pl.BlockSpec((1, tm, K), lambda bb, i: (bb, i, 0)),
            pl.BlockSpec((K, D), lambda bb, i: (0, 0)),
            pl.BlockSpec((1, tm, D), lambda bb, i: (bb, i, 0)),
            pl.BlockSpec((1, 1, D), lambda bb, i: (_mod_row(k, bb, GATE, B), 0, 0)),
            pl.BlockSpec((1, D), lambda bb, i: (0, 0)),
            pl.BlockSpec((1, D), lambda bb, i: (0, 0)),
        ],
        out_specs=pl.BlockSpec((1, tm, D), lambda bb, i: (bb, i, 0)),
        out_shape=jax.ShapeDtypeStruct((B, S, D), F32),
        compiler_params=_cparams(("arbitrary", "arbitrary")),
        name="out_proj_ln",
    )(a, w, x, mods, g.reshape(1, D), b.reshape(1, D))


def _rope_kernel(pos_ref, invf_ref, sign_ref, cc_ref, ss_ref):
    ang = pos_ref[0].astype(F32) * invf_ref[...]
    cc_ref[0] = jnp.cos(ang)
    ss_ref[0] = jnp.sin(ang) * sign_ref[...]


def _rope_tables(positions):
    B, S = positions.shape
    half = RET_QK_DIM // 2
    inv_freq = 1.0 / (ROPE_BASE ** (jnp.arange(0, RET_QK_DIM, 2, dtype=F32) / RET_QK_DIM))
    invf2 = jnp.concatenate([inv_freq, inv_freq]).reshape(1, RET_QK_DIM)
    sign = jnp.concatenate([-jnp.ones((half,), F32), jnp.ones((half,), F32)]).reshape(1, RET_QK_DIM)
    ts = PROJ_TM
    shp = jax.ShapeDtypeStruct((B, S, RET_QK_DIM), F32)
    return pl.pallas_call(
        _rope_kernel,
        grid=(B, S // ts),
        in_specs=[
            pl.BlockSpec((1, ts, 1), lambda b, i: (b, i, 0)),
            pl.BlockSpec((1, RET_QK_DIM), lambda b, i: (0, 0)),
            pl.BlockSpec((1, RET_QK_DIM), lambda b, i: (0, 0)),
        ],
        out_specs=[pl.BlockSpec((1, ts, RET_QK_DIM), lambda b, i: (b, i, 0))] * 2,
        out_shape=[shp, shp],
        compiler_params=_cparams(("arbitrary", "arbitrary")),
        name="rope_tables",
    )(positions.reshape(B, S, 1), invf2, sign)


def _retention_decay_tables():
    C = RET_CHUNK
    kscale = RET_QK_DIM ** -0.5
    log_gamma = jnp.log(1.0 - jnp.exp2(-5.0 - jnp.arange(RET_HEADS, dtype=F32)))
    n = jnp.arange(C, dtype=F32)
    diff = n[:, None] - n[None, :]
    dmask = jnp.where(diff >= 0, jnp.exp(jnp.maximum(diff, 0.0) * log_gamma[:, None, None]), 0.0)
    xi = jnp.exp((n + 1.0) * log_gamma[:, None])
    zeta = jnp.exp((C - 1.0 - n) * log_gamma[:, None])
    chunk_decay = jnp.exp(C * log_gamma)
    dm = dmask * kscale
    xi_t = jnp.broadcast_to(xi[:, :, None], (RET_HEADS, C, RET_V_DIM))
    zeta_t = jnp.broadcast_to(zeta[:, :, None] * kscale, (RET_HEADS, C, RET_QK_DIM))
    decay_t = jnp.broadcast_to(chunk_decay[:, None, None], (RET_HEADS, RET_QK_DIM, RET_V_DIM))
    return dm, xi_t, zeta_t, decay_t


def _ret_kernel(q_ref, k_ref, v_ref, g_ref, cc_ref, ss_ref, dm_ref, xi_ref, zeta_ref, decay_ref,
                gn_ref, o_ref, state_ref):
    S = q_ref.shape[1]
    C = RET_CHUNK
    half = RET_QK_DIM // 2
    state_ref[...] = jnp.zeros_like(state_ref)

    def prep(c):
        r0 = pl.multiple_of(c * C, C)
        rows = pl.ds(r0, C)
        cc = cc_ref[0, rows, :]
        ss = ss_ref[0, rows, :]
        q = q_ref[0, rows, :].astype(F32)
        k = k_ref[0, rows, :].astype(F32)
        qr = q * cc + pltpu.roll(q, half, 1) * ss
        kr = k * cc + pltpu.roll(k, half, 1) * ss
        kzT = (kr * zeta_ref[0]).T.astype(BF16)
        return rows, qr.astype(BF16), kr.astype(BF16), v_ref[0, rows, :], kzT

    def finish(rows, inner, cross):
        o = inner + cross * xi_ref[0]
        mu = jnp.mean(o, axis=-1, keepdims=True)
        d = o - mu
        var = jnp.mean(d * d, axis=-1, keepdims=True)
        on = d * lax.rsqrt(var + LN_EPS) * gn_ref[...]
        g = g_ref[0, rows, :].astype(F32)
        o_ref[0, rows, :] = (g * jax.nn.sigmoid(g) * on).astype(o_ref.dtype)

    def chunk_pair(t, carry):
        nt = (((1,), (1,)), ((), ()))
        pa = prep(2 * t)
        pb = prep(2 * t + 1)
        qk_a = lax.dot_general(pa[1], pa[2], nt, preferred_element_type=F32)
        qk_b = lax.dot_general(pb[1], pb[2], nt, preferred_element_type=F32)
        ds_a = jnp.dot(pa[4], pa[3], preferred_element_type=F32)
        ds_b = jnp.dot(pb[4], pb[3], preferred_element_type=F32)
        state0 = state_ref[...]
        cross_a = jnp.dot(pa[1], state0.astype(BF16), preferred_element_type=F32)
        state1 = state0 * decay_ref[0] + ds_a
        cross_b = jnp.dot(pb[1], state1.astype(BF16), preferred_element_type=F32)
        state_ref[...] = state1 * decay_ref[0] + ds_b
        inner_a = jnp.dot((qk_a * dm_ref[0]).astype(BF16), pa[3], preferred_element_type=F32)
        inner_b = jnp.dot((qk_b * dm_ref[0]).astype(BF16), pb[3], preferred_element_type=F32)
        finish(pa[0], inner_a, cross_a)
        finish(pb[0], inner_b, cross_b)
        return carry

    lax.fori_loop(0, S // (2 * C), chunk_pair, 0)


def _retention(proj, cc, ss, tables, gn_g):
    B, S, _ = proj.shape
    dk, dv, H, C = RET_QK_DIM, RET_V_DIM, RET_HEADS, RET_CHUNK
    dm, xi_t, zeta_t, decay_t = tables
    return pl.pallas_call(
        _ret_kernel,
        grid=(B, H),
        in_specs=[
            pl.BlockSpec((1, S, dk), lambda b, h: (b, 0, h)),
            pl.BlockSpec((1, S, dk), lambda b, h: (b, 0, H + h)),
            pl.BlockSpec((1, S, dv), lambda b, h: (b, 0, H + h)),
            pl.BlockSpec((1, S, dv), lambda b, h: (b, 0, 2 * H + h)),
            pl.BlockSpec((1, S, dk), lambda b, h: (b, 0, 0)),
            pl.BlockSpec((1, S, dk), lambda b, h: (b, 0, 0)),
            pl.BlockSpec((1, C, C), lambda b, h: (h, 0, 0)),
            pl.BlockSpec((1, C, dv), lambda b, h: (h, 0, 0)),
            pl.BlockSpec((1, C, dk), lambda b, h: (h, 0, 0)),
            pl.BlockSpec((1, dk, dv), lambda b, h: (h, 0, 0)),
            pl.BlockSpec((1, dv), lambda b, h: (0, h)),
        ],
        out_specs=pl.BlockSpec((1, S, dv), lambda b, h: (b, 0, h)),
        out_shape=jax.ShapeDtypeStruct((B, S, H * dv), BF16),
        scratch_shapes=[pltpu.VMEM((dk, dv), F32)],
        compiler_params=_cparams(("arbitrary", "arbitrary")),
        name="retention",
    )(proj, proj, proj, proj, cc, ss, dm, xi_t, zeta_t, decay_t, gn_g.reshape(1, H * dv))


def _store_slabs(ref, val):
    tm = val.shape[0]
    for j in range(ROW_SLAB):
        ref[pl.ds(j, tm, stride=ROW_SLAB), :] = val[:, j * LANES:(j + 1) * LANES]


def _load_slab_chunks(ref, tm):
    return [ref[pl.ds(j, tm, stride=ROW_SLAB), :] for j in range(ROW_SLAB)]


def _top2_flags(vals):
    n = len(vals)
    rank = []
    for k in range(n):
        r = jnp.zeros_like(vals[0])
        for j in range(n):
            if j < k:
                r = r + (vals[j] >= vals[k]).astype(F32)
            elif j > k:
                r = r + (vals[j] > vals[k]).astype(F32)
        rank.append(r)
    is1 = [r == 0.0 for r in rank]
    is2 = [r == 1.0 for r in rank]
    m1 = sum(jnp.where(f, v, 0.0) for f, v in zip(is1, vals))
    m2 = sum(jnp.where(f, v, 0.0) for f, v in zip(is2, vals))
    return is1, is2, m1, m2


def _router_kernel(x_ref, sc_ref, sh_ref, rw_ref, rb_ref, lt_ref,
                   bucket_ref, rank_ref, counts_ref, carry_ref):
    i = pl.program_id(0)
    tm = x_ref.shape[0]

    @pl.when(i == 0)
    def _():
        carry_ref[...] = jnp.zeros_like(carry_ref)

    h = x_ref[...] * (1.0 + sc_ref[0]) + sh_ref[0]
    hh, hl = _split_bf16(h)
    wh, wl = _split_bf16(rw_ref[...])
    nt = (((1,), (1,)), ((), ()))
    logits = lax.dot_general(wh, hh, nt, preferred_element_type=F32)
    logits += lax.dot_general(wl, hh, nt, preferred_element_type=F32)
    logits += lax.dot_general(wh, hl, nt, preferred_element_type=F32)
    logits = logits + rb_ref[:, 0:1]

    rows = [logits[e:e + 1, :] for e in range(N_EXPERTS)]
    gmax = functools.reduce(jnp.maximum, rows)
    groups = []
    for g in range(N_GROUPS):
        vals = rows[g * EXPERTS_PER_GROUP:(g + 1) * EXPERTS_PER_GROUP]
        is1, is2, m1, m2 = _top2_flags(vals)
        score = jnp.exp(m1 - gmax) + jnp.exp(m2 - gmax)
        groups.append(([jnp.logical_or(a, b) for a, b in zip(is1, is2)], score))
    best = groups[0][1]
    gsel = jnp.zeros_like(best)
    for g in range(1, N_GROUPS):
        better = groups[g][1] > best
        best = jnp.where(better, groups[g][1], best)
        gsel = jnp.where(better, float(g), gsel)
    sel = [jnp.zeros_like(best) > 1.0] * EXPERTS_PER_GROUP
    for g in range(N_GROUPS):
        here = gsel == float(g)
        sel = [jnp.logical_or(s, jnp.logical_and(here, f)) for s, f in zip(sel, groups[g][0])]
    pair = jnp.where(sel[0],
                     jnp.where(sel[1], 0.0, jnp.where(sel[2], 1.0, 2.0)),
                     jnp.where(sel[1], jnp.where(sel[2], 3.0, 4.0), 5.0))
    bucket = (gsel * float(PAIRS_PER_GROUP) + pair).astype(jnp.int32)

    brow = lax.broadcasted_iota(jnp.int32, (BUCKET_ROWS, tm), 0)
    onehot = (brow == bucket).astype(F32)
    before = jnp.dot(onehot.astype(BF16), lt_ref[...], preferred_element_type=F32)
    carry = carry_ref[...]
    in_tile = jnp.sum(onehot * before, axis=0, keepdims=True)
    base = jnp.sum(onehot * carry[:, 0:1], axis=0, keepdims=True)
    rank_ref[...] = (base + in_tile).astype(jnp.int32)
    bucket_ref[...] = bucket
    new_carry = carry + jnp.sum(onehot, axis=1, keepdims=True)
    carry_ref[...] = new_carry
    counts_ref[...] = new_carry.astype(jnp.int32)


def _router(x2, mods, k, B, S, router_wT, router_bt, lt):
    T, D = x2.shape
    tm = ROUTE_TM
    per_b = S // tm
    return pl.pallas_call(
        _router_kernel,
        grid=(T // tm,),
        in_specs=[
            pl.BlockSpec((tm, D), lambda i: (i, 0)),
            pl.BlockSpec((1, 1, D), lambda i: (_mod_row(k, i // per_b, SCALE, B), 0, 0)),
            pl.BlockSpec((1, 1, D), lambda i: (_mod_row(k, i // per_b, SHIFT, B), 0, 0)),
            pl.BlockSpec((N_EXPERTS, D), lambda i: (0, 0)),
            pl.BlockSpec((N_EXPERTS, LANES), lambda i: (0, 0)),
            pl.BlockSpec((tm, tm), lambda i: (0, 0)),
        ],
        out_specs=[
            pl.BlockSpec((1, tm), lambda i: (0, i)),
            pl.BlockSpec((1, tm), lambda i: (0, i)),
            pl.BlockSpec((BUCKET_ROWS, LANES), lambda i: (0, 0)),
        ],
        out_shape=[
            jax.ShapeDtypeStruct((1, T), jnp.int32),
            jax.ShapeDtypeStruct((1, T), jnp.int32),
            jax.ShapeDtypeStruct((BUCKET_ROWS, LANES), jnp.int32),
        ],
        scratch_shapes=[pltpu.VMEM((BUCKET_ROWS, LANES), F32)],
        compiler_params=_cparams(("arbitrary",)),
        name="router",
    )(x2, mods, mods, router_wT, router_bt, lt)


def _slab(ref, r):
    return ref.at[pl.ds(pl.multiple_of(r * ROW_SLAB, ROW_SLAB), ROW_SLAB)]


def _wait_rows(src_ref, dst_ref, sem, n):
    def body(c, carry):
        for _ in range(DMA_CHUNK):
            pltpu.make_async_copy(_slab(src_ref, 0), _slab(dst_ref, 0), sem).wait()
        return carry
    lax.fori_loop(0, n // DMA_CHUNK, body, 0)


def _scatter_kernel(dest_ref, pad_lo_ref, pad_hi_ref, x_ref, sc_ref, sh_ref, xs_ref, hbuf, sems):
    i = pl.program_id(0)
    last = pl.num_programs(0) - 1
    tm = x_ref.shape[0]
    h = x_ref[...] * (1.0 + sc_ref[0]) + sh_ref[0]

    for s in range(2):
        @pl.when(lax.rem(i, 2) == s)
        def _(s=s):
            buf = hbuf.at[s]
            _store_slabs(buf, h)

            def issue(c, carry):
                for u in range(DMA_CHUNK):
                    r = c * DMA_CHUNK + u
                    pltpu.make_async_copy(_slab(buf, r), _slab(xs_ref, dest_ref[i * tm + r]), sems.at[s]).start(priority=u % 2)
                return carry
            lax.fori_loop(0, tm // DMA_CHUNK, issue, 0)

            @pl.when(i > 0)
            def _():
                _wait_rows(hbuf.at[1 - s], xs_ref, sems.at[1 - s], tm)

            @pl.when(i == last)
            def _():
                _wait_rows(buf, xs_ref, sems.at[s], tm)

                def filler(row0, n_rows, start):
                    cp = pltpu.make_async_copy(
                        buf.at[pl.ds(0, n_rows * ROW_SLAB)],
                        xs_ref.at[pl.ds(pl.multiple_of(row0 * ROW_SLAB, ROW_SLAB), n_rows * ROW_SLAB)],
                        sems.at[s])
                    if start:
                        cp.start()
                    else:
                        cp.wait()

                def fill(start):
                    def bucket(b, carry):
                        pos = pad_lo_ref[b]
                        n = pad_hi_ref[b] - pos
                        for bit in reversed(range(MOE_TM.bit_length() - 1)):
                            take = lax.rem(lax.shift_right_logical(n, bit), 2)

                            @pl.when(take == 1)
                            def _(pos=pos, bit=bit):
                                filler(pos, 1 << bit, start)
                            pos = pos + take * (1 << bit)
                        return carry
                    lax.fori_loop(0, N_BUCKETS, bucket, 0)

                    def tail(t, carry):
                        filler(t * MOE_TM, MOE_TM, start)
                        return carry
                    n_tiles = xs_ref.shape[0] // (MOE_TM * ROW_SLAB)
                    lax.fori_loop(pad_hi_ref[N_BUCKETS - 1] // MOE_TM, n_tiles, tail, 0)
                fill(True)
                fill(False)


def _scatter_rows(x2, mods, k, B, S, plan, n_out):
    T, D = x2.shape
    tm = PROJ_TM
    per_b = S // tm
    return pl.pallas_call(
        _scatter_kernel,
        grid_spec=pltpu.PrefetchScalarGridSpec(
            num_scalar_prefetch=3,
            grid=(T // tm,),
            in_specs=[
                pl.BlockSpec((tm, D), lambda i, *_: (i, 0)),
                pl.BlockSpec((1, 1, D), lambda i, *_: (_mod_row(k, i // per_b, SCALE, B), 0, 0)),
                pl.BlockSpec((1, 1, D), lambda i, *_: (_mod_row(k, i // per_b, SHIFT, B), 0, 0)),
            ],
            out_specs=pl.BlockSpec(memory_space=pl.ANY),
            scratch_shapes=[pltpu.VMEM((2, tm * ROW_SLAB, LANES), F32), pltpu.SemaphoreType.DMA((2,))],
        ),
        out_shape=jax.ShapeDtypeStruct((n_out * ROW_SLAB, LANES), F32),
        compiler_params=_cparams(("arbitrary",)),
        name="scatter_rows",
    )(plan["dest"], plan["pad_lo"], plan["pad_hi"], x2, mods, mods)


def _expert(x, wg_ref, wu_ref, wd_ref):
    gte = jnp.dot(x, wg_ref[0], preferred_element_type=F32)
    up = jnp.dot(x, wu_ref[0], preferred_element_type=F32)
    hid = gte * jax.nn.sigmoid(gte) * up
    return jnp.dot(hid.astype(BF16), wd_ref[0], preferred_element_type=F32)


def _router_logit(chunks, rw_ref, rb_ref, e):
    w_row = rw_ref[pl.ds(e, 1), :]
    acc = chunks[0] * w_row[:, 0:LANES]
    for j in range(1, ROW_SLAB):
        acc = acc + chunks[j] * w_row[:, j * LANES:(j + 1) * LANES]
    return jnp.sum(acc, axis=-1, keepdims=True) + rb_ref[pl.ds(e, 1), 0:1]


def _moe_kernel(src_ref, ea_ref, eb_ref, valid_ref,
                xs_ref, rw_ref, rb_ref, wga_ref, wua_ref, wda_ref, wgb_ref, wub_ref, wdb_ref, o_ref):
    i = pl.program_id(0)

    @pl.when(valid_ref[i] == 1)
    def _():
        chunks = _load_slab_chunks(xs_ref, MOE_TM)
        l_a = _router_logit(chunks, rw_ref, rb_ref, ea_ref[i])
        l_b = _router_logit(chunks, rw_ref, rb_ref, eb_ref[i])
        w_a = jax.nn.sigmoid(l_a - l_b)
        w_b = jax.nn.sigmoid(l_b - l_a)
        x = jnp.concatenate([c.astype(BF16) for c in chunks], axis=-1)
        y = w_a * _expert(x, wga_ref, wua_ref, wda_ref) + w_b * _expert(x, wgb_ref, wub_ref, wdb_ref)
        _store_slabs(o_ref, y)

    @pl.when(valid_ref[i] == 0)
    def _():
        o_ref[...] = jnp.zeros_like(o_ref)


def _grouped_experts(xs, plan, router_wT, router_bt, w_gate, w_up, w_down):
    n_tiles = xs.shape[0] // (MOE_TM * ROW_SLAB)
    D, Fd = D_MODEL, D_EXPERT
    wspec_a = lambda shape: pl.BlockSpec(shape, lambda i, src, ea, eb, va: (ea[i], 0, 0))
    wspec_b = lambda shape: pl.BlockSpec(shape, lambda i, src, ea, eb, va: (eb[i], 0, 0))
    row_spec = pl.BlockSpec((MOE_TM * ROW_SLAB, LANES), lambda i, src, ea, eb, va: (src[i], 0))
    return pl.pallas_call(
        _moe_kernel,
        grid_spec=pltpu.PrefetchScalarGridSpec(
            num_scalar_prefetch=4,
            grid=(n_tiles,),
            in_specs=[
                row_spec,
                pl.BlockSpec((N_EXPERTS, D), lambda i, src, ea, eb, va: (0, 0)),
                pl.BlockSpec((N_EXPERTS, LANES), lambda i, src, ea, eb, va: (0, 0)),
                wspec_a((1, D, Fd)), wspec_a((1, D, Fd)), wspec_a((1, Fd, D)),
                wspec_b((1, D, Fd)), wspec_b((1, D, Fd)), wspec_b((1, Fd, D)),
            ],
            out_specs=pl.BlockSpec((MOE_TM * ROW_SLAB, LANES), lambda i, src, ea, eb, va: (i, 0)),
        ),
        out_shape=jax.ShapeDtypeStruct(xs.shape, F32),
        compiler_params=_cparams(("arbitrary",)),
        name="grouped_experts",
    )(plan["tile_src"], plan["tile_ea"], plan["tile_eb"], plan["tile_valid"],
      xs, router_wT, router_bt, w_gate, w_up, w_down, w_gate, w_up, w_down)


def _moe_plan(bucket, rank, counts, n_tiles):
    cnt = counts[:N_BUCKETS, 0]
    tiles_b = (cnt + MOE_TM - 1) // MOE_TM
    tile_end = jnp.cumsum(tiles_b)
    tile_start = tile_end - tiles_b
    row_start = tile_start * MOE_TM
    dest = jnp.take(row_start, bucket[0]) + rank[0]
    n_used = tile_end[-1]
    t = jnp.arange(n_tiles, dtype=jnp.int32)
    t_c = jnp.minimum(t, n_used - 1)
    tb = jnp.sum((t_c[:, None] >= tile_end[None, :]).astype(jnp.int32), axis=1)
    pair_lo = jnp.array([0, 0, 0, 1, 1, 2], jnp.int32)
    pair_hi = jnp.array([1, 2, 3, 2, 3, 3], jnp.int32)
    grp = tb // PAIRS_PER_GROUP
    pr = tb % PAIRS_PER_GROUP
    return {
        "dest": dest.astype(jnp.int32),
        "pad_lo": (row_start + cnt).astype(jnp.int32),
        "pad_hi": (tile_end * MOE_TM).astype(jnp.int32),
        "tile_src": t_c.astype(jnp.int32),
        "tile_ea": (grp * EXPERTS_PER_GROUP + jnp.take(pair_lo, pr)).astype(jnp.int32),
        "tile_eb": (grp * EXPERTS_PER_GROUP + jnp.take(pair_hi, pr)).astype(jnp.int32),
        "tile_valid": (t < n_used).astype(jnp.int32),
    }


def _resln_kernel(dest_ref, ys_ref, x_ref, gate_ref, g_ref, b_ref, o_ref, ybuf, sems):
    i = pl.program_id(0)
    n = pl.num_programs(0)
    tm = x_ref.shape[0]

    def issue(tile, s):
        def body(c, carry):
            for u in range(DMA_CHUNK):
                r = c * DMA_CHUNK + u
                pltpu.make_async_copy(_slab(ys_ref, dest_ref[tile * tm + r]), _slab(ybuf.at[s], r), sems.at[s]).start(priority=u % 2)
            return carry
        lax.fori_loop(0, tm // DMA_CHUNK, body, 0)

    @pl.when(i == 0)
    def _():
        issue(0, 0)

    for s in range(2):
        @pl.when(lax.rem(i, 2) == s)
        def _(s=s):
            @pl.when(i + 1 < n)
            def _():
                issue(i + 1, 1 - s)

            _wait_rows(ys_ref, ybuf.at[s], sems.at[s], tm)
            y = jnp.concatenate(_load_slab_chunks(ybuf.at[s], tm), axis=-1)
            u = DEEPNORM_ALPHA * x_ref[...] + (1.0 + gate_ref[0]) * y
            o_ref[...] = _layer_norm_rows(u, g_ref[...], b_ref[...])


def _gather_residual_ln(ys, plan, x2, mods, k, B, S, g, b):
    T, D = x2.shape
    tm = PROJ_TM
    per_b = S // tm
    return pl.pallas_call(
        _resln_kernel,
        grid_spec=pltpu.PrefetchScalarGridSpec(
            num_scalar_prefetch=1,
            grid=(T // tm,),
            in_specs=[
                pl.BlockSpec(memory_space=pl.ANY),
                pl.BlockSpec((tm, D), lambda i, *_: (i, 0)),
                pl.BlockSpec((1, 1, D), lambda i, *_: (_mod_row(k, i // per_b, GATE, B), 0, 0)),
                pl.BlockSpec((1, D), lambda i, *_: (0, 0)),
                pl.BlockSpec((1, D), lambda i, *_: (0, 0)),
            ],
            out_specs=pl.BlockSpec((tm, D), lambda i, *_: (i, 0)),
            scratch_shapes=[pltpu.VMEM((2, tm * ROW_SLAB, LANES), F32), pltpu.SemaphoreType.DMA((2,))],
        ),
        out_shape=jax.ShapeDtypeStruct((T, D), F32),
        compiler_params=_cparams(("arbitrary",)),
        name="gather_residual_ln",
    )(plan["dest"], ys, x2, mods, g.reshape(1, D), b.reshape(1, D))


def _moe_sublayer(x, mods, k, router_wT, router_bt, lt, w_gate, w_up, w_down, g, b):
    B, S, D = x.shape
    T = B * S
    x2 = x.reshape(T, D)
    bucket, rank, counts = _router(x2, mods, k, B, S, router_wT, router_bt, lt)
    n_tiles = T // MOE_TM + N_BUCKETS
    plan = _moe_plan(bucket, rank, counts, n_tiles)
    xs = _scatter_rows(x2, mods, k, B, S, plan, n_tiles * MOE_TM)
    ys = _grouped_experts(xs, plan, router_wT, router_bt, w_gate, w_up, w_down)
    return _gather_residual_ln(ys, plan, x2, mods, k, B, S, g, b).reshape(B, S, D)


def kernel(x, c, positions, ada_w, ada_b, ln_g, ln_b, sb_w_in, sb_w_out, ret_w_in, ret_gn_g, ret_w_out,
           router_w, router_b, moe_w_gate, moe_w_up, moe_w_down):
    mods = _ada_params(c, ada_w, ada_b)
    cc, ss = _rope_tables(positions)
    tables = _retention_decay_tables()
    u = (jnp.arange(SB_TK)[:, None] < jnp.arange(SB_TK)[None, :]).astype(BF16)
    lt = (jnp.arange(ROUTE_TM)[:, None] < jnp.arange(ROUTE_TM)[None, :]).astype(BF16)
    router_wT = router_w.T
    router_bt = jnp.broadcast_to(router_b[:, None], (N_EXPERTS, LANES))
    for i in range(DEPTH):
        k_mix, k_moe = 2 * i, 2 * i + 1
        if i % 2 == 0:
            qkv = _mod_matmul(x, mods, k_mix, sb_w_in[i // 2].astype(BF16), tn=3 * D_MODEL // 2)
            a = _sb_attention(qkv, u)
            w_out = sb_w_out[i // 2]
        else:
            proj = _mod_matmul(x, mods, k_mix, ret_w_in[i // 2].astype(BF16), tn=2 * D_MODEL)
            a = _retention(proj, cc, ss, tables, ret_gn_g[i // 2])
            w_out = ret_w_out[i // 2]
        x = _out_proj_ln(a, w_out.astype(BF16), x, mods, k_mix, ln_g[i, 0], ln_b[i, 0])
        x = _moe_sublayer(x, mods, k_moe, router_wT, router_bt, lt,
                          moe_w_gate[i].astype(BF16), moe_w_up[i].astype(BF16), moe_w_down[i].astype(BF16),
                          ln_g[i, 1], ln_b[i, 1])
    return x
```

```python
import functools

import jax
import jax.numpy as jnp
from jax import lax
from jax.experimental import pallas as pl
from jax.experimental.pallas import tpu as pltpu

F32 = jnp.float32
BF16 = jnp.bfloat16

D_MODEL = 1024
DEPTH = 4
SB_HEADS = 16
SB_HEAD_DIM = D_MODEL // SB_HEADS
RET_HEADS = 8
RET_QK_DIM = D_MODEL // RET_HEADS
RET_V_DIM = 2 * RET_QK_DIM
ROPE_BASE = 10000.0
N_EXPERTS = 16
N_GROUPS = 4
EXPERTS_PER_GROUP = N_EXPERTS // N_GROUPS
D_EXPERT = D_MODEL // 2
DEEPNORM_ALPHA = (2 * DEPTH) ** 0.25
LN_EPS = 1e-5
LOG2_E = 1.4426950408889634

LANES = 128
SUBLANES = 8
MXU_DIM = 256
VMEM_LIMIT_BYTES = 56 * 1024 * 1024

PROJ_TM = 512
SB_TQ = MXU_DIM
SB_TK = MXU_DIM
RET_CHUNK = 256
RET_GROUP = 4
ROUTE_TM = 512
MOE_TM = 256
PAIRS_PER_GROUP = 6
N_BUCKETS = N_GROUPS * PAIRS_PER_GROUP
BUCKET_ROWS = 32
ROW_SLAB = D_MODEL // LANES
DMA_CHUNK = 16
assert ROW_SLAB == SUBLANES


def _cparams(semantics):
    return pltpu.CompilerParams(dimension_semantics=semantics, vmem_limit_bytes=VMEM_LIMIT_BYTES)


def _split_bf16(a):
    hi = a.astype(BF16)
    lo = (a - hi.astype(F32)).astype(BF16)
    return hi, lo


def _ada_kernel(c_ref, w_ref, b_ref, o_ref):
    c = c_ref[...]
    s = c * jax.nn.sigmoid(c)
    sh, sl = _split_bf16(s)
    wh, wl = _split_bf16(w_ref[0])
    acc = jnp.dot(sh, wh, preferred_element_type=F32)
    acc += jnp.dot(sl, wh, preferred_element_type=F32)
    acc += jnp.dot(sh, wl, preferred_element_type=F32)
    o_ref[0] = acc + b_ref[0]


def _ada_params(c, ada_w, ada_b):
    B, D = c.shape
    n_sub = ada_w.shape[0] * ada_w.shape[1]
    N = ada_w.shape[-1]
    tn = N // 2
    w = ada_w.reshape(n_sub, D, N)
    b = ada_b.reshape(n_sub, 1, N)
    out = pl.pallas_call(
        _ada_kernel,
        grid=(n_sub, N // tn),
        in_specs=[
            pl.BlockSpec((B, D), lambda k, j: (0, 0)),
            pl.BlockSpec((1, D, tn), lambda k, j: (k, 0, j)),
            pl.BlockSpec((1, 1, tn), lambda k, j: (k, 0, j)),
        ],
        out_specs=pl.BlockSpec((1, B, tn), lambda k, j: (k, 0, j)),
        out_shape=jax.ShapeDtypeStruct((n_sub, B, N), F32),
        compiler_params=_cparams(("arbitrary", "arbitrary")),
        name="ada_params",
    )(c, w, b)
    return out.reshape(n_sub * B * 3, 1, D)


def _mod_row(k, b, which, B):
    return (k * B + b) * 3 + which


SHIFT, SCALE, GATE = 0, 1, 2


def _modmm_kernel(x_ref, sc_ref, sh_ref, w_ref, o_ref, w_bf):
    @pl.when(jnp.logical_and(pl.program_id(1) == 0, pl.program_id(2) == 0))
    def _():
        w_bf[...] = w_ref[0].astype(BF16)

    h = x_ref[0] * (1.0 + sc_ref[0]) + sh_ref[0]
    o_ref[0] = jnp.dot(h.astype(BF16), w_bf[...], preferred_element_type=F32).astype(o_ref.dtype)


def _mod_matmul(x, mods, k, w_all, layer, tn):
    B, S, D = x.shape
    N = w_all.shape[2]
    tm = PROJ_TM
    return pl.pallas_call(
        _modmm_kernel,
        grid=(N // tn, B, S // tm),
        in_specs=[
            pl.BlockSpec((1, tm, D), lambda n, b, i: (b, i, 0)),
            pl.BlockSpec((1, 1, D), lambda n, b, i: (_mod_row(k, b, SCALE, B), 0, 0)),
            pl.BlockSpec((1, 1, D), lambda n, b, i: (_mod_row(k, b, SHIFT, B), 0, 0)),
            pl.BlockSpec((1, D, tn), lambda n, b, i: (layer, 0, n)),
        ],
        out_specs=pl.BlockSpec((1, tm, tn), lambda n, b, i: (b, i, n)),
        out_shape=jax.ShapeDtypeStruct((B, S, N), BF16),
        scratch_shapes=[pltpu.VMEM((D, tn), BF16)],
        compiler_params=_cparams(("arbitrary", "arbitrary", "arbitrary")),
        name="mod_matmul",
    )(x, mods, mods, w_all)


SB_PIPE = 4
SB_UNROLL = 4


def _sb_tile_tables(nq):
    diag = [(i, i) for i in range(nq)]
    off = [(i, i - o) for o in range(1, nq) for i in range(o, nq)]
    ti, tj, spans = [], [], []
    for tiles in (diag, off):
        padded = [(nq, 0)] * SB_PIPE + tiles + [(nq, 0)] * (SB_PIPE + SB_UNROLL)
        spans.append((len(ti), len(tiles)))
        ti += [t[0] for t in padded]
        tj += [t[1] for t in padded]
    return ti, tj, spans


def _sb_kernel(ti_ref, tj_ref, q_ref, k_ref, v_ref, u_ref, o_ref,
               vT_ref, qT_ref, z_ref, sp_ref, d_ref, a_ref, c_ref, run_ref, acc_ref, alive_ref, *, spans):
    S = q_ref.shape[1]
    tq, tk, dh = SB_TQ, SB_TK, SB_HEAD_DIM
    nq = S // tq

    for c in range(S // tk):
        v_c = v_ref[0, c * tk:(c + 1) * tk, :].astype(F32)
        vT_ref[c] = v_c.T.astype(BF16)

    scale = dh ** -0.5 * LOG2_E
    head_rows = lax.broadcasted_iota(jnp.int32, (2 * dh, tq), 0)
    for i in range(nq):
        qT = (q_ref[0, i * tq:(i + 1) * tq, :].astype(F32) * scale).T
        qT_ref[i, 0] = jnp.where(head_rows < dh, qT, 0.0).astype(BF16)
        qT_ref[i, 1] = jnp.where(head_rows >= dh, qT, 0.0).astype(BF16)
    qT_ref[nq] = jnp.zeros_like(qT_ref[nq])
    z_ref[...] = jnp.zeros_like(z_ref)
    sp_ref[...] = jnp.zeros_like(sp_ref)
    d_ref[...] = jnp.zeros_like(d_ref)
    a_ref[...] = jnp.zeros_like(a_ref)
    c_ref[...] = jnp.zeros_like(c_ref)
    for i in range(nq + 1):
        alive_ref[i] = jnp.int32(1)

    def iteration(t1, t3, t4, p, masked):
        q = 1 - p
        i1, j1 = ti_ref[t1], tj_ref[t1]
        i3 = ti_ref[t3]
        i4, j4 = ti_ref[t4], tj_ref[t4]
        if masked:
            rows = lax.broadcasted_iota(jnp.int32, (tk, tq), 0)
            cols = lax.broadcasted_iota(jnp.int32, (tk, tq), 1)
            valid = rows < cols
        u = u_ref[...]
        suf = [jnp.dot(u, sp_ref[q, h], preferred_element_type=F32) for h in range(2)]
        vT = vT_ref[j4]
        pv = [jnp.dot(vT[h * dh:(h + 1) * dh, :], a_ref[p, h], preferred_element_type=F32) for h in range(2)]
        k_blk = k_ref[0, pl.ds(pl.multiple_of(j1 * tk, tk), tk), :]
        z_new = [jnp.dot(k_blk, qT_ref[i1, h], preferred_element_type=F32) for h in range(2)]
        for h in range(2):
            z = z_ref[p, h]
            sp = jnp.maximum(z, 0.0) + jnp.log2(1.0 + jnp.exp2(-jnp.abs(z)))
            d_ref[p, h] = z - sp
            if masked:
                sp = jnp.where(valid, sp, 0.0)
            sp_ref[p, h] = sp.astype(BF16)
        for h in range(2):
            a = jnp.exp2(d_ref[q, h] - suf[h])
            if masked:
                a = jnp.where(valid, a, 0.0)
            a_ref[q, h] = a.astype(BF16)
            tile_mass = suf[h][0:1, :] + sp_ref[q, h, 0:1, :].astype(F32)
            if masked:
                run_ref[i3, h, 0:1, :] = tile_mass
            else:
                run = run_ref[i3, h, 0:1, :]
                c = jnp.exp2(-run)
                c_ref[q, h, 0:1, :] = c
                run_ref[i3, h, 0:1, :] = run + tile_mass
                c_max = c if h == 0 else jnp.maximum(c_max, c)
        if not masked:
            live = jnp.logical_or(jnp.max(c_max) > 0.0, i3 == nq)
            alive_ref[i3] = live.astype(jnp.int32)
        for h in range(2):
            if masked:
                acc_ref[i4, h] = pv[h]
            else:
                acc_ref[i4, h] += pv[h] * c_ref[p, h, 0:1, :]
            z_ref[p, h] = z_new[h]

    base, n_real = spans[0]
    n_iter = n_real + SB_PIPE
    n_iter += -n_iter % SB_UNROLL

    def diag_body(t, carry):
        for r in range(SB_UNROLL):
            t1 = base + SB_PIPE + SB_UNROLL * t + r
            iteration(t1, t1 - 3, t1 - 4, r % 2, True)
        return carry
    lax.fori_loop(0, n_iter // SB_UNROLL, diag_body, 0)

    base, n_real = spans[1]
    first = base + SB_PIPE
    end = first + n_real

    def next_live(t):
        return lax.while_loop(lambda u: alive_ref[ti_ref[u]] == 0, lambda u: u + 1, t)

    def off_body(carry):
        ptr, h1, h2, h3, h4 = carry
        picks = []
        for r in range(SB_UNROLL):
            t1 = next_live(ptr)
            picks.append(t1)
            ptr = t1 + 1
        for r, t1 in enumerate(picks):
            iteration(t1, h3, h4, r % 2, False)
            h1, h2, h3, h4 = t1, h1, h2, h3
        return ptr, h1, h2, h3, h4

    lax.while_loop(lambda carry: carry[0] < end + SB_PIPE, off_body,
                   (jnp.int32(first), jnp.int32(first - 1), jnp.int32(first - 2), jnp.int32(first - 3), jnp.int32(first - 4)))

    for i in range(nq):
        oT = jnp.concatenate([acc_ref[i, 0], acc_ref[i, 1]], axis=0)
        o_ref[0, i * tq:(i + 1) * tq, :] = oT.T.astype(o_ref.dtype)


def _sb_attention(qkv, u):
    B, S, _ = qkv.shape
    n_pairs = SB_HEADS // 2
    nq = S // SB_TQ
    blk = (1, S, 2 * SB_HEAD_DIM)
    ti, tj, spans = _sb_tile_tables(nq)
    tile = (SB_TK, SB_TQ)
    return pl.pallas_call(
        functools.partial(_sb_kernel, spans=spans),
        grid_spec=pltpu.PrefetchScalarGridSpec(
            num_scalar_prefetch=2,
            grid=(B, n_pairs),
            in_specs=[
                pl.BlockSpec(blk, lambda b, p, *_: (b, 0, p)),
                pl.BlockSpec(blk, lambda b, p, *_: (b, 0, n_pairs + p)),
                pl.BlockSpec(blk, lambda b, p, *_: (b, 0, 2 * n_pairs + p)),
                pl.BlockSpec((SB_TK, SB_TK), lambda b, p, *_: (0, 0)),
            ],
            out_specs=pl.BlockSpec(blk, lambda b, p, *_: (b, 0, p)),
            scratch_shapes=[
                pltpu.VMEM((S // SB_TK, 2 * SB_HEAD_DIM, SB_TK), BF16),
                pltpu.VMEM((nq + 1, 2, 2 * SB_HEAD_DIM, SB_TQ), BF16),
                pltpu.VMEM((2, 2) + tile, F32),
                pltpu.VMEM((2, 2) + tile, BF16),
                pltpu.VMEM((2, 2) + tile, F32),
                pltpu.VMEM((2, 2) + tile, BF16),
                pltpu.VMEM((2, 2, SUBLANES, SB_TQ), F32),
                pltpu.VMEM((nq + 1, 2, SUBLANES, SB_TQ), F32),
                pltpu.VMEM((nq + 1, 2, SB_HEAD_DIM, SB_TQ), F32),
                pltpu.SMEM((nq + 1,), jnp.int32),
            ],
        ),
        out_shape=jax.ShapeDtypeStruct((B, S, D_MODEL), BF16),
        compiler_params=_cparams(("arbitrary", "arbitrary")),
        name="sb_attention",
    )(jnp.asarray(ti, jnp.int32), jnp.asarray(tj, jnp.int32), qkv, qkv, qkv, u)


def _layer_norm_rows(u, g, b):
    mu = jnp.mean(u, axis=-1, keepdims=True)
    d = u - mu
    var = jnp.mean(d * d, axis=-1, keepdims=True)
    return d * lax.rsqrt(var + LN_EPS) * g + b


def _outln_kernel(a_ref, w_ref, x_ref, gate_ref, g_ref, b_ref, o_ref, w_bf):
    @pl.when(jnp.logical_and(pl.program_id(0) == 0, pl.program_id(1) == 0))
    def _():
        w_bf[...] = w_ref[0].astype(BF16)

    y = jnp.dot(a_ref[0], w_bf[...], preferred_element_type=F32)
    u = DEEPNORM_ALPHA * x_ref[0] + (1.0 + gate_ref[0]) * y
    o_ref[0] = _layer_norm_rows(u, g_ref[...], b_ref[...])


def _out_proj_ln(a, w_all, layer, x, mods, k, g, b):
    B, S, D = x.shape
    K = a.shape[-1]
    tm = PROJ_TM
    return pl.pallas_call(
        _outln_kernel,
        grid=(B, S // tm),
        in_specs=[
            pl.BlockSpec((1, tm, K), lambda bb, i: (bb, i, 0)),
            pl.BlockSpec((1, K, D), lambda bb, i: (layer, 0, 0)),
            pl.BlockSpec((1, tm, D), lambda bb, i: (bb, i, 0)),
            pl.BlockSpec((1, 1, D), lambda bb, i: (_mod_row(k, bb, GATE, B), 0, 0)),
            pl.BlockSpec((1, D), lambda bb, i: (0, 0)),
            pl.BlockSpec((1, D), lambda bb, i: (0, 0)),
        ],
        out_specs=pl.BlockSpec((1, tm, D), lambda bb, i: (bb, i, 0)),
        out_shape=jax.ShapeDtypeStruct((B, S, D), F32),
        scratch_shapes=[pltpu.VMEM((K, D), BF16)],
        compiler_params=_cparams(("arbitrary", "arbitrary")),
        name="out_proj_ln",
    )(a, w_all, x, mods, g.reshape(1, D), b.reshape(1, D))


def _rope_kernel(pos_ref, invf_ref, sign_ref, cc_ref, ss_ref):
    ang = pos_ref[0].astype(F32) * invf_ref[...]
    cc_ref[0] = jnp.cos(ang)
    ss_ref[0] = jnp.sin(ang) * sign_ref[...]


def _rope_tables(positions):
    B, S = positions.shape
    half = RET_QK_DIM // 2
    inv_freq = 1.0 / (ROPE_BASE ** (jnp.arange(0, RET_QK_DIM, 2, dtype=F32) / RET_QK_DIM))
    invf2 = jnp.concatenate([inv_freq, inv_freq]).reshape(1, RET_QK_DIM)
    sign = jnp.concatenate([-jnp.ones((half,), F32), jnp.ones((half,), F32)]).reshape(1, RET_QK_DIM)
    ts = PROJ_TM
    shp = jax.ShapeDtypeStruct((B, S, RET_QK_DIM), F32)
    return pl.pallas_call(
        _rope_kernel,
        grid=(B, S // ts),
        in_specs=[
            pl.BlockSpec((1, ts, 1), lambda b, i: (b, i, 0)),
            pl.BlockSpec((1, RET_QK_DIM), lambda b, i: (0, 0)),
            pl.BlockSpec((1, RET_QK_DIM), lambda b, i: (0, 0)),
        ],
        out_specs=[pl.BlockSpec((1, ts, RET_QK_DIM), lambda b, i: (b, i, 0))] * 2,
        out_shape=[shp, shp],
        compiler_params=_cparams(("arbitrary", "arbitrary")),
        name="rope_tables",
    )(positions.reshape(B, S, 1), invf2, sign)


def _retention_decay_tables():
    C = RET_CHUNK
    kscale = RET_QK_DIM ** -0.5
    log_gamma = jnp.log(1.0 - jnp.exp2(-5.0 - jnp.arange(RET_HEADS, dtype=F32)))
    n = jnp.arange(C, dtype=F32)
    diff = n[:, None] - n[None, :]
    dmask = jnp.where(diff >= 0, jnp.exp(jnp.maximum(diff, 0.0) * log_gamma[:, None, None]), 0.0)
    xi = jnp.exp((n + 1.0) * log_gamma[:, None])
    zeta = jnp.exp((C - 1.0 - n) * log_gamma[:, None])
    chunk_decay = jnp.exp(C * log_gamma)
    dm = dmask * kscale
    xi_t = jnp.broadcast_to(xi[:, :, None], (RET_HEADS, C, RET_V_DIM))
    zeta_t = jnp.broadcast_to(zeta[:, :, None] * kscale, (RET_HEADS, C, RET_QK_DIM))
    decay_t = jnp.broadcast_to(chunk_decay[:, None, None], (RET_HEADS, RET_QK_DIM, RET_V_DIM))
    return dm, xi_t, zeta_t, decay_t


def _ret_kernel(q_ref, k_ref, v_ref, g_ref, cc_ref, ss_ref, dm_ref, xi_ref, zeta_ref, decay_ref,
                gn_ref, o_ref, state_ref):
    S = q_ref.shape[1]
    C = RET_CHUNK
    half = RET_QK_DIM // 2
    state_ref[...] = jnp.zeros_like(state_ref)

    def prep(c):
        r0 = pl.multiple_of(c * C, C)
        rows = pl.ds(r0, C)
        cc = cc_ref[0, rows, :]
        ss = ss_ref[0, rows, :]
        q = q_ref[0, rows, :].astype(F32)
        k = k_ref[0, rows, :].astype(F32)
        qr = q * cc + pltpu.roll(q, half, 1) * ss
        kr = k * cc + pltpu.roll(k, half, 1) * ss
        kzT = (kr * zeta_ref[0]).T.astype(BF16)
        return rows, qr.astype(BF16), kr.astype(BF16), v_ref[0, rows, :], kzT

    def finish(rows, inner, cross):
        o = inner + cross * xi_ref[0]
        mu = jnp.mean(o, axis=-1, keepdims=True)
        d = o - mu
        var = jnp.mean(d * d, axis=-1, keepdims=True)
        on = d * lax.rsqrt(var + LN_EPS) * gn_ref[...]
        g = g_ref[0, rows, :].astype(F32)
        o_ref[0, rows, :] = (g * jax.nn.sigmoid(g) * on).astype(o_ref.dtype)

    def chunk_group(t, carry):
        nt = (((1,), (1,)), ((), ()))
        ps = [prep(RET_GROUP * t + r) for r in range(RET_GROUP)]
        qks = [lax.dot_general(p[1], p[2], nt, preferred_element_type=F32) for p in ps]
        dss = [jnp.dot(p[4], p[3], preferred_element_type=F32) for p in ps]
        state = state_ref[...]
        crosses = []
        for p, ds in zip(ps, dss):
            crosses.append(jnp.dot(p[1], state.astype(BF16), preferred_element_type=F32))
            state = state * decay_ref[0] + ds
        state_ref[...] = state
        inners = [jnp.dot((qk * dm_ref[0]).astype(BF16), p[3], preferred_element_type=F32) for qk, p in zip(qks, ps)]
        for p, inner, cross in zip(ps, inners, crosses):
            finish(p[0], inner, cross)
        return carry

    lax.fori_loop(0, S // (RET_GROUP * C), chunk_group, 0)


def _retention(proj, cc, ss, tables, gn_g):
    B, S, _ = proj.shape
    dk, dv, H, C = RET_QK_DIM, RET_V_DIM, RET_HEADS, RET_CHUNK
    dm, xi_t, zeta_t, decay_t = tables
    return pl.pallas_call(
        _ret_kernel,
        grid=(B, H),
        in_specs=[
            pl.BlockSpec((1, S, dk), lambda b, h: (b, 0, h)),
            pl.BlockSpec((1, S, dk), lambda b, h: (b, 0, H + h)),
            pl.BlockSpec((1, S, dv), lambda b, h: (b, 0, H + h)),
            pl.BlockSpec((1, S, dv), lambda b, h: (b, 0, 2 * H + h)),
            pl.BlockSpec((1, S, dk), lambda b, h: (b, 0, 0)),
            pl.BlockSpec((1, S, dk), lambda b, h: (b, 0, 0)),
            pl.BlockSpec((1, C, C), lambda b, h: (h, 0, 0)),
            pl.BlockSpec((1, C, dv), lambda b, h: (h, 0, 0)),
            pl.BlockSpec((1, C, dk), lambda b, h: (h, 0, 0)),
            pl.BlockSpec((1, dk, dv), lambda b, h: (h, 0, 0)),
            pl.BlockSpec((1, dv), lambda b, h: (0, h)),
        ],
        out_specs=pl.BlockSpec((1, S, dv), lambda b, h: (b, 0, h)),
        out_shape=jax.ShapeDtypeStruct((B, S, H * dv), BF16),
        scratch_shapes=[pltpu.VMEM((dk, dv), F32)],
        compiler_params=_cparams(("arbitrary", "arbitrary")),
        name="retention",
    )(proj, proj, proj, proj, cc, ss, dm, xi_t, zeta_t, decay_t, gn_g.reshape(1, H * dv))


def _store_slabs(ref, val):
    tm = val.shape[0]
    for j in range(ROW_SLAB):
        ref[pl.ds(j, tm, stride=ROW_SLAB), :] = val[:, j * LANES:(j + 1) * LANES]


def _load_slab_chunks(ref, tm):
    return [ref[pl.ds(j, tm, stride=ROW_SLAB), :] for j in range(ROW_SLAB)]


def _top2_flags(vals):
    n = len(vals)
    rank = []
    for k in range(n):
        r = jnp.zeros_like(vals[0])
        for j in range(n):
            if j < k:
                r = r + (vals[j] >= vals[k]).astype(F32)
            elif j > k:
                r = r + (vals[j] > vals[k]).astype(F32)
        rank.append(r)
    is1 = [r == 0.0 for r in rank]
    is2 = [r == 1.0 for r in rank]
    m1 = sum(jnp.where(f, v, 0.0) for f, v in zip(is1, vals))
    m2 = sum(jnp.where(f, v, 0.0) for f, v in zip(is2, vals))
    return is1, is2, m1, m2


def _router_kernel(x_ref, sc_ref, sh_ref, rw_ref, rb_ref, lt_ref,
                   bucket_ref, rank_ref, counts_ref, carry_ref):
    i = pl.program_id(0)
    tm = x_ref.shape[0]

    @pl.when(i == 0)
    def _():
        carry_ref[...] = jnp.zeros_like(carry_ref)

    h = x_ref[...] * (1.0 + sc_ref[0]) + sh_ref[0]
    hh, hl = _split_bf16(h)
    wh, wl = _split_bf16(rw_ref[...])
    nt = (((1,), (1,)), ((), ()))
    logits = lax.dot_general(wh, hh, nt, preferred_element_type=F32)
    logits += lax.dot_general(wl, hh, nt, preferred_element_type=F32)
    logits += lax.dot_general(wh, hl, nt, preferred_element_type=F32)
    logits = logits + rb_ref[:, 0:1]

    rows = [logits[e:e + 1, :] for e in range(N_EXPERTS)]
    gmax = functools.reduce(jnp.maximum, rows)
    groups = []
    for g in range(N_GROUPS):
        vals = rows[g * EXPERTS_PER_GROUP:(g + 1) * EXPERTS_PER_GROUP]
        is1, is2, m1, m2 = _top2_flags(vals)
        score = jnp.exp(m1 - gmax) + jnp.exp(m2 - gmax)
        groups.append(([jnp.logical_or(a, b) for a, b in zip(is1, is2)], score))
    best = groups[0][1]
    gsel = jnp.zeros_like(best)
    for g in range(1, N_GROUPS):
        better = groups[g][1] > best
        best = jnp.where(better, groups[g][1], best)
        gsel = jnp.where(better, float(g), gsel)
    sel = [jnp.zeros_like(best) > 1.0] * EXPERTS_PER_GROUP
    for g in range(N_GROUPS):
        here = gsel == float(g)
        sel = [jnp.logical_or(s, jnp.logical_and(here, f)) for s, f in zip(sel, groups[g][0])]
    pair = jnp.where(sel[0],
                     jnp.where(sel[1], 0.0, jnp.where(sel[2], 1.0, 2.0)),
                     jnp.where(sel[1], jnp.where(sel[2], 3.0, 4.0), 5.0))
    bucket = (gsel * float(PAIRS_PER_GROUP) + pair).astype(jnp.int32)

    brow = lax.broadcasted_iota(jnp.int32, (BUCKET_ROWS, tm), 0)
    onehot = (brow == bucket).astype(F32)
    before = jnp.dot(onehot.astype(BF16), lt_ref[...], preferred_element_type=F32)
    carry = carry_ref[...]
    in_tile = jnp.sum(onehot * before, axis=0, keepdims=True)
    base = jnp.sum(onehot * carry[:, 0:1], axis=0, keepdims=True)
    rank_ref[...] = (base + in_tile).astype(jnp.int32)
    bucket_ref[...] = bucket
    new_carry = carry + jnp.sum(onehot, axis=1, keepdims=True)
    carry_ref[...] = new_carry
    counts_ref[...] = new_carry.astype(jnp.int32)


def _router(x2, mods, k, B, S, router_wT, router_bt, lt):
    T, D = x2.shape
    tm = ROUTE_TM
    per_b = S // tm
    return pl.pallas_call(
        _router_kernel,
        grid=(T // tm,),
        in_specs=[
            pl.BlockSpec((tm, D), lambda i: (i, 0)),
            pl.BlockSpec((1, 1, D), lambda i: (_mod_row(k, i // per_b, SCALE, B), 0, 0)),
            pl.BlockSpec((1, 1, D), lambda i: (_mod_row(k, i // per_b, SHIFT, B), 0, 0)),
            pl.BlockSpec((N_EXPERTS, D), lambda i: (0, 0)),
            pl.BlockSpec((N_EXPERTS, LANES), lambda i: (0, 0)),
            pl.BlockSpec((tm, tm), lambda i: (0, 0)),
        ],
        out_specs=[
            pl.BlockSpec((1, tm), lambda i: (0, i)),
            pl.BlockSpec((1, tm), lambda i: (0, i)),
            pl.BlockSpec((BUCKET_ROWS, LANES), lambda i: (0, 0)),
        ],
        out_shape=[
            jax.ShapeDtypeStruct((1, T), jnp.int32),
            jax.ShapeDtypeStruct((1, T), jnp.int32),
            jax.ShapeDtypeStruct((BUCKET_ROWS, LANES), jnp.int32),
        ],
        scratch_shapes=[pltpu.VMEM((BUCKET_ROWS, LANES), F32)],
        compiler_params=_cparams(("arbitrary",)),
        name="router",
    )(x2, mods, mods, router_wT, router_bt, lt)


def _slab(ref, r):
    return ref.at[pl.ds(pl.multiple_of(r * ROW_SLAB, ROW_SLAB), ROW_SLAB)]


def _wait_rows(src_ref, dst_ref, sem, n):
    def body(c, carry):
        for _ in range(DMA_CHUNK):
            pltpu.make_async_copy(_slab(src_ref, 0), _slab(dst_ref, 0), sem).wait()
        return carry
    lax.fori_loop(0, n // DMA_CHUNK, body, 0)


def _scatter_kernel(dest_ref, pad_lo_ref, pad_hi_ref, x_ref, sc_ref, sh_ref, xs_ref, hbuf, sems):
    i = pl.program_id(0)
    last = pl.num_programs(0) - 1
    tm = x_ref.shape[0]
    h = x_ref[...] * (1.0 + sc_ref[0]) + sh_ref[0]

    for s in range(2):
        @pl.when(lax.rem(i, 2) == s)
        def _(s=s):
            buf = hbuf.at[s]
            _store_slabs(buf, h)

            def issue(c, carry):
                for u in range(DMA_CHUNK):
                    r = c * DMA_CHUNK + u
                    pltpu.make_async_copy(_slab(buf, r), _slab(xs_ref, dest_ref[i * tm + r]), sems.at[s]).start(priority=u % 2)
                return carry
            lax.fori_loop(0, tm // DMA_CHUNK, issue, 0)

            @pl.when(i > 0)
            def _():
                _wait_rows(hbuf.at[1 - s], xs_ref, sems.at[1 - s], tm)

            @pl.when(i == last)
            def _():
                _wait_rows(buf, xs_ref, sems.at[s], tm)

                def filler(row0, n_rows, start):
                    cp = pltpu.make_async_copy(
                        buf.at[pl.ds(0, n_rows * ROW_SLAB)],
                        xs_ref.at[pl.ds(pl.multiple_of(row0 * ROW_SLAB, ROW_SLAB), n_rows * ROW_SLAB)],
                        sems.at[s])
                    if start:
                        cp.start()
                    else:
                        cp.wait()

                def fill(start):
                    def bucket(b, carry):
                        pos = pad_lo_ref[b]
                        n = pad_hi_ref[b] - pos
                        for bit in reversed(range(MOE_TM.bit_length() - 1)):
                            take = lax.rem(lax.shift_right_logical(n, bit), 2)

                            @pl.when(take == 1)
                            def _(pos=pos, bit=bit):
                                filler(pos, 1 << bit, start)
                            pos = pos + take * (1 << bit)
                        return carry
                    lax.fori_loop(0, N_BUCKETS, bucket, 0)

                    def tail(t, carry):
                        filler(t * MOE_TM, MOE_TM, start)
                        return carry
                    n_tiles = xs_ref.shape[0] // (MOE_TM * ROW_SLAB)
                    lax.fori_loop(pad_hi_ref[N_BUCKETS - 1] // MOE_TM, n_tiles, tail, 0)
                fill(True)
                fill(False)


def _scatter_rows(x2, mods, k, B, S, plan, n_out):
    T, D = x2.shape
    tm = PROJ_TM
    per_b = S // tm
    return pl.pallas_call(
        _scatter_kernel,
        grid_spec=pltpu.PrefetchScalarGridSpec(
            num_scalar_prefetch=3,
            grid=(T // tm,),
            in_specs=[
                pl.BlockSpec((tm, D), lambda i, *_: (i, 0)),
                pl.BlockSpec((1, 1, D), lambda i, *_: (_mod_row(k, i // per_b, SCALE, B), 0, 0)),
                pl.BlockSpec((1, 1, D), lambda i, *_: (_mod_row(k, i // per_b, SHIFT, B), 0, 0)),
            ],
            out_specs=pl.BlockSpec(memory_space=pl.ANY),
            scratch_shapes=[pltpu.VMEM((2, tm * ROW_SLAB, LANES), F32), pltpu.SemaphoreType.DMA((2,))],
        ),
        out_shape=jax.ShapeDtypeStruct((n_out * ROW_SLAB, LANES), F32),
        compiler_params=_cparams(("arbitrary",)),
        name="scatter_rows",
    )(plan["dest"], plan["pad_lo"], plan["pad_hi"], x2, mods, mods)


def _expert(x, wg, wu, wd):
    gte = jnp.dot(x, wg, preferred_element_type=F32)
    up = jnp.dot(x, wu, preferred_element_type=F32)
    hid = gte * jax.nn.sigmoid(gte) * up
    return jnp.dot(hid.astype(BF16), wd, preferred_element_type=F32)


def _router_logit(chunks, rw_ref, rb_ref, e):
    w_row = rw_ref[pl.ds(e, 1), :]
    acc = chunks[0] * w_row[:, 0:LANES]
    for j in range(1, ROW_SLAB):
        acc = acc + chunks[j] * w_row[:, j * LANES:(j + 1) * LANES]
    return jnp.sum(acc, axis=-1, keepdims=True) + rb_ref[pl.ds(e, 1), 0:1]


def _moe_kernel(src_ref, ea_ref, eb_ref, valid_ref,
                xs_ref, rw_ref, rb_ref, wga_ref, wua_ref, wda_ref, wgb_ref, wub_ref, wdb_ref, o_ref,
                wg_bf, wu_bf, wd_bf):
    i = pl.program_id(0)
    prev = jnp.maximum(i - 1, 0)

    for slot, e_ref, refs in ((0, ea_ref, (wga_ref, wua_ref, wda_ref)), (1, eb_ref, (wgb_ref, wub_ref, wdb_ref))):
        @pl.when(jnp.logical_or(i == 0, e_ref[i] != e_ref[prev]))
        def _(slot=slot, refs=refs):
            wg_bf[slot] = refs[0][0, 0].astype(BF16)
            wu_bf[slot] = refs[1][0, 0].astype(BF16)
            wd_bf[slot] = refs[2][0, 0].astype(BF16)

    @pl.when(valid_ref[i] == 1)
    def _():
        chunks = _load_slab_chunks(xs_ref, MOE_TM)
        l_a = _router_logit(chunks, rw_ref, rb_ref, ea_ref[i])
        l_b = _router_logit(chunks, rw_ref, rb_ref, eb_ref[i])
        w_a = jax.nn.sigmoid(l_a - l_b)
        w_b = jax.nn.sigmoid(l_b - l_a)
        x = jnp.concatenate([c.astype(BF16) for c in chunks], axis=-1)
        y = (w_a * _expert(x, wg_bf[0], wu_bf[0], wd_bf[0])
             + w_b * _expert(x, wg_bf[1], wu_bf[1], wd_bf[1]))
        _store_slabs(o_ref, y)

    @pl.when(valid_ref[i] == 0)
    def _():
        o_ref[...] = jnp.zeros_like(o_ref)


def _grouped_experts(xs, plan, router_wT, router_bt, w_gate, w_up, w_down, layer):
    n_tiles = xs.shape[0] // (MOE_TM * ROW_SLAB)
    D, Fd = D_MODEL, D_EXPERT
    wspec_a = lambda shape: pl.BlockSpec((1, 1) + shape, lambda i, src, ea, eb, va: (layer, ea[i], 0, 0))
    wspec_b = lambda shape: pl.BlockSpec((1, 1) + shape, lambda i, src, ea, eb, va: (layer, eb[i], 0, 0))
    row_spec = pl.BlockSpec((MOE_TM * ROW_SLAB, LANES), lambda i, src, ea, eb, va: (src[i], 0))
    return pl.pallas_call(
        _moe_kernel,
        grid_spec=pltpu.PrefetchScalarGridSpec(
            num_scalar_prefetch=4,
            grid=(n_tiles,),
            in_specs=[
                row_spec,
                pl.BlockSpec((N_EXPERTS, D), lambda i, src, ea, eb, va: (0, 0)),
                pl.BlockSpec((N_EXPERTS, LANES), lambda i, src, ea, eb, va: (0, 0)),
                wspec_a((D, Fd)), wspec_a((D, Fd)), wspec_a((Fd, D)),
                wspec_b((D, Fd)), wspec_b((D, Fd)), wspec_b((Fd, D)),
            ],
            out_specs=pl.BlockSpec((MOE_TM * ROW_SLAB, LANES), lambda i, src, ea, eb, va: (i, 0)),
            scratch_shapes=[pltpu.VMEM((2, D, Fd), BF16), pltpu.VMEM((2, D, Fd), BF16), pltpu.VMEM((2, Fd, D), BF16)],
        ),
        out_shape=jax.ShapeDtypeStruct(xs.shape, F32),
        compiler_params=_cparams(("arbitrary",)),
        name="grouped_experts",
    )(plan["tile_src"], plan["tile_ea"], plan["tile_eb"], plan["tile_valid"],
      xs, router_wT, router_bt, w_gate, w_up, w_down, w_gate, w_up, w_down)


def _moe_plan(bucket, rank, counts, n_tiles):
    cnt = counts[:N_BUCKETS, 0]
    tiles_b = (cnt + MOE_TM - 1) // MOE_TM
    tile_end = jnp.cumsum(tiles_b)
    tile_start = tile_end - tiles_b
    row_start = tile_start * MOE_TM
    dest = jnp.take(row_start, bucket[0]) + rank[0]
    n_used = tile_end[-1]
    t = jnp.arange(n_tiles, dtype=jnp.int32)
    t_c = jnp.minimum(t, n_used - 1)
    tb = jnp.sum((t_c[:, None] >= tile_end[None, :]).astype(jnp.int32), axis=1)
    pair_lo = jnp.array([0, 0, 0, 1, 1, 2], jnp.int32)
    pair_hi = jnp.array([1, 2, 3, 2, 3, 3], jnp.int32)
    grp = tb // PAIRS_PER_GROUP
    pr = tb % PAIRS_PER_GROUP
    return {
        "dest": dest.astype(jnp.int32),
        "pad_lo": (row_start + cnt).astype(jnp.int32),
        "pad_hi": (tile_end * MOE_TM).astype(jnp.int32),
        "tile_src": t_c.astype(jnp.int32),
        "tile_ea": (grp * EXPERTS_PER_GROUP + jnp.take(pair_lo, pr)).astype(jnp.int32),
        "tile_eb": (grp * EXPERTS_PER_GROUP + jnp.take(pair_hi, pr)).astype(jnp.int32),
        "tile_valid": (t < n_used).astype(jnp.int32),
    }


def _resln_kernel(dest_ref, ys_ref, x_ref, gate_ref, g_ref, b_ref, o_ref, ybuf, sems):
    i = pl.program_id(0)
    n = pl.num_programs(0)
    tm = x_ref.shape[0]

    def issue(tile, s):
        def body(c, carry):
            for u in range(DMA_CHUNK):
                r = c * DMA_CHUNK + u
                pltpu.make_async_copy(_slab(ys_ref, dest_ref[tile * tm + r]), _slab(ybuf.at[s], r), sems.at[s]).start(priority=u % 2)
            return carry
        lax.fori_loop(0, tm // DMA_CHUNK, body, 0)

    @pl.when(i == 0)
    def _():
        issue(0, 0)

    for s in range(2):
        @pl.when(lax.rem(i, 2) == s)
        def _(s=s):
            @pl.when(i + 1 < n)
            def _():
                issue(i + 1, 1 - s)

            _wait_rows(ys_ref, ybuf.at[s], sems.at[s], tm)
            y = jnp.concatenate(_load_slab_chunks(ybuf.at[s], tm), axis=-1)
            u = DEEPNORM_ALPHA * x_ref[...] + (1.0 + gate_ref[0]) * y
            o_ref[...] = _layer_norm_rows(u, g_ref[...], b_ref[...])


def _gather_residual_ln(ys, plan, x2, mods, k, B, S, g, b):
    T, D = x2.shape
    tm = PROJ_TM
    per_b = S // tm
    return pl.pallas_call(
        _resln_kernel,
        grid_spec=pltpu.PrefetchScalarGridSpec(
            num_scalar_prefetch=1,
            grid=(T // tm,),
            in_specs=[
                pl.BlockSpec(memory_space=pl.ANY),
                pl.BlockSpec((tm, D), lambda i, *_: (i, 0)),
                pl.BlockSpec((1, 1, D), lambda i, *_: (_mod_row(k, i // per_b, GATE, B), 0, 0)),
                pl.BlockSpec((1, D), lambda i, *_: (0, 0)),
                pl.BlockSpec((1, D), lambda i, *_: (0, 0)),
            ],
            out_specs=pl.BlockSpec((tm, D), lambda i, *_: (i, 0)),
            scratch_shapes=[pltpu.VMEM((2, tm * ROW_SLAB, LANES), F32), pltpu.SemaphoreType.DMA((2,))],
        ),
        out_shape=jax.ShapeDtypeStruct((T, D), F32),
        compiler_params=_cparams(("arbitrary",)),
        name="gather_residual_ln",
    )(plan["dest"], ys, x2, mods, g.reshape(1, D), b.reshape(1, D))


def _moe_sublayer(x, mods, k, router_wT, router_bt, lt, w_gate, w_up, w_down, layer, g, b):
    B, S, D = x.shape
    T = B * S
    x2 = x.reshape(T, D)
    bucket, rank, counts = _router(x2, mods, k, B, S, router_wT, router_bt, lt)
    n_tiles = T // MOE_TM + N_BUCKETS
    plan = _moe_plan(bucket, rank, counts, n_tiles)
    xs = _scatter_rows(x2, mods, k, B, S, plan, n_tiles * MOE_TM)
    ys = _grouped_experts(xs, plan, router_wT, router_bt, w_gate, w_up, w_down, layer)
    return _gather_residual_ln(ys, plan, x2, mods, k, B, S, g, b).reshape(B, S, D)


def kernel(x, c, positions, ada_w, ada_b, ln_g, ln_b, sb_w_in, sb_w_out, ret_w_in, ret_gn_g, ret_w_out,
           router_w, router_b, moe_w_gate, moe_w_up, moe_w_down):
    mods = _ada_params(c, ada_w, ada_b)
    cc, ss = _rope_tables(positions)
    tables = _retention_decay_tables()
    u = (jnp.arange(SB_TK)[:, None] < jnp.arange(SB_TK)[None, :]).astype(BF16)
    lt = (jnp.arange(ROUTE_TM)[:, None] < jnp.arange(ROUTE_TM)[None, :]).astype(BF16)
    router_wT = router_w.T
    router_bt = jnp.broadcast_to(router_b[:, None], (N_EXPERTS, LANES))
    for i in range(DEPTH):
        k_mix, k_moe = 2 * i, 2 * i + 1
        if i % 2 == 0:
            qkv = _mod_matmul(x, mods, k_mix, sb_w_in, i // 2, tn=3 * D_MODEL // 2)
            a = _sb_attention(qkv, u)
            w_out = sb_w_out
        else:
            proj = _mod_matmul(x, mods, k_mix, ret_w_in, i // 2, tn=2 * D_MODEL)
            a = _retention(proj, cc, ss, tables, ret_gn_g[i // 2])
            w_out = ret_w_out
        x = _out_proj_ln(a, w_out, i // 2, x, mods, k_mix, ln_g[i, 0], ln_b[i, 0])
        x = _moe_sublayer(x, mods, k_moe, router_wT, router_bt, lt, moe_w_gate, moe_w_up, moe_w_down, i,
                          ln_g[i, 1], ln_b[i, 1])
    return x
```

```python
import functools

import jax
import jax.numpy as jnp
from jax import lax
from jax.experimental import pallas as pl
from jax.experimental.pallas import tpu as pltpu

F32 = jnp.float32
BF16 = jnp.bfloat16

D_MODEL = 1024
DEPTH = 4
SB_HEADS = 16
SB_HEAD_DIM = D_MODEL // SB_HEADS
RET_HEADS = 8
RET_QK_DIM = D_MODEL // RET_HEADS
RET_V_DIM = 2 * RET_QK_DIM
ROPE_BASE = 10000.0
N_EXPERTS = 16
N_GROUPS = 4
EXPERTS_PER_GROUP = N_EXPERTS // N_GROUPS
D_EXPERT = D_MODEL // 2
DEEPNORM_ALPHA = (2 * DEPTH) ** 0.25
LN_EPS = 1e-5
LOG2_E = 1.4426950408889634

LANES = 128
SUBLANES = 8
MXU_DIM = 256
VMEM_LIMIT_BYTES = 56 * 1024 * 1024

PROJ_TM = 512
MM_TM = 1024
SB_TQ = MXU_DIM
SB_TK = MXU_DIM
RET_CHUNK = 256
RET_GROUP = 4
ROUTE_TM = 512
MOE_TM = 256
PAIRS_PER_GROUP = 6
N_BUCKETS = N_GROUPS * PAIRS_PER_GROUP
BUCKET_ROWS = 32
ROW_SLAB = D_MODEL // LANES
DMA_CHUNK = 16
assert ROW_SLAB == SUBLANES


def _cparams(semantics):
    return pltpu.CompilerParams(dimension_semantics=semantics, vmem_limit_bytes=VMEM_LIMIT_BYTES)


def _split_bf16(a):
    hi = a.astype(BF16)
    lo = (a - hi.astype(F32)).astype(BF16)
    return hi, lo


def _ada_kernel(c_ref, w_ref, b_ref, o_ref):
    c = c_ref[...]
    s = c * jax.nn.sigmoid(c)
    sh, sl = _split_bf16(s)
    wh, wl = _split_bf16(w_ref[0])
    acc = jnp.dot(sh, wh, preferred_element_type=F32)
    acc += jnp.dot(sl, wh, preferred_element_type=F32)
    acc += jnp.dot(sh, wl, preferred_element_type=F32)
    o_ref[0] = acc + b_ref[0]


def _ada_params(c, ada_w, ada_b):
    B, D = c.shape
    n_sub = ada_w.shape[0] * ada_w.shape[1]
    N = ada_w.shape[-1]
    tn = N // 2
    w = ada_w.reshape(n_sub, D, N)
    b = ada_b.reshape(n_sub, 1, N)
    out = pl.pallas_call(
        _ada_kernel,
        grid=(n_sub, N // tn),
        in_specs=[
            pl.BlockSpec((B, D), lambda k, j: (0, 0)),
            pl.BlockSpec((1, D, tn), lambda k, j: (k, 0, j)),
            pl.BlockSpec((1, 1, tn), lambda k, j: (k, 0, j)),
        ],
        out_specs=pl.BlockSpec((1, B, tn), lambda k, j: (k, 0, j)),
        out_shape=jax.ShapeDtypeStruct((n_sub, B, N), F32),
        compiler_params=_cparams(("arbitrary", "arbitrary")),
        name="ada_params",
    )(c, w, b)
    return out.reshape(n_sub * B * 3, 1, D)


def _mod_row(k, b, which, B):
    return (k * B + b) * 3 + which


SHIFT, SCALE, GATE = 0, 1, 2


def _modmm_kernel(x_ref, sc_ref, sh_ref, w_ref, o_ref, w_bf):
    @pl.when(jnp.logical_and(pl.program_id(1) == 0, pl.program_id(2) == 0))
    def _():
        w_bf[...] = w_ref[0].astype(BF16)

    h = x_ref[0] * (1.0 + sc_ref[0]) + sh_ref[0]
    o_ref[0] = jnp.dot(h.astype(BF16), w_bf[...], preferred_element_type=F32).astype(o_ref.dtype)


def _mod_matmul(x, mods, k, w_all, layer, tn):
    B, S, D = x.shape
    N = w_all.shape[2]
    tm = MM_TM
    return pl.pallas_call(
        _modmm_kernel,
        grid=(N // tn, B, S // tm),
        in_specs=[
            pl.BlockSpec((1, tm, D), lambda n, b, i: (b, i, 0)),
            pl.BlockSpec((1, 1, D), lambda n, b, i: (_mod_row(k, b, SCALE, B), 0, 0)),
            pl.BlockSpec((1, 1, D), lambda n, b, i: (_mod_row(k, b, SHIFT, B), 0, 0)),
            pl.BlockSpec((1, D, tn), lambda n, b, i: (layer, 0, n)),
        ],
        out_specs=pl.BlockSpec((1, tm, tn), lambda n, b, i: (b, i, n)),
        out_shape=jax.ShapeDtypeStruct((B, S, N), BF16),
        scratch_shapes=[pltpu.VMEM((D, tn), BF16)],
        compiler_params=_cparams(("arbitrary", "arbitrary", "arbitrary")),
        name="mod_matmul",
    )(x, mods, mods, w_all)


SB_PIPE = 4
SB_UNROLL = 4


def _sb_tile_tables(nq):
    diag = [(i, i) for i in range(nq)]
    off = [(i, i - o) for o in range(1, nq) for i in range(o, nq)]
    ti, tj, spans = [], [], []
    for tiles in (diag, off):
        padded = [(nq, 0)] * SB_PIPE + tiles + [(nq, 0)] * (SB_PIPE + SB_UNROLL)
        spans.append((len(ti), len(tiles)))
        ti += [t[0] for t in padded]
        tj += [t[1] for t in padded]
    return ti, tj, spans


def _sb_kernel(ti_ref, tj_ref, q_ref, k_ref, v_ref, u_ref, o_ref,
               vT_ref, qT_ref, z_ref, sp_ref, d_ref, a_ref, c_ref, run_ref, acc_ref, alive_ref, *, spans):
    S = q_ref.shape[1]
    tq, tk, dh = SB_TQ, SB_TK, SB_HEAD_DIM
    nq = S // tq

    for c in range(S // tk):
        v_c = v_ref[0, c * tk:(c + 1) * tk, :].astype(F32)
        vT_ref[c] = v_c.T.astype(BF16)

    scale = dh ** -0.5 * LOG2_E
    head_rows = lax.broadcasted_iota(jnp.int32, (2 * dh, tq), 0)
    for i in range(nq):
        qT = (q_ref[0, i * tq:(i + 1) * tq, :].astype(F32) * scale).T
        qT_ref[i, 0] = jnp.where(head_rows < dh, qT, 0.0).astype(BF16)
        qT_ref[i, 1] = jnp.where(head_rows >= dh, qT, 0.0).astype(BF16)
    qT_ref[nq] = jnp.zeros_like(qT_ref[nq])
    z_ref[...] = jnp.zeros_like(z_ref)
    sp_ref[...] = jnp.zeros_like(sp_ref)
    d_ref[...] = jnp.zeros_like(d_ref)
    a_ref[...] = jnp.zeros_like(a_ref)
    c_ref[...] = jnp.zeros_like(c_ref)
    for i in range(nq + 1):
        alive_ref[i] = jnp.int32(1)

    def iteration(t1, t3, t4, p, masked):
        q = 1 - p
        i1, j1 = ti_ref[t1], tj_ref[t1]
        i3 = ti_ref[t3]
        i4, j4 = ti_ref[t4], tj_ref[t4]
        if masked:
            rows = lax.broadcasted_iota(jnp.int32, (tk, tq), 0)
            cols = lax.broadcasted_iota(jnp.int32, (tk, tq), 1)
            valid = rows < cols
        u = u_ref[...]
        suf = [jnp.dot(u, sp_ref[q, h], preferred_element_type=F32) for h in range(2)]
        vT = vT_ref[j4]
        pv = [jnp.dot(vT[h * dh:(h + 1) * dh, :], a_ref[p, h], preferred_element_type=F32) for h in range(2)]
        k_blk = k_ref[0, pl.ds(pl.multiple_of(j1 * tk, tk), tk), :]
        z_new = [jnp.dot(k_blk, qT_ref[i1, h], preferred_element_type=F32) for h in range(2)]
        for h in range(2):
            z = z_ref[p, h]
            sp = jnp.maximum(z, 0.0) + jnp.log2(1.0 + jnp.exp2(-jnp.abs(z)))
            d_ref[p, h] = z - sp
            if masked:
                sp = jnp.where(valid, sp, 0.0)
            sp_ref[p, h] = sp.astype(BF16)
        for h in range(2):
            a = jnp.exp2(d_ref[q, h] - suf[h])
            if masked:
                a = jnp.where(valid, a, 0.0)
            a_ref[q, h] = a.astype(BF16)
            tile_mass = suf[h][0:1, :] + sp_ref[q, h, 0:1, :].astype(F32)
            if masked:
                run_ref[i3, h, 0:1, :] = tile_mass
            else:
                run = run_ref[i3, h, 0:1, :]
                c_ref[q, h, 0:1, :] = jnp.exp2(-run)
                run_ref[i3, h, 0:1, :] = run + tile_mass
                c_next = jnp.exp2(-(run + tile_mass))
                c_max = c_next if h == 0 else jnp.maximum(c_max, c_next)
        if not masked:
            live = jnp.logical_or(jnp.max(c_max) > 0.0, i3 == nq)
            alive_ref[i3] = live.astype(jnp.int32)
        for h in range(2):
            if masked:
                acc_ref[i4, h] = pv[h]
            else:
                acc_ref[i4, h] += pv[h] * c_ref[p, h, 0:1, :]
            z_ref[p, h] = z_new[h]

    base, n_real = spans[0]
    n_iter = n_real + SB_PIPE
    n_iter += -n_iter % SB_UNROLL

    def diag_body(t, carry):
        for r in range(SB_UNROLL):
            t1 = base + SB_PIPE + SB_UNROLL * t + r
            iteration(t1, t1 - 3, t1 - 4, r % 2, True)
        return carry
    lax.fori_loop(0, n_iter // SB_UNROLL, diag_body, 0)

    base, n_real = spans[1]
    first = base + SB_PIPE
    end = first + n_real

    def next_live(t):
        return lax.while_loop(lambda u: alive_ref[ti_ref[u]] == 0, lambda u: u + 1, t)

    def off_body(carry):
        ptr, h1, h2, h3, h4 = carry
        picks = []
        for r in range(SB_UNROLL):
            t1 = next_live(ptr)
            picks.append(t1)
            ptr = t1 + 1
        for r, t1 in enumerate(picks):
            iteration(t1, h3, h4, r % 2, False)
            h1, h2, h3, h4 = t1, h1, h2, h3
        return ptr, h1, h2, h3, h4

    lax.while_loop(lambda carry: carry[0] < end + SB_PIPE, off_body,
                   (jnp.int32(first), jnp.int32(first - 1), jnp.int32(first - 2), jnp.int32(first - 3), jnp.int32(first - 4)))

    for i in range(nq):
        oT = jnp.concatenate([acc_ref[i, 0], acc_ref[i, 1]], axis=0)
        o_ref[0, i * tq:(i + 1) * tq, :] = oT.T.astype(o_ref.dtype)


def _sb_attention(qkv, u):
    B, S, _ = qkv.shape
    n_pairs = SB_HEADS // 2
    nq = S // SB_TQ
    blk = (1, S, 2 * SB_HEAD_DIM)
    ti, tj, spans = _sb_tile_tables(nq)
    tile = (SB_TK, SB_TQ)
    return pl.pallas_call(
        functools.partial(_sb_kernel, spans=spans),
        grid_spec=pltpu.PrefetchScalarGridSpec(
            num_scalar_prefetch=2,
            grid=(B, n_pairs),
            in_specs=[
                pl.BlockSpec(blk, lambda b, p, *_: (b, 0, p)),
                pl.BlockSpec(blk, lambda b, p, *_: (b, 0, n_pairs + p)),
                pl.BlockSpec(blk, lambda b, p, *_: (b, 0, 2 * n_pairs + p)),
                pl.BlockSpec((SB_TK, SB_TK), lambda b, p, *_: (0, 0)),
            ],
            out_specs=pl.BlockSpec(blk, lambda b, p, *_: (b, 0, p)),
            scratch_shapes=[
                pltpu.VMEM((S // SB_TK, 2 * SB_HEAD_DIM, SB_TK), BF16),
                pltpu.VMEM((nq + 1, 2, 2 * SB_HEAD_DIM, SB_TQ), BF16),
                pltpu.VMEM((2, 2) + tile, F32),
                pltpu.VMEM((2, 2) + tile, BF16),
                pltpu.VMEM((2, 2) + tile, F32),
                pltpu.VMEM((2, 2) + tile, BF16),
                pltpu.VMEM((2, 2, SUBLANES, SB_TQ), F32),
                pltpu.VMEM((nq + 1, 2, SUBLANES, SB_TQ), F32),
                pltpu.VMEM((nq + 1, 2, SB_HEAD_DIM, SB_TQ), F32),
                pltpu.SMEM((nq + 1,), jnp.int32),
            ],
        ),
        out_shape=jax.ShapeDtypeStruct((B, S, D_MODEL), BF16),
        compiler_params=_cparams(("arbitrary", "arbitrary")),
        name="sb_attention",
    )(jnp.asarray(ti, jnp.int32), jnp.asarray(tj, jnp.int32), qkv, qkv, qkv, u)


def _layer_norm_rows(u, g, b):
    mu = jnp.mean(u, axis=-1, keepdims=True)
    d = u - mu
    var = jnp.mean(d * d, axis=-1, keepdims=True)
    return d * lax.rsqrt(var + LN_EPS) * g + b


def _outln_kernel(a_ref, w_ref, x_ref, gate_ref, g_ref, b_ref, o_ref, w_bf):
    @pl.when(jnp.logical_and(pl.program_id(0) == 0, pl.program_id(1) == 0))
    def _():
        w_bf[...] = w_ref[0].astype(BF16)

    y = jnp.dot(a_ref[0], w_bf[...], preferred_element_type=F32)
    u = DEEPNORM_ALPHA * x_ref[0] + (1.0 + gate_ref[0]) * y
    o_ref[0] = _layer_norm_rows(u, g_ref[...], b_ref[...])


def _out_proj_ln(a, w_all, layer, x, mods, k, g, b):
    B, S, D = x.shape
    K = a.shape[-1]
    tm = PROJ_TM
    return pl.pallas_call(
        _outln_kernel,
        grid=(B, S // tm),
        in_specs=[
            pl.BlockSpec((1, tm, K), lambda bb, i: (bb, i, 0)),
            pl.BlockSpec((1, K, D), lambda bb, i: (layer, 0, 0)),
            pl.BlockSpec((1, tm, D), lambda bb, i: (bb, i, 0)),
            pl.BlockSpec((1, 1, D), lambda bb, i: (_mod_row(k, bb, GATE, B), 0, 0)),
            pl.BlockSpec((1, D), lambda bb, i: (0, 0)),
            pl.BlockSpec((1, D), lambda bb, i: (0, 0)),
        ],
        out_specs=pl.BlockSpec((1, tm, D), lambda bb, i: (bb, i, 0)),
        out_shape=jax.ShapeDtypeStruct((B, S, D), F32),
        scratch_shapes=[pltpu.VMEM((K, D), BF16)],
        compiler_params=_cparams(("arbitrary", "arbitrary")),
        name="out_proj_ln",
    )(a, w_all, x, mods, g.reshape(1, D), b.reshape(1, D))


def _rope_kernel(pos_ref, invf_ref, sign_ref, cc_ref, ss_ref):
    ang = pos_ref[0].astype(F32) * invf_ref[...]
    cc_ref[0] = jnp.cos(ang)
    ss_ref[0] = jnp.sin(ang) * sign_ref[...]


def _rope_tables(positions):
    B, S = positions.shape
    half = RET_QK_DIM // 2
    inv_freq = 1.0 / (ROPE_BASE ** (jnp.arange(0, RET_QK_DIM, 2, dtype=F32) / RET_QK_DIM))
    invf2 = jnp.concatenate([inv_freq, inv_freq]).reshape(1, RET_QK_DIM)
    sign = jnp.concatenate([-jnp.ones((half,), F32), jnp.ones((half,), F32)]).reshape(1, RET_QK_DIM)
    ts = PROJ_TM
    shp = jax.ShapeDtypeStruct((B, S, RET_QK_DIM), F32)
    return pl.pallas_call(
        _rope_kernel,
        grid=(B, S // ts),
        in_specs=[
            pl.BlockSpec((1, ts, 1), lambda b, i: (b, i, 0)),
            pl.BlockSpec((1, RET_QK_DIM), lambda b, i: (0, 0)),
            pl.BlockSpec((1, RET_QK_DIM), lambda b, i: (0, 0)),
        ],
        out_specs=[pl.BlockSpec((1, ts, RET_QK_DIM), lambda b, i: (b, i, 0))] * 2,
        out_shape=[shp, shp],
        compiler_params=_cparams(("arbitrary", "arbitrary")),
        name="rope_tables",
    )(positions.reshape(B, S, 1), invf2, sign)


def _retention_decay_tables():
    C = RET_CHUNK
    kscale = RET_QK_DIM ** -0.5
    log_gamma = jnp.log(1.0 - jnp.exp2(-5.0 - jnp.arange(RET_HEADS, dtype=F32)))
    n = jnp.arange(C, dtype=F32)
    diff = n[:, None] - n[None, :]
    dmask = jnp.where(diff >= 0, jnp.exp(jnp.maximum(diff, 0.0) * log_gamma[:, None, None]), 0.0)
    xi = jnp.exp((n + 1.0) * log_gamma[:, None])
    zeta = jnp.exp((C - 1.0 - n) * log_gamma[:, None])
    chunk_decay = jnp.exp(C * log_gamma)
    dm = dmask * kscale
    xi_t = jnp.broadcast_to(xi[:, :, None], (RET_HEADS, C, RET_V_DIM))
    zeta_t = jnp.broadcast_to(zeta[:, :, None] * kscale, (RET_HEADS, C, RET_QK_DIM))
    decay_t = jnp.broadcast_to(chunk_decay[:, None, None], (RET_HEADS, RET_QK_DIM, RET_V_DIM))
    return dm, xi_t, zeta_t, decay_t


def _ret_kernel(q_ref, k_ref, v_ref, g_ref, cc_ref, ss_ref, dm_ref, xi_ref, zeta_ref, decay_ref,
                gn_ref, o_ref, state_ref):
    S = q_ref.shape[1]
    C = RET_CHUNK
    half = RET_QK_DIM // 2
    state_ref[...] = jnp.zeros_like(state_ref)

    def prep(c):
        r0 = pl.multiple_of(c * C, C)
        rows = pl.ds(r0, C)
        cc = cc_ref[0, rows, :]
        ss = ss_ref[0, rows, :]
        q = q_ref[0, rows, :].astype(F32)
        k = k_ref[0, rows, :].astype(F32)
        qr = q * cc + pltpu.roll(q, half, 1) * ss
        kr = k * cc + pltpu.roll(k, half, 1) * ss
        kzT = (kr * zeta_ref[0]).T.astype(BF16)
        return rows, qr.astype(BF16), kr.astype(BF16), v_ref[0, rows, :], kzT

    def finish(rows, inner, cross):
        o = inner + cross * xi_ref[0]
        mu = jnp.mean(o, axis=-1, keepdims=True)
        d = o - mu
        var = jnp.mean(d * d, axis=-1, keepdims=True)
        on = d * lax.rsqrt(var + LN_EPS) * gn_ref[...]
        g = g_ref[0, rows, :].astype(F32)
        o_ref[0, rows, :] = (g * jax.nn.sigmoid(g) * on).astype(o_ref.dtype)

    def chunk_group(t, carry):
        nt = (((1,), (1,)), ((), ()))
        ps = [prep(RET_GROUP * t + r) for r in range(RET_GROUP)]
        qks = [lax.dot_general(p[1], p[2], nt, preferred_element_type=F32) for p in ps]
        dss = [jnp.dot(p[4], p[3], preferred_element_type=F32) for p in ps]
        state = state_ref[...]
        crosses = []
        for p, ds in zip(ps, dss):
            crosses.append(jnp.dot(p[1], state.astype(BF16), preferred_element_type=F32))
            state = state * decay_ref[0] + ds
        state_ref[...] = state
        inners = [jnp.dot((qk * dm_ref[0]).astype(BF16), p[3], preferred_element_type=F32) for qk, p in zip(qks, ps)]
        for p, inner, cross in zip(ps, inners, crosses):
            finish(p[0], inner, cross)
        return carry

    lax.fori_loop(0, S // (RET_GROUP * C), chunk_group, 0)


def _retention(proj, cc, ss, tables, gn_g):
    B, S, _ = proj.shape
    dk, dv, H, C = RET_QK_DIM, RET_V_DIM, RET_HEADS, RET_CHUNK
    dm, xi_t, zeta_t, decay_t = tables
    return pl.pallas_call(
        _ret_kernel,
        grid=(B, H),
        in_specs=[
            pl.BlockSpec((1, S, dk), lambda b, h: (b, 0, h)),
            pl.BlockSpec((1, S, dk), lambda b, h: (b, 0, H + h)),
            pl.BlockSpec((1, S, dv), lambda b, h: (b, 0, H + h)),
            pl.BlockSpec((1, S, dv), lambda b, h: (b, 0, 2 * H + h)),
            pl.BlockSpec((1, S, dk), lambda b, h: (b, 0, 0)),
            pl.BlockSpec((1, S, dk), lambda b, h: (b, 0, 0)),
            pl.BlockSpec((1, C, C), lambda b, h: (h, 0, 0)),
            pl.BlockSpec((1, C, dv), lambda b, h: (h, 0, 0)),
            pl.BlockSpec((1, C, dk), lambda b, h: (h, 0, 0)),
            pl.BlockSpec((1, dk, dv), lambda b, h: (h, 0, 0)),
            pl.BlockSpec((1, dv), lambda b, h: (0, h)),
        ],
        out_specs=pl.BlockSpec((1, S, dv), lambda b, h: (b, 0, h)),
        out_shape=jax.ShapeDtypeStruct((B, S, H * dv), BF16),
        scratch_shapes=[pltpu.VMEM((dk, dv), F32)],
        compiler_params=_cparams(("arbitrary", "arbitrary")),
        name="retention",
    )(proj, proj, proj, proj, cc, ss, dm, xi_t, zeta_t, decay_t, gn_g.reshape(1, H * dv))


def _store_slabs(ref, val):
    tm = val.shape[0]
    for j in range(ROW_SLAB):
        ref[pl.ds(j, tm, stride=ROW_SLAB), :] = val[:, j * LANES:(j + 1) * LANES]


def _load_slab_chunks(ref, tm):
    return [ref[pl.ds(j, tm, stride=ROW_SLAB), :] for j in range(ROW_SLAB)]


def _top2_flags(vals):
    n = len(vals)
    rank = []
    for k in range(n):
        r = jnp.zeros_like(vals[0])
        for j in range(n):
            if j < k:
                r = r + (vals[j] >= vals[k]).astype(F32)
            elif j > k:
                r = r + (vals[j] > vals[k]).astype(F32)
        rank.append(r)
    is1 = [r == 0.0 for r in rank]
    is2 = [r == 1.0 for r in rank]
    m1 = sum(jnp.where(f, v, 0.0) for f, v in zip(is1, vals))
    m2 = sum(jnp.where(f, v, 0.0) for f, v in zip(is2, vals))
    return is1, is2, m1, m2


def _router_kernel(x_ref, sc_ref, sh_ref, rw_ref, rb_ref, lt_ref,
                   bucket_ref, rank_ref, counts_ref, carry_ref):
    i = pl.program_id(0)
    tm = x_ref.shape[0]

    @pl.when(i == 0)
    def _():
        carry_ref[...] = jnp.zeros_like(carry_ref)

    h = x_ref[...] * (1.0 + sc_ref[0]) + sh_ref[0]
    hh, hl = _split_bf16(h)
    wh, wl = _split_bf16(rw_ref[...])
    nt = (((1,), (1,)), ((), ()))
    logits = lax.dot_general(wh, hh, nt, preferred_element_type=F32)
    logits += lax.dot_general(wl, hh, nt, preferred_element_type=F32)
    logits += lax.dot_general(wh, hl, nt, preferred_element_type=F32)
    logits = logits + rb_ref[:, 0:1]

    rows = [logits[e:e + 1, :] for e in range(N_EXPERTS)]
    gmax = functools.reduce(jnp.maximum, rows)
    groups = []
    for g in range(N_GROUPS):
        vals = rows[g * EXPERTS_PER_GROUP:(g + 1) * EXPERTS_PER_GROUP]
        is1, is2, m1, m2 = _top2_flags(vals)
        score = jnp.exp(m1 - gmax) + jnp.exp(m2 - gmax)
        groups.append(([jnp.logical_or(a, b) for a, b in zip(is1, is2)], score))
    best = groups[0][1]
    gsel = jnp.zeros_like(best)
    for g in range(1, N_GROUPS):
        better = groups[g][1] > best
        best = jnp.where(better, groups[g][1], best)
        gsel = jnp.where(better, float(g), gsel)
    sel = [jnp.zeros_like(best) > 1.0] * EXPERTS_PER_GROUP
    for g in range(N_GROUPS):
        here = gsel == float(g)
        sel = [jnp.logical_or(s, jnp.logical_and(here, f)) for s, f in zip(sel, groups[g][0])]
    pair = jnp.where(sel[0],
                     jnp.where(sel[1], 0.0, jnp.where(sel[2], 1.0, 2.0)),
                     jnp.where(sel[1], jnp.where(sel[2], 3.0, 4.0), 5.0))
    bucket = (gsel * float(PAIRS_PER_GROUP) + pair).astype(jnp.int32)

    brow = lax.broadcasted_iota(jnp.int32, (BUCKET_ROWS, tm), 0)
    onehot = (brow == bucket).astype(F32)
    before = jnp.dot(onehot.astype(BF16), lt_ref[...], preferred_element_type=F32)
    carry = carry_ref[...]
    in_tile = jnp.sum(onehot * before, axis=0, keepdims=True)
    base = jnp.sum(onehot * carry[:, 0:1], axis=0, keepdims=True)
    rank_ref[...] = (base + in_tile).astype(jnp.int32)
    bucket_ref[...] = bucket
    new_carry = carry + jnp.sum(onehot, axis=1, keepdims=True)
    carry_ref[...] = new_carry
    counts_ref[...] = new_carry.astype(jnp.int32)


def _router(x2, mods, k, B, S, router_wT, router_bt, lt):
    T, D = x2.shape
    tm = ROUTE_TM
    per_b = S // tm
    return pl.pallas_call(
        _router_kernel,
        grid=(T // tm,),
        in_specs=[
            pl.BlockSpec((tm, D), lambda i: (i, 0)),
            pl.BlockSpec((1, 1, D), lambda i: (_mod_row(k, i // per_b, SCALE, B), 0, 0)),
            pl.BlockSpec((1, 1, D), lambda i: (_mod_row(k, i // per_b, SHIFT, B), 0, 0)),
            pl.BlockSpec((N_EXPERTS, D), lambda i: (0, 0)),
            pl.BlockSpec((N_EXPERTS, LANES), lambda i: (0, 0)),
            pl.BlockSpec((tm, tm), lambda i: (0, 0)),
        ],
        out_specs=[
            pl.BlockSpec((1, tm), lambda i: (0, i)),
            pl.BlockSpec((1, tm), lambda i: (0, i)),
            pl.BlockSpec((BUCKET_ROWS, LANES), lambda i: (0, 0)),
        ],
        out_shape=[
            jax.ShapeDtypeStruct((1, T), jnp.int32),
            jax.ShapeDtypeStruct((1, T), jnp.int32),
            jax.ShapeDtypeStruct((BUCKET_ROWS, LANES), jnp.int32),
        ],
        scratch_shapes=[pltpu.VMEM((BUCKET_ROWS, LANES), F32)],
        compiler_params=_cparams(("arbitrary",)),
        name="router",
    )(x2, mods, mods, router_wT, router_bt, lt)


def _slab(ref, r):
    return ref.at[pl.ds(pl.multiple_of(r * ROW_SLAB, ROW_SLAB), ROW_SLAB)]


def _wait_rows(src_ref, dst_ref, sem, n):
    def body(c, carry):
        for _ in range(DMA_CHUNK):
            pltpu.make_async_copy(_slab(src_ref, 0), _slab(dst_ref, 0), sem).wait()
        return carry
    lax.fori_loop(0, n // DMA_CHUNK, body, 0)


def _scatter_kernel(dest_ref, pad_lo_ref, pad_hi_ref, x_ref, sc_ref, sh_ref, xs_ref, hbuf, sems):
    i = pl.program_id(0)
    last = pl.num_programs(0) - 1
    tm = x_ref.shape[0]
    h = x_ref[...] * (1.0 + sc_ref[0]) + sh_ref[0]

    for s in range(2):
        @pl.when(lax.rem(i, 2) == s)
        def _(s=s):
            buf = hbuf.at[s]
            _store_slabs(buf, h)

            def issue(c, carry):
                for u in range(DMA_CHUNK):
                    r = c * DMA_CHUNK + u
                    pltpu.make_async_copy(_slab(buf, r), _slab(xs_ref, dest_ref[i * tm + r]), sems.at[s]).start(priority=u % 2)
                return carry
            lax.fori_loop(0, tm // DMA_CHUNK, issue, 0)

            @pl.when(i > 0)
            def _():
                _wait_rows(hbuf.at[1 - s], xs_ref, sems.at[1 - s], tm)

            @pl.when(i == last)
            def _():
                _wait_rows(buf, xs_ref, sems.at[s], tm)

                def filler(row0, n_rows, start):
                    cp = pltpu.make_async_copy(
                        buf.at[pl.ds(0, n_rows * ROW_SLAB)],
                        xs_ref.at[pl.ds(pl.multiple_of(row0 * ROW_SLAB, ROW_SLAB), n_rows * ROW_SLAB)],
                        sems.at[s])
                    if start:
                        cp.start()
                    else:
                        cp.wait()

                def fill(start):
                    def bucket(b, carry):
                        pos = pad_lo_ref[b]
                        n = pad_hi_ref[b] - pos
                        for bit in reversed(range(MOE_TM.bit_length() - 1)):
                            take = lax.rem(lax.shift_right_logical(n, bit), 2)

                            @pl.when(take == 1)
                            def _(pos=pos, bit=bit):
                                filler(pos, 1 << bit, start)
                            pos = pos + take * (1 << bit)
                        return carry
                    lax.fori_loop(0, N_BUCKETS, bucket, 0)

                    def tail(t, carry):
                        filler(t * MOE_TM, MOE_TM, start)
                        return carry
                    n_tiles = xs_ref.shape[0] // (MOE_TM * ROW_SLAB)
                    lax.fori_loop(pad_hi_ref[N_BUCKETS - 1] // MOE_TM, n_tiles, tail, 0)
                fill(True)
                fill(False)


def _scatter_rows(x2, mods, k, B, S, plan, n_out):
    T, D = x2.shape
    tm = PROJ_TM
    per_b = S // tm
    return pl.pallas_call(
        _scatter_kernel,
        grid_spec=pltpu.PrefetchScalarGridSpec(
            num_scalar_prefetch=3,
            grid=(T // tm,),
            in_specs=[
                pl.BlockSpec((tm, D), lambda i, *_: (i, 0)),
                pl.BlockSpec((1, 1, D), lambda i, *_: (_mod_row(k, i // per_b, SCALE, B), 0, 0)),
                pl.BlockSpec((1, 1, D), lambda i, *_: (_mod_row(k, i // per_b, SHIFT, B), 0, 0)),
            ],
            out_specs=pl.BlockSpec(memory_space=pl.ANY),
            scratch_shapes=[pltpu.VMEM((2, tm * ROW_SLAB, LANES), F32), pltpu.SemaphoreType.DMA((2,))],
        ),
        out_shape=jax.ShapeDtypeStruct((n_out * ROW_SLAB, LANES), F32),
        compiler_params=_cparams(("arbitrary",)),
        name="scatter_rows",
    )(plan["dest"], plan["pad_lo"], plan["pad_hi"], x2, mods, mods)


def _expert(x, wg, wu, wd):
    gte = jnp.dot(x, wg, preferred_element_type=F32)
    up = jnp.dot(x, wu, preferred_element_type=F32)
    hid = gte * jax.nn.sigmoid(gte) * up
    return jnp.dot(hid.astype(BF16), wd, preferred_element_type=F32)


def _router_logit(chunks, rw_ref, rb_ref, e):
    w_row = rw_ref[pl.ds(e, 1), :]
    acc = chunks[0] * w_row[:, 0:LANES]
    for j in range(1, ROW_SLAB):
        acc = acc + chunks[j] * w_row[:, j * LANES:(j + 1) * LANES]
    return jnp.sum(acc, axis=-1, keepdims=True) + rb_ref[pl.ds(e, 1), 0:1]


def _moe_kernel(src_ref, ea_ref, eb_ref, valid_ref,
                xs_ref, rw_ref, rb_ref, wga_ref, wua_ref, wda_ref, wgb_ref, wub_ref, wdb_ref, o_ref,
                wg_bf, wu_bf, wd_bf):
    i = pl.program_id(0)
    prev = jnp.maximum(i - 1, 0)

    for slot, e_ref, refs in ((0, ea_ref, (wga_ref, wua_ref, wda_ref)), (1, eb_ref, (wgb_ref, wub_ref, wdb_ref))):
        @pl.when(jnp.logical_or(i == 0, e_ref[i] != e_ref[prev]))
        def _(slot=slot, refs=refs):
            wg_bf[slot] = refs[0][0, 0].astype(BF16)
            wu_bf[slot] = refs[1][0, 0].astype(BF16)
            wd_bf[slot] = refs[2][0, 0].astype(BF16)

    @pl.when(valid_ref[i] == 1)
    def _():
        chunks = _load_slab_chunks(xs_ref, MOE_TM)
        l_a = _router_logit(chunks, rw_ref, rb_ref, ea_ref[i])
        l_b = _router_logit(chunks, rw_ref, rb_ref, eb_ref[i])
        w_a = jax.nn.sigmoid(l_a - l_b)
        w_b = jax.nn.sigmoid(l_b - l_a)
        x = jnp.concatenate([c.astype(BF16) for c in chunks], axis=-1)
        y = (w_a * _expert(x, wg_bf[0], wu_bf[0], wd_bf[0])
             + w_b * _expert(x, wg_bf[1], wu_bf[1], wd_bf[1]))
        _store_slabs(o_ref, y)

    @pl.when(valid_ref[i] == 0)
    def _():
        o_ref[...] = jnp.zeros_like(o_ref)


def _grouped_experts(xs, plan, router_wT, router_bt, w_gate, w_up, w_down, layer):
    n_tiles = xs.shape[0] // (MOE_TM * ROW_SLAB)
    D, Fd = D_MODEL, D_EXPERT
    wspec_a = lambda shape: pl.BlockSpec((1, 1) + shape, lambda i, src, ea, eb, va: (layer, ea[i], 0, 0))
    wspec_b = lambda shape: pl.BlockSpec((1, 1) + shape, lambda i, src, ea, eb, va: (layer, eb[i], 0, 0))
    row_spec = pl.BlockSpec((MOE_TM * ROW_SLAB, LANES), lambda i, src, ea, eb, va: (src[i], 0))
    return pl.pallas_call(
        _moe_kernel,
        grid_spec=pltpu.PrefetchScalarGridSpec(
            num_scalar_prefetch=4,
            grid=(n_tiles,),
            in_specs=[
                row_spec,
                pl.BlockSpec((N_EXPERTS, D), lambda i, src, ea, eb, va: (0, 0)),
                pl.BlockSpec((N_EXPERTS, LANES), lambda i, src, ea, eb, va: (0, 0)),
                wspec_a((D, Fd)), wspec_a((D, Fd)), wspec_a((Fd, D)),
                wspec_b((D, Fd)), wspec_b((D, Fd)), wspec_b((Fd, D)),
            ],
            out_specs=pl.BlockSpec((MOE_TM * ROW_SLAB, LANES), lambda i, src, ea, eb, va: (i, 0)),
            scratch_shapes=[pltpu.VMEM((2, D, Fd), BF16), pltpu.VMEM((2, D, Fd), BF16), pltpu.VMEM((2, Fd, D), BF16)],
        ),
        out_shape=jax.ShapeDtypeStruct(xs.shape, F32),
        compiler_params=_cparams(("arbitrary",)),
        name="grouped_experts",
    )(plan["tile_src"], plan["tile_ea"], plan["tile_eb"], plan["tile_valid"],
      xs, router_wT, router_bt, w_gate, w_up, w_down, w_gate, w_up, w_down)


def _moe_plan(bucket, rank, counts, n_tiles):
    cnt = counts[:N_BUCKETS, 0]
    tiles_b = (cnt + MOE_TM - 1) // MOE_TM
    tile_end = jnp.cumsum(tiles_b)
    tile_start = tile_end - tiles_b
    row_start = tile_start * MOE_TM
    dest = jnp.take(row_start, bucket[0]) + rank[0]
    n_used = tile_end[-1]
    t = jnp.arange(n_tiles, dtype=jnp.int32)
    t_c = jnp.minimum(t, n_used - 1)
    tb = jnp.sum((t_c[:, None] >= tile_end[None, :]).astype(jnp.int32), axis=1)
    pair_lo = jnp.array([0, 0, 0, 1, 1, 2], jnp.int32)
    pair_hi = jnp.array([1, 2, 3, 2, 3, 3], jnp.int32)
    grp = tb // PAIRS_PER_GROUP
    pr = tb % PAIRS_PER_GROUP
    return {
        "dest": dest.astype(jnp.int32),
        "pad_lo": (row_start + cnt).astype(jnp.int32),
        "pad_hi": (tile_end * MOE_TM).astype(jnp.int32),
        "tile_src": t_c.astype(jnp.int32),
        "tile_ea": (grp * EXPERTS_PER_GROUP + jnp.take(pair_lo, pr)).astype(jnp.int32),
        "tile_eb": (grp * EXPERTS_PER_GROUP + jnp.take(pair_hi, pr)).astype(jnp.int32),
        "tile_valid": (t < n_used).astype(jnp.int32),
    }


def _resln_kernel(dest_ref, ys_ref, x_ref, gate_ref, g_ref, b_ref, o_ref, ybuf, sems):
    i = pl.program_id(0)
    n = pl.num_programs(0)
    tm = x_ref.shape[0]

    def issue(tile, s):
        def body(c, carry):
            for u in range(DMA_CHUNK):
                r = c * DMA_CHUNK + u
                pltpu.make_async_copy(_slab(ys_ref, dest_ref[tile * tm + r]), _slab(ybuf.at[s], r), sems.at[s]).start(priority=u % 2)
            return carry
        lax.fori_loop(0, tm // DMA_CHUNK, body, 0)

    @pl.when(i == 0)
    def _():
        issue(0, 0)

    for s in range(2):
        @pl.when(lax.rem(i, 2) == s)
        def _(s=s):
            @pl.when(i + 1 < n)
            def _():
                issue(i + 1, 1 - s)

            _wait_rows(ys_ref, ybuf.at[s], sems.at[s], tm)
            y = jnp.concatenate(_load_slab_chunks(ybuf.at[s], tm), axis=-1)
            u = DEEPNORM_ALPHA * x_ref[...] + (1.0 + gate_ref[0]) * y
            o_ref[...] = _layer_norm_rows(u, g_ref[...], b_ref[...])


def _gather_residual_ln(ys, plan, x2, mods, k, B, S, g, b):
    T, D = x2.shape
    tm = PROJ_TM
    per_b = S // tm
    return pl.pallas_call(
        _resln_kernel,
        grid_spec=pltpu.PrefetchScalarGridSpec(
            num_scalar_prefetch=1,
            grid=(T // tm,),
            in_specs=[
                pl.BlockSpec(memory_space=pl.ANY),
                pl.BlockSpec((tm, D), lambda i, *_: (i, 0)),
                pl.BlockSpec((1, 1, D), lambda i, *_: (_mod_row(k, i // per_b, GATE, B), 0, 0)),
                pl.BlockSpec((1, D), lambda i, *_: (0, 0)),
                pl.BlockSpec((1, D), lambda i, *_: (0, 0)),
            ],
            out_specs=pl.BlockSpec((tm, D), lambda i, *_: (i, 0)),
            scratch_shapes=[pltpu.VMEM((2, tm * ROW_SLAB, LANES), F32), pltpu.SemaphoreType.DMA((2,))],
        ),
        out_shape=jax.ShapeDtypeStruct((T, D), F32),
        compiler_params=_cparams(("arbitrary",)),
        name="gather_residual_ln",
    )(plan["dest"], ys, x2, mods, g.reshape(1, D), b.reshape(1, D))


def _moe_sublayer(x, mods, k, router_wT, router_bt, lt, w_gate, w_up, w_down, layer, g, b):
    B, S, D = x.shape
    T = B * S
    x2 = x.reshape(T, D)
    bucket, rank, counts = _router(x2, mods, k, B, S, router_wT, router_bt, lt)
    n_tiles = T // MOE_TM + N_BUCKETS
    plan = _moe_plan(bucket, rank, counts, n_tiles)
    xs = _scatter_rows(x2, mods, k, B, S, plan, n_tiles * MOE_TM)
    ys = _grouped_experts(xs, plan, router_wT, router_bt, w_gate, w_up, w_down, layer)
    return _gather_residual_ln(ys, plan, x2, mods, k, B, S, g, b).reshape(B, S, D)


def kernel(x, c, positions, ada_w, ada_b, ln_g, ln_b, sb_w_in, sb_w_out, ret_w_in, ret_gn_g, ret_w_out,
           router_w, router_b, moe_w_gate, moe_w_up, moe_w_down):
    mods = _ada_params(c, ada_w, ada_b)
    cc, ss = _rope_tables(positions)
    tables = _retention_decay_tables()
    u = (jnp.arange(SB_TK)[:, None] < jnp.arange(SB_TK)[None, :]).astype(BF16)
    lt = (jnp.arange(ROUTE_TM)[:, None] < jnp.arange(ROUTE_TM)[None, :]).astype(BF16)
    router_wT = router_w.T
    router_bt = jnp.broadcast_to(router_b[:, None], (N_EXPERTS, LANES))
    for i in range(DEPTH):
        k_mix, k_moe = 2 * i, 2 * i + 1
        if i % 2 == 0:
            qkv = _mod_matmul(x, mods, k_mix, sb_w_in, i // 2, tn=3 * D_MODEL // 2)
            a = _sb_attention(qkv, u)
            w_out = sb_w_out
        else:
            proj = _mod_matmul(x, mods, k_mix, ret_w_in, i // 2, tn=2 * D_MODEL)
            a = _retention(proj, cc, ss, tables, ret_gn_g[i // 2])
            w_out = ret_w_out
        x = _out_proj_ln(a, w_out, i // 2, x, mods, k_mix, ln_g[i, 0], ln_b[i, 0])
        x = _moe_sublayer(x, mods, k_moe, router_wT, router_bt, lt, moe_w_gate, moe_w_up, moe_w_down, i,
                          ln_g[i, 1], ln_b[i, 1])
    return x
```

```python
import functools

import jax
import jax.numpy as jnp
from jax import lax
from jax.experimental import pallas as pl
from jax.experimental.pallas import tpu as pltpu

F32 = jnp.float32
BF16 = jnp.bfloat16

D_MODEL = 1024
DEPTH = 4
SB_HEADS = 16
SB_HEAD_DIM = D_MODEL // SB_HEADS
RET_HEADS = 8
RET_QK_DIM = D_MODEL // RET_HEADS
RET_V_DIM = 2 * RET_QK_DIM
ROPE_BASE = 10000.0
N_EXPERTS = 16
N_GROUPS = 4
EXPERTS_PER_GROUP = N_EXPERTS // N_GROUPS
D_EXPERT = D_MODEL // 2
DEEPNORM_ALPHA = (2 * DEPTH) ** 0.25
LN_EPS = 1e-5
LOG2_E = 1.4426950408889634

LANES = 128
SUBLANES = 8
MXU_DIM = 256
VMEM_LIMIT_BYTES = 56 * 1024 * 1024

PROJ_TM = 512
MM_TM = 1024
OUT_SPLIT = 4
SB_TQ = MXU_DIM
SB_TK = MXU_DIM
RET_CHUNK = 256
RET_GROUP = 4
ROUTE_TM = 512
MOE_TM = 256
PAIRS_PER_GROUP = 6
N_BUCKETS = N_GROUPS * PAIRS_PER_GROUP
BUCKET_ROWS = 32
ROW_SLAB = D_MODEL // LANES
DMA_CHUNK = 16
assert ROW_SLAB == SUBLANES


def _cparams(semantics):
    return pltpu.CompilerParams(dimension_semantics=semantics, vmem_limit_bytes=VMEM_LIMIT_BYTES)


def _split_bf16(a):
    hi = a.astype(BF16)
    lo = (a - hi.astype(F32)).astype(BF16)
    return hi, lo


def _ada_kernel(c_ref, w_ref, b_ref, o_ref):
    c = c_ref[...]
    s = c * jax.nn.sigmoid(c)
    sh, sl = _split_bf16(s)
    wh, wl = _split_bf16(w_ref[0])
    acc = jnp.dot(sh, wh, preferred_element_type=F32)
    acc += jnp.dot(sl, wh, preferred_element_type=F32)
    acc += jnp.dot(sh, wl, preferred_element_type=F32)
    o_ref[0] = acc + b_ref[0]


def _ada_params(c, ada_w, ada_b):
    B, D = c.shape
    n_sub = ada_w.shape[0] * ada_w.shape[1]
    N = ada_w.shape[-1]
    tn = N // 2
    w = ada_w.reshape(n_sub, D, N)
    b = ada_b.reshape(n_sub, 1, N)
    out = pl.pallas_call(
        _ada_kernel,
        grid=(n_sub, N // tn),
        in_specs=[
            pl.BlockSpec((B, D), lambda k, j: (0, 0)),
            pl.BlockSpec((1, D, tn), lambda k, j: (k, 0, j)),
            pl.BlockSpec((1, 1, tn), lambda k, j: (k, 0, j)),
        ],
        out_specs=pl.BlockSpec((1, B, tn), lambda k, j: (k, 0, j)),
        out_shape=jax.ShapeDtypeStruct((n_sub, B, N), F32),
        compiler_params=_cparams(("arbitrary", "arbitrary")),
        name="ada_params",
    )(c, w, b)
    return out.reshape(n_sub * B * 3, 1, D)


def _mod_row(k, b, which, B):
    return (k * B + b) * 3 + which


SHIFT, SCALE, GATE = 0, 1, 2


def _modmm_kernel(x_ref, sc_ref, sh_ref, w_ref, o_ref, w_bf):
    @pl.when(jnp.logical_and(pl.program_id(1) == 0, pl.program_id(2) == 0))
    def _():
        w_bf[...] = w_ref[0].astype(BF16)

    h = x_ref[0] * (1.0 + sc_ref[0]) + sh_ref[0]
    o_ref[0] = jnp.dot(h.astype(BF16), w_bf[...], preferred_element_type=F32).astype(o_ref.dtype)


def _mod_matmul(x, mods, k, w_all, layer, tn):
    B, S, D = x.shape
    N = w_all.shape[2]
    tm = MM_TM
    return pl.pallas_call(
        _modmm_kernel,
        grid=(N // tn, B, S // tm),
        in_specs=[
            pl.BlockSpec((1, tm, D), lambda n, b, i: (b, i, 0)),
            pl.BlockSpec((1, 1, D), lambda n, b, i: (_mod_row(k, b, SCALE, B), 0, 0)),
            pl.BlockSpec((1, 1, D), lambda n, b, i: (_mod_row(k, b, SHIFT, B), 0, 0)),
            pl.BlockSpec((1, D, tn), lambda n, b, i: (layer, 0, n)),
        ],
        out_specs=pl.BlockSpec((1, tm, tn), lambda n, b, i: (b, i, n)),
        out_shape=jax.ShapeDtypeStruct((B, S, N), BF16),
        scratch_shapes=[pltpu.VMEM((D, tn), BF16)],
        compiler_params=_cparams(("arbitrary", "arbitrary", "arbitrary")),
        name="mod_matmul",
    )(x, mods, mods, w_all)


SB_PIPE = 4
SB_UNROLL = 4


def _sb_tile_tables(nq):
    diag = [(i, i) for i in range(nq)]
    off = [(i, i - o) for o in range(1, nq) for i in range(o, nq)]
    ti, tj, spans = [], [], []
    for tiles in (diag, off):
        padded = [(nq, 0)] * SB_PIPE + tiles + [(nq, 0)] * (SB_PIPE + SB_UNROLL)
        spans.append((len(ti), len(tiles)))
        ti += [t[0] for t in padded]
        tj += [t[1] for t in padded]
    return ti, tj, spans


def _sb_kernel(ti_ref, tj_ref, q_ref, k_ref, v_ref, u_ref, o_ref,
               vT_ref, qT_ref, z_ref, sp_ref, d_ref, a_ref, c_ref, run_ref, acc_ref, alive_ref, *, spans):
    S = q_ref.shape[1]
    tq, tk, dh = SB_TQ, SB_TK, SB_HEAD_DIM
    nq = S // tq

    for c in range(S // tk):
        v_c = v_ref[0, c * tk:(c + 1) * tk, :].astype(F32)
        vT_ref[c] = v_c.T.astype(BF16)

    scale = dh ** -0.5 * LOG2_E
    head_rows = lax.broadcasted_iota(jnp.int32, (2 * dh, tq), 0)
    for i in range(nq):
        qT = (q_ref[0, i * tq:(i + 1) * tq, :].astype(F32) * scale).T
        qT_ref[i, 0] = jnp.where(head_rows < dh, qT, 0.0).astype(BF16)
        qT_ref[i, 1] = jnp.where(head_rows >= dh, qT, 0.0).astype(BF16)
    qT_ref[nq] = jnp.zeros_like(qT_ref[nq])
    z_ref[...] = jnp.zeros_like(z_ref)
    sp_ref[...] = jnp.zeros_like(sp_ref)
    d_ref[...] = jnp.zeros_like(d_ref)
    a_ref[...] = jnp.zeros_like(a_ref)
    c_ref[...] = jnp.zeros_like(c_ref)
    for i in range(nq + 1):
        alive_ref[i] = jnp.int32(1)

    def iteration(t1, t3, t4, p, masked):
        q = 1 - p
        i1, j1 = ti_ref[t1], tj_ref[t1]
        i3 = ti_ref[t3]
        i4, j4 = ti_ref[t4], tj_ref[t4]
        if masked:
            rows = lax.broadcasted_iota(jnp.int32, (tk, tq), 0)
            cols = lax.broadcasted_iota(jnp.int32, (tk, tq), 1)
            valid = rows < cols
        u = u_ref[...]
        suf = [jnp.dot(u, sp_ref[q, h], preferred_element_type=F32) for h in range(2)]
        vT = vT_ref[j4]
        pv = [jnp.dot(vT[h * dh:(h + 1) * dh, :], a_ref[p, h], preferred_element_type=F32) for h in range(2)]
        k_blk = k_ref[0, pl.ds(pl.multiple_of(j1 * tk, tk), tk), :]
        z_new = [jnp.dot(k_blk, qT_ref[i1, h], preferred_element_type=F32) for h in range(2)]
        for h in range(2):
            z = z_ref[p, h]
            sp = jnp.maximum(z, 0.0) + jnp.log2(1.0 + jnp.exp2(-jnp.abs(z)))
            d_ref[p, h] = z - sp
            if masked:
                sp = jnp.where(valid, sp, 0.0)
            sp_ref[p, h] = sp.astype(BF16)
        for h in range(2):
            a = jnp.exp2(d_ref[q, h] - suf[h])
            if masked:
                a = jnp.where(valid, a, 0.0)
            a_ref[q, h] = a.astype(BF16)
            tile_mass = suf[h][0:1, :] + sp_ref[q, h, 0:1, :].astype(F32)
            if masked:
                run_ref[i3, h, 0:1, :] = tile_mass
            else:
                run = run_ref[i3, h, 0:1, :]
                c_ref[q, h, 0:1, :] = jnp.exp2(-run)
                run_ref[i3, h, 0:1, :] = run + tile_mass
                c_next = jnp.exp2(-(run + tile_mass))
                c_max = c_next if h == 0 else jnp.maximum(c_max, c_next)
        if not masked:
            live = jnp.logical_or(jnp.max(c_max) > 0.0, i3 == nq)
            alive_ref[i3] = live.astype(jnp.int32)
        for h in range(2):
            if masked:
                acc_ref[i4, h] = pv[h]
            else:
                acc_ref[i4, h] += pv[h] * c_ref[p, h, 0:1, :]
            z_ref[p, h] = z_new[h]

    base, n_real = spans[0]
    n_iter = n_real + SB_PIPE
    n_iter += -n_iter % SB_UNROLL

    def diag_body(t, carry):
        for r in range(SB_UNROLL):
            t1 = base + SB_PIPE + SB_UNROLL * t + r
            iteration(t1, t1 - 3, t1 - 4, r % 2, True)
        return carry
    lax.fori_loop(0, n_iter // SB_UNROLL, diag_body, 0)

    base, n_real = spans[1]
    first = base + SB_PIPE
    end = first + n_real

    def next_live(t):
        return lax.while_loop(lambda u: alive_ref[ti_ref[u]] == 0, lambda u: u + 1, t)

    def off_body(carry):
        ptr, h1, h2, h3, h4 = carry
        picks = []
        for r in range(SB_UNROLL):
            t1 = next_live(ptr)
            picks.append(t1)
            ptr = t1 + 1
        for r, t1 in enumerate(picks):
            iteration(t1, h3, h4, r % 2, False)
            h1, h2, h3, h4 = t1, h1, h2, h3
        return ptr, h1, h2, h3, h4

    lax.while_loop(lambda carry: carry[0] < end + SB_PIPE, off_body,
                   (jnp.int32(first), jnp.int32(first - 1), jnp.int32(first - 2), jnp.int32(first - 3), jnp.int32(first - 4)))

    for i in range(nq):
        oT = jnp.concatenate([acc_ref[i, 0], acc_ref[i, 1]], axis=0)
        o_ref[0, i * tq:(i + 1) * tq, :] = oT.T.astype(o_ref.dtype)


def _sb_attention(qkv, u):
    B, S, _ = qkv.shape
    n_pairs = SB_HEADS // 2
    nq = S // SB_TQ
    blk = (1, S, 2 * SB_HEAD_DIM)
    ti, tj, spans = _sb_tile_tables(nq)
    tile = (SB_TK, SB_TQ)
    return pl.pallas_call(
        functools.partial(_sb_kernel, spans=spans),
        grid_spec=pltpu.PrefetchScalarGridSpec(
            num_scalar_prefetch=2,
            grid=(B, n_pairs),
            in_specs=[
                pl.BlockSpec(blk, lambda b, p, *_: (b, 0, p)),
                pl.BlockSpec(blk, lambda b, p, *_: (b, 0, n_pairs + p)),
                pl.BlockSpec(blk, lambda b, p, *_: (b, 0, 2 * n_pairs + p)),
                pl.BlockSpec((SB_TK, SB_TK), lambda b, p, *_: (0, 0)),
            ],
            out_specs=pl.BlockSpec(blk, lambda b, p, *_: (b, 0, p)),
            scratch_shapes=[
                pltpu.VMEM((S // SB_TK, 2 * SB_HEAD_DIM, SB_TK), BF16),
                pltpu.VMEM((nq + 1, 2, 2 * SB_HEAD_DIM, SB_TQ), BF16),
                pltpu.VMEM((2, 2) + tile, F32),
                pltpu.VMEM((2, 2) + tile, BF16),
                pltpu.VMEM((2, 2) + tile, F32),
                pltpu.VMEM((2, 2) + tile, BF16),
                pltpu.VMEM((2, 2, SUBLANES, SB_TQ), F32),
                pltpu.VMEM((nq + 1, 2, SUBLANES, SB_TQ), F32),
                pltpu.VMEM((nq + 1, 2, SB_HEAD_DIM, SB_TQ), F32),
                pltpu.SMEM((nq + 1,), jnp.int32),
            ],
        ),
        out_shape=jax.ShapeDtypeStruct((B, S, D_MODEL), BF16),
        compiler_params=_cparams(("arbitrary", "arbitrary")),
        name="sb_attention",
    )(jnp.asarray(ti, jnp.int32), jnp.asarray(tj, jnp.int32), qkv, qkv, qkv, u)


def _layer_norm_rows(u, g, b):
    mu = jnp.mean(u, axis=-1, keepdims=True)
    d = u - mu
    var = jnp.mean(d * d, axis=-1, keepdims=True)
    return d * lax.rsqrt(var + LN_EPS) * g + b


def _outln_kernel(a_ref, w_ref, x_ref, gate_ref, g_ref, b_ref, o_ref, w_bf):
    @pl.when(jnp.logical_and(pl.program_id(0) == 0, pl.program_id(1) == 0))
    def _():
        w_bf[...] = w_ref[0].astype(BF16)

    tm = a_ref.shape[1]
    sub = tm // OUT_SPLIT
    ys = [jnp.dot(a_ref[0, r * sub:(r + 1) * sub, :], w_bf[...], preferred_element_type=F32) for r in range(OUT_SPLIT)]
    for r in range(OUT_SPLIT):
        rows = slice(r * sub, (r + 1) * sub)
        u = DEEPNORM_ALPHA * x_ref[0, rows, :] + (1.0 + gate_ref[0]) * ys[r]
        o_ref[0, rows, :] = _layer_norm_rows(u, g_ref[...], b_ref[...])


def _out_proj_ln(a, w_all, layer, x, mods, k, g, b):
    B, S, D = x.shape
    K = a.shape[-1]
    tm = PROJ_TM
    return pl.pallas_call(
        _outln_kernel,
        grid=(B, S // tm),
        in_specs=[
            pl.BlockSpec((1, tm, K), lambda bb, i: (bb, i, 0)),
            pl.BlockSpec((1, K, D), lambda bb, i: (layer, 0, 0)),
            pl.BlockSpec((1, tm, D), lambda bb, i: (bb, i, 0)),
            pl.BlockSpec((1, 1, D), lambda bb, i: (_mod_row(k, bb, GATE, B), 0, 0)),
            pl.BlockSpec((1, D), lambda bb, i: (0, 0)),
            pl.BlockSpec((1, D), lambda bb, i: (0, 0)),
        ],
        out_specs=pl.BlockSpec((1, tm, D), lambda bb, i: (bb, i, 0)),
        out_shape=jax.ShapeDtypeStruct((B, S, D), F32),
        scratch_shapes=[pltpu.VMEM((K, D), BF16)],
        compiler_params=_cparams(("arbitrary", "arbitrary")),
        name="out_proj_ln",
    )(a, w_all, x, mods, g.reshape(1, D), b.reshape(1, D))


def _rope_kernel(pos_ref, invf_ref, sign_ref, cc_ref, ss_ref):
    ang = pos_ref[0].astype(F32) * invf_ref[...]
    cc_ref[0] = jnp.cos(ang)
    ss_ref[0] = jnp.sin(ang) * sign_ref[...]


def _rope_tables(positions):
    B, S = positions.shape
    half = RET_QK_DIM // 2
    inv_freq = 1.0 / (ROPE_BASE ** (jnp.arange(0, RET_QK_DIM, 2, dtype=F32) / RET_QK_DIM))
    invf2 = jnp.concatenate([inv_freq, inv_freq]).reshape(1, RET_QK_DIM)
    sign = jnp.concatenate([-jnp.ones((half,), F32), jnp.ones((half,), F32)]).reshape(1, RET_QK_DIM)
    ts = PROJ_TM
    shp = jax.ShapeDtypeStruct((B, S, RET_QK_DIM), F32)
    return pl.pallas_call(
        _rope_kernel,
        grid=(B, S // ts),
        in_specs=[
            pl.BlockSpec((1, ts, 1), lambda b, i: (b, i, 0)),
            pl.BlockSpec((1, RET_QK_DIM), lambda b, i: (0, 0)),
            pl.BlockSpec((1, RET_QK_DIM), lambda b, i: (0, 0)),
        ],
        out_specs=[pl.BlockSpec((1, ts, RET_QK_DIM), lambda b, i: (b, i, 0))] * 2,
        out_shape=[shp, shp],
        compiler_params=_cparams(("arbitrary", "arbitrary")),
        name="rope_tables",
    )(positions.reshape(B, S, 1), invf2, sign)


def _retention_decay_tables():
    C = RET_CHUNK
    kscale = RET_QK_DIM ** -0.5
    log_gamma = jnp.log(1.0 - jnp.exp2(-5.0 - jnp.arange(RET_HEADS, dtype=F32)))
    n = jnp.arange(C, dtype=F32)
    diff = n[:, None] - n[None, :]
    dmask = jnp.where(diff >= 0, jnp.exp(jnp.maximum(diff, 0.0) * log_gamma[:, None, None]), 0.0)
    xi = jnp.exp((n + 1.0) * log_gamma[:, None])
    zeta = jnp.exp((C - 1.0 - n) * log_gamma[:, None])
    chunk_decay = jnp.exp(C * log_gamma)
    dm = dmask * kscale
    xi_t = jnp.broadcast_to(xi[:, :, None], (RET_HEADS, C, RET_V_DIM))
    zeta_t = jnp.broadcast_to(zeta[:, :, None] * kscale, (RET_HEADS, C, RET_QK_DIM))
    decay_t = jnp.broadcast_to(chunk_decay[:, None, None], (RET_HEADS, RET_QK_DIM, RET_V_DIM))
    return dm, xi_t, zeta_t, decay_t


def _ret_kernel(q_ref, k_ref, v_ref, g_ref, cc_ref, ss_ref, dm_ref, xi_ref, zeta_ref, decay_ref,
                gn_ref, o_ref, state_ref):
    S = q_ref.shape[1]
    C = RET_CHUNK
    half = RET_QK_DIM // 2
    state_ref[...] = jnp.zeros_like(state_ref)

    def prep(c):
        r0 = pl.multiple_of(c * C, C)
        rows = pl.ds(r0, C)
        cc = cc_ref[0, rows, :]
        ss = ss_ref[0, rows, :]
        q = q_ref[0, rows, :].astype(F32)
        k = k_ref[0, rows, :].astype(F32)
        qr = q * cc + pltpu.roll(q, half, 1) * ss
        kr = k * cc + pltpu.roll(k, half, 1) * ss
        kzT = (kr * zeta_ref[0]).T.astype(BF16)
        return rows, qr.astype(BF16), kr.astype(BF16), v_ref[0, rows, :], kzT

    def finish(rows, inner, cross):
        o = inner + cross * xi_ref[0]
        mu = jnp.mean(o, axis=-1, keepdims=True)
        d = o - mu
        var = jnp.mean(d * d, axis=-1, keepdims=True)
        on = d * lax.rsqrt(var + LN_EPS) * gn_ref[...]
        g = g_ref[0, rows, :].astype(F32)
        o_ref[0, rows, :] = (g * jax.nn.sigmoid(g) * on).astype(o_ref.dtype)

    def chunk_group(t, carry):
        nt = (((1,), (1,)), ((), ()))
        ps = [prep(RET_GROUP * t + r) for r in range(RET_GROUP)]
        qks = [lax.dot_general(p[1], p[2], nt, preferred_element_type=F32) for p in ps]
        dss = [jnp.dot(p[4], p[3], preferred_element_type=F32) for p in ps]
        state = state_ref[...]
        crosses = []
        for p, ds in zip(ps, dss):
            crosses.append(jnp.dot(p[1], state.astype(BF16), preferred_element_type=F32))
            state = state * decay_ref[0] + ds
        state_ref[...] = state
        inners = [jnp.dot((qk * dm_ref[0]).astype(BF16), p[3], preferred_element_type=F32) for qk, p in zip(qks, ps)]
        for p, inner, cross in zip(ps, inners, crosses):
            finish(p[0], inner, cross)
        return carry

    lax.fori_loop(0, S // (RET_GROUP * C), chunk_group, 0)


def _retention(proj, cc, ss, tables, gn_g):
    B, S, _ = proj.shape
    dk, dv, H, C = RET_QK_DIM, RET_V_DIM, RET_HEADS, RET_CHUNK
    dm, xi_t, zeta_t, decay_t = tables
    return pl.pallas_call(
        _ret_kernel,
        grid=(B, H),
        in_specs=[
            pl.BlockSpec((1, S, dk), lambda b, h: (b, 0, h)),
            pl.BlockSpec((1, S, dk), lambda b, h: (b, 0, H + h)),
            pl.BlockSpec((1, S, dv), lambda b, h: (b, 0, H + h)),
            pl.BlockSpec((1, S, dv), lambda b, h: (b, 0, 2 * H + h)),
            pl.BlockSpec((1, S, dk), lambda b, h: (b, 0, 0)),
            pl.BlockSpec((1, S, dk), lambda b, h: (b, 0, 0)),
            pl.BlockSpec((1, C, C), lambda b, h: (h, 0, 0)),
            pl.BlockSpec((1, C, dv), lambda b, h: (h, 0, 0)),
            pl.BlockSpec((1, C, dk), lambda b, h: (h, 0, 0)),
            pl.BlockSpec((1, dk, dv), lambda b, h: (h, 0, 0)),
            pl.BlockSpec((1, dv), lambda b, h: (0, h)),
        ],
        out_specs=pl.BlockSpec((1, S, dv), lambda b, h: (b, 0, h)),
        out_shape=jax.ShapeDtypeStruct((B, S, H * dv), BF16),
        scratch_shapes=[pltpu.VMEM((dk, dv), F32)],
        compiler_params=_cparams(("arbitrary", "arbitrary")),
        name="retention",
    )(proj, proj, proj, proj, cc, ss, dm, xi_t, zeta_t, decay_t, gn_g.reshape(1, H * dv))


def _store_slabs(ref, val):
    tm = val.shape[0]
    for j in range(ROW_SLAB):
        ref[pl.ds(j, tm, stride=ROW_SLAB), :] = val[:, j * LANES:(j + 1) * LANES]


def _load_slab_chunks(ref, tm):
    return [ref[pl.ds(j, tm, stride=ROW_SLAB), :] for j in range(ROW_SLAB)]


def _top2_flags(vals):
    n = len(vals)
    rank = []
    for k in range(n):
        r = jnp.zeros_like(vals[0])
        for j in range(n):
            if j < k:
                r = r + (vals[j] >= vals[k]).astype(F32)
            elif j > k:
                r = r + (vals[j] > vals[k]).astype(F32)
        rank.append(r)
    is1 = [r == 0.0 for r in rank]
    is2 = [r == 1.0 for r in rank]
    m1 = sum(jnp.where(f, v, 0.0) for f, v in zip(is1, vals))
    m2 = sum(jnp.where(f, v, 0.0) for f, v in zip(is2, vals))
    return is1, is2, m1, m2


def _router_kernel(x_ref, sc_ref, sh_ref, rw_ref, rb_ref, lt_ref,
                   bucket_ref, rank_ref, counts_ref, carry_ref):
    i = pl.program_id(0)
    tm = x_ref.shape[0]

    @pl.when(i == 0)
    def _():
        carry_ref[...] = jnp.zeros_like(carry_ref)

    h = x_ref[...] * (1.0 + sc_ref[0]) + sh_ref[0]
    hh, hl = _split_bf16(h)
    wh, wl = _split_bf16(rw_ref[...])
    nt = (((1,), (1,)), ((), ()))
    logits = lax.dot_general(wh, hh, nt, preferred_element_type=F32)
    logits += lax.dot_general(wl, hh, nt, preferred_element_type=F32)
    logits += lax.dot_general(wh, hl, nt, preferred_element_type=F32)
    logits = logits + rb_ref[:, 0:1]

    rows = [logits[e:e + 1, :] for e in range(N_EXPERTS)]
    gmax = functools.reduce(jnp.maximum, rows)
    groups = []
    for g in range(N_GROUPS):
        vals = rows[g * EXPERTS_PER_GROUP:(g + 1) * EXPERTS_PER_GROUP]
        is1, is2, m1, m2 = _top2_flags(vals)
        score = jnp.exp(m1 - gmax) + jnp.exp(m2 - gmax)
        groups.append(([jnp.logical_or(a, b) for a, b in zip(is1, is2)], score))
    best = groups[0][1]
    gsel = jnp.zeros_like(best)
    for g in range(1, N_GROUPS):
        better = groups[g][1] > best
        best = jnp.where(better, groups[g][1], best)
        gsel = jnp.where(better, float(g), gsel)
    sel = [jnp.zeros_like(best) > 1.0] * EXPERTS_PER_GROUP
    for g in range(N_GROUPS):
        here = gsel == float(g)
        sel = [jnp.logical_or(s, jnp.logical_and(here, f)) for s, f in zip(sel, groups[g][0])]
    pair = jnp.where(sel[0],
                     jnp.where(sel[1], 0.0, jnp.where(sel[2], 1.0, 2.0)),
                     jnp.where(sel[1], jnp.where(sel[2], 3.0, 4.0), 5.0))
    bucket = (gsel * float(PAIRS_PER_GROUP) + pair).astype(jnp.int32)

    brow = lax.broadcasted_iota(jnp.int32, (BUCKET_ROWS, tm), 0)
    onehot = (brow == bucket).astype(F32)
    before = jnp.dot(onehot.astype(BF16), lt_ref[...], preferred_element_type=F32)
    carry = carry_ref[...]
    in_tile = jnp.sum(onehot * before, axis=0, keepdims=True)
    base = jnp.sum(onehot * carry[:, 0:1], axis=0, keepdims=True)
    rank_ref[...] = (base + in_tile).astype(jnp.int32)
    bucket_ref[...] = bucket
    new_carry = carry + jnp.sum(onehot, axis=1, keepdims=True)
    carry_ref[...] = new_carry
    counts_ref[...] = new_carry.astype(jnp.int32)


def _router(x2, mods, k, B, S, router_wT, router_bt, lt):
    T, D = x2.shape
    tm = ROUTE_TM
    per_b = S // tm
    return pl.pallas_call(
        _router_kernel,
        grid=(T // tm,),
        in_specs=[
            pl.BlockSpec((tm, D), lambda i: (i, 0)),
            pl.BlockSpec((1, 1, D), lambda i: (_mod_row(k, i // per_b, SCALE, B), 0, 0)),
            pl.BlockSpec((1, 1, D), lambda i: (_mod_row(k, i // per_b, SHIFT, B), 0, 0)),
            pl.BlockSpec((N_EXPERTS, D), lambda i: (0, 0)),
            pl.BlockSpec((N_EXPERTS, LANES), lambda i: (0, 0)),
            pl.BlockSpec((tm, tm), lambda i: (0, 0)),
        ],
        out_specs=[
            pl.BlockSpec((1, tm), lambda i: (0, i)),
            pl.BlockSpec((1, tm), lambda i: (0, i)),
            pl.BlockSpec((BUCKET_ROWS, LANES), lambda i: (0, 0)),
        ],
        out_shape=[
            jax.ShapeDtypeStruct((1, T), jnp.int32),
            jax.ShapeDtypeStruct((1, T), jnp.int32),
            jax.ShapeDtypeStruct((BUCKET_ROWS, LANES), jnp.int32),
        ],
        scratch_shapes=[pltpu.VMEM((BUCKET_ROWS, LANES), F32)],
        compiler_params=_cparams(("arbitrary",)),
        name="router",
    )(x2, mods, mods, router_wT, router_bt, lt)


def _slab(ref, r):
    return ref.at[pl.ds(pl.multiple_of(r * ROW_SLAB, ROW_SLAB), ROW_SLAB)]


def _wait_rows(src_ref, dst_ref, sem, n):
    def body(c, carry):
        for _ in range(DMA_CHUNK):
            pltpu.make_async_copy(_slab(src_ref, 0), _slab(dst_ref, 0), sem).wait()
        return carry
    lax.fori_loop(0, n // DMA_CHUNK, body, 0)


def _scatter_kernel(dest_ref, pad_lo_ref, pad_hi_ref, x_ref, sc_ref, sh_ref, xs_ref, hbuf, sems):
    i = pl.program_id(0)
    last = pl.num_programs(0) - 1
    tm = x_ref.shape[0]
    h = x_ref[...] * (1.0 + sc_ref[0]) + sh_ref[0]

    for s in range(2):
        @pl.when(lax.rem(i, 2) == s)
        def _(s=s):
            buf = hbuf.at[s]
            _store_slabs(buf, h)

            def issue(c, carry):
                for u in range(DMA_CHUNK):
                    r = c * DMA_CHUNK + u
                    pltpu.make_async_copy(_slab(buf, r), _slab(xs_ref, dest_ref[i * tm + r]), sems.at[s]).start(priority=u % 2)
                return carry
            lax.fori_loop(0, tm // DMA_CHUNK, issue, 0)

            @pl.when(i > 0)
            def _():
                _wait_rows(hbuf.at[1 - s], xs_ref, sems.at[1 - s], tm)

            @pl.when(i == last)
            def _():
                _wait_rows(buf, xs_ref, sems.at[s], tm)

                def filler(row0, n_rows, start):
                    cp = pltpu.make_async_copy(
                        buf.at[pl.ds(0, n_rows * ROW_SLAB)],
                        xs_ref.at[pl.ds(pl.multiple_of(row0 * ROW_SLAB, ROW_SLAB), n_rows * ROW_SLAB)],
                        sems.at[s])
                    if start:
                        cp.start()
                    else:
                        cp.wait()

                def fill(start):
                    def bucket(b, carry):
                        pos = pad_lo_ref[b]
                        n = pad_hi_ref[b] - pos
                        for bit in reversed(range(MOE_TM.bit_length() - 1)):
                            take = lax.rem(lax.shift_right_logical(n, bit), 2)

                            @pl.when(take == 1)
                            def _(pos=pos, bit=bit):
                                filler(pos, 1 << bit, start)
                            pos = pos + take * (1 << bit)
                        return carry
                    lax.fori_loop(0, N_BUCKETS, bucket, 0)

                    def tail(t, carry):
                        filler(t * MOE_TM, MOE_TM, start)
                        return carry
                    n_tiles = xs_ref.shape[0] // (MOE_TM * ROW_SLAB)
                    lax.fori_loop(pad_hi_ref[N_BUCKETS - 1] // MOE_TM, n_tiles, tail, 0)
                fill(True)
                fill(False)


def _scatter_rows(x2, mods, k, B, S, plan, n_out):
    T, D = x2.shape
    tm = PROJ_TM
    per_b = S // tm
    return pl.pallas_call(
        _scatter_kernel,
        grid_spec=pltpu.PrefetchScalarGridSpec(
            num_scalar_prefetch=3,
            grid=(T // tm,),
            in_specs=[
                pl.BlockSpec((tm, D), lambda i, *_: (i, 0)),
                pl.BlockSpec((1, 1, D), lambda i, *_: (_mod_row(k, i // per_b, SCALE, B), 0, 0)),
                pl.BlockSpec((1, 1, D), lambda i, *_: (_mod_row(k, i // per_b, SHIFT, B), 0, 0)),
            ],
            out_specs=pl.BlockSpec(memory_space=pl.ANY),
            scratch_shapes=[pltpu.VMEM((2, tm * ROW_SLAB, LANES), F32), pltpu.SemaphoreType.DMA((2,))],
        ),
        out_shape=jax.ShapeDtypeStruct((n_out * ROW_SLAB, LANES), F32),
        compiler_params=_cparams(("arbitrary",)),
        name="scatter_rows",
    )(plan["dest"], plan["pad_lo"], plan["pad_hi"], x2, mods, mods)


def _expert_pair(x, wg_bf, wu_bf, wd_bf):
    gates = [jnp.dot(x, wg_bf[e], preferred_element_type=F32) for e in range(2)]
    ups = [jnp.dot(x, wu_bf[e], preferred_element_type=F32) for e in range(2)]
    outs = []
    for e in range(2):
        hid = gates[e] * jax.nn.sigmoid(gates[e]) * ups[e]
        outs.append(jnp.dot(hid.astype(BF16), wd_bf[e], preferred_element_type=F32))
    return outs


def _router_logit(chunks, rw_ref, rb_ref, e):
    w_row = rw_ref[pl.ds(e, 1), :]
    acc = chunks[0] * w_row[:, 0:LANES]
    for j in range(1, ROW_SLAB):
        acc = acc + chunks[j] * w_row[:, j * LANES:(j + 1) * LANES]
    return jnp.sum(acc, axis=-1, keepdims=True) + rb_ref[pl.ds(e, 1), 0:1]


def _moe_kernel(src_ref, ea_ref, eb_ref, valid_ref, na_ref, nb_ref,
                xs_ref, rw_ref, rb_ref, wg_hbm, wu_hbm, wd_hbm, o_ref,
                wg_st, wu_st, wd_st, wg_bf, wu_bf, wd_bf, sems, *, layer):
    i = pl.program_id(0)
    prev = jnp.maximum(i - 1, 0)

    def fetch(slot, e):
        return [pltpu.make_async_copy(wg_hbm.at[layer, e], wg_st.at[slot], sems.at[slot, 0]),
                pltpu.make_async_copy(wu_hbm.at[layer, e], wu_st.at[slot], sems.at[slot, 1]),
                pltpu.make_async_copy(wd_hbm.at[layer, e], wd_st.at[slot], sems.at[slot, 2])]

    for slot, e_ref, n_ref in ((0, ea_ref, na_ref), (1, eb_ref, nb_ref)):
        @pl.when(i == 0)
        def _(slot=slot, e_ref=e_ref):
            for cp in fetch(slot, e_ref[0]):
                cp.start()

        @pl.when(jnp.logical_or(i == 0, e_ref[i] != e_ref[prev]))
        def _(slot=slot, e_ref=e_ref, n_ref=n_ref):
            for cp in fetch(slot, e_ref[i]):
                cp.wait()
            wg_bf[slot] = wg_st[slot].astype(BF16)
            wu_bf[slot] = wu_st[slot].astype(BF16)
            wd_bf[slot] = wd_st[slot].astype(BF16)

            @pl.when(n_ref[i] >= 0)
            def _():
                for cp in fetch(slot, n_ref[i]):
                    cp.start()

    @pl.when(valid_ref[i] == 1)
    def _():
        chunks = _load_slab_chunks(xs_ref, MOE_TM)
        l_a = _router_logit(chunks, rw_ref, rb_ref, ea_ref[i])
        l_b = _router_logit(chunks, rw_ref, rb_ref, eb_ref[i])
        w_a = jax.nn.sigmoid(l_a - l_b)
        w_b = jax.nn.sigmoid(l_b - l_a)
        x = jnp.concatenate([c.astype(BF16) for c in chunks], axis=-1)
        y_a, y_b = _expert_pair(x, wg_bf, wu_bf, wd_bf)
        y = w_a * y_a + w_b * y_b
        _store_slabs(o_ref, y)

    @pl.when(valid_ref[i] == 0)
    def _():
        o_ref[...] = jnp.zeros_like(o_ref)


def _grouped_experts(xs, plan, router_wT, router_bt, w_gate, w_up, w_down, layer):
    n_tiles = xs.shape[0] // (MOE_TM * ROW_SLAB)
    D, Fd = D_MODEL, D_EXPERT
    row_spec = lambda index: pl.BlockSpec((MOE_TM * ROW_SLAB, LANES), index)
    hbm = pl.BlockSpec(memory_space=pl.ANY)
    return pl.pallas_call(
        functools.partial(_moe_kernel, layer=layer),
        grid_spec=pltpu.PrefetchScalarGridSpec(
            num_scalar_prefetch=6,
            grid=(n_tiles,),
            in_specs=[
                row_spec(lambda i, src, *_: (src[i], 0)),
                pl.BlockSpec((N_EXPERTS, D), lambda i, *_: (0, 0)),
                pl.BlockSpec((N_EXPERTS, LANES), lambda i, *_: (0, 0)),
                hbm, hbm, hbm,
            ],
            out_specs=row_spec(lambda i, *_: (i, 0)),
            scratch_shapes=[
                pltpu.VMEM((2, D, Fd), F32), pltpu.VMEM((2, D, Fd), F32), pltpu.VMEM((2, Fd, D), F32),
                pltpu.VMEM((2, D, Fd), BF16), pltpu.VMEM((2, D, Fd), BF16), pltpu.VMEM((2, Fd, D), BF16),
                pltpu.SemaphoreType.DMA((2, 3)),
            ],
        ),
        out_shape=jax.ShapeDtypeStruct(xs.shape, F32),
        compiler_params=_cparams(("arbitrary",)),
        name="grouped_experts",
    )(plan["tile_src"], plan["tile_ea"], plan["tile_eb"], plan["tile_valid"], plan["next_ea"], plan["next_eb"],
      xs, router_wT, router_bt, w_gate, w_up, w_down)


def _next_change(e):
    n = e.shape[0]
    idx = jnp.arange(n, dtype=jnp.int32)
    change = jnp.concatenate([jnp.ones((1,), jnp.bool_), e[1:] != e[:-1]])
    at = jnp.where(change, idx, n)
    nxt = lax.cummin(at[::-1])[::-1]
    nxt = jnp.concatenate([nxt[1:], jnp.full((1,), n, jnp.int32)])
    return jnp.where(nxt < n, jnp.take(e, jnp.minimum(nxt, n - 1)), -1).astype(jnp.int32)


def _moe_plan(bucket, rank, counts, n_tiles):
    cnt = counts[:N_BUCKETS, 0]
    tiles_b = (cnt + MOE_TM - 1) // MOE_TM
    tile_end = jnp.cumsum(tiles_b)
    tile_start = tile_end - tiles_b
    row_start = tile_start * MOE_TM
    dest = jnp.take(row_start, bucket[0]) + rank[0]
    n_used = tile_end[-1]
    t = jnp.arange(n_tiles, dtype=jnp.int32)
    t_c = jnp.minimum(t, n_used - 1)
    tb = jnp.sum((t_c[:, None] >= tile_end[None, :]).astype(jnp.int32), axis=1)
    pair_lo = jnp.array([0, 0, 0, 1, 1, 2], jnp.int32)
    pair_hi = jnp.array([1, 2, 3, 2, 3, 3], jnp.int32)
    grp = tb // PAIRS_PER_GROUP
    pr = tb % PAIRS_PER_GROUP
    tile_ea = (grp * EXPERTS_PER_GROUP + jnp.take(pair_lo, pr)).astype(jnp.int32)
    tile_eb = (grp * EXPERTS_PER_GROUP + jnp.take(pair_hi, pr)).astype(jnp.int32)
    return {
        "dest": dest.astype(jnp.int32),
        "pad_lo": (row_start + cnt).astype(jnp.int32),
        "pad_hi": (tile_end * MOE_TM).astype(jnp.int32),
        "tile_src": t_c.astype(jnp.int32),
        "tile_ea": tile_ea,
        "tile_eb": tile_eb,
        "next_ea": _next_change(tile_ea),
        "next_eb": _next_change(tile_eb),
        "tile_valid": (t < n_used).astype(jnp.int32),
    }


def _resln_kernel(dest_ref, ys_ref, x_ref, gate_ref, g_ref, b_ref, o_ref, ybuf, sems):
    i = pl.program_id(0)
    n = pl.num_programs(0)
    tm = x_ref.shape[0]

    def issue(tile, s):
        def body(c, carry):
            for u in range(DMA_CHUNK):
                r = c * DMA_CHUNK + u
                pltpu.make_async_copy(_slab(ys_ref, dest_ref[tile * tm + r]), _slab(ybuf.at[s], r), sems.at[s]).start(priority=u % 2)
            return carry
        lax.fori_loop(0, tm // DMA_CHUNK, body, 0)

    @pl.when(i == 0)
    def _():
        issue(0, 0)

    for s in range(2):
        @pl.when(lax.rem(i, 2) == s)
        def _(s=s):
            @pl.when(i + 1 < n)
            def _():
                issue(i + 1, 1 - s)

            _wait_rows(ys_ref, ybuf.at[s], sems.at[s], tm)
            y = jnp.concatenate(_load_slab_chunks(ybuf.at[s], tm), axis=-1)
            u = DEEPNORM_ALPHA * x_ref[...] + (1.0 + gate_ref[0]) * y
            o_ref[...] = _layer_norm_rows(u, g_ref[...], b_ref[...])


def _gather_residual_ln(ys, plan, x2, mods, k, B, S, g, b):
    T, D = x2.shape
    tm = PROJ_TM
    per_b = S // tm
    return pl.pallas_call(
        _resln_kernel,
        grid_spec=pltpu.PrefetchScalarGridSpec(
            num_scalar_prefetch=1,
            grid=(T // tm,),
            in_specs=[
                pl.BlockSpec(memory_space=pl.ANY),
                pl.BlockSpec((tm, D), lambda i, *_: (i, 0)),
                pl.BlockSpec((1, 1, D), lambda i, *_: (_mod_row(k, i // per_b, GATE, B), 0, 0)),
                pl.BlockSpec((1, D), lambda i, *_: (0, 0)),
                pl.BlockSpec((1, D), lambda i, *_: (0, 0)),
            ],
            out_specs=pl.BlockSpec((tm, D), lambda i, *_: (i, 0)),
            scratch_shapes=[pltpu.VMEM((2, tm * ROW_SLAB, LANES), F32), pltpu.SemaphoreType.DMA((2,))],
        ),
        out_shape=jax.ShapeDtypeStruct((T, D), F32),
        compiler_params=_cparams(("arbitrary",)),
        name="gather_residual_ln",
    )(plan["dest"], ys, x2, mods, g.reshape(1, D), b.reshape(1, D))


def _moe_sublayer(x, mods, k, router_wT, router_bt, lt, w_gate, w_up, w_down, layer, g, b):
    B, S, D = x.shape
    T = B * S
    x2 = x.reshape(T, D)
    bucket, rank, counts = _router(x2, mods, k, B, S, router_wT, router_bt, lt)
    n_tiles = T // MOE_TM + N_BUCKETS
    plan = _moe_plan(bucket, rank, counts, n_tiles)
    xs = _scatter_rows(x2, mods, k, B, S, plan, n_tiles * MOE_TM)
    ys = _grouped_experts(xs, plan, router_wT, router_bt, w_gate, w_up, w_down, layer)
    return _gather_residual_ln(ys, plan, x2, mods, k, B, S, g, b).reshape(B, S, D)


def kernel(x, c, positions, ada_w, ada_b, ln_g, ln_b, sb_w_in, sb_w_out, ret_w_in, ret_gn_g, ret_w_out,
           router_w, router_b, moe_w_gate, moe_w_up, moe_w_down):
    mods = _ada_params(c, ada_w, ada_b)
    cc, ss = _rope_tables(positions)
    tables = _retention_decay_tables()
    u = (jnp.arange(SB_TK)[:, None] < jnp.arange(SB_TK)[None, :]).astype(BF16)
    lt = (jnp.arange(ROUTE_TM)[:, None] < jnp.arange(ROUTE_TM)[None, :]).astype(BF16)
    router_wT = router_w.T
    router_bt = jnp.broadcast_to(router_b[:, None], (N_EXPERTS, LANES))
    for i in range(DEPTH):
        k_mix, k_moe = 2 * i, 2 * i + 1
        if i % 2 == 0:
            qkv = _mod_matmul(x, mods, k_mix, sb_w_in, i // 2, tn=3 * D_MODEL // 2)
            a = _sb_attention(qkv, u)
            w_out = sb_w_out
        else:
            proj = _mod_matmul(x, mods, k_mix, ret_w_in, i // 2, tn=2 * D_MODEL)
            a = _retention(proj, cc, ss, tables, ret_gn_g[i // 2])
            w_out = ret_w_out
        x = _out_proj_ln(a, w_out, i // 2, x, mods, k_mix, ln_g[i, 0], ln_b[i, 0])
        x = _moe_sublayer(x, mods, k_moe, router_wT, router_bt, lt, moe_w_gate, moe_w_up, moe_w_down, i,
                          ln_g[i, 1], ln_b[i, 1])
    return x
```

```python
import functools

import jax
import jax.numpy as jnp
from jax import lax
from jax.experimental import pallas as pl
from jax.experimental.pallas import tpu as pltpu

F32 = jnp.float32
BF16 = jnp.bfloat16

D_MODEL = 1024
DEPTH = 4
SB_HEADS = 16
SB_HEAD_DIM = D_MODEL // SB_HEADS
RET_HEADS = 8
RET_QK_DIM = D_MODEL // RET_HEADS
RET_V_DIM = 2 * RET_QK_DIM
ROPE_BASE = 10000.0
N_EXPERTS = 16
N_GROUPS = 4
EXPERTS_PER_GROUP = N_EXPERTS // N_GROUPS
D_EXPERT = D_MODEL // 2
DEEPNORM_ALPHA = (2 * DEPTH) ** 0.25
LN_EPS = 1e-5
LOG2_E = 1.4426950408889634

LANES = 128
SUBLANES = 8
MXU_DIM = 256
VMEM_LIMIT_BYTES = 56 * 1024 * 1024

PROJ_TM = 512
MM_TM = 1024
OUT_SPLIT = 4
SB_TQ = MXU_DIM
SB_TK = MXU_DIM
RET_CHUNK = 256
RET_GROUP = 4
ROUTE_TM = 512
MOE_TM = 256
PAIRS_PER_GROUP = 6
N_BUCKETS = N_GROUPS * PAIRS_PER_GROUP
BUCKET_ROWS = 32
ROW_SLAB = D_MODEL // LANES
DMA_CHUNK = 16
assert ROW_SLAB == SUBLANES


def _cparams(semantics):
    return pltpu.CompilerParams(dimension_semantics=semantics, vmem_limit_bytes=VMEM_LIMIT_BYTES)


def _split_bf16(a):
    hi = a.astype(BF16)
    lo = (a - hi.astype(F32)).astype(BF16)
    return hi, lo


def _ada_kernel(c_ref, w_ref, b_ref, o_ref):
    c = c_ref[...]
    s = c * jax.nn.sigmoid(c)
    sh, sl = _split_bf16(s)
    wh, wl = _split_bf16(w_ref[0])
    acc = jnp.dot(sh, wh, preferred_element_type=F32)
    acc += jnp.dot(sl, wh, preferred_element_type=F32)
    acc += jnp.dot(sh, wl, preferred_element_type=F32)
    o_ref[0] = acc + b_ref[0]


def _ada_params(c, ada_w, ada_b):
    B, D = c.shape
    n_sub = ada_w.shape[0] * ada_w.shape[1]
    N = ada_w.shape[-1]
    tn = N // 2
    w = ada_w.reshape(n_sub, D, N)
    b = ada_b.reshape(n_sub, 1, N)
    out = pl.pallas_call(
        _ada_kernel,
        grid=(n_sub, N // tn),
        in_specs=[
            pl.BlockSpec((B, D), lambda k, j: (0, 0)),
            pl.BlockSpec((1, D, tn), lambda k, j: (k, 0, j)),
            pl.BlockSpec((1, 1, tn), lambda k, j: (k, 0, j)),
        ],
        out_specs=pl.BlockSpec((1, B, tn), lambda k, j: (k, 0, j)),
        out_shape=jax.ShapeDtypeStruct((n_sub, B, N), F32),
        compiler_params=_cparams(("arbitrary", "arbitrary")),
        name="ada_params",
    )(c, w, b)
    return out.reshape(n_sub * B * 3, 1, D)


def _mod_row(k, b, which, B):
    return (k * B + b) * 3 + which


SHIFT, SCALE, GATE = 0, 1, 2


def _modmm_kernel(x_ref, sc_ref, sh_ref, w_ref, o_ref, w_bf):
    @pl.when(jnp.logical_and(pl.program_id(1) == 0, pl.program_id(2) == 0))
    def _():
        w_bf[...] = w_ref[0].astype(BF16)

    h = x_ref[0] * (1.0 + sc_ref[0]) + sh_ref[0]
    o_ref[0] = jnp.dot(h.astype(BF16), w_bf[...], preferred_element_type=F32).astype(o_ref.dtype)


def _mod_matmul(x, mods, k, w_all, layer, tn):
    B, S, D = x.shape
    N = w_all.shape[2]
    tm = MM_TM
    return pl.pallas_call(
        _modmm_kernel,
        grid=(N // tn, B, S // tm),
        in_specs=[
            pl.BlockSpec((1, tm, D), lambda n, b, i: (b, i, 0)),
            pl.BlockSpec((1, 1, D), lambda n, b, i: (_mod_row(k, b, SCALE, B), 0, 0)),
            pl.BlockSpec((1, 1, D), lambda n, b, i: (_mod_row(k, b, SHIFT, B), 0, 0)),
            pl.BlockSpec((1, D, tn), lambda n, b, i: (layer, 0, n)),
        ],
        out_specs=pl.BlockSpec((1, tm, tn), lambda n, b, i: (b, i, n)),
        out_shape=jax.ShapeDtypeStruct((B, S, N), BF16),
        scratch_shapes=[pltpu.VMEM((D, tn), BF16)],
        compiler_params=_cparams(("arbitrary", "arbitrary", "arbitrary")),
        name="mod_matmul",
    )(x, mods, mods, w_all)


SB_PIPE = 4
SB_UNROLL = 4


def _sb_tile_tables(nq):
    diag = [(i, i) for i in range(nq)]
    off = [(i, i - o) for o in range(1, nq) for i in range(o, nq)]
    ti, tj, spans = [], [], []
    for tiles in (diag, off):
        padded = [(nq, 0)] * SB_PIPE + tiles + [(nq, 0)] * (SB_PIPE + SB_UNROLL)
        spans.append((len(ti), len(tiles)))
        ti += [t[0] for t in padded]
        tj += [t[1] for t in padded]
    return ti, tj, spans


def _sb_kernel(ti_ref, tj_ref, q_ref, k_ref, v_ref, u_ref, o_ref,
               vT_ref, qT_ref, z_ref, sp_ref, d_ref, a_ref, c_ref, run_ref, acc_ref, alive_ref, n_alive_ref, *, spans):
    S = q_ref.shape[1]
    tq, tk, dh = SB_TQ, SB_TK, SB_HEAD_DIM
    nq = S // tq

    for c in range(S // tk):
        v_c = v_ref[0, c * tk:(c + 1) * tk, :].astype(F32)
        vT_ref[c] = v_c.T.astype(BF16)

    scale = dh ** -0.5 * LOG2_E
    head_rows = lax.broadcasted_iota(jnp.int32, (2 * dh, tq), 0)
    for i in range(nq):
        qT = (q_ref[0, i * tq:(i + 1) * tq, :].astype(F32) * scale).T
        qT_ref[i, 0] = jnp.where(head_rows < dh, qT, 0.0).astype(BF16)
        qT_ref[i, 1] = jnp.where(head_rows >= dh, qT, 0.0).astype(BF16)
    qT_ref[nq] = jnp.zeros_like(qT_ref[nq])
    z_ref[...] = jnp.zeros_like(z_ref)
    sp_ref[...] = jnp.zeros_like(sp_ref)
    d_ref[...] = jnp.zeros_like(d_ref)
    a_ref[...] = jnp.zeros_like(a_ref)
    c_ref[...] = jnp.zeros_like(c_ref)
    for i in range(nq + 1):
        alive_ref[i] = jnp.int32(1)
    n_alive_ref[0] = jnp.int32(nq)

    def iteration(t1, t3, t4, p, masked):
        q = 1 - p
        i1, j1 = ti_ref[t1], tj_ref[t1]
        i3 = ti_ref[t3]
        i4, j4 = ti_ref[t4], tj_ref[t4]
        if masked:
            rows = lax.broadcasted_iota(jnp.int32, (tk, tq), 0)
            cols = lax.broadcasted_iota(jnp.int32, (tk, tq), 1)
            valid = rows < cols
        u = u_ref[...]
        suf = [jnp.dot(u, sp_ref[q, h], preferred_element_type=F32) for h in range(2)]
        vT = vT_ref[j4]
        pv = [jnp.dot(vT[h * dh:(h + 1) * dh, :], a_ref[p, h], preferred_element_type=F32) for h in range(2)]
        k_blk = k_ref[0, pl.ds(pl.multiple_of(j1 * tk, tk), tk), :]
        z_new = [jnp.dot(k_blk, qT_ref[i1, h], preferred_element_type=F32) for h in range(2)]
        for h in range(2):
            z = z_ref[p, h]
            sp = jnp.maximum(z, 0.0) + jnp.log2(1.0 + jnp.exp2(-jnp.abs(z)))
            d_ref[p, h] = z - sp
            if masked:
                sp = jnp.where(valid, sp, 0.0)
            sp_ref[p, h] = sp.astype(BF16)
        for h in range(2):
            a = jnp.exp2(d_ref[q, h] - suf[h])
            if masked:
                a = jnp.where(valid, a, 0.0)
            a_ref[q, h] = a.astype(BF16)
            tile_mass = suf[h][0:1, :] + sp_ref[q, h, 0:1, :].astype(F32)
            if masked:
                run_ref[i3, h, 0:1, :] = tile_mass
            else:
                run = run_ref[i3, h, 0:1, :]
                c_ref[q, h, 0:1, :] = jnp.exp2(-run)
                run_ref[i3, h, 0:1, :] = run + tile_mass
                c_next = jnp.exp2(-(run + tile_mass))
                c_max = c_next if h == 0 else jnp.maximum(c_max, c_next)
        if not masked:
            live = jnp.logical_or(jnp.max(c_max) > 0.0, i3 == nq).astype(jnp.int32)
            n_alive_ref[0] += live - alive_ref[i3]
            alive_ref[i3] = live
        for h in range(2):
            if masked:
                acc_ref[i4, h] = pv[h]
            else:
                acc_ref[i4, h] += pv[h] * c_ref[p, h, 0:1, :]
            z_ref[p, h] = z_new[h]

    base, n_real = spans[0]
    n_iter = n_real + SB_PIPE
    n_iter += -n_iter % SB_UNROLL

    def diag_body(t, carry):
        for r in range(SB_UNROLL):
            t1 = base + SB_PIPE + SB_UNROLL * t + r
            iteration(t1, t1 - 3, t1 - 4, r % 2, True)
        return carry
    lax.fori_loop(0, n_iter // SB_UNROLL, diag_body, 0)

    base, n_real = spans[1]
    first = base + SB_PIPE
    end = first + n_real

    def next_live(t):
        t = jnp.where(n_alive_ref[0] == 0, jnp.maximum(t, end), t)
        return lax.while_loop(lambda u: alive_ref[ti_ref[u]] == 0, lambda u: u + 1, t)

    def off_body(carry):
        ptr, h1, h2, h3, h4 = carry
        picks = []
        for r in range(SB_UNROLL):
            t1 = next_live(ptr)
            picks.append(t1)
            ptr = t1 + 1
        for r, t1 in enumerate(picks):
            iteration(t1, h3, h4, r % 2, False)
            h1, h2, h3, h4 = t1, h1, h2, h3
        return ptr, h1, h2, h3, h4

    lax.while_loop(lambda carry: carry[0] < end + SB_PIPE, off_body,
                   (jnp.int32(first), jnp.int32(first - 1), jnp.int32(first - 2), jnp.int32(first - 3), jnp.int32(first - 4)))

    for i in range(nq):
        oT = jnp.concatenate([acc_ref[i, 0], acc_ref[i, 1]], axis=0)
        o_ref[0, i * tq:(i + 1) * tq, :] = oT.T.astype(o_ref.dtype)


def _sb_attention(qkv, u):
    B, S, _ = qkv.shape
    n_pairs = SB_HEADS // 2
    nq = S // SB_TQ
    blk = (1, S, 2 * SB_HEAD_DIM)
    ti, tj, spans = _sb_tile_tables(nq)
    tile = (SB_TK, SB_TQ)
    return pl.pallas_call(
        functools.partial(_sb_kernel, spans=spans),
        grid_spec=pltpu.PrefetchScalarGridSpec(
            num_scalar_prefetch=2,
            grid=(B, n_pairs),
            in_specs=[
                pl.BlockSpec(blk, lambda b, p, *_: (b, 0, p)),
                pl.BlockSpec(blk, lambda b, p, *_: (b, 0, n_pairs + p)),
                pl.BlockSpec(blk, lambda b, p, *_: (b, 0, 2 * n_pairs + p)),
                pl.BlockSpec((SB_TK, SB_TK), lambda b, p, *_: (0, 0)),
            ],
            out_specs=pl.BlockSpec(blk, lambda b, p, *_: (b, 0, p)),
            scratch_shapes=[
                pltpu.VMEM((S // SB_TK, 2 * SB_HEAD_DIM, SB_TK), BF16),
                pltpu.VMEM((nq + 1, 2, 2 * SB_HEAD_DIM, SB_TQ), BF16),
                pltpu.VMEM((2, 2) + tile, F32),
                pltpu.VMEM((2, 2) + tile, BF16),
                pltpu.VMEM((2, 2) + tile, F32),
                pltpu.VMEM((2, 2) + tile, BF16),
                pltpu.VMEM((2, 2, SUBLANES, SB_TQ), F32),
                pltpu.VMEM((nq + 1, 2, SUBLANES, SB_TQ), F32),
                pltpu.VMEM((nq + 1, 2, SB_HEAD_DIM, SB_TQ), F32),
                pltpu.SMEM((nq + 1,), jnp.int32),
                pltpu.SMEM((1,), jnp.int32),
            ],
        ),
        out_shape=jax.ShapeDtypeStruct((B, S, D_MODEL), BF16),
        compiler_params=_cparams(("arbitrary", "arbitrary")),
        name="sb_attention",
    )(jnp.asarray(ti, jnp.int32), jnp.asarray(tj, jnp.int32), qkv, qkv, qkv, u)


def _layer_norm_rows(u, g, b):
    mu = jnp.mean(u, axis=-1, keepdims=True)
    d = u - mu
    var = jnp.mean(d * d, axis=-1, keepdims=True)
    return d * lax.rsqrt(var + LN_EPS) * g + b


def _outln_kernel(a_ref, w_ref, x_ref, gate_ref, g_ref, b_ref, o_ref, w_bf):
    @pl.when(jnp.logical_and(pl.program_id(0) == 0, pl.program_id(1) == 0))
    def _():
        w_bf[...] = w_ref[0].astype(BF16)

    tm = a_ref.shape[1]
    sub = tm // OUT_SPLIT
    ys = [jnp.dot(a_ref[0, r * sub:(r + 1) * sub, :], w_bf[...], preferred_element_type=F32) for r in range(OUT_SPLIT)]
    for r in range(OUT_SPLIT):
        rows = slice(r * sub, (r + 1) * sub)
        u = DEEPNORM_ALPHA * x_ref[0, rows, :] + (1.0 + gate_ref[0]) * ys[r]
        o_ref[0, rows, :] = _layer_norm_rows(u, g_ref[...], b_ref[...])


def _out_proj_ln(a, w_all, layer, x, mods, k, g, b):
    B, S, D = x.shape
    K = a.shape[-1]
    tm = PROJ_TM
    return pl.pallas_call(
        _outln_kernel,
        grid=(B, S // tm),
        in_specs=[
            pl.BlockSpec((1, tm, K), lambda bb, i: (bb, i, 0)),
            pl.BlockSpec((1, K, D), lambda bb, i: (layer, 0, 0)),
            pl.BlockSpec((1, tm, D), lambda bb, i: (bb, i, 0)),
            pl.BlockSpec((1, 1, D), lambda bb, i: (_mod_row(k, bb, GATE, B), 0, 0)),
            pl.BlockSpec((1, D), lambda bb, i: (0, 0)),
            pl.BlockSpec((1, D), lambda bb, i: (0, 0)),
        ],
        out_specs=pl.BlockSpec((1, tm, D), lambda bb, i: (bb, i, 0)),
        out_shape=jax.ShapeDtypeStruct((B, S, D), F32),
        scratch_shapes=[pltpu.VMEM((K, D), BF16)],
        compiler_params=_cparams(("arbitrary", "arbitrary")),
        name="out_proj_ln",
    )(a, w_all, x, mods, g.reshape(1, D), b.reshape(1, D))


def _rope_kernel(pos_ref, invf_ref, sign_ref, cc_ref, ss_ref):
    ang = pos_ref[0].astype(F32) * invf_ref[...]
    cc_ref[0] = jnp.cos(ang)
    ss_ref[0] = jnp.sin(ang) * sign_ref[...]


def _rope_tables(positions):
    B, S = positions.shape
    half = RET_QK_DIM // 2
    inv_freq = 1.0 / (ROPE_BASE ** (jnp.arange(0, RET_QK_DIM, 2, dtype=F32) / RET_QK_DIM))
    invf2 = jnp.concatenate([inv_freq, inv_freq]).reshape(1, RET_QK_DIM)
    sign = jnp.concatenate([-jnp.ones((half,), F32), jnp.ones((half,), F32)]).reshape(1, RET_QK_DIM)
    ts = PROJ_TM
    shp = jax.ShapeDtypeStruct((B, S, RET_QK_DIM), F32)
    return pl.pallas_call(
        _rope_kernel,
        grid=(B, S // ts),
        in_specs=[
            pl.BlockSpec((1, ts, 1), lambda b, i: (b, i, 0)),
            pl.BlockSpec((1, RET_QK_DIM), lambda b, i: (0, 0)),
            pl.BlockSpec((1, RET_QK_DIM), lambda b, i: (0, 0)),
        ],
        out_specs=[pl.BlockSpec((1, ts, RET_QK_DIM), lambda b, i: (b, i, 0))] * 2,
        out_shape=[shp, shp],
        compiler_params=_cparams(("arbitrary", "arbitrary")),
        name="rope_tables",
    )(positions.reshape(B, S, 1), invf2, sign)


def _retention_decay_tables():
    C = RET_CHUNK
    kscale = RET_QK_DIM ** -0.5
    log_gamma = jnp.log(1.0 - jnp.exp2(-5.0 - jnp.arange(RET_HEADS, dtype=F32)))
    n = jnp.arange(C, dtype=F32)
    diff = n[:, None] - n[None, :]
    dmask = jnp.where(diff >= 0, jnp.exp(jnp.maximum(diff, 0.0) * log_gamma[:, None, None]), 0.0)
    xi = jnp.exp((n + 1.0) * log_gamma[:, None])
    zeta = jnp.exp((C - 1.0 - n) * log_gamma[:, None])
    chunk_decay = jnp.exp(C * log_gamma)
    dm = dmask * kscale
    xi_t = jnp.broadcast_to(xi[:, :, None], (RET_HEADS, C, RET_V_DIM))
    zeta_t = jnp.broadcast_to(zeta[:, :, None] * kscale, (RET_HEADS, C, RET_QK_DIM))
    decay_t = jnp.broadcast_to(chunk_decay[:, None, None], (RET_HEADS, RET_QK_DIM, RET_V_DIM))
    return dm, xi_t, zeta_t, decay_t


def _ret_kernel(q_ref, k_ref, v_ref, g_ref, cc_ref, ss_ref, dm_ref, xi_ref, zeta_ref, decay_ref,
                gn_ref, o_ref, state_ref):
    S = q_ref.shape[1]
    C = RET_CHUNK
    half = RET_QK_DIM // 2
    state_ref[...] = jnp.zeros_like(state_ref)

    def prep(c):
        r0 = pl.multiple_of(c * C, C)
        rows = pl.ds(r0, C)
        cc = cc_ref[0, rows, :]
        ss = ss_ref[0, rows, :]
        q = q_ref[0, rows, :].astype(F32)
        k = k_ref[0, rows, :].astype(F32)
        qr = q * cc + pltpu.roll(q, half, 1) * ss
        kr = k * cc + pltpu.roll(k, half, 1) * ss
        kzT = (kr * zeta_ref[0]).T.astype(BF16)
        return rows, qr.astype(BF16), kr.astype(BF16), v_ref[0, rows, :], kzT

    def finish(rows, inner, cross):
        o = inner + cross * xi_ref[0]
        mu = jnp.mean(o, axis=-1, keepdims=True)
        d = o - mu
        var = jnp.mean(d * d, axis=-1, keepdims=True)
        on = d * lax.rsqrt(var + LN_EPS) * gn_ref[...]
        g = g_ref[0, rows, :].astype(F32)
        o_ref[0, rows, :] = (g * jax.nn.sigmoid(g) * on).astype(o_ref.dtype)

    def chunk_group(t, carry):
        nt = (((1,), (1,)), ((), ()))
        ps = [prep(RET_GROUP * t + r) for r in range(RET_GROUP)]
        qks = [lax.dot_general(p[1], p[2], nt, preferred_element_type=F32) for p in ps]
        dss = [jnp.dot(p[4], p[3], preferred_element_type=F32) for p in ps]
        state = state_ref[...]
        crosses = []
        for p, ds in zip(ps, dss):
            crosses.append(jnp.dot(p[1], state.astype(BF16), preferred_element_type=F32))
            state = state * decay_ref[0] + ds
        state_ref[...] = state
        inners = [jnp.dot((qk * dm_ref[0]).astype(BF16), p[3], preferred_element_type=F32) for qk, p in zip(qks, ps)]
        for p, inner, cross in zip(ps, inners, crosses):
            finish(p[0], inner, cross)
        return carry

    lax.fori_loop(0, S // (RET_GROUP * C), chunk_group, 0)


def _retention(proj, cc, ss, tables, gn_g):
    B, S, _ = proj.shape
    dk, dv, H, C = RET_QK_DIM, RET_V_DIM, RET_HEADS, RET_CHUNK
    dm, xi_t, zeta_t, decay_t = tables
    return pl.pallas_call(
        _ret_kernel,
        grid=(B, H),
        in_specs=[
            pl.BlockSpec((1, S, dk), lambda b, h: (b, 0, h)),
            pl.BlockSpec((1, S, dk), lambda b, h: (b, 0, H + h)),
            pl.BlockSpec((1, S, dv), lambda b, h: (b, 0, H + h)),
            pl.BlockSpec((1, S, dv), lambda b, h: (b, 0, 2 * H + h)),
            pl.BlockSpec((1, S, dk), lambda b, h: (b, 0, 0)),
            pl.BlockSpec((1, S, dk), lambda b, h: (b, 0, 0)),
            pl.BlockSpec((1, C, C), lambda b, h: (h, 0, 0)),
            pl.BlockSpec((1, C, dv), lambda b, h: (h, 0, 0)),
            pl.BlockSpec((1, C, dk), lambda b, h: (h, 0, 0)),
            pl.BlockSpec((1, dk, dv), lambda b, h: (h, 0, 0)),
            pl.BlockSpec((1, dv), lambda b, h: (0, h)),
        ],
        out_specs=pl.BlockSpec((1, S, dv), lambda b, h: (b, 0, h)),
        out_shape=jax.ShapeDtypeStruct((B, S, H * dv), BF16),
        scratch_shapes=[pltpu.VMEM((dk, dv), F32)],
        compiler_params=_cparams(("arbitrary", "arbitrary")),
        name="retention",
    )(proj, proj, proj, proj, cc, ss, dm, xi_t, zeta_t, decay_t, gn_g.reshape(1, H * dv))


def _store_slabs(ref, val):
    tm = val.shape[0]
    for j in range(ROW_SLAB):
        ref[pl.ds(j, tm, stride=ROW_SLAB), :] = val[:, j * LANES:(j + 1) * LANES]


def _load_slab_chunks(ref, tm):
    return [ref[pl.ds(j, tm, stride=ROW_SLAB), :] for j in range(ROW_SLAB)]


def _top2_flags(vals):
    n = len(vals)
    rank = []
    for k in range(n):
        r = jnp.zeros_like(vals[0])
        for j in range(n):
            if j < k:
                r = r + (vals[j] >= vals[k]).astype(F32)
            elif j > k:
                r = r + (vals[j] > vals[k]).astype(F32)
        rank.append(r)
    is1 = [r == 0.0 for r in rank]
    is2 = [r == 1.0 for r in rank]
    m1 = sum(jnp.where(f, v, 0.0) for f, v in zip(is1, vals))
    m2 = sum(jnp.where(f, v, 0.0) for f, v in zip(is2, vals))
    return is1, is2, m1, m2


def _router_kernel(x_ref, sc_ref, sh_ref, rw_ref, rb_ref, lt_ref,
                   bucket_ref, rank_ref, counts_ref, carry_ref):
    i = pl.program_id(0)
    tm = x_ref.shape[0]

    @pl.when(i == 0)
    def _():
        carry_ref[...] = jnp.zeros_like(carry_ref)

    h = x_ref[...] * (1.0 + sc_ref[0]) + sh_ref[0]
    nt = (((1,), (1,)), ((), ()))
    logits = lax.dot_general(rw_ref[...].astype(BF16), h.astype(BF16), nt, preferred_element_type=F32)
    logits = logits + rb_ref[:, 0:1]

    rows = [logits[e:e + 1, :] for e in range(N_EXPERTS)]
    gmax = functools.reduce(jnp.maximum, rows)
    groups = []
    for g in range(N_GROUPS):
        vals = rows[g * EXPERTS_PER_GROUP:(g + 1) * EXPERTS_PER_GROUP]
        is1, is2, m1, m2 = _top2_flags(vals)
        score = jnp.exp(m1 - gmax) + jnp.exp(m2 - gmax)
        groups.append(([jnp.logical_or(a, b) for a, b in zip(is1, is2)], score))
    best = groups[0][1]
    gsel = jnp.zeros_like(best)
    for g in range(1, N_GROUPS):
        better = groups[g][1] > best
        best = jnp.where(better, groups[g][1], best)
        gsel = jnp.where(better, float(g), gsel)
    sel = [jnp.zeros_like(best) > 1.0] * EXPERTS_PER_GROUP
    for g in range(N_GROUPS):
        here = gsel == float(g)
        sel = [jnp.logical_or(s, jnp.logical_and(here, f)) for s, f in zip(sel, groups[g][0])]
    pair = jnp.where(sel[0],
                     jnp.where(sel[1], 0.0, jnp.where(sel[2], 1.0, 2.0)),
                     jnp.where(sel[1], jnp.where(sel[2], 3.0, 4.0), 5.0))
    bucket = (gsel * float(PAIRS_PER_GROUP) + pair).astype(jnp.int32)

    brow = lax.broadcasted_iota(jnp.int32, (BUCKET_ROWS, tm), 0)
    onehot = (brow == bucket).astype(F32)
    before = jnp.dot(onehot.astype(BF16), lt_ref[...], preferred_element_type=F32)
    carry = carry_ref[...]
    in_tile = jnp.sum(onehot * before, axis=0, keepdims=True)
    base = jnp.sum(onehot * carry[:, 0:1], axis=0, keepdims=True)
    rank_ref[...] = (base + in_tile).astype(jnp.int32)
    bucket_ref[...] = bucket
    new_carry = carry + jnp.sum(onehot, axis=1, keepdims=True)
    carry_ref[...] = new_carry
    counts_ref[...] = new_carry.astype(jnp.int32)


def _router(x2, mods, k, B, S, router_wT, router_bt, lt):
    T, D = x2.shape
    tm = ROUTE_TM
    per_b = S // tm
    return pl.pallas_call(
        _router_kernel,
        grid=(T // tm,),
        in_specs=[
            pl.BlockSpec((tm, D), lambda i: (i, 0)),
            pl.BlockSpec((1, 1, D), lambda i: (_mod_row(k, i // per_b, SCALE, B), 0, 0)),
            pl.BlockSpec((1, 1, D), lambda i: (_mod_row(k, i // per_b, SHIFT, B), 0, 0)),
            pl.BlockSpec((N_EXPERTS, D), lambda i: (0, 0)),
            pl.BlockSpec((N_EXPERTS, LANES), lambda i: (0, 0)),
            pl.BlockSpec((tm, tm), lambda i: (0, 0)),
        ],
        out_specs=[
            pl.BlockSpec((1, tm), lambda i: (0, i)),
            pl.BlockSpec((1, tm), lambda i: (0, i)),
            pl.BlockSpec((BUCKET_ROWS, LANES), lambda i: (0, 0)),
        ],
        out_shape=[
            jax.ShapeDtypeStruct((1, T), jnp.int32),
            jax.ShapeDtypeStruct((1, T), jnp.int32),
            jax.ShapeDtypeStruct((BUCKET_ROWS, LANES), jnp.int32),
        ],
        scratch_shapes=[pltpu.VMEM((BUCKET_ROWS, LANES), F32)],
        compiler_params=_cparams(("arbitrary",)),
        name="router",
    )(x2, mods, mods, router_wT, router_bt, lt)


def _slab(ref, r):
    return ref.at[pl.ds(pl.multiple_of(r * ROW_SLAB, ROW_SLAB), ROW_SLAB)]


def _wait_rows(src_ref, dst_ref, sem, n):
    def body(c, carry):
        for _ in range(DMA_CHUNK):
            pltpu.make_async_copy(_slab(src_ref, 0), _slab(dst_ref, 0), sem).wait()
        return carry
    lax.fori_loop(0, n // DMA_CHUNK, body, 0)


def _scatter_kernel(dest_ref, pad_lo_ref, pad_hi_ref, x_ref, sc_ref, sh_ref, xs_ref, hbuf, sems):
    i = pl.program_id(0)
    last = pl.num_programs(0) - 1
    tm = x_ref.shape[0]
    h = x_ref[...] * (1.0 + sc_ref[0]) + sh_ref[0]

    for s in range(2):
        @pl.when(lax.rem(i, 2) == s)
        def _(s=s):
            buf = hbuf.at[s]
            _store_slabs(buf, h)

            def issue(c, carry):
                for u in range(DMA_CHUNK):
                    r = c * DMA_CHUNK + u
                    pltpu.make_async_copy(_slab(buf, r), _slab(xs_ref, dest_ref[i * tm + r]), sems.at[s]).start(priority=u % 2)
                return carry
            lax.fori_loop(0, tm // DMA_CHUNK, issue, 0)

            @pl.when(i > 0)
            def _():
                _wait_rows(hbuf.at[1 - s], xs_ref, sems.at[1 - s], tm)

            @pl.when(i == last)
            def _():
                _wait_rows(buf, xs_ref, sems.at[s], tm)

                def filler(row0, n_rows, start):
                    cp = pltpu.make_async_copy(
                        buf.at[pl.ds(0, n_rows * ROW_SLAB)],
                        xs_ref.at[pl.ds(pl.multiple_of(row0 * ROW_SLAB, ROW_SLAB), n_rows * ROW_SLAB)],
                        sems.at[s])
                    if start:
                        cp.start()
                    else:
                        cp.wait()

                def fill(start):
                    def bucket(b, carry):
                        pos = pad_lo_ref[b]
                        n = pad_hi_ref[b] - pos
                        for bit in reversed(range(MOE_TM.bit_length() - 1)):
                            take = lax.rem(lax.shift_right_logical(n, bit), 2)

                            @pl.when(take == 1)
                            def _(pos=pos, bit=bit):
                                filler(pos, 1 << bit, start)
                            pos = pos + take * (1 << bit)
                        return carry
                    lax.fori_loop(0, N_BUCKETS, bucket, 0)

                    def tail(t, carry):
                        filler(t * MOE_TM, MOE_TM, start)
                        return carry
                    n_tiles = xs_ref.shape[0] // (MOE_TM * ROW_SLAB)
                    lax.fori_loop(pad_hi_ref[N_BUCKETS - 1] // MOE_TM, n_tiles, tail, 0)
                fill(True)
                fill(False)


def _scatter_rows(x2, mods, k, B, S, plan, n_out):
    T, D = x2.shape
    tm = PROJ_TM
    per_b = S // tm
    return pl.pallas_call(
        _scatter_kernel,
        grid_spec=pltpu.PrefetchScalarGridSpec(
            num_scalar_prefetch=3,
            grid=(T // tm,),
            in_specs=[
                pl.BlockSpec((tm, D), lambda i, *_: (i, 0)),
                pl.BlockSpec((1, 1, D), lambda i, *_: (_mod_row(k, i // per_b, SCALE, B), 0, 0)),
                pl.BlockSpec((1, 1, D), lambda i, *_: (_mod_row(k, i // per_b, SHIFT, B), 0, 0)),
            ],
            out_specs=pl.BlockSpec(memory_space=pl.ANY),
            scratch_shapes=[pltpu.VMEM((2, tm * ROW_SLAB, LANES), F32), pltpu.SemaphoreType.DMA((2,))],
        ),
        out_shape=jax.ShapeDtypeStruct((n_out * ROW_SLAB, LANES), F32),
        compiler_params=_cparams(("arbitrary",)),
        name="scatter_rows",
    )(plan["dest"], plan["pad_lo"], plan["pad_hi"], x2, mods, mods)


def _expert_pair(x, wg_bf, wu_bf, wd_bf):
    gates = [jnp.dot(x, wg_bf[e], preferred_element_type=F32) for e in range(2)]
    ups = [jnp.dot(x, wu_bf[e], preferred_element_type=F32) for e in range(2)]
    outs = []
    for e in range(2):
        hid = gates[e] * jax.nn.sigmoid(gates[e]) * ups[e]
        outs.append(jnp.dot(hid.astype(BF16), wd_bf[e], preferred_element_type=F32))
    return outs


def _router_logit(chunks, rw_ref, rb_ref, e):
    w_row = rw_ref[pl.ds(e, 1), :]
    acc = chunks[0] * w_row[:, 0:LANES]
    for j in range(1, ROW_SLAB):
        acc = acc + chunks[j] * w_row[:, j * LANES:(j + 1) * LANES]
    return jnp.sum(acc, axis=-1, keepdims=True) + rb_ref[pl.ds(e, 1), 0:1]


def _moe_kernel(src_ref, ea_ref, eb_ref, valid_ref, na_ref, nb_ref,
                xs_ref, rw_ref, rb_ref, wg_hbm, wu_hbm, wd_hbm, o_ref,
                wg_st, wu_st, wd_st, wg_bf, wu_bf, wd_bf, sems, *, layer):
    i = pl.program_id(0)
    prev = jnp.maximum(i - 1, 0)

    def fetch(slot, e):
        return [pltpu.make_async_copy(wg_hbm.at[layer, e], wg_st.at[slot], sems.at[slot, 0]),
                pltpu.make_async_copy(wu_hbm.at[layer, e], wu_st.at[slot], sems.at[slot, 1]),
                pltpu.make_async_copy(wd_hbm.at[layer, e], wd_st.at[slot], sems.at[slot, 2])]

    for slot, e_ref, n_ref in ((0, ea_ref, na_ref), (1, eb_ref, nb_ref)):
        @pl.when(i == 0)
        def _(slot=slot, e_ref=e_ref):
            for cp in fetch(slot, e_ref[0]):
                cp.start()

        @pl.when(jnp.logical_or(i == 0, e_ref[i] != e_ref[prev]))
        def _(slot=slot, e_ref=e_ref, n_ref=n_ref):
            for cp in fetch(slot, e_ref[i]):
                cp.wait()
            wg_bf[slot] = wg_st[slot].astype(BF16)
            wu_bf[slot] = wu_st[slot].astype(BF16)
            wd_bf[slot] = wd_st[slot].astype(BF16)

            @pl.when(n_ref[i] >= 0)
            def _():
                for cp in fetch(slot, n_ref[i]):
                    cp.start()

    @pl.when(valid_ref[i] == 1)
    def _():
        chunks = _load_slab_chunks(xs_ref, MOE_TM)
        l_a = _router_logit(chunks, rw_ref, rb_ref, ea_ref[i])
        l_b = _router_logit(chunks, rw_ref, rb_ref, eb_ref[i])
        w_a = jax.nn.sigmoid(l_a - l_b)
        w_b = jax.nn.sigmoid(l_b - l_a)
        x = jnp.concatenate([c.astype(BF16) for c in chunks], axis=-1)
        y_a, y_b = _expert_pair(x, wg_bf, wu_bf, wd_bf)
        y = w_a * y_a + w_b * y_b
        _store_slabs(o_ref, y)

    @pl.when(valid_ref[i] == 0)
    def _():
        o_ref[...] = jnp.zeros_like(o_ref)


def _grouped_experts(xs, plan, router_wT, router_bt, w_gate, w_up, w_down, layer):
    n_tiles = xs.shape[0] // (MOE_TM * ROW_SLAB)
    D, Fd = D_MODEL, D_EXPERT
    row_spec = lambda index: pl.BlockSpec((MOE_TM * ROW_SLAB, LANES), index)
    hbm = pl.BlockSpec(memory_space=pl.ANY)
    return pl.pallas_call(
        functools.partial(_moe_kernel, layer=layer),
        grid_spec=pltpu.PrefetchScalarGridSpec(
            num_scalar_prefetch=6,
            grid=(n_tiles,),
            in_specs=[
                row_spec(lambda i, src, *_: (src[i], 0)),
                pl.BlockSpec((N_EXPERTS, D), lambda i, *_: (0, 0)),
                pl.BlockSpec((N_EXPERTS, LANES), lambda i, *_: (0, 0)),
                hbm, hbm, hbm,
            ],
            out_specs=row_spec(lambda i, *_: (i, 0)),
            scratch_shapes=[
                pltpu.VMEM((2, D, Fd), F32), pltpu.VMEM((2, D, Fd), F32), pltpu.VMEM((2, Fd, D), F32),
                pltpu.VMEM((2, D, Fd), BF16), pltpu.VMEM((2, D, Fd), BF16), pltpu.VMEM((2, Fd, D), BF16),
                pltpu.SemaphoreType.DMA((2, 3)),
            ],
        ),
        out_shape=jax.ShapeDtypeStruct(xs.shape, F32),
        compiler_params=_cparams(("arbitrary",)),
        name="grouped_experts",
    )(plan["tile_src"], plan["tile_ea"], plan["tile_eb"], plan["tile_valid"], plan["next_ea"], plan["next_eb"],
      xs, router_wT, router_bt, w_gate, w_up, w_down)


def _next_change(e):
    n = e.shape[0]
    idx = jnp.arange(n, dtype=jnp.int32)
    change = jnp.concatenate([jnp.ones((1,), jnp.bool_), e[1:] != e[:-1]])
    at = jnp.where(change, idx, n)
    nxt = lax.cummin(at[::-1])[::-1]
    nxt = jnp.concatenate([nxt[1:], jnp.full((1,), n, jnp.int32)])
    return jnp.where(nxt < n, jnp.take(e, jnp.minimum(nxt, n - 1)), -1).astype(jnp.int32)


def _moe_plan(bucket, rank, counts, n_tiles):
    cnt = counts[:N_BUCKETS, 0]
    tiles_b = (cnt + MOE_TM - 1) // MOE_TM
    tile_end = jnp.cumsum(tiles_b)
    tile_start = tile_end - tiles_b
    row_start = tile_start * MOE_TM
    dest = jnp.take(row_start, bucket[0]) + rank[0]
    n_used = tile_end[-1]
    t = jnp.arange(n_tiles, dtype=jnp.int32)
    t_c = jnp.minimum(t, n_used - 1)
    tb = jnp.sum((t_c[:, None] >= tile_end[None, :]).astype(jnp.int32), axis=1)
    pair_lo = jnp.array([0, 0, 0, 1, 1, 2], jnp.int32)
    pair_hi = jnp.array([1, 2, 3, 2, 3, 3], jnp.int32)
    grp = tb // PAIRS_PER_GROUP
    pr = tb % PAIRS_PER_GROUP
    tile_ea = (grp * EXPERTS_PER_GROUP + jnp.take(pair_lo, pr)).astype(jnp.int32)
    tile_eb = (grp * EXPERTS_PER_GROUP + jnp.take(pair_hi, pr)).astype(jnp.int32)
    return {
        "dest": dest.astype(jnp.int32),
        "pad_lo": (row_start + cnt).astype(jnp.int32),
        "pad_hi": (tile_end * MOE_TM).astype(jnp.int32),
        "tile_src": t_c.astype(jnp.int32),
        "tile_ea": tile_ea,
        "tile_eb": tile_eb,
        "next_ea": _next_change(tile_ea),
        "next_eb": _next_change(tile_eb),
        "tile_valid": (t < n_used).astype(jnp.int32),
    }


def _resln_kernel(dest_ref, ys_ref, x_ref, gate_ref, g_ref, b_ref, o_ref, ybuf, sems):
    i = pl.program_id(0)
    n = pl.num_programs(0)
    tm = x_ref.shape[0]

    def issue(tile, s):
        def body(c, carry):
            for u in range(DMA_CHUNK):
                r = c * DMA_CHUNK + u
                pltpu.make_async_copy(_slab(ys_ref, dest_ref[tile * tm + r]), _slab(ybuf.at[s], r), sems.at[s]).start(priority=u % 2)
            return carry
        lax.fori_loop(0, tm // DMA_CHUNK, body, 0)

    @pl.when(i == 0)
    def _():
        issue(0, 0)

    for s in range(2):
        @pl.when(lax.rem(i, 2) == s)
        def _(s=s):
            @pl.when(i + 1 < n)
            def _():
                issue(i + 1, 1 - s)

            _wait_rows(ys_ref, ybuf.at[s], sems.at[s], tm)
            y = jnp.concatenate(_load_slab_chunks(ybuf.at[s], tm), axis=-1)
            u = DEEPNORM_ALPHA * x_ref[...] + (1.0 + gate_ref[0]) * y
            o_ref[...] = _layer_norm_rows(u, g_ref[...], b_ref[...])


def _gather_residual_ln(ys, plan, x2, mods, k, B, S, g, b):
    T, D = x2.shape
    tm = PROJ_TM
    per_b = S // tm
    return pl.pallas_call(
        _resln_kernel,
        grid_spec=pltpu.PrefetchScalarGridSpec(
            num_scalar_prefetch=1,
            grid=(T // tm,),
            in_specs=[
                pl.BlockSpec(memory_space=pl.ANY),
                pl.BlockSpec((tm, D), lambda i, *_: (i, 0)),
                pl.BlockSpec((1, 1, D), lambda i, *_: (_mod_row(k, i // per_b, GATE, B), 0, 0)),
                pl.BlockSpec((1, D), lambda i, *_: (0, 0)),
                pl.BlockSpec((1, D), lambda i, *_: (0, 0)),
            ],
            out_specs=pl.BlockSpec((tm, D), lambda i, *_: (i, 0)),
            scratch_shapes=[pltpu.VMEM((2, tm * ROW_SLAB, LANES), F32), pltpu.SemaphoreType.DMA((2,))],
        ),
        out_shape=jax.ShapeDtypeStruct((T, D), F32),
        compiler_params=_cparams(("arbitrary",)),
        name="gather_residual_ln",
    )(plan["dest"], ys, x2, mods, g.reshape(1, D), b.reshape(1, D))


def _moe_sublayer(x, mods, k, router_wT, router_bt, lt, w_gate, w_up, w_down, layer, g, b):
    B, S, D = x.shape
    T = B * S
    x2 = x.reshape(T, D)
    bucket, rank, counts = _router(x2, mods, k, B, S, router_wT, router_bt, lt)
    n_tiles = T // MOE_TM + N_BUCKETS
    plan = _moe_plan(bucket, rank, counts, n_tiles)
    xs = _scatter_rows(x2, mods, k, B, S, plan, n_tiles * MOE_TM)
    ys = _grouped_experts(xs, plan, router_wT, router_bt, w_gate, w_up, w_down, layer)
    return _gather_residual_ln(ys, plan, x2, mods, k, B, S, g, b).reshape(B, S, D)


def kernel(x, c, positions, ada_w, ada_b, ln_g, ln_b, sb_w_in, sb_w_out, ret_w_in, ret_gn_g, ret_w_out,
           router_w, router_b, moe_w_gate, moe_w_up, moe_w_down):
    mods = _ada_params(c, ada_w, ada_b)
    cc, ss = _rope_tables(positions)
    tables = _retention_decay_tables()
    u = (jnp.arange(SB_TK)[:, None] < jnp.arange(SB_TK)[None, :]).astype(BF16)
    lt = (jnp.arange(ROUTE_TM)[:, None] < jnp.arange(ROUTE_TM)[None, :]).astype(BF16)
    router_wT = router_w.T
    router_bt = jnp.broadcast_to(router_b[:, None], (N_EXPERTS, LANES))
    for i in range(DEPTH):
        k_mix, k_moe = 2 * i, 2 * i + 1
        if i % 2 == 0:
            qkv = _mod_matmul(x, mods, k_mix, sb_w_in, i // 2, tn=3 * D_MODEL // 2)
            a = _sb_attention(qkv, u)
            w_out = sb_w_out
        else:
            proj = _mod_matmul(x, mods, k_mix, ret_w_in, i // 2, tn=2 * D_MODEL)
            a = _retention(proj, cc, ss, tables, ret_gn_g[i // 2])
            w_out = ret_w_out
        x = _out_proj_ln(a, w_out, i // 2, x, mods, k_mix, ln_g[i, 0], ln_b[i, 0])
        x = _moe_sublayer(x, mods, k_moe, router_wT, router_bt, lt, moe_w_gate, moe_w_up, moe_w_down, i,
                          ln_g[i, 1], ln_b[i, 1])
    return x
```

```python
import functools

import jax
import jax.numpy as jnp
from jax import lax
from jax.experimental import pallas as pl
from jax.experimental.pallas import tpu as pltpu

F32 = jnp.float32
BF16 = jnp.bfloat16

D_MODEL = 1024
DEPTH = 4
SB_HEADS = 16
SB_HEAD_DIM = D_MODEL // SB_HEADS
RET_HEADS = 8
RET_QK_DIM = D_MODEL // RET_HEADS
RET_V_DIM = 2 * RET_QK_DIM
ROPE_BASE = 10000.0
N_EXPERTS = 16
N_GROUPS = 4
EXPERTS_PER_GROUP = N_EXPERTS // N_GROUPS
D_EXPERT = D_MODEL // 2
DEEPNORM_ALPHA = (2 * DEPTH) ** 0.25
LN_EPS = 1e-5
LOG2_E = 1.4426950408889634

LANES = 128
SUBLANES = 8
MXU_DIM = 256
VMEM_LIMIT_BYTES = 56 * 1024 * 1024

PROJ_TM = 512
MM_TM = 1024
OUT_SPLIT = 4
SB_TQ = MXU_DIM
SB_TK = MXU_DIM
RET_CHUNK = 256
RET_GROUP = 4
ROUTE_TM = 512
MOE_TM = 256
PAIRS_PER_GROUP = 6
N_BUCKETS = N_GROUPS * PAIRS_PER_GROUP
BUCKET_ROWS = 32
ROW_SLAB = D_MODEL // LANES
DMA_CHUNK = 16
assert ROW_SLAB == SUBLANES


def _cparams(semantics):
    return pltpu.CompilerParams(dimension_semantics=semantics, vmem_limit_bytes=VMEM_LIMIT_BYTES)


def _split_bf16(a):
    hi = a.astype(BF16)
    lo = (a - hi.astype(F32)).astype(BF16)
    return hi, lo


def _ada_kernel(c_ref, w_ref, b_ref, o_ref):
    c = c_ref[...]
    s = c * jax.nn.sigmoid(c)
    sh, sl = _split_bf16(s)
    wh, wl = _split_bf16(w_ref[0])
    acc = jnp.dot(sh, wh, preferred_element_type=F32)
    acc += jnp.dot(sl, wh, preferred_element_type=F32)
    acc += jnp.dot(sh, wl, preferred_element_type=F32)
    o_ref[0] = acc + b_ref[0]


def _ada_params(c, ada_w, ada_b):
    B, D = c.shape
    n_sub = ada_w.shape[0] * ada_w.shape[1]
    N = ada_w.shape[-1]
    tn = N // 2
    w = ada_w.reshape(n_sub, D, N)
    b = ada_b.reshape(n_sub, 1, N)
    out = pl.pallas_call(
        _ada_kernel,
        grid=(n_sub, N // tn),
        in_specs=[
            pl.BlockSpec((B, D), lambda k, j: (0, 0)),
            pl.BlockSpec((1, D, tn), lambda k, j: (k, 0, j)),
            pl.BlockSpec((1, 1, tn), lambda k, j: (k, 0, j)),
        ],
        out_specs=pl.BlockSpec((1, B, tn), lambda k, j: (k, 0, j)),
        out_shape=jax.ShapeDtypeStruct((n_sub, B, N), F32),
        compiler_params=_cparams(("arbitrary", "arbitrary")),
        name="ada_params",
    )(c, w, b)
    return out.reshape(n_sub * B * 3, 1, D)


def _mod_row(k, b, which, B):
    return (k * B + b) * 3 + which


SHIFT, SCALE, GATE = 0, 1, 2


def _modmm_kernel(x_ref, sc_ref, sh_ref, w_ref, o_ref, w_bf):
    @pl.when(jnp.logical_and(pl.program_id(1) == 0, pl.program_id(2) == 0))
    def _():
        w_bf[...] = w_ref[0].astype(BF16)

    h = x_ref[0] * (1.0 + sc_ref[0]) + sh_ref[0]
    o_ref[0] = jnp.dot(h.astype(BF16), w_bf[...], preferred_element_type=F32).astype(o_ref.dtype)


def _mod_matmul(x, mods, k, w_all, layer, tn):
    B, S, D = x.shape
    N = w_all.shape[2]
    tm = MM_TM
    return pl.pallas_call(
        _modmm_kernel,
        grid=(N // tn, B, S // tm),
        in_specs=[
            pl.BlockSpec((1, tm, D), lambda n, b, i: (b, i, 0)),
            pl.BlockSpec((1, 1, D), lambda n, b, i: (_mod_row(k, b, SCALE, B), 0, 0)),
            pl.BlockSpec((1, 1, D), lambda n, b, i: (_mod_row(k, b, SHIFT, B), 0, 0)),
            pl.BlockSpec((1, D, tn), lambda n, b, i: (layer, 0, n)),
        ],
        out_specs=pl.BlockSpec((1, tm, tn), lambda n, b, i: (b, i, n)),
        out_shape=jax.ShapeDtypeStruct((B, S, N), BF16),
        scratch_shapes=[pltpu.VMEM((D, tn), BF16)],
        compiler_params=_cparams(("arbitrary", "arbitrary", "arbitrary")),
        name="mod_matmul",
    )(x, mods, mods, w_all)


SB_PIPE = 4
SB_UNROLL = 4


def _sb_tile_tables(nq):
    near = []
    for i in range(nq):
        near += [(i, i), (i, i - 1) if i else (nq, 0)]
    far = [(i, i - o) for o in range(2, nq) for i in range(o, nq)]
    ti, tj, spans = [], [], []
    for tiles in (near, far):
        padded = [(nq, 0)] * SB_PIPE + tiles + [(nq, 0)] * (SB_PIPE + SB_UNROLL)
        spans.append((len(ti), len(tiles)))
        ti += [t[0] for t in padded]
        tj += [t[1] for t in padded]
    return ti, tj, spans


def _sb_kernel(ti_ref, tj_ref, q_ref, k_ref, v_ref, u_ref, o_ref,
               vT_ref, qT_ref, z_ref, sp_ref, d_ref, a_ref, c_ref, run_ref, acc_ref, alive_ref, n_alive_ref, *, spans):
    S = q_ref.shape[1]
    tq, tk, dh = SB_TQ, SB_TK, SB_HEAD_DIM
    nq = S // tq

    for c in range(S // tk):
        v_c = v_ref[0, c * tk:(c + 1) * tk, :].astype(F32)
        vT_ref[c] = v_c.T.astype(BF16)

    scale = dh ** -0.5 * LOG2_E
    head_rows = lax.broadcasted_iota(jnp.int32, (2 * dh, tq), 0)
    for i in range(nq):
        qT = (q_ref[0, i * tq:(i + 1) * tq, :].astype(F32) * scale).T
        qT_ref[i, 0] = jnp.where(head_rows < dh, qT, 0.0).astype(BF16)
        qT_ref[i, 1] = jnp.where(head_rows >= dh, qT, 0.0).astype(BF16)
    qT_ref[nq] = jnp.zeros_like(qT_ref[nq])
    z_ref[...] = jnp.zeros_like(z_ref)
    sp_ref[...] = jnp.zeros_like(sp_ref)
    d_ref[...] = jnp.zeros_like(d_ref)
    a_ref[...] = jnp.zeros_like(a_ref)
    c_ref[...] = jnp.zeros_like(c_ref)
    for i in range(nq + 1):
        alive_ref[i] = jnp.int32(1)
    n_alive_ref[0] = jnp.int32(nq)
    run_ref[nq] = jnp.zeros_like(run_ref[nq])
    acc_ref[nq] = jnp.zeros_like(acc_ref[nq])

    def iteration(t1, t3, t4, p, diag2, diag3, diag4):
        q = 1 - p
        i1, j1 = ti_ref[t1], tj_ref[t1]
        i3 = ti_ref[t3]
        i4, j4 = ti_ref[t4], tj_ref[t4]
        if diag2 or diag3:
            rows = lax.broadcasted_iota(jnp.int32, (tk, tq), 0)
            cols = lax.broadcasted_iota(jnp.int32, (tk, tq), 1)
            valid = rows < cols
        u = u_ref[...]
        suf = [jnp.dot(u, sp_ref[q, h], preferred_element_type=F32) for h in range(2)]
        vT = vT_ref[j4]
        pv = [jnp.dot(vT[h * dh:(h + 1) * dh, :], a_ref[p, h], preferred_element_type=F32) for h in range(2)]
        k_blk = k_ref[0, pl.ds(pl.multiple_of(j1 * tk, tk), tk), :]
        z_new = [jnp.dot(k_blk, qT_ref[i1, h], preferred_element_type=F32) for h in range(2)]
        for h in range(2):
            z = z_ref[p, h]
            sp = jnp.maximum(z, 0.0) + jnp.log2(1.0 + jnp.exp2(-jnp.abs(z)))
            d_ref[p, h] = z - sp
            if diag2:
                sp = jnp.where(valid, sp, 0.0)
            sp_ref[p, h] = sp.astype(BF16)
        for h in range(2):
            a = jnp.exp2(d_ref[q, h] - suf[h])
            if diag3:
                a = jnp.where(valid, a, 0.0)
            a_ref[q, h] = a.astype(BF16)
            tile_mass = suf[h][0:1, :] + sp_ref[q, h, 0:1, :].astype(F32)
            if diag3:
                run_ref[i3, h, 0:1, :] = tile_mass
            else:
                run = run_ref[i3, h, 0:1, :]
                c_ref[q, h, 0:1, :] = jnp.exp2(-run)
                run_ref[i3, h, 0:1, :] = run + tile_mass
                c_next = jnp.exp2(-(run + tile_mass))
                c_max = c_next if h == 0 else jnp.maximum(c_max, c_next)
        if not diag3:
            live = jnp.logical_or(jnp.max(c_max) > 0.0, i3 == nq).astype(jnp.int32)
            n_alive_ref[0] += live - alive_ref[i3]
            alive_ref[i3] = live
        for h in range(2):
            if diag4:
                acc_ref[i4, h] = pv[h]
            else:
                acc_ref[i4, h] += pv[h] * c_ref[p, h, 0:1, :]
            z_ref[p, h] = z_new[h]

    base, n_real = spans[0]
    n_iter = n_real + SB_PIPE
    assert base % 2 == 0 and SB_PIPE % 2 == 0 and n_iter % SB_UNROLL == 0

    def near_body(t, carry):
        for r in range(SB_UNROLL):
            t1 = base + SB_PIPE + SB_UNROLL * t + r
            even = r % 2 == 0
            iteration(t1, t1 - 3, t1 - 4, r % 2, even, not even, even)
        return carry
    lax.fori_loop(0, n_iter // SB_UNROLL, near_body, 0)

    base, n_real = spans[1]
    first = base + SB_PIPE
    end = first + n_real

    def next_live(t):
        return lax.while_loop(lambda u: alive_ref[ti_ref[u]] == 0, lambda u: u + 1, t)

    def far_body(carry):
        ptr, h1, h2, h3, h4 = carry
        picks = []
        for r in range(SB_UNROLL):
            t1 = next_live(ptr)
            picks.append(t1)
            ptr = t1 + 1
        for r, t1 in enumerate(picks):
            iteration(t1, h3, h4, r % 2, False, False, False)
            h1, h2, h3, h4 = t1, h1, h2, h3
        return ptr, h1, h2, h3, h4

    start = jnp.where(n_alive_ref[0] == 0, end + SB_PIPE, first).astype(jnp.int32)
    lax.while_loop(lambda carry: carry[0] < end + SB_PIPE, far_body,
                   (start, jnp.int32(first - 1), jnp.int32(first - 2), jnp.int32(first - 3), jnp.int32(first - 4)))

    for i in range(nq):
        oT = jnp.concatenate([acc_ref[i, 0], acc_ref[i, 1]], axis=0)
        o_ref[0, i * tq:(i + 1) * tq, :] = oT.T.astype(o_ref.dtype)


def _sb_attention(qkv, u):
    B, S, _ = qkv.shape
    n_pairs = SB_HEADS // 2
    nq = S // SB_TQ
    blk = (1, S, 2 * SB_HEAD_DIM)
    ti, tj, spans = _sb_tile_tables(nq)
    tile = (SB_TK, SB_TQ)
    return pl.pallas_call(
        functools.partial(_sb_kernel, spans=spans),
        grid_spec=pltpu.PrefetchScalarGridSpec(
            num_scalar_prefetch=2,
            grid=(B, n_pairs),
            in_specs=[
                pl.BlockSpec(blk, lambda b, p, *_: (b, 0, p)),
                pl.BlockSpec(blk, lambda b, p, *_: (b, 0, n_pairs + p)),
                pl.BlockSpec(blk, lambda b, p, *_: (b, 0, 2 * n_pairs + p)),
                pl.BlockSpec((SB_TK, SB_TK), lambda b, p, *_: (0, 0)),
            ],
            out_specs=pl.BlockSpec(blk, lambda b, p, *_: (b, 0, p)),
            scratch_shapes=[
                pltpu.VMEM((S // SB_TK, 2 * SB_HEAD_DIM, SB_TK), BF16),
                pltpu.VMEM((nq + 1, 2, 2 * SB_HEAD_DIM, SB_TQ), BF16),
                pltpu.VMEM((2, 2) + tile, F32),
                pltpu.VMEM((2, 2) + tile, BF16),
                pltpu.VMEM((2, 2) + tile, F32),
                pltpu.VMEM((2, 2) + tile, BF16),
                pltpu.VMEM((2, 2, SUBLANES, SB_TQ), F32),
                pltpu.VMEM((nq + 1, 2, SUBLANES, SB_TQ), F32),
                pltpu.VMEM((nq + 1, 2, SB_HEAD_DIM, SB_TQ), F32),
                pltpu.SMEM((nq + 1,), jnp.int32),
                pltpu.SMEM((1,), jnp.int32),
            ],
        ),
        out_shape=jax.ShapeDtypeStruct((B, S, D_MODEL), BF16),
        compiler_params=_cparams(("arbitrary", "arbitrary")),
        name="sb_attention",
    )(jnp.asarray(ti, jnp.int32), jnp.asarray(tj, jnp.int32), qkv, qkv, qkv, u)


def _layer_norm_rows(u, g, b):
    mu = jnp.mean(u, axis=-1, keepdims=True)
    d = u - mu
    var = jnp.mean(d * d, axis=-1, keepdims=True)
    return d * lax.rsqrt(var + LN_EPS) * g + b


def _outln_kernel(a_ref, w_ref, x_ref, gate_ref, g_ref, b_ref, o_ref, w_bf):
    @pl.when(jnp.logical_and(pl.program_id(0) == 0, pl.program_id(1) == 0))
    def _():
        w_bf[...] = w_ref[0].astype(BF16)

    tm = a_ref.shape[1]
    sub = tm // OUT_SPLIT
    ys = [jnp.dot(a_ref[0, r * sub:(r + 1) * sub, :], w_bf[...], preferred_element_type=F32) for r in range(OUT_SPLIT)]
    for r in range(OUT_SPLIT):
        rows = slice(r * sub, (r + 1) * sub)
        u = DEEPNORM_ALPHA * x_ref[0, rows, :] + (1.0 + gate_ref[0]) * ys[r]
        o_ref[0, rows, :] = _layer_norm_rows(u, g_ref[...], b_ref[...])


def _out_proj_ln(a, w_all, layer, x, mods, k, g, b):
    B, S, D = x.shape
    K = a.shape[-1]
    tm = PROJ_TM
    return pl.pallas_call(
        _outln_kernel,
        grid=(B, S // tm),
        in_specs=[
            pl.BlockSpec((1, tm, K), lambda bb, i: (bb, i, 0)),
            pl.BlockSpec((1, K, D), lambda bb, i: (layer, 0, 0)),
            pl.BlockSpec((1, tm, D), lambda bb, i: (bb, i, 0)),
            pl.BlockSpec((1, 1, D), lambda bb, i: (_mod_row(k, bb, GATE, B), 0, 0)),
            pl.BlockSpec((1, D), lambda bb, i: (0, 0)),
            pl.BlockSpec((1, D), lambda bb, i: (0, 0)),
        ],
        out_specs=pl.BlockSpec((1, tm, D), lambda bb, i: (bb, i, 0)),
        out_shape=jax.ShapeDtypeStruct((B, S, D), F32),
        scratch_shapes=[pltpu.VMEM((K, D), BF16)],
        compiler_params=_cparams(("arbitrary", "arbitrary")),
        name="out_proj_ln",
    )(a, w_all, x, mods, g.reshape(1, D), b.reshape(1, D))


def _rope_kernel(pos_ref, invf_ref, sign_ref, cc_ref, ss_ref):
    ang = pos_ref[0].astype(F32) * invf_ref[...]
    cc_ref[0] = jnp.cos(ang)
    ss_ref[0] = jnp.sin(ang) * sign_ref[...]


def _rope_tables(positions):
    B, S = positions.shape
    half = RET_QK_DIM // 2
    inv_freq = 1.0 / (ROPE_BASE ** (jnp.arange(0, RET_QK_DIM, 2, dtype=F32) / RET_QK_DIM))
    invf2 = jnp.concatenate([inv_freq, inv_freq]).reshape(1, RET_QK_DIM)
    sign = jnp.concatenate([-jnp.ones((half,), F32), jnp.ones((half,), F32)]).reshape(1, RET_QK_DIM)
    ts = PROJ_TM
    shp = jax.ShapeDtypeStruct((B, S, RET_QK_DIM), F32)
    return pl.pallas_call(
        _rope_kernel,
        grid=(B, S // ts),
        in_specs=[
            pl.BlockSpec((1, ts, 1), lambda b, i: (b, i, 0)),
            pl.BlockSpec((1, RET_QK_DIM), lambda b, i: (0, 0)),
            pl.BlockSpec((1, RET_QK_DIM), lambda b, i: (0, 0)),
        ],
        out_specs=[pl.BlockSpec((1, ts, RET_QK_DIM), lambda b, i: (b, i, 0))] * 2,
        out_shape=[shp, shp],
        compiler_params=_cparams(("arbitrary", "arbitrary")),
        name="rope_tables",
    )(positions.reshape(B, S, 1), invf2, sign)


def _retention_decay_tables():
    C = RET_CHUNK
    kscale = RET_QK_DIM ** -0.5
    log_gamma = jnp.log(1.0 - jnp.exp2(-5.0 - jnp.arange(RET_HEADS, dtype=F32)))
    n = jnp.arange(C, dtype=F32)
    diff = n[:, None] - n[None, :]
    dmask = jnp.where(diff >= 0, jnp.exp(jnp.maximum(diff, 0.0) * log_gamma[:, None, None]), 0.0)
    xi = jnp.exp((n + 1.0) * log_gamma[:, None])
    zeta = jnp.exp((C - 1.0 - n) * log_gamma[:, None])
    chunk_decay = jnp.exp(C * log_gamma)
    dm = dmask * kscale
    xi_t = jnp.broadcast_to(xi[:, :, None], (RET_HEADS, C, RET_V_DIM))
    zeta_t = jnp.broadcast_to(zeta[:, :, None] * kscale, (RET_HEADS, C, RET_QK_DIM))
    decay_t = jnp.broadcast_to(chunk_decay[:, None, None], (RET_HEADS, RET_QK_DIM, RET_V_DIM))
    return dm, xi_t, zeta_t, decay_t


def _ret_kernel(q_ref, k_ref, v_ref, g_ref, cc_ref, ss_ref, dm_ref, xi_ref, zeta_ref, decay_ref,
                gn_ref, o_ref, state_ref):
    S = q_ref.shape[1]
    C = RET_CHUNK
    half = RET_QK_DIM // 2
    state_ref[...] = jnp.zeros_like(state_ref)

    def prep(c):
        r0 = pl.multiple_of(c * C, C)
        rows = pl.ds(r0, C)
        cc = cc_ref[0, rows, :]
        ss = ss_ref[0, rows, :]
        q = q_ref[0, rows, :].astype(F32)
        k = k_ref[0, rows, :].astype(F32)
        qr = q * cc + pltpu.roll(q, half, 1) * ss
        kr = k * cc + pltpu.roll(k, half, 1) * ss
        kzT = (kr * zeta_ref[0]).T.astype(BF16)
        return rows, qr.astype(BF16), kr.astype(BF16), v_ref[0, rows, :], kzT

    def finish(rows, inner, cross):
        o = inner + cross * xi_ref[0]
        mu = jnp.mean(o, axis=-1, keepdims=True)
        d = o - mu
        var = jnp.mean(d * d, axis=-1, keepdims=True)
        on = d * lax.rsqrt(var + LN_EPS) * gn_ref[...]
        g = g_ref[0, rows, :].astype(F32)
        o_ref[0, rows, :] = (g * jax.nn.sigmoid(g) * on).astype(o_ref.dtype)

    def chunk_group(t, carry):
        nt = (((1,), (1,)), ((), ()))
        ps = [prep(RET_GROUP * t + r) for r in range(RET_GROUP)]
        qks = [lax.dot_general(p[1], p[2], nt, preferred_element_type=F32) for p in ps]
        dss = [jnp.dot(p[4], p[3], preferred_element_type=F32) for p in ps]
        state = state_ref[...]
        crosses = []
        for p, ds in zip(ps, dss):
            crosses.append(jnp.dot(p[1], state.astype(BF16), preferred_element_type=F32))
            state = state * decay_ref[0] + ds
        state_ref[...] = state
        inners = [jnp.dot((qk * dm_ref[0]).astype(BF16), p[3], preferred_element_type=F32) for qk, p in zip(qks, ps)]
        for p, inner, cross in zip(ps, inners, crosses):
            finish(p[0], inner, cross)
        return carry

    lax.fori_loop(0, S // (RET_GROUP * C), chunk_group, 0)


def _retention(proj, cc, ss, tables, gn_g):
    B, S, _ = proj.shape
    dk, dv, H, C = RET_QK_DIM, RET_V_DIM, RET_HEADS, RET_CHUNK
    dm, xi_t, zeta_t, decay_t = tables
    return pl.pallas_call(
        _ret_kernel,
        grid=(B, H),
        in_specs=[
            pl.BlockSpec((1, S, dk), lambda b, h: (b, 0, h)),
            pl.BlockSpec((1, S, dk), lambda b, h: (b, 0, H + h)),
            pl.BlockSpec((1, S, dv), lambda b, h: (b, 0, H + h)),
            pl.BlockSpec((1, S, dv), lambda b, h: (b, 0, 2 * H + h)),
            pl.BlockSpec((1, S, dk), lambda b, h: (b, 0, 0)),
            pl.BlockSpec((1, S, dk), lambda b, h: (b, 0, 0)),
            pl.BlockSpec((1, C, C), lambda b, h: (h, 0, 0)),
            pl.BlockSpec((1, C, dv), lambda b, h: (h, 0, 0)),
            pl.BlockSpec((1, C, dk), lambda b, h: (h, 0, 0)),
            pl.BlockSpec((1, dk, dv), lambda b, h: (h, 0, 0)),
            pl.BlockSpec((1, dv), lambda b, h: (0, h)),
        ],
        out_specs=pl.BlockSpec((1, S, dv), lambda b, h: (b, 0, h)),
        out_shape=jax.ShapeDtypeStruct((B, S, H * dv), BF16),
        scratch_shapes=[pltpu.VMEM((dk, dv), F32)],
        compiler_params=_cparams(("arbitrary", "arbitrary")),
        name="retention",
    )(proj, proj, proj, proj, cc, ss, dm, xi_t, zeta_t, decay_t, gn_g.reshape(1, H * dv))


def _store_slabs(ref, val):
    tm = val.shape[0]
    for j in range(ROW_SLAB):
        ref[pl.ds(j, tm, stride=ROW_SLAB), :] = val[:, j * LANES:(j + 1) * LANES]


def _load_slab_chunks(ref, tm):
    return [ref[pl.ds(j, tm, stride=ROW_SLAB), :] for j in range(ROW_SLAB)]


def _top2_flags(vals):
    n = len(vals)
    rank = []
    for k in range(n):
        r = jnp.zeros_like(vals[0])
        for j in range(n):
            if j < k:
                r = r + (vals[j] >= vals[k]).astype(F32)
            elif j > k:
                r = r + (vals[j] > vals[k]).astype(F32)
        rank.append(r)
    is1 = [r == 0.0 for r in rank]
    is2 = [r == 1.0 for r in rank]
    m1 = sum(jnp.where(f, v, 0.0) for f, v in zip(is1, vals))
    m2 = sum(jnp.where(f, v, 0.0) for f, v in zip(is2, vals))
    return is1, is2, m1, m2


def _router_kernel(x_ref, sc_ref, sh_ref, rw_ref, rb_ref, lt_ref,
                   bucket_ref, rank_ref, counts_ref, carry_ref):
    i = pl.program_id(0)
    tm = x_ref.shape[0]

    @pl.when(i == 0)
    def _():
        carry_ref[...] = jnp.zeros_like(carry_ref)

    h = x_ref[...] * (1.0 + sc_ref[0]) + sh_ref[0]
    nt = (((1,), (1,)), ((), ()))
    logits = lax.dot_general(rw_ref[...].astype(BF16), h.astype(BF16), nt, preferred_element_type=F32)
    logits = logits + rb_ref[:, 0:1]

    rows = [logits[e:e + 1, :] for e in range(N_EXPERTS)]
    gmax = functools.reduce(jnp.maximum, rows)
    groups = []
    for g in range(N_GROUPS):
        vals = rows[g * EXPERTS_PER_GROUP:(g + 1) * EXPERTS_PER_GROUP]
        is1, is2, m1, m2 = _top2_flags(vals)
        score = jnp.exp(m1 - gmax) + jnp.exp(m2 - gmax)
        groups.append(([jnp.logical_or(a, b) for a, b in zip(is1, is2)], score))
    best = groups[0][1]
    gsel = jnp.zeros_like(best)
    for g in range(1, N_GROUPS):
        better = groups[g][1] > best
        best = jnp.where(better, groups[g][1], best)
        gsel = jnp.where(better, float(g), gsel)
    sel = [jnp.zeros_like(best) > 1.0] * EXPERTS_PER_GROUP
    for g in range(N_GROUPS):
        here = gsel == float(g)
        sel = [jnp.logical_or(s, jnp.logical_and(here, f)) for s, f in zip(sel, groups[g][0])]
    pair = jnp.where(sel[0],
                     jnp.where(sel[1], 0.0, jnp.where(sel[2], 1.0, 2.0)),
                     jnp.where(sel[1], jnp.where(sel[2], 3.0, 4.0), 5.0))
    bucket = (gsel * float(PAIRS_PER_GROUP) + pair).astype(jnp.int32)

    brow = lax.broadcasted_iota(jnp.int32, (BUCKET_ROWS, tm), 0)
    onehot = (brow == bucket).astype(F32)
    before = jnp.dot(onehot.astype(BF16), lt_ref[...], preferred_element_type=F32)
    carry = carry_ref[...]
    in_tile = jnp.sum(onehot * before, axis=0, keepdims=True)
    base = jnp.sum(onehot * carry[:, 0:1], axis=0, keepdims=True)
    rank_ref[...] = (base + in_tile).astype(jnp.int32)
    bucket_ref[...] = bucket
    new_carry = carry + jnp.sum(onehot, axis=1, keepdims=True)
    carry_ref[...] = new_carry
    counts_ref[...] = new_carry.astype(jnp.int32)


def _router(x2, mods, k, B, S, router_wT, router_bt, lt):
    T, D = x2.shape
    tm = ROUTE_TM
    per_b = S // tm
    return pl.pallas_call(
        _router_kernel,
        grid=(T // tm,),
        in_specs=[
            pl.BlockSpec((tm, D), lambda i: (i, 0)),
            pl.BlockSpec((1, 1, D), lambda i: (_mod_row(k, i // per_b, SCALE, B), 0, 0)),
            pl.BlockSpec((1, 1, D), lambda i: (_mod_row(k, i // per_b, SHIFT, B), 0, 0)),
            pl.BlockSpec((N_EXPERTS, D), lambda i: (0, 0)),
            pl.BlockSpec((N_EXPERTS, LANES), lambda i: (0, 0)),
            pl.BlockSpec((tm, tm), lambda i: (0, 0)),
        ],
        out_specs=[
            pl.BlockSpec((1, tm), lambda i: (0, i)),
            pl.BlockSpec((1, tm), lambda i: (0, i)),
            pl.BlockSpec((BUCKET_ROWS, LANES), lambda i: (0, 0)),
        ],
        out_shape=[
            jax.ShapeDtypeStruct((1, T), jnp.int32),
            jax.ShapeDtypeStruct((1, T), jnp.int32),
            jax.ShapeDtypeStruct((BUCKET_ROWS, LANES), jnp.int32),
        ],
        scratch_shapes=[pltpu.VMEM((BUCKET_ROWS, LANES), F32)],
        compiler_params=_cparams(("arbitrary",)),
        name="router",
    )(x2, mods, mods, router_wT, router_bt, lt)


def _slab(ref, r):
    return ref.at[pl.ds(pl.multiple_of(r * ROW_SLAB, ROW_SLAB), ROW_SLAB)]


def _wait_rows(src_ref, dst_ref, sem, n):
    def body(c, carry):
        for _ in range(DMA_CHUNK):
            pltpu.make_async_copy(_slab(src_ref, 0), _slab(dst_ref, 0), sem).wait()
        return carry
    lax.fori_loop(0, n // DMA_CHUNK, body, 0)


def _scatter_kernel(dest_ref, pad_lo_ref, pad_hi_ref, x_ref, sc_ref, sh_ref, xs_ref, hbuf, sems):
    i = pl.program_id(0)
    last = pl.num_programs(0) - 1
    tm = x_ref.shape[0]
    h = x_ref[...] * (1.0 + sc_ref[0]) + sh_ref[0]

    for s in range(2):
        @pl.when(lax.rem(i, 2) == s)
        def _(s=s):
            buf = hbuf.at[s]
            _store_slabs(buf, h)

            def issue(c, carry):
                for u in range(DMA_CHUNK):
                    r = c * DMA_CHUNK + u
                    pltpu.make_async_copy(_slab(buf, r), _slab(xs_ref, dest_ref[i * tm + r]), sems.at[s]).start(priority=u % 2)
                return carry
            lax.fori_loop(0, tm // DMA_CHUNK, issue, 0)

            @pl.when(i > 0)
            def _():
                _wait_rows(hbuf.at[1 - s], xs_ref, sems.at[1 - s], tm)

            @pl.when(i == last)
            def _():
                _wait_rows(buf, xs_ref, sems.at[s], tm)

                def filler(row0, n_rows, start):
                    cp = pltpu.make_async_copy(
                        buf.at[pl.ds(0, n_rows * ROW_SLAB)],
                        xs_ref.at[pl.ds(pl.multiple_of(row0 * ROW_SLAB, ROW_SLAB), n_rows * ROW_SLAB)],
                        sems.at[s])
                    if start:
                        cp.start()
                    else:
                        cp.wait()

                def fill(start):
                    def bucket(b, carry):
                        pos = pad_lo_ref[b]
                        n = pad_hi_ref[b] - pos
                        for bit in reversed(range(MOE_TM.bit_length() - 1)):
                            take = lax.rem(lax.shift_right_logical(n, bit), 2)

                            @pl.when(take == 1)
                            def _(pos=pos, bit=bit):
                                filler(pos, 1 << bit, start)
                            pos = pos + take * (1 << bit)
                        return carry
                    lax.fori_loop(0, N_BUCKETS, bucket, 0)

                    def tail(t, carry):
                        filler(t * MOE_TM, MOE_TM, start)
                        return carry
                    n_tiles = xs_ref.shape[0] // (MOE_TM * ROW_SLAB)
                    lax.fori_loop(pad_hi_ref[N_BUCKETS - 1] // MOE_TM, n_tiles, tail, 0)
                fill(True)
                fill(False)


def _scatter_rows(x2, mods, k, B, S, plan, n_out):
    T, D = x2.shape
    tm = PROJ_TM
    per_b = S // tm
    return pl.pallas_call(
        _scatter_kernel,
        grid_spec=pltpu.PrefetchScalarGridSpec(
            num_scalar_prefetch=3,
            grid=(T // tm,),
            in_specs=[
                pl.BlockSpec((tm, D), lambda i, *_: (i, 0)),
                pl.BlockSpec((1, 1, D), lambda i, *_: (_mod_row(k, i // per_b, SCALE, B), 0, 0)),
                pl.BlockSpec((1, 1, D), lambda i, *_: (_mod_row(k, i // per_b, SHIFT, B), 0, 0)),
            ],
            out_specs=pl.BlockSpec(memory_space=pl.ANY),
            scratch_shapes=[pltpu.VMEM((2, tm * ROW_SLAB, LANES), F32), pltpu.SemaphoreType.DMA((2,))],
        ),
        out_shape=jax.ShapeDtypeStruct((n_out * ROW_SLAB, LANES), F32),
        compiler_params=_cparams(("arbitrary",)),
        name="scatter_rows",
    )(plan["dest"], plan["pad_lo"], plan["pad_hi"], x2, mods, mods)


def _expert_pair(x, wg_bf, wu_bf, wd_bf):
    gates = [jnp.dot(x, wg_bf[e], preferred_element_type=F32) for e in range(2)]
    ups = [jnp.dot(x, wu_bf[e], preferred_element_type=F32) for e in range(2)]
    outs = []
    for e in range(2):
        hid = gates[e] * jax.nn.sigmoid(gates[e]) * ups[e]
        outs.append(jnp.dot(hid.astype(BF16), wd_bf[e], preferred_element_type=F32))
    return outs


def _router_logit(chunks, rw_ref, rb_ref, e):
    w_row = rw_ref[pl.ds(e, 1), :]
    acc = chunks[0] * w_row[:, 0:LANES]
    for j in range(1, ROW_SLAB):
        acc = acc + chunks[j] * w_row[:, j * LANES:(j + 1) * LANES]
    return jnp.sum(acc, axis=-1, keepdims=True) + rb_ref[pl.ds(e, 1), 0:1]


def _moe_kernel(src_ref, ea_ref, eb_ref, valid_ref, na_ref, nb_ref,
                xs_ref, rw_ref, rb_ref, wg_hbm, wu_hbm, wd_hbm, o_ref,
                wg_st, wu_st, wd_st, wg_bf, wu_bf, wd_bf, sems, *, layer):
    i = pl.program_id(0)
    prev = jnp.maximum(i - 1, 0)

    def fetch(slot, e):
        return [pltpu.make_async_copy(wg_hbm.at[layer, e], wg_st.at[slot], sems.at[slot, 0]),
                pltpu.make_async_copy(wu_hbm.at[layer, e], wu_st.at[slot], sems.at[slot, 1]),
                pltpu.make_async_copy(wd_hbm.at[layer, e], wd_st.at[slot], sems.at[slot, 2])]

    for slot, e_ref, n_ref in ((0, ea_ref, na_ref), (1, eb_ref, nb_ref)):
        @pl.when(i == 0)
        def _(slot=slot, e_ref=e_ref):
            for cp in fetch(slot, e_ref[0]):
                cp.start()

        @pl.when(jnp.logical_or(i == 0, e_ref[i] != e_ref[prev]))
        def _(slot=slot, e_ref=e_ref, n_ref=n_ref):
            for cp in fetch(slot, e_ref[i]):
                cp.wait()
            wg_bf[slot] = wg_st[slot].astype(BF16)
            wu_bf[slot] = wu_st[slot].astype(BF16)
            wd_bf[slot] = wd_st[slot].astype(BF16)

            @pl.when(n_ref[i] >= 0)
            def _():
                for cp in fetch(slot, n_ref[i]):
                    cp.start()

    @pl.when(valid_ref[i] == 1)
    def _():
        chunks = _load_slab_chunks(xs_ref, MOE_TM)
        l_a = _router_logit(chunks, rw_ref, rb_ref, ea_ref[i])
        l_b = _router_logit(chunks, rw_ref, rb_ref, eb_ref[i])
        w_a = jax.nn.sigmoid(l_a - l_b)
        w_b = jax.nn.sigmoid(l_b - l_a)
        x = jnp.concatenate([c.astype(BF16) for c in chunks], axis=-1)
        y_a, y_b = _expert_pair(x, wg_bf, wu_bf, wd_bf)
        y = w_a * y_a + w_b * y_b
        _store_slabs(o_ref, y)

    @pl.when(valid_ref[i] == 0)
    def _():
        o_ref[...] = jnp.zeros_like(o_ref)


def _grouped_experts(xs, plan, router_wT, router_bt, w_gate, w_up, w_down, layer):
    n_tiles = xs.shape[0] // (MOE_TM * ROW_SLAB)
    D, Fd = D_MODEL, D_EXPERT
    row_spec = lambda index: pl.BlockSpec((MOE_TM * ROW_SLAB, LANES), index)
    hbm = pl.BlockSpec(memory_space=pl.ANY)
    return pl.pallas_call(
        functools.partial(_moe_kernel, layer=layer),
        grid_spec=pltpu.PrefetchScalarGridSpec(
            num_scalar_prefetch=6,
            grid=(n_tiles,),
            in_specs=[
                row_spec(lambda i, src, *_: (src[i], 0)),
                pl.BlockSpec((N_EXPERTS, D), lambda i, *_: (0, 0)),
                pl.BlockSpec((N_EXPERTS, LANES), lambda i, *_: (0, 0)),
                hbm, hbm, hbm,
            ],
            out_specs=row_spec(lambda i, *_: (i, 0)),
            scratch_shapes=[
                pltpu.VMEM((2, D, Fd), F32), pltpu.VMEM((2, D, Fd), F32), pltpu.VMEM((2, Fd, D), F32),
                pltpu.VMEM((2, D, Fd), BF16), pltpu.VMEM((2, D, Fd), BF16), pltpu.VMEM((2, Fd, D), BF16),
                pltpu.SemaphoreType.DMA((2, 3)),
            ],
        ),
        out_shape=jax.ShapeDtypeStruct(xs.shape, F32),
        compiler_params=_cparams(("arbitrary",)),
        name="grouped_experts",
    )(plan["tile_src"], plan["tile_ea"], plan["tile_eb"], plan["tile_valid"], plan["next_ea"], plan["next_eb"],
      xs, router_wT, router_bt, w_gate, w_up, w_down)


def _next_change(e):
    n = e.shape[0]
    idx = jnp.arange(n, dtype=jnp.int32)
    change = jnp.concatenate([jnp.ones((1,), jnp.bool_), e[1:] != e[:-1]])
    at = jnp.where(change, idx, n)
    nxt = lax.cummin(at[::-1])[::-1]
    nxt = jnp.concatenate([nxt[1:], jnp.full((1,), n, jnp.int32)])
    return jnp.where(nxt < n, jnp.take(e, jnp.minimum(nxt, n - 1)), -1).astype(jnp.int32)


def _moe_plan(bucket, rank, counts, n_tiles):
    cnt = counts[:N_BUCKETS, 0]
    tiles_b = (cnt + MOE_TM - 1) // MOE_TM
    tile_end = jnp.cumsum(tiles_b)
    tile_start = tile_end - tiles_b
    row_start = tile_start * MOE_TM
    dest = jnp.take(row_start, bucket[0]) + rank[0]
    n_used = tile_end[-1]
    t = jnp.arange(n_tiles, dtype=jnp.int32)
    t_c = jnp.minimum(t, n_used - 1)
    tb = jnp.sum((t_c[:, None] >= tile_end[None, :]).astype(jnp.int32), axis=1)
    pair_lo = jnp.array([0, 0, 0, 1, 1, 2], jnp.int32)
    pair_hi = jnp.array([1, 2, 3, 2, 3, 3], jnp.int32)
    grp = tb // PAIRS_PER_GROUP
    pr = tb % PAIRS_PER_GROUP
    tile_ea = (grp * EXPERTS_PER_GROUP + jnp.take(pair_lo, pr)).astype(jnp.int32)
    tile_eb = (grp * EXPERTS_PER_GROUP + jnp.take(pair_hi, pr)).astype(jnp.int32)
    return {
        "dest": dest.astype(jnp.int32),
        "pad_lo": (row_start + cnt).astype(jnp.int32),
        "pad_hi": (tile_end * MOE_TM).astype(jnp.int32),
        "tile_src": t_c.astype(jnp.int32),
        "tile_ea": tile_ea,
        "tile_eb": tile_eb,
        "next_ea": _next_change(tile_ea),
        "next_eb": _next_change(tile_eb),
        "tile_valid": (t < n_used).astype(jnp.int32),
    }


def _resln_kernel(dest_ref, ys_ref, x_ref, gate_ref, g_ref, b_ref, o_ref, ybuf, sems):
    i = pl.program_id(0)
    n = pl.num_programs(0)
    tm = x_ref.shape[0]

    def issue(tile, s):
        def body(c, carry):
            for u in range(DMA_CHUNK):
                r = c * DMA_CHUNK + u
                pltpu.make_async_copy(_slab(ys_ref, dest_ref[tile * tm + r]), _slab(ybuf.at[s], r), sems.at[s]).start(priority=u % 2)
            return carry
        lax.fori_loop(0, tm // DMA_CHUNK, body, 0)

    @pl.when(i == 0)
    def _():
        issue(0, 0)

    for s in range(2):
        @pl.when(lax.rem(i, 2) == s)
        def _(s=s):
            @pl.when(i + 1 < n)
            def _():
                issue(i + 1, 1 - s)

            _wait_rows(ys_ref, ybuf.at[s], sems.at[s], tm)
            y = jnp.concatenate(_load_slab_chunks(ybuf.at[s], tm), axis=-1)
            u = DEEPNORM_ALPHA * x_ref[...] + (1.0 + gate_ref[0]) * y
            o_ref[...] = _layer_norm_rows(u, g_ref[...], b_ref[...])


def _gather_residual_ln(ys, plan, x2, mods, k, B, S, g, b):
    T, D = x2.shape
    tm = PROJ_TM
    per_b = S // tm
    return pl.pallas_call(
        _resln_kernel,
        grid_spec=pltpu.PrefetchScalarGridSpec(
            num_scalar_prefetch=1,
            grid=(T // tm,),
            in_specs=[
                pl.BlockSpec(memory_space=pl.ANY),
                pl.BlockSpec((tm, D), lambda i, *_: (i, 0)),
                pl.BlockSpec((1, 1, D), lambda i, *_: (_mod_row(k, i // per_b, GATE, B), 0, 0)),
                pl.BlockSpec((1, D), lambda i, *_: (0, 0)),
                pl.BlockSpec((1, D), lambda i, *_: (0, 0)),
            ],
            out_specs=pl.BlockSpec((tm, D), lambda i, *_: (i, 0)),
            scratch_shapes=[pltpu.VMEM((2, tm * ROW_SLAB, LANES), F32), pltpu.SemaphoreType.DMA((2,))],
        ),
        out_shape=jax.ShapeDtypeStruct((T, D), F32),
        compiler_params=_cparams(("arbitrary",)),
        name="gather_residual_ln",
    )(plan["dest"], ys, x2, mods, g.reshape(1, D), b.reshape(1, D))


def _moe_sublayer(x, mods, k, router_wT, router_bt, lt, w_gate, w_up, w_down, layer, g, b):
    B, S, D = x.shape
    T = B * S
    x2 = x.reshape(T, D)
    bucket, rank, counts = _router(x2, mods, k, B, S, router_wT, router_bt, lt)
    n_tiles = T // MOE_TM + N_BUCKETS
    plan = _moe_plan(bucket, rank, counts, n_tiles)
    xs = _scatter_rows(x2, mods, k, B, S, plan, n_tiles * MOE_TM)
    ys = _grouped_experts(xs, plan, router_wT, router_bt, w_gate, w_up, w_down, layer)
    return _gather_residual_ln(ys, plan, x2, mods, k, B, S, g, b).reshape(B, S, D)


def kernel(x, c, positions, ada_w, ada_b, ln_g, ln_b, sb_w_in, sb_w_out, ret_w_in, ret_gn_g, ret_w_out,
           router_w, router_b, moe_w_gate, moe_w_up, moe_w_down):
    mods = _ada_params(c, ada_w, ada_b)
    cc, ss = _rope_tables(positions)
    tables = _retention_decay_tables()
    u = (jnp.arange(SB_TK)[:, None] < jnp.arange(SB_TK)[None, :]).astype(BF16)
    lt = (jnp.arange(ROUTE_TM)[:, None] < jnp.arange(ROUTE_TM)[None, :]).astype(BF16)
    router_wT = router_w.T
    router_bt = jnp.broadcast_to(router_b[:, None], (N_EXPERTS, LANES))
    for i in range(DEPTH):
        k_mix, k_moe = 2 * i, 2 * i + 1
        if i % 2 == 0:
            qkv = _mod_matmul(x, mods, k_mix, sb_w_in, i // 2, tn=3 * D_MODEL // 2)
            a = _sb_attention(qkv, u)
            w_out = sb_w_out
        else:
            proj = _mod_matmul(x, mods, k_mix, ret_w_in, i // 2, tn=2 * D_MODEL)
            a = _retention(proj, cc, ss, tables, ret_gn_g[i // 2])
            w_out = ret_w_out
        x = _out_proj_ln(a, w_out, i // 2, x, mods, k_mix, ln_g[i, 0], ln_b[i, 0])
        x = _moe_sublayer(x, mods, k_moe, router_wT, router_bt, lt, moe_w_gate, moe_w_up, moe_w_down, i,
                          ln_g[i, 1], ln_b[i, 1])
    return x
```

```python
import functools

import jax
import jax.numpy as jnp
from jax import lax
from jax.experimental import pallas as pl
from jax.experimental.pallas import tpu as pltpu

F32 = jnp.float32
BF16 = jnp.bfloat16

D_MODEL = 1024
DEPTH = 4
SB_HEADS = 16
SB_HEAD_DIM = D_MODEL // SB_HEADS
RET_HEADS = 8
RET_QK_DIM = D_MODEL // RET_HEADS
RET_V_DIM = 2 * RET_QK_DIM
ROPE_BASE = 10000.0
N_EXPERTS = 16
N_GROUPS = 4
EXPERTS_PER_GROUP = N_EXPERTS // N_GROUPS
D_EXPERT = D_MODEL // 2
DEEPNORM_ALPHA = (2 * DEPTH) ** 0.25
LN_EPS = 1e-5
LOG2_E = 1.4426950408889634

LANES = 128
SUBLANES = 8
MXU_DIM = 256
VMEM_LIMIT_BYTES = 56 * 1024 * 1024

PROJ_TM = 512
MM_TM = 1024
OUT_SPLIT = 4
SB_TQ = MXU_DIM
SB_TK = MXU_DIM
RET_CHUNK = 256
RET_GROUP = 4
MOE_TM = 256
PAIRS_PER_GROUP = 6
N_BUCKETS = N_GROUPS * PAIRS_PER_GROUP
BUCKET_ROWS = 32
ROW_SLAB = D_MODEL // LANES
DMA_CHUNK = 16
assert ROW_SLAB == SUBLANES


def _cparams(semantics):
    return pltpu.CompilerParams(dimension_semantics=semantics, vmem_limit_bytes=VMEM_LIMIT_BYTES)


def _split_bf16(a):
    hi = a.astype(BF16)
    lo = (a - hi.astype(F32)).astype(BF16)
    return hi, lo


def _ada_kernel(c_ref, w_ref, b_ref, o_ref):
    c = c_ref[...]
    s = c * jax.nn.sigmoid(c)
    sh, sl = _split_bf16(s)
    wh, wl = _split_bf16(w_ref[0])
    acc = jnp.dot(sh, wh, preferred_element_type=F32)
    acc += jnp.dot(sl, wh, preferred_element_type=F32)
    acc += jnp.dot(sh, wl, preferred_element_type=F32)
    o_ref[0] = acc + b_ref[0]


def _ada_params(c, ada_w, ada_b):
    B, D = c.shape
    n_sub = ada_w.shape[0] * ada_w.shape[1]
    N = ada_w.shape[-1]
    tn = N // 2
    w = ada_w.reshape(n_sub, D, N)
    b = ada_b.reshape(n_sub, 1, N)
    out = pl.pallas_call(
        _ada_kernel,
        grid=(n_sub, N // tn),
        in_specs=[
            pl.BlockSpec((B, D), lambda k, j: (0, 0)),
            pl.BlockSpec((1, D, tn), lambda k, j: (k, 0, j)),
            pl.BlockSpec((1, 1, tn), lambda k, j: (k, 0, j)),
        ],
        out_specs=pl.BlockSpec((1, B, tn), lambda k, j: (k, 0, j)),
        out_shape=jax.ShapeDtypeStruct((n_sub, B, N), F32),
        compiler_params=_cparams(("arbitrary", "arbitrary")),
        name="ada_params",
    )(c, w, b)
    return out.reshape(n_sub * B * 3, 1, D)


def _mod_row(k, b, which, B):
    return (k * B + b) * 3 + which


SHIFT, SCALE, GATE = 0, 1, 2


def _modmm_kernel(x_ref, sc_ref, sh_ref, w_ref, o_ref, w_bf):
    @pl.when(jnp.logical_and(pl.program_id(1) == 0, pl.program_id(2) == 0))
    def _():
        w_bf[...] = w_ref[0].astype(BF16)

    h = x_ref[0] * (1.0 + sc_ref[0]) + sh_ref[0]
    o_ref[0] = jnp.dot(h.astype(BF16), w_bf[...], preferred_element_type=F32).astype(o_ref.dtype)


def _mod_matmul(x, mods, k, w_all, layer, tn):
    B, S, D = x.shape
    N = w_all.shape[2]
    tm = MM_TM
    return pl.pallas_call(
        _modmm_kernel,
        grid=(N // tn, B, S // tm),
        in_specs=[
            pl.BlockSpec((1, tm, D), lambda n, b, i: (b, i, 0)),
            pl.BlockSpec((1, 1, D), lambda n, b, i: (_mod_row(k, b, SCALE, B), 0, 0)),
            pl.BlockSpec((1, 1, D), lambda n, b, i: (_mod_row(k, b, SHIFT, B), 0, 0)),
            pl.BlockSpec((1, D, tn), lambda n, b, i: (layer, 0, n)),
        ],
        out_specs=pl.BlockSpec((1, tm, tn), lambda n, b, i: (b, i, n)),
        out_shape=jax.ShapeDtypeStruct((B, S, N), BF16),
        scratch_shapes=[pltpu.VMEM((D, tn), BF16)],
        compiler_params=_cparams(("arbitrary", "arbitrary", "arbitrary")),
        name="mod_matmul",
    )(x, mods, mods, w_all)


SB_PIPE = 4
SB_UNROLL = 4


def _sb_tile_tables(nq):
    near = []
    for i in range(nq):
        near += [(i, i), (i, i - 1) if i else (nq, 0)]
    far = [(i, i - o) for o in range(2, nq) for i in range(o, nq)]
    ti, tj, spans = [], [], []
    for tiles in (near, far):
        padded = [(nq, 0)] * SB_PIPE + tiles + [(nq, 0)] * (SB_PIPE + SB_UNROLL)
        spans.append((len(ti), len(tiles)))
        ti += [t[0] for t in padded]
        tj += [t[1] for t in padded]
    return ti, tj, spans


def _sb_kernel(ti_ref, tj_ref, q_ref, k_ref, v_ref, u_ref, o_ref,
               vT_ref, qT_ref, z_ref, sp_ref, d_ref, a_ref, c_ref, run_ref, acc_ref, alive_ref, n_alive_ref, *, spans):
    S = q_ref.shape[1]
    tq, tk, dh = SB_TQ, SB_TK, SB_HEAD_DIM
    nq = S // tq

    for c in range(S // tk):
        v_c = v_ref[0, c * tk:(c + 1) * tk, :].astype(F32)
        vT_ref[c] = v_c.T.astype(BF16)

    scale = dh ** -0.5 * LOG2_E
    head_rows = lax.broadcasted_iota(jnp.int32, (2 * dh, tq), 0)
    for i in range(nq):
        qT = (q_ref[0, i * tq:(i + 1) * tq, :].astype(F32) * scale).T
        qT_ref[i, 0] = jnp.where(head_rows < dh, qT, 0.0).astype(BF16)
        qT_ref[i, 1] = jnp.where(head_rows >= dh, qT, 0.0).astype(BF16)
    qT_ref[nq] = jnp.zeros_like(qT_ref[nq])
    z_ref[...] = jnp.zeros_like(z_ref)
    sp_ref[...] = jnp.zeros_like(sp_ref)
    d_ref[...] = jnp.zeros_like(d_ref)
    a_ref[...] = jnp.zeros_like(a_ref)
    c_ref[...] = jnp.zeros_like(c_ref)
    for i in range(nq + 1):
        alive_ref[i] = jnp.int32(1)
    n_alive_ref[0] = jnp.int32(nq)
    run_ref[nq] = jnp.zeros_like(run_ref[nq])
    acc_ref[nq] = jnp.zeros_like(acc_ref[nq])

    def iteration(t1, t3, t4, p, diag2, diag3, diag4):
        q = 1 - p
        i1, j1 = ti_ref[t1], tj_ref[t1]
        i3 = ti_ref[t3]
        i4, j4 = ti_ref[t4], tj_ref[t4]
        if diag2 or diag3:
            rows = lax.broadcasted_iota(jnp.int32, (tk, tq), 0)
            cols = lax.broadcasted_iota(jnp.int32, (tk, tq), 1)
            valid = rows < cols
        u = u_ref[...]
        suf = [jnp.dot(u, sp_ref[q, h], preferred_element_type=F32) for h in range(2)]
        vT = vT_ref[j4]
        pv = [jnp.dot(vT[h * dh:(h + 1) * dh, :], a_ref[p, h], preferred_element_type=F32) for h in range(2)]
        k_blk = k_ref[0, pl.ds(pl.multiple_of(j1 * tk, tk), tk), :]
        z_new = [jnp.dot(k_blk, qT_ref[i1, h], preferred_element_type=F32) for h in range(2)]
        for h in range(2):
            z = z_ref[p, h]
            sp = jnp.maximum(z, 0.0) + jnp.log2(1.0 + jnp.exp2(-jnp.abs(z)))
            d_ref[p, h] = z - sp
            if diag2:
                sp = jnp.where(valid, sp, 0.0)
            sp_ref[p, h] = sp.astype(BF16)
        for h in range(2):
            a = jnp.exp2(d_ref[q, h] - suf[h])
            if diag3:
                a = jnp.where(valid, a, 0.0)
            a_ref[q, h] = a.astype(BF16)
            tile_mass = suf[h][0:1, :] + sp_ref[q, h, 0:1, :].astype(F32)
            if diag3:
                run_ref[i3, h, 0:1, :] = tile_mass
            else:
                run = run_ref[i3, h, 0:1, :]
                c_ref[q, h, 0:1, :] = jnp.exp2(-run)
                run_ref[i3, h, 0:1, :] = run + tile_mass
                c_next = jnp.exp2(-(run + tile_mass))
                c_max = c_next if h == 0 else jnp.maximum(c_max, c_next)
        if not diag3:
            live = jnp.logical_or(jnp.max(c_max) > 0.0, i3 == nq).astype(jnp.int32)
            n_alive_ref[0] += live - alive_ref[i3]
            alive_ref[i3] = live
        for h in range(2):
            if diag4:
                acc_ref[i4, h] = pv[h]
            else:
                acc_ref[i4, h] += pv[h] * c_ref[p, h, 0:1, :]
            z_ref[p, h] = z_new[h]

    base, n_real = spans[0]
    n_iter = n_real + SB_PIPE
    assert base % 2 == 0 and SB_PIPE % 2 == 0 and n_iter % SB_UNROLL == 0

    def near_body(t, carry):
        for r in range(SB_UNROLL):
            t1 = base + SB_PIPE + SB_UNROLL * t + r
            even = r % 2 == 0
            iteration(t1, t1 - 3, t1 - 4, r % 2, even, not even, even)
        return carry
    lax.fori_loop(0, n_iter // SB_UNROLL, near_body, 0)

    base, n_real = spans[1]
    first = base + SB_PIPE
    end = first + n_real

    def next_live(t):
        return lax.while_loop(lambda u: alive_ref[ti_ref[u]] == 0, lambda u: u + 1, t)

    def far_body(carry):
        ptr, h1, h2, h3, h4 = carry
        picks = []
        for r in range(SB_UNROLL):
            t1 = next_live(ptr)
            picks.append(t1)
            ptr = t1 + 1
        for r, t1 in enumerate(picks):
            iteration(t1, h3, h4, r % 2, False, False, False)
            h1, h2, h3, h4 = t1, h1, h2, h3
        return ptr, h1, h2, h3, h4

    start = jnp.where(n_alive_ref[0] == 0, end + SB_PIPE, first).astype(jnp.int32)
    lax.while_loop(lambda carry: carry[0] < end + SB_PIPE, far_body,
                   (start, jnp.int32(first - 1), jnp.int32(first - 2), jnp.int32(first - 3), jnp.int32(first - 4)))

    for i in range(nq):
        oT = jnp.concatenate([acc_ref[i, 0], acc_ref[i, 1]], axis=0)
        o_ref[0, i * tq:(i + 1) * tq, :] = oT.T.astype(o_ref.dtype)


def _sb_attention(qkv, u):
    B, S, _ = qkv.shape
    n_pairs = SB_HEADS // 2
    nq = S // SB_TQ
    blk = (1, S, 2 * SB_HEAD_DIM)
    ti, tj, spans = _sb_tile_tables(nq)
    tile = (SB_TK, SB_TQ)
    return pl.pallas_call(
        functools.partial(_sb_kernel, spans=spans),
        grid_spec=pltpu.PrefetchScalarGridSpec(
            num_scalar_prefetch=2,
            grid=(B, n_pairs),
            in_specs=[
                pl.BlockSpec(blk, lambda b, p, *_: (b, 0, p)),
                pl.BlockSpec(blk, lambda b, p, *_: (b, 0, n_pairs + p)),
                pl.BlockSpec(blk, lambda b, p, *_: (b, 0, 2 * n_pairs + p)),
                pl.BlockSpec((SB_TK, SB_TK), lambda b, p, *_: (0, 0)),
            ],
            out_specs=pl.BlockSpec(blk, lambda b, p, *_: (b, 0, p)),
            scratch_shapes=[
                pltpu.VMEM((S // SB_TK, 2 * SB_HEAD_DIM, SB_TK), BF16),
                pltpu.VMEM((nq + 1, 2, 2 * SB_HEAD_DIM, SB_TQ), BF16),
                pltpu.VMEM((2, 2) + tile, F32),
                pltpu.VMEM((2, 2) + tile, BF16),
                pltpu.VMEM((2, 2) + tile, F32),
                pltpu.VMEM((2, 2) + tile, BF16),
                pltpu.VMEM((2, 2, SUBLANES, SB_TQ), F32),
                pltpu.VMEM((nq + 1, 2, SUBLANES, SB_TQ), F32),
                pltpu.VMEM((nq + 1, 2, SB_HEAD_DIM, SB_TQ), F32),
                pltpu.SMEM((nq + 1,), jnp.int32),
                pltpu.SMEM((1,), jnp.int32),
            ],
        ),
        out_shape=jax.ShapeDtypeStruct((B, S, D_MODEL), BF16),
        compiler_params=_cparams(("arbitrary", "arbitrary")),
        name="sb_attention",
    )(jnp.asarray(ti, jnp.int32), jnp.asarray(tj, jnp.int32), qkv, qkv, qkv, u)


def _layer_norm_rows(u, g, b):
    mu = jnp.mean(u, axis=-1, keepdims=True)
    d = u - mu
    var = jnp.mean(d * d, axis=-1, keepdims=True)
    return d * lax.rsqrt(var + LN_EPS) * g + b


def _outln_kernel(a_ref, w_ref, x_ref, gate_ref, g_ref, b_ref, sc2_ref, sh2_ref, rw_ref, rb_ref, lt_ref,
                  o_ref, bucket_ref, rank_ref, counts_ref, w_bf, carry_ref):
    @pl.when(jnp.logical_and(pl.program_id(0) == 0, pl.program_id(1) == 0))
    def _():
        w_bf[...] = w_ref[0].astype(BF16)
        carry_ref[...] = jnp.zeros_like(carry_ref)

    tm = a_ref.shape[1]
    sub = tm // OUT_SPLIT
    nt = (((1,), (1,)), ((), ()))
    rw = rw_ref[...].astype(BF16)
    ys = [jnp.dot(a_ref[0, r * sub:(r + 1) * sub, :], w_bf[...], preferred_element_type=F32) for r in range(OUT_SPLIT)]
    logits = []
    for r in range(OUT_SPLIT):
        rows = slice(r * sub, (r + 1) * sub)
        u = DEEPNORM_ALPHA * x_ref[0, rows, :] + (1.0 + gate_ref[0]) * ys[r]
        xn = _layer_norm_rows(u, g_ref[...], b_ref[...])
        o_ref[0, rows, :] = xn
        h = xn * (1.0 + sc2_ref[0]) + sh2_ref[0]
        logits.append(lax.dot_general(rw, h.astype(BF16), nt, preferred_element_type=F32))
    logits = jnp.concatenate(logits, axis=1) + rb_ref[:, 0:1]
    _route(logits, lt_ref, bucket_ref, rank_ref, counts_ref, carry_ref)


def _out_proj_ln(a, w_all, layer, x, mods, k, g, b, k_moe, router_wT, router_bt, lt):
    B, S, D = x.shape
    K = a.shape[-1]
    tm = PROJ_TM
    per_b = S // tm
    T = B * S
    mod = lambda kk, which: pl.BlockSpec((1, 1, D), lambda bb, i: (_mod_row(kk, bb, which, B), 0, 0))
    const = lambda shape: pl.BlockSpec(shape, lambda bb, i: (0,) * len(shape))
    return pl.pallas_call(
        _outln_kernel,
        grid=(B, per_b),
        in_specs=[
            pl.BlockSpec((1, tm, K), lambda bb, i: (bb, i, 0)),
            pl.BlockSpec((1, K, D), lambda bb, i: (layer, 0, 0)),
            pl.BlockSpec((1, tm, D), lambda bb, i: (bb, i, 0)),
            mod(k, GATE), const((1, D)), const((1, D)),
            mod(k_moe, SCALE), mod(k_moe, SHIFT),
            const((N_EXPERTS, D)), const((N_EXPERTS, LANES)), const((tm, tm)),
        ],
        out_specs=[
            pl.BlockSpec((1, tm, D), lambda bb, i: (bb, i, 0)),
            pl.BlockSpec((1, tm), lambda bb, i: (0, bb * per_b + i)),
            pl.BlockSpec((1, tm), lambda bb, i: (0, bb * per_b + i)),
            const((BUCKET_ROWS, LANES)),
        ],
        out_shape=[
            jax.ShapeDtypeStruct((B, S, D), F32),
            jax.ShapeDtypeStruct((1, T), jnp.int32),
            jax.ShapeDtypeStruct((1, T), jnp.int32),
            jax.ShapeDtypeStruct((BUCKET_ROWS, LANES), jnp.int32),
        ],
        scratch_shapes=[pltpu.VMEM((K, D), BF16), pltpu.VMEM((BUCKET_ROWS, LANES), F32)],
        compiler_params=_cparams(("arbitrary", "arbitrary")),
        name="out_proj_ln",
    )(a, w_all, x, mods, g.reshape(1, D), b.reshape(1, D), mods, mods, router_wT, router_bt, lt)


def _rope_kernel(pos_ref, invf_ref, sign_ref, cc_ref, ss_ref):
    ang = pos_ref[0].astype(F32) * invf_ref[...]
    cc_ref[0] = jnp.cos(ang)
    ss_ref[0] = jnp.sin(ang) * sign_ref[...]


def _rope_tables(positions):
    B, S = positions.shape
    half = RET_QK_DIM // 2
    inv_freq = 1.0 / (ROPE_BASE ** (jnp.arange(0, RET_QK_DIM, 2, dtype=F32) / RET_QK_DIM))
    invf2 = jnp.concatenate([inv_freq, inv_freq]).reshape(1, RET_QK_DIM)
    sign = jnp.concatenate([-jnp.ones((half,), F32), jnp.ones((half,), F32)]).reshape(1, RET_QK_DIM)
    ts = PROJ_TM
    shp = jax.ShapeDtypeStruct((B, S, RET_QK_DIM), F32)
    return pl.pallas_call(
        _rope_kernel,
        grid=(B, S // ts),
        in_specs=[
            pl.BlockSpec((1, ts, 1), lambda b, i: (b, i, 0)),
            pl.BlockSpec((1, RET_QK_DIM), lambda b, i: (0, 0)),
            pl.BlockSpec((1, RET_QK_DIM), lambda b, i: (0, 0)),
        ],
        out_specs=[pl.BlockSpec((1, ts, RET_QK_DIM), lambda b, i: (b, i, 0))] * 2,
        out_shape=[shp, shp],
        compiler_params=_cparams(("arbitrary", "arbitrary")),
        name="rope_tables",
    )(positions.reshape(B, S, 1), invf2, sign)


def _retention_decay_tables():
    C = RET_CHUNK
    kscale = RET_QK_DIM ** -0.5
    log_gamma = jnp.log(1.0 - jnp.exp2(-5.0 - jnp.arange(RET_HEADS, dtype=F32)))
    n = jnp.arange(C, dtype=F32)
    diff = n[:, None] - n[None, :]
    dmask = jnp.where(diff >= 0, jnp.exp(jnp.maximum(diff, 0.0) * log_gamma[:, None, None]), 0.0)
    xi = jnp.exp((n + 1.0) * log_gamma[:, None])
    zeta = jnp.exp((C - 1.0 - n) * log_gamma[:, None])
    chunk_decay = jnp.exp(C * log_gamma)
    dm = dmask * kscale
    xi_t = jnp.broadcast_to(xi[:, :, None], (RET_HEADS, C, RET_V_DIM))
    zeta_t = jnp.broadcast_to(zeta[:, :, None] * kscale, (RET_HEADS, C, RET_QK_DIM))
    decay_t = jnp.broadcast_to(chunk_decay[:, None, None], (RET_HEADS, RET_QK_DIM, RET_V_DIM))
    return dm, xi_t, zeta_t, decay_t


def _ret_kernel(q_ref, k_ref, v_ref, g_ref, cc_ref, ss_ref, dm_ref, xi_ref, zeta_ref, decay_ref,
                gn_ref, o_ref, state_ref):
    S = q_ref.shape[1]
    C = RET_CHUNK
    half = RET_QK_DIM // 2
    state_ref[...] = jnp.zeros_like(state_ref)

    def prep(c):
        r0 = pl.multiple_of(c * C, C)
        rows = pl.ds(r0, C)
        cc = cc_ref[0, rows, :]
        ss = ss_ref[0, rows, :]
        q = q_ref[0, rows, :].astype(F32)
        k = k_ref[0, rows, :].astype(F32)
        qr = q * cc + pltpu.roll(q, half, 1) * ss
        kr = k * cc + pltpu.roll(k, half, 1) * ss
        kzT = (kr * zeta_ref[0]).T.astype(BF16)
        return rows, qr.astype(BF16), kr.astype(BF16), v_ref[0, rows, :], kzT

    def finish(rows, inner, cross):
        o = inner + cross * xi_ref[0]
        mu = jnp.mean(o, axis=-1, keepdims=True)
        d = o - mu
        var = jnp.mean(d * d, axis=-1, keepdims=True)
        on = d * lax.rsqrt(var + LN_EPS) * gn_ref[...]
        g = g_ref[0, rows, :].astype(F32)
        o_ref[0, rows, :] = (g * jax.nn.sigmoid(g) * on).astype(o_ref.dtype)

    def chunk_group(t, carry):
        nt = (((1,), (1,)), ((), ()))
        ps = [prep(RET_GROUP * t + r) for r in range(RET_GROUP)]
        qks = [lax.dot_general(p[1], p[2], nt, preferred_element_type=F32) for p in ps]
        dss = [jnp.dot(p[4], p[3], preferred_element_type=F32) for p in ps]
        state = state_ref[...]
        crosses = []
        for p, ds in zip(ps, dss):
            crosses.append(jnp.dot(p[1], state.astype(BF16), preferred_element_type=F32))
            state = state * decay_ref[0] + ds
        state_ref[...] = state
        inners = [jnp.dot((qk * dm_ref[0]).astype(BF16), p[3], preferred_element_type=F32) for qk, p in zip(qks, ps)]
        for p, inner, cross in zip(ps, inners, crosses):
            finish(p[0], inner, cross)
        return carry

    lax.fori_loop(0, S // (RET_GROUP * C), chunk_group, 0)


def _retention(proj, cc, ss, tables, gn_g):
    B, S, _ = proj.shape
    dk, dv, H, C = RET_QK_DIM, RET_V_DIM, RET_HEADS, RET_CHUNK
    dm, xi_t, zeta_t, decay_t = tables
    return pl.pallas_call(
        _ret_kernel,
        grid=(B, H),
        in_specs=[
            pl.BlockSpec((1, S, dk), lambda b, h: (b, 0, h)),
            pl.BlockSpec((1, S, dk), lambda b, h: (b, 0, H + h)),
            pl.BlockSpec((1, S, dv), lambda b, h: (b, 0, H + h)),
            pl.BlockSpec((1, S, dv), lambda b, h: (b, 0, 2 * H + h)),
            pl.BlockSpec((1, S, dk), lambda b, h: (b, 0, 0)),
            pl.BlockSpec((1, S, dk), lambda b, h: (b, 0, 0)),
            pl.BlockSpec((1, C, C), lambda b, h: (h, 0, 0)),
            pl.BlockSpec((1, C, dv), lambda b, h: (h, 0, 0)),
            pl.BlockSpec((1, C, dk), lambda b, h: (h, 0, 0)),
            pl.BlockSpec((1, dk, dv), lambda b, h: (h, 0, 0)),
            pl.BlockSpec((1, dv), lambda b, h: (0, h)),
        ],
        out_specs=pl.BlockSpec((1, S, dv), lambda b, h: (b, 0, h)),
        out_shape=jax.ShapeDtypeStruct((B, S, H * dv), BF16),
        scratch_shapes=[pltpu.VMEM((dk, dv), F32)],
        compiler_params=_cparams(("arbitrary", "arbitrary")),
        name="retention",
    )(proj, proj, proj, proj, cc, ss, dm, xi_t, zeta_t, decay_t, gn_g.reshape(1, H * dv))


def _store_slabs(ref, val):
    tm = val.shape[0]
    for j in range(ROW_SLAB):
        ref[pl.ds(j, tm, stride=ROW_SLAB), :] = val[:, j * LANES:(j + 1) * LANES]


def _load_slab_chunks(ref, tm):
    return [ref[pl.ds(j, tm, stride=ROW_SLAB), :] for j in range(ROW_SLAB)]


def _top2_flags(vals):
    n = len(vals)
    rank = []
    for k in range(n):
        r = jnp.zeros_like(vals[0])
        for j in range(n):
            if j < k:
                r = r + (vals[j] >= vals[k]).astype(F32)
            elif j > k:
                r = r + (vals[j] > vals[k]).astype(F32)
        rank.append(r)
    is1 = [r == 0.0 for r in rank]
    is2 = [r == 1.0 for r in rank]
    m1 = sum(jnp.where(f, v, 0.0) for f, v in zip(is1, vals))
    m2 = sum(jnp.where(f, v, 0.0) for f, v in zip(is2, vals))
    return is1, is2, m1, m2


def _route(logits, lt_ref, bucket_ref, rank_ref, counts_ref, carry_ref):
    tm = logits.shape[1]
    rows = [logits[e:e + 1, :] for e in range(N_EXPERTS)]
    gmax = functools.reduce(jnp.maximum, rows)
    groups = []
    for g in range(N_GROUPS):
        vals = rows[g * EXPERTS_PER_GROUP:(g + 1) * EXPERTS_PER_GROUP]
        is1, is2, m1, m2 = _top2_flags(vals)
        score = jnp.exp(m1 - gmax) + jnp.exp(m2 - gmax)
        groups.append(([jnp.logical_or(a, b) for a, b in zip(is1, is2)], score))
    best = groups[0][1]
    gsel = jnp.zeros_like(best)
    for g in range(1, N_GROUPS):
        better = groups[g][1] > best
        best = jnp.where(better, groups[g][1], best)
        gsel = jnp.where(better, float(g), gsel)
    sel = [jnp.zeros_like(best) > 1.0] * EXPERTS_PER_GROUP
    for g in range(N_GROUPS):
        here = gsel == float(g)
        sel = [jnp.logical_or(s, jnp.logical_and(here, f)) for s, f in zip(sel, groups[g][0])]
    pair = jnp.where(sel[0],
                     jnp.where(sel[1], 0.0, jnp.where(sel[2], 1.0, 2.0)),
                     jnp.where(sel[1], jnp.where(sel[2], 3.0, 4.0), 5.0))
    bucket = (gsel * float(PAIRS_PER_GROUP) + pair).astype(jnp.int32)

    brow = lax.broadcasted_iota(jnp.int32, (BUCKET_ROWS, tm), 0)
    onehot = (brow == bucket).astype(F32)
    before = jnp.dot(onehot.astype(BF16), lt_ref[...], preferred_element_type=F32)
    carry = carry_ref[...]
    in_tile = jnp.sum(onehot * before, axis=0, keepdims=True)
    base = jnp.sum(onehot * carry[:, 0:1], axis=0, keepdims=True)
    rank_ref[...] = (base + in_tile).astype(jnp.int32)
    bucket_ref[...] = bucket
    new_carry = carry + jnp.sum(onehot, axis=1, keepdims=True)
    carry_ref[...] = new_carry
    counts_ref[...] = new_carry.astype(jnp.int32)


def _slab(ref, r):
    return ref.at[pl.ds(pl.multiple_of(r * ROW_SLAB, ROW_SLAB), ROW_SLAB)]


def _wait_rows(src_ref, dst_ref, sem, n):
    def body(c, carry):
        for _ in range(DMA_CHUNK):
            pltpu.make_async_copy(_slab(src_ref, 0), _slab(dst_ref, 0), sem).wait()
        return carry
    lax.fori_loop(0, n // DMA_CHUNK, body, 0)


def _scatter_kernel(dest_ref, pad_lo_ref, pad_hi_ref, x_ref, sc_ref, sh_ref, xs_ref, hbuf, sems):
    i = pl.program_id(0)
    last = pl.num_programs(0) - 1
    tm = x_ref.shape[0]
    h = x_ref[...] * (1.0 + sc_ref[0]) + sh_ref[0]

    for s in range(2):
        @pl.when(lax.rem(i, 2) == s)
        def _(s=s):
            buf = hbuf.at[s]
            _store_slabs(buf, h)

            def issue(c, carry):
                for u in range(DMA_CHUNK):
                    r = c * DMA_CHUNK + u
                    pltpu.make_async_copy(_slab(buf, r), _slab(xs_ref, dest_ref[i * tm + r]), sems.at[s]).start(priority=u % 2)
                return carry
            lax.fori_loop(0, tm // DMA_CHUNK, issue, 0)

            @pl.when(i > 0)
            def _():
                _wait_rows(hbuf.at[1 - s], xs_ref, sems.at[1 - s], tm)

            @pl.when(i == last)
            def _():
                _wait_rows(buf, xs_ref, sems.at[s], tm)

                def filler(row0, n_rows, start):
                    cp = pltpu.make_async_copy(
                        buf.at[pl.ds(0, n_rows * ROW_SLAB)],
                        xs_ref.at[pl.ds(pl.multiple_of(row0 * ROW_SLAB, ROW_SLAB), n_rows * ROW_SLAB)],
                        sems.at[s])
                    if start:
                        cp.start()
                    else:
                        cp.wait()

                def fill(start):
                    def bucket(b, carry):
                        pos = pad_lo_ref[b]
                        n = pad_hi_ref[b] - pos
                        for bit in reversed(range(MOE_TM.bit_length() - 1)):
                            take = lax.rem(lax.shift_right_logical(n, bit), 2)

                            @pl.when(take == 1)
                            def _(pos=pos, bit=bit):
                                filler(pos, 1 << bit, start)
                            pos = pos + take * (1 << bit)
                        return carry
                    lax.fori_loop(0, N_BUCKETS, bucket, 0)

                    def tail(t, carry):
                        filler(t * MOE_TM, MOE_TM, start)
                        return carry
                    n_tiles = xs_ref.shape[0] // (MOE_TM * ROW_SLAB)
                    lax.fori_loop(pad_hi_ref[N_BUCKETS - 1] // MOE_TM, n_tiles, tail, 0)
                fill(True)
                fill(False)


def _scatter_rows(x2, mods, k, B, S, plan, n_out):
    T, D = x2.shape
    tm = PROJ_TM
    per_b = S // tm
    return pl.pallas_call(
        _scatter_kernel,
        grid_spec=pltpu.PrefetchScalarGridSpec(
            num_scalar_prefetch=3,
            grid=(T // tm,),
            in_specs=[
                pl.BlockSpec((tm, D), lambda i, *_: (i, 0)),
                pl.BlockSpec((1, 1, D), lambda i, *_: (_mod_row(k, i // per_b, SCALE, B), 0, 0)),
                pl.BlockSpec((1, 1, D), lambda i, *_: (_mod_row(k, i // per_b, SHIFT, B), 0, 0)),
            ],
            out_specs=pl.BlockSpec(memory_space=pl.ANY),
            scratch_shapes=[pltpu.VMEM((2, tm * ROW_SLAB, LANES), F32), pltpu.SemaphoreType.DMA((2,))],
        ),
        out_shape=jax.ShapeDtypeStruct((n_out * ROW_SLAB, LANES), F32),
        compiler_params=_cparams(("arbitrary",)),
        name="scatter_rows",
    )(plan["dest"], plan["pad_lo"], plan["pad_hi"], x2, mods, mods)


def _expert_pair(x, wg_bf, wu_bf, wd_bf):
    gates = [jnp.dot(x, wg_bf[e], preferred_element_type=F32) for e in range(2)]
    ups = [jnp.dot(x, wu_bf[e], preferred_element_type=F32) for e in range(2)]
    outs = []
    for e in range(2):
        hid = gates[e] * jax.nn.sigmoid(gates[e]) * ups[e]
        outs.append(jnp.dot(hid.astype(BF16), wd_bf[e], preferred_element_type=F32))
    return outs


def _router_logit(chunks, rw_ref, rb_ref, e):
    w_row = rw_ref[pl.ds(e, 1), :]
    acc = chunks[0] * w_row[:, 0:LANES]
    for j in range(1, ROW_SLAB):
        acc = acc + chunks[j] * w_row[:, j * LANES:(j + 1) * LANES]
    return jnp.sum(acc, axis=-1, keepdims=True) + rb_ref[pl.ds(e, 1), 0:1]


def _moe_kernel(src_ref, ea_ref, eb_ref, valid_ref, na_ref, nb_ref,
                xs_ref, rw_ref, rb_ref, wg_hbm, wu_hbm, wd_hbm, o_ref,
                wg_st, wu_st, wd_st, wg_bf, wu_bf, wd_bf, sems, *, layer):
    i = pl.program_id(0)
    prev = jnp.maximum(i - 1, 0)

    def fetch(slot, e):
        return [pltpu.make_async_copy(wg_hbm.at[layer, e], wg_st.at[slot], sems.at[slot, 0]),
                pltpu.make_async_copy(wu_hbm.at[layer, e], wu_st.at[slot], sems.at[slot, 1]),
                pltpu.make_async_copy(wd_hbm.at[layer, e], wd_st.at[slot], sems.at[slot, 2])]

    for slot, e_ref, n_ref in ((0, ea_ref, na_ref), (1, eb_ref, nb_ref)):
        @pl.when(i == 0)
        def _(slot=slot, e_ref=e_ref):
            for cp in fetch(slot, e_ref[0]):
                cp.start()

        @pl.when(jnp.logical_or(i == 0, e_ref[i] != e_ref[prev]))
        def _(slot=slot, e_ref=e_ref, n_ref=n_ref):
            for cp in fetch(slot, e_ref[i]):
                cp.wait()
            wg_bf[slot] = wg_st[slot].astype(BF16)
            wu_bf[slot] = wu_st[slot].astype(BF16)
            wd_bf[slot] = wd_st[slot].astype(BF16)

            @pl.when(n_ref[i] >= 0)
            def _():
                for cp in fetch(slot, n_ref[i]):
                    cp.start()

    @pl.when(valid_ref[i] == 1)
    def _():
        chunks = _load_slab_chunks(xs_ref, MOE_TM)
        l_a = _router_logit(chunks, rw_ref, rb_ref, ea_ref[i])
        l_b = _router_logit(chunks, rw_ref, rb_ref, eb_ref[i])
        w_a = jax.nn.sigmoid(l_a - l_b)
        w_b = jax.nn.sigmoid(l_b - l_a)
        x = jnp.concatenate([c.astype(BF16) for c in chunks], axis=-1)
        y_a, y_b = _expert_pair(x, wg_bf, wu_bf, wd_bf)
        y = w_a * y_a + w_b * y_b
        _store_slabs(o_ref, y)

    @pl.when(valid_ref[i] == 0)
    def _():
        o_ref[...] = jnp.zeros_like(o_ref)


def _grouped_experts(xs, plan, router_wT, router_bt, w_gate, w_up, w_down, layer):
    n_tiles = xs.shape[0] // (MOE_TM * ROW_SLAB)
    D, Fd = D_MODEL, D_EXPERT
    row_spec = lambda index: pl.BlockSpec((MOE_TM * ROW_SLAB, LANES), index)
    hbm = pl.BlockSpec(memory_space=pl.ANY)
    return pl.pallas_call(
        functools.partial(_moe_kernel, layer=layer),
        grid_spec=pltpu.PrefetchScalarGridSpec(
            num_scalar_prefetch=6,
            grid=(n_tiles,),
            in_specs=[
                row_spec(lambda i, src, *_: (src[i], 0)),
                pl.BlockSpec((N_EXPERTS, D), lambda i, *_: (0, 0)),
                pl.BlockSpec((N_EXPERTS, LANES), lambda i, *_: (0, 0)),
                hbm, hbm, hbm,
            ],
            out_specs=row_spec(lambda i, *_: (i, 0)),
            scratch_shapes=[
                pltpu.VMEM((2, D, Fd), F32), pltpu.VMEM((2, D, Fd), F32), pltpu.VMEM((2, Fd, D), F32),
                pltpu.VMEM((2, D, Fd), BF16), pltpu.VMEM((2, D, Fd), BF16), pltpu.VMEM((2, Fd, D), BF16),
                pltpu.SemaphoreType.DMA((2, 3)),
            ],
        ),
        out_shape=jax.ShapeDtypeStruct(xs.shape, F32),
        compiler_params=_cparams(("arbitrary",)),
        name="grouped_experts",
    )(plan["tile_src"], plan["tile_ea"], plan["tile_eb"], plan["tile_valid"], plan["next_ea"], plan["next_eb"],
      xs, router_wT, router_bt, w_gate, w_up, w_down)


def _next_change(e):
    n = e.shape[0]
    idx = jnp.arange(n, dtype=jnp.int32)
    change = jnp.concatenate([jnp.ones((1,), jnp.bool_), e[1:] != e[:-1]])
    at = jnp.where(change, idx, n)
    nxt = lax.cummin(at[::-1])[::-1]
    nxt = jnp.concatenate([nxt[1:], jnp.full((1,), n, jnp.int32)])
    return jnp.where(nxt < n, jnp.take(e, jnp.minimum(nxt, n - 1)), -1).astype(jnp.int32)


def _moe_plan(bucket, rank, counts, n_tiles):
    cnt = counts[:N_BUCKETS, 0]
    tiles_b = (cnt + MOE_TM - 1) // MOE_TM
    tile_end = jnp.cumsum(tiles_b)
    tile_start = tile_end - tiles_b
    row_start = tile_start * MOE_TM
    dest = jnp.take(row_start, bucket[0]) + rank[0]
    n_used = tile_end[-1]
    t = jnp.arange(n_tiles, dtype=jnp.int32)
    t_c = jnp.minimum(t, n_used - 1)
    tb = jnp.sum((t_c[:, None] >= tile_end[None, :]).astype(jnp.int32), axis=1)
    pair_lo = jnp.array([0, 0, 0, 1, 1, 2], jnp.int32)
    pair_hi = jnp.array([1, 2, 3, 2, 3, 3], jnp.int32)
    grp = tb // PAIRS_PER_GROUP
    pr = tb % PAIRS_PER_GROUP
    tile_ea = (grp * EXPERTS_PER_GROUP + jnp.take(pair_lo, pr)).astype(jnp.int32)
    tile_eb = (grp * EXPERTS_PER_GROUP + jnp.take(pair_hi, pr)).astype(jnp.int32)
    return {
        "dest": dest.astype(jnp.int32),
        "pad_lo": (row_start + cnt).astype(jnp.int32),
        "pad_hi": (tile_end * MOE_TM).astype(jnp.int32),
        "tile_src": t_c.astype(jnp.int32),
        "tile_ea": tile_ea,
        "tile_eb": tile_eb,
        "next_ea": _next_change(tile_ea),
        "next_eb": _next_change(tile_eb),
        "tile_valid": (t < n_used).astype(jnp.int32),
    }


def _resln_kernel(dest_ref, ys_ref, x_ref, gate_ref, g_ref, b_ref, o_ref, ybuf, sems):
    i = pl.program_id(0)
    n = pl.num_programs(0)
    tm = x_ref.shape[0]

    def issue(tile, s):
        def body(c, carry):
            for u in range(DMA_CHUNK):
                r = c * DMA_CHUNK + u
                pltpu.make_async_copy(_slab(ys_ref, dest_ref[tile * tm + r]), _slab(ybuf.at[s], r), sems.at[s]).start(priority=u % 2)
            return carry
        lax.fori_loop(0, tm // DMA_CHUNK, body, 0)

    @pl.when(i == 0)
    def _():
        issue(0, 0)

    for s in range(2):
        @pl.when(lax.rem(i, 2) == s)
        def _(s=s):
            @pl.when(i + 1 < n)
            def _():
                issue(i + 1, 1 - s)

            _wait_rows(ys_ref, ybuf.at[s], sems.at[s], tm)
            y = jnp.concatenate(_load_slab_chunks(ybuf.at[s], tm), axis=-1)
            u = DEEPNORM_ALPHA * x_ref[...] + (1.0 + gate_ref[0]) * y
            o_ref[...] = _layer_norm_rows(u, g_ref[...], b_ref[...])


def _gather_residual_ln(ys, plan, x2, mods, k, B, S, g, b):
    T, D = x2.shape
    tm = PROJ_TM
    per_b = S // tm
    return pl.pallas_call(
        _resln_kernel,
        grid_spec=pltpu.PrefetchScalarGridSpec(
            num_scalar_prefetch=1,
            grid=(T // tm,),
            in_specs=[
                pl.BlockSpec(memory_space=pl.ANY),
                pl.BlockSpec((tm, D), lambda i, *_: (i, 0)),
                pl.BlockSpec((1, 1, D), lambda i, *_: (_mod_row(k, i // per_b, GATE, B), 0, 0)),
                pl.BlockSpec((1, D), lambda i, *_: (0, 0)),
                pl.BlockSpec((1, D), lambda i, *_: (0, 0)),
            ],
            out_specs=pl.BlockSpec((tm, D), lambda i, *_: (i, 0)),
            scratch_shapes=[pltpu.VMEM((2, tm * ROW_SLAB, LANES), F32), pltpu.SemaphoreType.DMA((2,))],
        ),
        out_shape=jax.ShapeDtypeStruct((T, D), F32),
        compiler_params=_cparams(("arbitrary",)),
        name="gather_residual_ln",
    )(plan["dest"], ys, x2, mods, g.reshape(1, D), b.reshape(1, D))


def _moe_sublayer(x, routing, mods, k, router_wT, router_bt, w_gate, w_up, w_down, layer, g, b):
    B, S, D = x.shape
    T = B * S
    x2 = x.reshape(T, D)
    bucket, rank, counts = routing
    n_tiles = T // MOE_TM + N_BUCKETS
    plan = _moe_plan(bucket, rank, counts, n_tiles)
    xs = _scatter_rows(x2, mods, k, B, S, plan, n_tiles * MOE_TM)
    ys = _grouped_experts(xs, plan, router_wT, router_bt, w_gate, w_up, w_down, layer)
    return _gather_residual_ln(ys, plan, x2, mods, k, B, S, g, b).reshape(B, S, D)


def kernel(x, c, positions, ada_w, ada_b, ln_g, ln_b, sb_w_in, sb_w_out, ret_w_in, ret_gn_g, ret_w_out,
           router_w, router_b, moe_w_gate, moe_w_up, moe_w_down):
    mods = _ada_params(c, ada_w, ada_b)
    cc, ss = _rope_tables(positions)
    tables = _retention_decay_tables()
    u = (jnp.arange(SB_TK)[:, None] < jnp.arange(SB_TK)[None, :]).astype(BF16)
    lt = (jnp.arange(PROJ_TM)[:, None] < jnp.arange(PROJ_TM)[None, :]).astype(BF16)
    router_wT = router_w.T
    router_bt = jnp.broadcast_to(router_b[:, None], (N_EXPERTS, LANES))
    for i in range(DEPTH):
        k_mix, k_moe = 2 * i, 2 * i + 1
        if i % 2 == 0:
            qkv = _mod_matmul(x, mods, k_mix, sb_w_in, i // 2, tn=3 * D_MODEL // 2)
            a = _sb_attention(qkv, u)
            w_out = sb_w_out
        else:
            proj = _mod_matmul(x, mods, k_mix, ret_w_in, i // 2, tn=2 * D_MODEL)
            a = _retention(proj, cc, ss, tables, ret_gn_g[i // 2])
            w_out = ret_w_out
        x, *routing = _out_proj_ln(a, w_out, i // 2, x, mods, k_mix, ln_g[i, 0], ln_b[i, 0],
                                   k_moe, router_wT, router_bt, lt)
        x = _moe_sublayer(x, routing, mods, k_moe, router_wT, router_bt, moe_w_gate, moe_w_up, moe_w_down, i,
                          ln_g[i, 1], ln_b[i, 1])
    return x
```

```python
import functools

import jax
import jax.numpy as jnp
from jax import lax
from jax.experimental import pallas as pl
from jax.experimental.pallas import tpu as pltpu

F32 = jnp.float32
BF16 = jnp.bfloat16

D_MODEL = 1024
DEPTH = 4
SB_HEADS = 16
SB_HEAD_DIM = D_MODEL // SB_HEADS
RET_HEADS = 8
RET_QK_DIM = D_MODEL // RET_HEADS
RET_V_DIM = 2 * RET_QK_DIM
ROPE_BASE = 10000.0
N_EXPERTS = 16
N_GROUPS = 4
EXPERTS_PER_GROUP = N_EXPERTS // N_GROUPS
D_EXPERT = D_MODEL // 2
DEEPNORM_ALPHA = (2 * DEPTH) ** 0.25
LN_EPS = 1e-5
LOG2_E = 1.4426950408889634

LANES = 128
SUBLANES = 8
MXU_DIM = 256
VMEM_LIMIT_BYTES = 56 * 1024 * 1024

PROJ_TM = 512
MM_TM = 1024
OUT_SPLIT = 4
SB_TQ = MXU_DIM
SB_TK = MXU_DIM
RET_CHUNK = 256
RET_GROUP = 8
MOE_TM = 256
PAIRS_PER_GROUP = 6
N_BUCKETS = N_GROUPS * PAIRS_PER_GROUP
BUCKET_ROWS = 32
ROW_SLAB = D_MODEL // LANES
DMA_CHUNK = 16
assert ROW_SLAB == SUBLANES


def _cparams(semantics):
    return pltpu.CompilerParams(dimension_semantics=semantics, vmem_limit_bytes=VMEM_LIMIT_BYTES)


def _split_bf16(a):
    hi = a.astype(BF16)
    lo = (a - hi.astype(F32)).astype(BF16)
    return hi, lo


def _ada_kernel(c_ref, w_ref, b_ref, o_ref):
    c = c_ref[...]
    s = c * jax.nn.sigmoid(c)
    sh, sl = _split_bf16(s)
    wh, wl = _split_bf16(w_ref[0])
    acc = jnp.dot(sh, wh, preferred_element_type=F32)
    acc += jnp.dot(sl, wh, preferred_element_type=F32)
    acc += jnp.dot(sh, wl, preferred_element_type=F32)
    o_ref[0] = acc + b_ref[0]


def _ada_params(c, ada_w, ada_b):
    B, D = c.shape
    n_sub = ada_w.shape[0] * ada_w.shape[1]
    N = ada_w.shape[-1]
    tn = N // 2
    w = ada_w.reshape(n_sub, D, N)
    b = ada_b.reshape(n_sub, 1, N)
    out = pl.pallas_call(
        _ada_kernel,
        grid=(n_sub, N // tn),
        in_specs=[
            pl.BlockSpec((B, D), lambda k, j: (0, 0)),
            pl.BlockSpec((1, D, tn), lambda k, j: (k, 0, j)),
            pl.BlockSpec((1, 1, tn), lambda k, j: (k, 0, j)),
        ],
        out_specs=pl.BlockSpec((1, B, tn), lambda k, j: (k, 0, j)),
        out_shape=jax.ShapeDtypeStruct((n_sub, B, N), F32),
        compiler_params=_cparams(("arbitrary", "arbitrary")),
        name="ada_params",
    )(c, w, b)
    return out.reshape(n_sub * B * 3, 1, D)


def _mod_row(k, b, which, B):
    return (k * B + b) * 3 + which


SHIFT, SCALE, GATE = 0, 1, 2


def _modmm_kernel(x_ref, sc_ref, sh_ref, w_ref, o_ref, w_bf):
    @pl.when(jnp.logical_and(pl.program_id(1) == 0, pl.program_id(2) == 0))
    def _():
        w_bf[...] = w_ref[0].astype(BF16)

    h = x_ref[0] * (1.0 + sc_ref[0]) + sh_ref[0]
    o_ref[0] = jnp.dot(h.astype(BF16), w_bf[...], preferred_element_type=F32).astype(o_ref.dtype)


def _mod_matmul(x, mods, k, w_all, layer, tn):
    B, S, D = x.shape
    N = w_all.shape[2]
    tm = MM_TM
    return pl.pallas_call(
        _modmm_kernel,
        grid=(N // tn, B, S // tm),
        in_specs=[
            pl.BlockSpec((1, tm, D), lambda n, b, i: (b, i, 0)),
            pl.BlockSpec((1, 1, D), lambda n, b, i: (_mod_row(k, b, SCALE, B), 0, 0)),
            pl.BlockSpec((1, 1, D), lambda n, b, i: (_mod_row(k, b, SHIFT, B), 0, 0)),
            pl.BlockSpec((1, D, tn), lambda n, b, i: (layer, 0, n)),
        ],
        out_specs=pl.BlockSpec((1, tm, tn), lambda n, b, i: (b, i, n)),
        out_shape=jax.ShapeDtypeStruct((B, S, N), BF16),
        scratch_shapes=[pltpu.VMEM((D, tn), BF16)],
        compiler_params=_cparams(("arbitrary", "arbitrary", "arbitrary")),
        name="mod_matmul",
    )(x, mods, mods, w_all)


SB_PIPE = 4
SB_UNROLL = 4


def _sb_tile_tables(nq):
    near = []
    for i in range(nq):
        near += [(i, i), (i, i - 1) if i else (nq, 0)]
    far = [(i, i - o) for o in range(2, nq) for i in range(o, nq)]
    ti, tj, spans = [], [], []
    for tiles in (near, far):
        padded = [(nq, 0)] * SB_PIPE + tiles + [(nq, 0)] * (SB_PIPE + SB_UNROLL)
        spans.append((len(ti), len(tiles)))
        ti += [t[0] for t in padded]
        tj += [t[1] for t in padded]
    return ti, tj, spans


def _sb_kernel(ti_ref, tj_ref, q_ref, k_ref, v_ref, u_ref, o_ref,
               vT_ref, qT_ref, z_ref, sp_ref, d_ref, a_ref, c_ref, run_ref, acc_ref, alive_ref, n_alive_ref, *, spans):
    S = q_ref.shape[1]
    tq, tk, dh = SB_TQ, SB_TK, SB_HEAD_DIM
    nq = S // tq

    for c in range(S // tk):
        v_c = v_ref[0, c * tk:(c + 1) * tk, :].astype(F32)
        vT_ref[c] = v_c.T.astype(BF16)

    scale = dh ** -0.5 * LOG2_E
    head_rows = lax.broadcasted_iota(jnp.int32, (2 * dh, tq), 0)
    for i in range(nq):
        qT = (q_ref[0, i * tq:(i + 1) * tq, :].astype(F32) * scale).T
        qT_ref[i, 0] = jnp.where(head_rows < dh, qT, 0.0).astype(BF16)
        qT_ref[i, 1] = jnp.where(head_rows >= dh, qT, 0.0).astype(BF16)
    qT_ref[nq] = jnp.zeros_like(qT_ref[nq])
    z_ref[...] = jnp.zeros_like(z_ref)
    sp_ref[...] = jnp.zeros_like(sp_ref)
    d_ref[...] = jnp.zeros_like(d_ref)
    a_ref[...] = jnp.zeros_like(a_ref)
    c_ref[...] = jnp.zeros_like(c_ref)
    for i in range(nq + 1):
        alive_ref[i] = jnp.int32(1)
    n_alive_ref[0] = jnp.int32(nq)
    run_ref[nq] = jnp.zeros_like(run_ref[nq])
    acc_ref[nq] = jnp.zeros_like(acc_ref[nq])

    def iteration(t1, t3, t4, p, diag2, diag3, diag4):
        q = 1 - p
        i1, j1 = ti_ref[t1], tj_ref[t1]
        i3 = ti_ref[t3]
        i4, j4 = ti_ref[t4], tj_ref[t4]
        if diag2 or diag3:
            rows = lax.broadcasted_iota(jnp.int32, (tk, tq), 0)
            cols = lax.broadcasted_iota(jnp.int32, (tk, tq), 1)
            valid = rows < cols
        u = u_ref[...]
        suf = [jnp.dot(u, sp_ref[q, h], preferred_element_type=F32) for h in range(2)]
        vT = vT_ref[j4]
        pv = [jnp.dot(vT[h * dh:(h + 1) * dh, :], a_ref[p, h], preferred_element_type=F32) for h in range(2)]
        k_blk = k_ref[0, pl.ds(pl.multiple_of(j1 * tk, tk), tk), :]
        z_new = [jnp.dot(k_blk, qT_ref[i1, h], preferred_element_type=F32) for h in range(2)]
        for h in range(2):
            z = z_ref[p, h]
            sp = jnp.maximum(z, 0.0) + jnp.log2(1.0 + jnp.exp2(-jnp.abs(z)))
            d_ref[p, h] = z - sp
            if diag2:
                sp = jnp.where(valid, sp, 0.0)
            sp_ref[p, h] = sp.astype(BF16)
        for h in range(2):
            a = jnp.exp2(d_ref[q, h] - suf[h])
            if diag3:
                a = jnp.where(valid, a, 0.0)
            a_ref[q, h] = a.astype(BF16)
            tile_mass = suf[h][0:1, :] + sp_ref[q, h, 0:1, :].astype(F32)
            if diag3:
                run_ref[i3, h, 0:1, :] = tile_mass
            else:
                run = run_ref[i3, h, 0:1, :]
                c_ref[q, h, 0:1, :] = jnp.exp2(-run)
                run_ref[i3, h, 0:1, :] = run + tile_mass
                c_next = jnp.exp2(-(run + tile_mass))
                c_max = c_next if h == 0 else jnp.maximum(c_max, c_next)
        if not diag3:
            live = jnp.logical_or(jnp.max(c_max) > 0.0, i3 == nq).astype(jnp.int32)
            n_alive_ref[0] += live - alive_ref[i3]
            alive_ref[i3] = live
        for h in range(2):
            if diag4:
                acc_ref[i4, h] = pv[h]
            else:
                acc_ref[i4, h] += pv[h] * c_ref[p, h, 0:1, :]
            z_ref[p, h] = z_new[h]

    base, n_real = spans[0]
    n_iter = n_real + SB_PIPE
    assert base % 2 == 0 and SB_PIPE % 2 == 0 and n_iter % SB_UNROLL == 0

    def near_body(t, carry):
        for r in range(SB_UNROLL):
            t1 = base + SB_PIPE + SB_UNROLL * t + r
            even = r % 2 == 0
            iteration(t1, t1 - 3, t1 - 4, r % 2, even, not even, even)
        return carry
    lax.fori_loop(0, n_iter // SB_UNROLL, near_body, 0)

    base, n_real = spans[1]
    first = base + SB_PIPE
    end = first + n_real

    def next_live(t):
        return lax.while_loop(lambda u: alive_ref[ti_ref[u]] == 0, lambda u: u + 1, t)

    def far_body(carry):
        ptr, h1, h2, h3, h4 = carry
        picks = []
        for r in range(SB_UNROLL):
            t1 = next_live(ptr)
            picks.append(t1)
            ptr = t1 + 1
        for r, t1 in enumerate(picks):
            iteration(t1, h3, h4, r % 2, False, False, False)
            h1, h2, h3, h4 = t1, h1, h2, h3
        return ptr, h1, h2, h3, h4

    start = jnp.where(n_alive_ref[0] == 0, end + SB_PIPE, first).astype(jnp.int32)
    lax.while_loop(lambda carry: carry[0] < end + SB_PIPE, far_body,
                   (start, jnp.int32(first - 1), jnp.int32(first - 2), jnp.int32(first - 3), jnp.int32(first - 4)))

    for i in range(nq):
        oT = jnp.concatenate([acc_ref[i, 0], acc_ref[i, 1]], axis=0)
        o_ref[0, i * tq:(i + 1) * tq, :] = oT.T.astype(o_ref.dtype)


def _sb_attention(qkv, u):
    B, S, _ = qkv.shape
    n_pairs = SB_HEADS // 2
    nq = S // SB_TQ
    blk = (1, S, 2 * SB_HEAD_DIM)
    ti, tj, spans = _sb_tile_tables(nq)
    tile = (SB_TK, SB_TQ)
    return pl.pallas_call(
        functools.partial(_sb_kernel, spans=spans),
        grid_spec=pltpu.PrefetchScalarGridSpec(
            num_scalar_prefetch=2,
            grid=(B, n_pairs),
            in_specs=[
                pl.BlockSpec(blk, lambda b, p, *_: (b, 0, p)),
                pl.BlockSpec(blk, lambda b, p, *_: (b, 0, n_pairs + p)),
                pl.BlockSpec(blk, lambda b, p, *_: (b, 0, 2 * n_pairs + p)),
                pl.BlockSpec((SB_TK, SB_TK), lambda b, p, *_: (0, 0)),
            ],
            out_specs=pl.BlockSpec(blk, lambda b, p, *_: (b, 0, p)),
            scratch_shapes=[
                pltpu.VMEM((S // SB_TK, 2 * SB_HEAD_DIM, SB_TK), BF16),
                pltpu.VMEM((nq + 1, 2, 2 * SB_HEAD_DIM, SB_TQ), BF16),
                pltpu.VMEM((2, 2) + tile, F32),
                pltpu.VMEM((2, 2) + tile, BF16),
                pltpu.VMEM((2, 2) + tile, F32),
                pltpu.VMEM((2, 2) + tile, BF16),
                pltpu.VMEM((2, 2, SUBLANES, SB_TQ), F32),
                pltpu.VMEM((nq + 1, 2, SUBLANES, SB_TQ), F32),
                pltpu.VMEM((nq + 1, 2, SB_HEAD_DIM, SB_TQ), F32),
                pltpu.SMEM((nq + 1,), jnp.int32),
                pltpu.SMEM((1,), jnp.int32),
            ],
        ),
        out_shape=jax.ShapeDtypeStruct((B, S, D_MODEL), BF16),
        compiler_params=_cparams(("arbitrary", "arbitrary")),
        name="sb_attention",
    )(jnp.asarray(ti, jnp.int32), jnp.asarray(tj, jnp.int32), qkv, qkv, qkv, u)


def _layer_norm_rows(u, g, b):
    mu = jnp.mean(u, axis=-1, keepdims=True)
    d = u - mu
    var = jnp.mean(d * d, axis=-1, keepdims=True)
    return d * lax.rsqrt(var + LN_EPS) * g + b


def _outln_kernel(a_ref, w_ref, x_ref, gate_ref, g_ref, b_ref, sc2_ref, sh2_ref, rw_ref, rb_ref, lt_ref,
                  o_ref, bucket_ref, rank_ref, counts_ref, w_bf, carry_ref):
    @pl.when(jnp.logical_and(pl.program_id(0) == 0, pl.program_id(1) == 0))
    def _():
        w_bf[...] = w_ref[0].astype(BF16)
        carry_ref[...] = jnp.zeros_like(carry_ref)

    tm = a_ref.shape[1]
    sub = tm // OUT_SPLIT
    nt = (((1,), (1,)), ((), ()))
    rw = rw_ref[...].astype(BF16)
    ys = [jnp.dot(a_ref[0, r * sub:(r + 1) * sub, :], w_bf[...], preferred_element_type=F32) for r in range(OUT_SPLIT)]
    logits = []
    for r in range(OUT_SPLIT):
        rows = slice(r * sub, (r + 1) * sub)
        u = DEEPNORM_ALPHA * x_ref[0, rows, :] + (1.0 + gate_ref[0]) * ys[r]
        xn = _layer_norm_rows(u, g_ref[...], b_ref[...])
        o_ref[0, rows, :] = xn
        h = xn * (1.0 + sc2_ref[0]) + sh2_ref[0]
        logits.append(lax.dot_general(rw, h.astype(BF16), nt, preferred_element_type=F32))
    logits = jnp.concatenate(logits, axis=1) + rb_ref[:, 0:1]
    _route(logits, lt_ref, bucket_ref, rank_ref, counts_ref, carry_ref)


def _out_proj_ln(a, w_all, layer, x, mods, k, g, b, k_moe, router_wT, router_bt, lt):
    B, S, D = x.shape
    K = a.shape[-1]
    tm = PROJ_TM
    per_b = S // tm
    T = B * S
    mod = lambda kk, which: pl.BlockSpec((1, 1, D), lambda bb, i: (_mod_row(kk, bb, which, B), 0, 0))
    const = lambda shape: pl.BlockSpec(shape, lambda bb, i: (0,) * len(shape))
    return pl.pallas_call(
        _outln_kernel,
        grid=(B, per_b),
        in_specs=[
            pl.BlockSpec((1, tm, K), lambda bb, i: (bb, i, 0)),
            pl.BlockSpec((1, K, D), lambda bb, i: (layer, 0, 0)),
            pl.BlockSpec((1, tm, D), lambda bb, i: (bb, i, 0)),
            mod(k, GATE), const((1, D)), const((1, D)),
            mod(k_moe, SCALE), mod(k_moe, SHIFT),
            const((N_EXPERTS, D)), const((N_EXPERTS, LANES)), const((tm, tm)),
        ],
        out_specs=[
            pl.BlockSpec((1, tm, D), lambda bb, i: (bb, i, 0)),
            pl.BlockSpec((1, tm), lambda bb, i: (0, bb * per_b + i)),
            pl.BlockSpec((1, tm), lambda bb, i: (0, bb * per_b + i)),
            const((BUCKET_ROWS, LANES)),
        ],
        out_shape=[
            jax.ShapeDtypeStruct((B, S, D), F32),
            jax.ShapeDtypeStruct((1, T), jnp.int32),
            jax.ShapeDtypeStruct((1, T), jnp.int32),
            jax.ShapeDtypeStruct((BUCKET_ROWS, LANES), jnp.int32),
        ],
        scratch_shapes=[pltpu.VMEM((K, D), BF16), pltpu.VMEM((BUCKET_ROWS, LANES), F32)],
        compiler_params=_cparams(("arbitrary", "arbitrary")),
        name="out_proj_ln",
    )(a, w_all, x, mods, g.reshape(1, D), b.reshape(1, D), mods, mods, router_wT, router_bt, lt)


def _rope_kernel(pos_ref, invf_ref, sign_ref, cc_ref, ss_ref):
    ang = pos_ref[0].astype(F32) * invf_ref[...]
    cc_ref[0] = jnp.cos(ang)
    ss_ref[0] = jnp.sin(ang) * sign_ref[...]


def _rope_tables(positions):
    B, S = positions.shape
    half = RET_QK_DIM // 2
    inv_freq = 1.0 / (ROPE_BASE ** (jnp.arange(0, RET_QK_DIM, 2, dtype=F32) / RET_QK_DIM))
    invf2 = jnp.concatenate([inv_freq, inv_freq]).reshape(1, RET_QK_DIM)
    sign = jnp.concatenate([-jnp.ones((half,), F32), jnp.ones((half,), F32)]).reshape(1, RET_QK_DIM)
    ts = PROJ_TM
    shp = jax.ShapeDtypeStruct((B, S, RET_QK_DIM), F32)
    return pl.pallas_call(
        _rope_kernel,
        grid=(B, S // ts),
        in_specs=[
            pl.BlockSpec((1, ts, 1), lambda b, i: (b, i, 0)),
            pl.BlockSpec((1, RET_QK_DIM), lambda b, i: (0, 0)),
            pl.BlockSpec((1, RET_QK_DIM), lambda b, i: (0, 0)),
        ],
        out_specs=[pl.BlockSpec((1, ts, RET_QK_DIM), lambda b, i: (b, i, 0))] * 2,
        out_shape=[shp, shp],
        compiler_params=_cparams(("arbitrary", "arbitrary")),
        name="rope_tables",
    )(positions.reshape(B, S, 1), invf2, sign)


def _retention_decay_tables():
    C = RET_CHUNK
    kscale = RET_QK_DIM ** -0.5
    log_gamma = jnp.log(1.0 - jnp.exp2(-5.0 - jnp.arange(RET_HEADS, dtype=F32)))
    n = jnp.arange(C, dtype=F32)
    diff = n[:, None] - n[None, :]
    dmask = jnp.where(diff >= 0, jnp.exp(jnp.maximum(diff, 0.0) * log_gamma[:, None, None]), 0.0)
    xi = jnp.exp((n + 1.0) * log_gamma[:, None])
    zeta = jnp.exp((C - 1.0 - n) * log_gamma[:, None])
    chunk_decay = jnp.exp(C * log_gamma)
    dm = dmask * kscale
    xi_t = jnp.broadcast_to(xi[:, :, None], (RET_HEADS, C, RET_V_DIM))
    zeta_t = jnp.broadcast_to(zeta[:, :, None] * kscale, (RET_HEADS, C, RET_QK_DIM))
    decay_t = jnp.broadcast_to(chunk_decay[:, None, None], (RET_HEADS, RET_QK_DIM, RET_V_DIM))
    return dm, xi_t, zeta_t, decay_t


def _ret_kernel(q_ref, k_ref, v_ref, g_ref, cc_ref, ss_ref, dm_ref, xi_ref, zeta_ref, decay_ref,
                gn_ref, o_ref, state_ref):
    S = q_ref.shape[1]
    C = RET_CHUNK
    half = RET_QK_DIM // 2
    state_ref[...] = jnp.zeros_like(state_ref)

    def prep(c):
        r0 = pl.multiple_of(c * C, C)
        rows = pl.ds(r0, C)
        cc = cc_ref[0, rows, :]
        ss = ss_ref[0, rows, :]
        q = q_ref[0, rows, :].astype(F32)
        k = k_ref[0, rows, :].astype(F32)
        qr = q * cc + pltpu.roll(q, half, 1) * ss
        kr = k * cc + pltpu.roll(k, half, 1) * ss
        kzT = (kr * zeta_ref[0]).T.astype(BF16)
        return rows, qr.astype(BF16), kr.astype(BF16), v_ref[0, rows, :], kzT

    def finish(rows, inner, cross):
        o = inner + cross * xi_ref[0]
        mu = jnp.mean(o, axis=-1, keepdims=True)
        d = o - mu
        var = jnp.mean(d * d, axis=-1, keepdims=True)
        on = d * lax.rsqrt(var + LN_EPS) * gn_ref[...]
        g = g_ref[0, rows, :].astype(F32)
        o_ref[0, rows, :] = (g * jax.nn.sigmoid(g) * on).astype(o_ref.dtype)

    def chunk_group(t, carry):
        nt = (((1,), (1,)), ((), ()))
        ps = [prep(RET_GROUP * t + r) for r in range(RET_GROUP)]
        qks = [lax.dot_general(p[1], p[2], nt, preferred_element_type=F32) for p in ps]
        dss = [jnp.dot(p[4], p[3], preferred_element_type=F32) for p in ps]
        state = state_ref[...]
        crosses = []
        for p, ds in zip(ps, dss):
            crosses.append(jnp.dot(p[1], state.astype(BF16), preferred_element_type=F32))
            state = state * decay_ref[0] + ds
        state_ref[...] = state
        inners = [jnp.dot((qk * dm_ref[0]).astype(BF16), p[3], preferred_element_type=F32) for qk, p in zip(qks, ps)]
        for p, inner, cross in zip(ps, inners, crosses):
            finish(p[0], inner, cross)
        return carry

    lax.fori_loop(0, S // (RET_GROUP * C), chunk_group, 0)


def _retention(proj, cc, ss, tables, gn_g):
    B, S, _ = proj.shape
    dk, dv, H, C = RET_QK_DIM, RET_V_DIM, RET_HEADS, RET_CHUNK
    dm, xi_t, zeta_t, decay_t = tables
    return pl.pallas_call(
        _ret_kernel,
        grid=(B, H),
        in_specs=[
            pl.BlockSpec((1, S, dk), lambda b, h: (b, 0, h)),
            pl.BlockSpec((1, S, dk), lambda b, h: (b, 0, H + h)),
            pl.BlockSpec((1, S, dv), lambda b, h: (b, 0, H + h)),
            pl.BlockSpec((1, S, dv), lambda b, h: (b, 0, 2 * H + h)),
            pl.BlockSpec((1, S, dk), lambda b, h: (b, 0, 0)),
            pl.BlockSpec((1, S, dk), lambda b, h: (b, 0, 0)),
            pl.BlockSpec((1, C, C), lambda b, h: (h, 0, 0)),
            pl.BlockSpec((1, C, dv), lambda b, h: (h, 0, 0)),
            pl.BlockSpec((1, C, dk), lambda b, h: (h, 0, 0)),
            pl.BlockSpec((1, dk, dv), lambda b, h: (h, 0, 0)),
            pl.BlockSpec((1, dv), lambda b, h: (0, h)),
        ],
        out_specs=pl.BlockSpec((1, S, dv), lambda b, h: (b, 0, h)),
        out_shape=jax.ShapeDtypeStruct((B, S, H * dv), BF16),
        scratch_shapes=[pltpu.VMEM((dk, dv), F32)],
        compiler_params=_cparams(("arbitrary", "arbitrary")),
        name="retention",
    )(proj, proj, proj, proj, cc, ss, dm, xi_t, zeta_t, decay_t, gn_g.reshape(1, H * dv))


def _store_slabs(ref, val):
    tm = val.shape[0]
    for j in range(ROW_SLAB):
        ref[pl.ds(j, tm, stride=ROW_SLAB), :] = val[:, j * LANES:(j + 1) * LANES]


def _load_slab_chunks(ref, tm):
    return [ref[pl.ds(j, tm, stride=ROW_SLAB), :] for j in range(ROW_SLAB)]


def _top2_flags(vals):
    n = len(vals)
    rank = []
    for k in range(n):
        r = jnp.zeros_like(vals[0])
        for j in range(n):
            if j < k:
                r = r + (vals[j] >= vals[k]).astype(F32)
            elif j > k:
                r = r + (vals[j] > vals[k]).astype(F32)
        rank.append(r)
    is1 = [r == 0.0 for r in rank]
    is2 = [r == 1.0 for r in rank]
    m1 = sum(jnp.where(f, v, 0.0) for f, v in zip(is1, vals))
    m2 = sum(jnp.where(f, v, 0.0) for f, v in zip(is2, vals))
    return is1, is2, m1, m2


def _route(logits, lt_ref, bucket_ref, rank_ref, counts_ref, carry_ref):
    tm = logits.shape[1]
    rows = [logits[e:e + 1, :] for e in range(N_EXPERTS)]
    gmax = functools.reduce(jnp.maximum, rows)
    groups = []
    for g in range(N_GROUPS):
        vals = rows[g * EXPERTS_PER_GROUP:(g + 1) * EXPERTS_PER_GROUP]
        is1, is2, m1, m2 = _top2_flags(vals)
        score = jnp.exp(m1 - gmax) + jnp.exp(m2 - gmax)
        groups.append(([jnp.logical_or(a, b) for a, b in zip(is1, is2)], score))
    best = groups[0][1]
    gsel = jnp.zeros_like(best)
    for g in range(1, N_GROUPS):
        better = groups[g][1] > best
        best = jnp.where(better, groups[g][1], best)
        gsel = jnp.where(better, float(g), gsel)
    sel = [jnp.zeros_like(best) > 1.0] * EXPERTS_PER_GROUP
    for g in range(N_GROUPS):
        here = gsel == float(g)
        sel = [jnp.logical_or(s, jnp.logical_and(here, f)) for s, f in zip(sel, groups[g][0])]
    pair = jnp.where(sel[0],
                     jnp.where(sel[1], 0.0, jnp.where(sel[2], 1.0, 2.0)),
                     jnp.where(sel[1], jnp.where(sel[2], 3.0, 4.0), 5.0))
    bucket = (gsel * float(PAIRS_PER_GROUP) + pair).astype(jnp.int32)

    brow = lax.broadcasted_iota(jnp.int32, (BUCKET_ROWS, tm), 0)
    onehot = (brow == bucket).astype(F32)
    before = jnp.dot(onehot.astype(BF16), lt_ref[...], preferred_element_type=F32)
    carry = carry_ref[...]
    in_tile = jnp.sum(onehot * before, axis=0, keepdims=True)
    base = jnp.sum(onehot * carry[:, 0:1], axis=0, keepdims=True)
    rank_ref[...] = (base + in_tile).astype(jnp.int32)
    bucket_ref[...] = bucket
    new_carry = carry + jnp.sum(onehot, axis=1, keepdims=True)
    carry_ref[...] = new_carry
    counts_ref[...] = new_carry.astype(jnp.int32)


def _slab(ref, r):
    return ref.at[pl.ds(pl.multiple_of(r * ROW_SLAB, ROW_SLAB), ROW_SLAB)]


def _wait_rows(src_ref, dst_ref, sem, n):
    def body(c, carry):
        for _ in range(DMA_CHUNK):
            pltpu.make_async_copy(_slab(src_ref, 0), _slab(dst_ref, 0), sem).wait()
        return carry
    lax.fori_loop(0, n // DMA_CHUNK, body, 0)


def _scatter_kernel(dest_ref, pad_lo_ref, pad_hi_ref, x_ref, sc_ref, sh_ref, xs_ref, hbuf, sems):
    i = pl.program_id(0)
    last = pl.num_programs(0) - 1
    tm = x_ref.shape[0]
    h = x_ref[...] * (1.0 + sc_ref[0]) + sh_ref[0]

    for s in range(2):
        @pl.when(lax.rem(i, 2) == s)
        def _(s=s):
            buf = hbuf.at[s]
            _store_slabs(buf, h)

            def issue(c, carry):
                for u in range(DMA_CHUNK):
                    r = c * DMA_CHUNK + u
                    pltpu.make_async_copy(_slab(buf, r), _slab(xs_ref, dest_ref[i * tm + r]), sems.at[s]).start(priority=u % 2)
                return carry
            lax.fori_loop(0, tm // DMA_CHUNK, issue, 0)

            @pl.when(i > 0)
            def _():
                _wait_rows(hbuf.at[1 - s], xs_ref, sems.at[1 - s], tm)

            @pl.when(i == last)
            def _():
                _wait_rows(buf, xs_ref, sems.at[s], tm)

                def filler(row0, n_rows, start):
                    cp = pltpu.make_async_copy(
                        buf.at[pl.ds(0, n_rows * ROW_SLAB)],
                        xs_ref.at[pl.ds(pl.multiple_of(row0 * ROW_SLAB, ROW_SLAB), n_rows * ROW_SLAB)],
                        sems.at[s])
                    if start:
                        cp.start()
                    else:
                        cp.wait()

                def fill(start):
                    def bucket(b, carry):
                        pos = pad_lo_ref[b]
                        n = pad_hi_ref[b] - pos
                        for bit in reversed(range(MOE_TM.bit_length() - 1)):
                            take = lax.rem(lax.shift_right_logical(n, bit), 2)

                            @pl.when(take == 1)
                            def _(pos=pos, bit=bit):
                                filler(pos, 1 << bit, start)
                            pos = pos + take * (1 << bit)
                        return carry
                    lax.fori_loop(0, N_BUCKETS, bucket, 0)

                    def tail(t, carry):
                        filler(t * MOE_TM, MOE_TM, start)
                        return carry
                    n_tiles = xs_ref.shape[0] // (MOE_TM * ROW_SLAB)
                    lax.fori_loop(pad_hi_ref[N_BUCKETS - 1] // MOE_TM, n_tiles, tail, 0)
                fill(True)
                fill(False)


def _scatter_rows(x2, mods, k, B, S, plan, n_out):
    T, D = x2.shape
    tm = PROJ_TM
    per_b = S // tm
    return pl.pallas_call(
        _scatter_kernel,
        grid_spec=pltpu.PrefetchScalarGridSpec(
            num_scalar_prefetch=3,
            grid=(T // tm,),
            in_specs=[
                pl.BlockSpec((tm, D), lambda i, *_: (i, 0)),
                pl.BlockSpec((1, 1, D), lambda i, *_: (_mod_row(k, i // per_b, SCALE, B), 0, 0)),
                pl.BlockSpec((1, 1, D), lambda i, *_: (_mod_row(k, i // per_b, SHIFT, B), 0, 0)),
            ],
            out_specs=pl.BlockSpec(memory_space=pl.ANY),
            scratch_shapes=[pltpu.VMEM((2, tm * ROW_SLAB, LANES), F32), pltpu.SemaphoreType.DMA((2,))],
        ),
        out_shape=jax.ShapeDtypeStruct((n_out * ROW_SLAB, LANES), F32),
        compiler_params=_cparams(("arbitrary",)),
        name="scatter_rows",
    )(plan["dest"], plan["pad_lo"], plan["pad_hi"], x2, mods, mods)


def _expert_pair(x, wg_bf, wu_bf, wd_bf):
    gates = [jnp.dot(x, wg_bf[e], preferred_element_type=F32) for e in range(2)]
    ups = [jnp.dot(x, wu_bf[e], preferred_element_type=F32) for e in range(2)]
    outs = []
    for e in range(2):
        hid = gates[e] * jax.nn.sigmoid(gates[e]) * ups[e]
        outs.append(jnp.dot(hid.astype(BF16), wd_bf[e], preferred_element_type=F32))
    return outs


def _router_logit(chunks, rw_ref, rb_ref, e):
    w_row = rw_ref[pl.ds(e, 1), :]
    acc = chunks[0] * w_row[:, 0:LANES]
    for j in range(1, ROW_SLAB):
        acc = acc + chunks[j] * w_row[:, j * LANES:(j + 1) * LANES]
    return jnp.sum(acc, axis=-1, keepdims=True) + rb_ref[pl.ds(e, 1), 0:1]


def _moe_kernel(src_ref, ea_ref, eb_ref, valid_ref, na_ref, nb_ref,
                xs_ref, rw_ref, rb_ref, wg_hbm, wu_hbm, wd_hbm, o_ref,
                wg_st, wu_st, wd_st, wg_bf, wu_bf, wd_bf, sems, *, layer):
    i = pl.program_id(0)
    prev = jnp.maximum(i - 1, 0)

    def fetch(slot, e):
        return [pltpu.make_async_copy(wg_hbm.at[layer, e], wg_st.at[slot], sems.at[slot, 0]),
                pltpu.make_async_copy(wu_hbm.at[layer, e], wu_st.at[slot], sems.at[slot, 1]),
                pltpu.make_async_copy(wd_hbm.at[layer, e], wd_st.at[slot], sems.at[slot, 2])]

    for slot, e_ref, n_ref in ((0, ea_ref, na_ref), (1, eb_ref, nb_ref)):
        @pl.when(i == 0)
        def _(slot=slot, e_ref=e_ref):
            for cp in fetch(slot, e_ref[0]):
                cp.start()

        @pl.when(jnp.logical_or(i == 0, e_ref[i] != e_ref[prev]))
        def _(slot=slot, e_ref=e_ref, n_ref=n_ref):
            for cp in fetch(slot, e_ref[i]):
                cp.wait()
            wg_bf[slot] = wg_st[slot].astype(BF16)
            wu_bf[slot] = wu_st[slot].astype(BF16)
            wd_bf[slot] = wd_st[slot].astype(BF16)

            @pl.when(n_ref[i] >= 0)
            def _():
                for cp in fetch(slot, n_ref[i]):
                    cp.start()

    @pl.when(valid_ref[i] == 1)
    def _():
        chunks = _load_slab_chunks(xs_ref, MOE_TM)
        l_a = _router_logit(chunks, rw_ref, rb_ref, ea_ref[i])
        l_b = _router_logit(chunks, rw_ref, rb_ref, eb_ref[i])
        w_a = jax.nn.sigmoid(l_a - l_b)
        w_b = jax.nn.sigmoid(l_b - l_a)
        x = jnp.concatenate([c.astype(BF16) for c in chunks], axis=-1)
        y_a, y_b = _expert_pair(x, wg_bf, wu_bf, wd_bf)
        y = w_a * y_a + w_b * y_b
        _store_slabs(o_ref, y)

    @pl.when(valid_ref[i] == 0)
    def _():
        o_ref[...] = jnp.zeros_like(o_ref)


def _grouped_experts(xs, plan, router_wT, router_bt, w_gate, w_up, w_down, layer):
    n_tiles = xs.shape[0] // (MOE_TM * ROW_SLAB)
    D, Fd = D_MODEL, D_EXPERT
    row_spec = lambda index: pl.BlockSpec((MOE_TM * ROW_SLAB, LANES), index)
    hbm = pl.BlockSpec(memory_space=pl.ANY)
    return pl.pallas_call(
        functools.partial(_moe_kernel, layer=layer),
        grid_spec=pltpu.PrefetchScalarGridSpec(
            num_scalar_prefetch=6,
            grid=(n_tiles,),
            in_specs=[
                row_spec(lambda i, src, *_: (src[i], 0)),
                pl.BlockSpec((N_EXPERTS, D), lambda i, *_: (0, 0)),
                pl.BlockSpec((N_EXPERTS, LANES), lambda i, *_: (0, 0)),
                hbm, hbm, hbm,
            ],
            out_specs=row_spec(lambda i, *_: (i, 0)),
            scratch_shapes=[
                pltpu.VMEM((2, D, Fd), F32), pltpu.VMEM((2, D, Fd), F32), pltpu.VMEM((2, Fd, D), F32),
                pltpu.VMEM((2, D, Fd), BF16), pltpu.VMEM((2, D, Fd), BF16), pltpu.VMEM((2, Fd, D), BF16),
                pltpu.SemaphoreType.DMA((2, 3)),
            ],
        ),
        out_shape=jax.ShapeDtypeStruct(xs.shape, F32),
        compiler_params=_cparams(("arbitrary",)),
        name="grouped_experts",
    )(plan["tile_src"], plan["tile_ea"], plan["tile_eb"], plan["tile_valid"], plan["next_ea"], plan["next_eb"],
      xs, router_wT, router_bt, w_gate, w_up, w_down)


def _next_change(e):
    n = e.shape[0]
    idx = jnp.arange(n, dtype=jnp.int32)
    change = jnp.concatenate([jnp.ones((1,), jnp.bool_), e[1:] != e[:-1]])
    at = jnp.where(change, idx, n)
    nxt = lax.cummin(at[::-1])[::-1]
    nxt = jnp.concatenate([nxt[1:], jnp.full((1,), n, jnp.int32)])
    return jnp.where(nxt < n, jnp.take(e, jnp.minimum(nxt, n - 1)), -1).astype(jnp.int32)


def _moe_plan(bucket, rank, counts, n_tiles):
    cnt = counts[:N_BUCKETS, 0]
    tiles_b = (cnt + MOE_TM - 1) // MOE_TM
    tile_end = jnp.cumsum(tiles_b)
    tile_start = tile_end - tiles_b
    row_start = tile_start * MOE_TM
    dest = jnp.take(row_start, bucket[0]) + rank[0]
    n_used = tile_end[-1]
    t = jnp.arange(n_tiles, dtype=jnp.int32)
    t_c = jnp.minimum(t, n_used - 1)
    tb = jnp.sum((t_c[:, None] >= tile_end[None, :]).astype(jnp.int32), axis=1)
    pair_lo = jnp.array([0, 0, 0, 1, 1, 2], jnp.int32)
    pair_hi = jnp.array([1, 2, 3, 2, 3, 3], jnp.int32)
    grp = tb // PAIRS_PER_GROUP
    pr = tb % PAIRS_PER_GROUP
    tile_ea = (grp * EXPERTS_PER_GROUP + jnp.take(pair_lo, pr)).astype(jnp.int32)
    tile_eb = (grp * EXPERTS_PER_GROUP + jnp.take(pair_hi, pr)).astype(jnp.int32)
    return {
        "dest": dest.astype(jnp.int32),
        "pad_lo": (row_start + cnt).astype(jnp.int32),
        "pad_hi": (tile_end * MOE_TM).astype(jnp.int32),
        "tile_src": t_c.astype(jnp.int32),
        "tile_ea": tile_ea,
        "tile_eb": tile_eb,
        "next_ea": _next_change(tile_ea),
        "next_eb": _next_change(tile_eb),
        "tile_valid": (t < n_used).astype(jnp.int32),
    }


def _resln_kernel(dest_ref, ys_ref, x_ref, gate_ref, g_ref, b_ref, o_ref, ybuf, sems):
    i = pl.program_id(0)
    n = pl.num_programs(0)
    tm = x_ref.shape[0]

    def issue(tile, s):
        def body(c, carry):
            for u in range(DMA_CHUNK):
                r = c * DMA_CHUNK + u
                pltpu.make_async_copy(_slab(ys_ref, dest_ref[tile * tm + r]), _slab(ybuf.at[s], r), sems.at[s]).start(priority=u % 2)
            return carry
        lax.fori_loop(0, tm // DMA_CHUNK, body, 0)

    @pl.when(i == 0)
    def _():
        issue(0, 0)

    for s in range(2):
        @pl.when(lax.rem(i, 2) == s)
        def _(s=s):
            @pl.when(i + 1 < n)
            def _():
                issue(i + 1, 1 - s)

            _wait_rows(ys_ref, ybuf.at[s], sems.at[s], tm)
            y = jnp.concatenate(_load_slab_chunks(ybuf.at[s], tm), axis=-1)
            u = DEEPNORM_ALPHA * x_ref[...] + (1.0 + gate_ref[0]) * y
            o_ref[...] = _layer_norm_rows(u, g_ref[...], b_ref[...])


def _gather_residual_ln(ys, plan, x2, mods, k, B, S, g, b):
    T, D = x2.shape
    tm = PROJ_TM
    per_b = S // tm
    return pl.pallas_call(
        _resln_kernel,
        grid_spec=pltpu.PrefetchScalarGridSpec(
            num_scalar_prefetch=1,
            grid=(T // tm,),
            in_specs=[
                pl.BlockSpec(memory_space=pl.ANY),
                pl.BlockSpec((tm, D), lambda i, *_: (i, 0)),
                pl.BlockSpec((1, 1, D), lambda i, *_: (_mod_row(k, i // per_b, GATE, B), 0, 0)),
                pl.BlockSpec((1, D), lambda i, *_: (0, 0)),
                pl.BlockSpec((1, D), lambda i, *_: (0, 0)),
            ],
            out_specs=pl.BlockSpec((tm, D), lambda i, *_: (i, 0)),
            scratch_shapes=[pltpu.VMEM((2, tm * ROW_SLAB, LANES), F32), pltpu.SemaphoreType.DMA((2,))],
        ),
        out_shape=jax.ShapeDtypeStruct((T, D), F32),
        compiler_params=_cparams(("arbitrary",)),
        name="gather_residual_ln",
    )(plan["dest"], ys, x2, mods, g.reshape(1, D), b.reshape(1, D))


def _moe_sublayer(x, routing, mods, k, router_wT, router_bt, w_gate, w_up, w_down, layer, g, b):
    B, S, D = x.shape
    T = B * S
    x2 = x.reshape(T, D)
    bucket, rank, counts = routing
    n_tiles = T // MOE_TM + N_BUCKETS
    plan = _moe_plan(bucket, rank, counts, n_tiles)
    xs = _scatter_rows(x2, mods, k, B, S, plan, n_tiles * MOE_TM)
    ys = _grouped_experts(xs, plan, router_wT, router_bt, w_gate, w_up, w_down, layer)
    return _gather_residual_ln(ys, plan, x2, mods, k, B, S, g, b).reshape(B, S, D)


def kernel(x, c, positions, ada_w, ada_b, ln_g, ln_b, sb_w_in, sb_w_out, ret_w_in, ret_gn_g, ret_w_out,
           router_w, router_b, moe_w_gate, moe_w_up, moe_w_down):
    mods = _ada_params(c, ada_w, ada_b)
    cc, ss = _rope_tables(positions)
    tables = _retention_decay_tables()
    u = (jnp.arange(SB_TK)[:, None] < jnp.arange(SB_TK)[None, :]).astype(BF16)
    lt = (jnp.arange(PROJ_TM)[:, None] < jnp.arange(PROJ_TM)[None, :]).astype(BF16)
    router_wT = router_w.T
    router_bt = jnp.broadcast_to(router_b[:, None], (N_EXPERTS, LANES))
    for i in range(DEPTH):
        k_mix, k_moe = 2 * i, 2 * i + 1
        if i % 2 == 0:
            qkv = _mod_matmul(x, mods, k_mix, sb_w_in, i // 2, tn=3 * D_MODEL // 2)
            a = _sb_attention(qkv, u)
            w_out = sb_w_out
        else:
            proj = _mod_matmul(x, mods, k_mix, ret_w_in, i // 2, tn=2 * D_MODEL)
            a = _retention(proj, cc, ss, tables, ret_gn_g[i // 2])
            w_out = ret_w_out
        x, *routing = _out_proj_ln(a, w_out, i // 2, x, mods, k_mix, ln_g[i, 0], ln_b[i, 0],
                                   k_moe, router_wT, router_bt, lt)
        x = _moe_sublayer(x, routing, mods, k_moe, router_wT, router_bt, moe_w_gate, moe_w_up, moe_w_down, i,
                          ln_g[i, 1], ln_b[i, 1])
    return x
```

```python
import functools

import jax
import jax.numpy as jnp
from jax import lax
from jax.experimental import pallas as pl
from jax.experimental.pallas import tpu as pltpu

F32 = jnp.float32
BF16 = jnp.bfloat16

D_MODEL = 1024
DEPTH = 4
SB_HEADS = 16
SB_HEAD_DIM = D_MODEL // SB_HEADS
RET_HEADS = 8
RET_QK_DIM = D_MODEL // RET_HEADS
RET_V_DIM = 2 * RET_QK_DIM
ROPE_BASE = 10000.0
N_EXPERTS = 16
N_GROUPS = 4
EXPERTS_PER_GROUP = N_EXPERTS // N_GROUPS
D_EXPERT = D_MODEL // 2
DEEPNORM_ALPHA = (2 * DEPTH) ** 0.25
LN_EPS = 1e-5
LOG2_E = 1.4426950408889634

LANES = 128
SUBLANES = 8
MXU_DIM = 256
VMEM_LIMIT_BYTES = 56 * 1024 * 1024

PROJ_TM = 512
MM_TM = 1024
OUT_SPLIT = 4
SB_TQ = MXU_DIM
SB_TK = MXU_DIM
RET_CHUNK = 256
RET_GROUP = 8
MOE_TM = 256
PAIRS_PER_GROUP = 6
N_BUCKETS = N_GROUPS * PAIRS_PER_GROUP
BUCKET_ROWS = 32
ROW_SLAB = D_MODEL // LANES
DMA_CHUNK = 16
assert ROW_SLAB == SUBLANES


def _cparams(semantics):
    return pltpu.CompilerParams(dimension_semantics=semantics, vmem_limit_bytes=VMEM_LIMIT_BYTES)


def _split_bf16(a):
    hi = a.astype(BF16)
    lo = (a - hi.astype(F32)).astype(BF16)
    return hi, lo


def _ada_kernel(c_ref, w_ref, b_ref, o_ref):
    c = c_ref[...]
    s = c * jax.nn.sigmoid(c)
    sh, sl = _split_bf16(s)
    wh, wl = _split_bf16(w_ref[0])
    acc = jnp.dot(sh, wh, preferred_element_type=F32)
    acc += jnp.dot(sl, wh, preferred_element_type=F32)
    acc += jnp.dot(sh, wl, preferred_element_type=F32)
    o_ref[0] = acc + b_ref[0]


def _ada_params(c, ada_w, ada_b):
    B, D = c.shape
    n_sub = ada_w.shape[0] * ada_w.shape[1]
    N = ada_w.shape[-1]
    tn = N // 2
    w = ada_w.reshape(n_sub, D, N)
    b = ada_b.reshape(n_sub, 1, N)
    out = pl.pallas_call(
        _ada_kernel,
        grid=(n_sub, N // tn),
        in_specs=[
            pl.BlockSpec((B, D), lambda k, j: (0, 0)),
            pl.BlockSpec((1, D, tn), lambda k, j: (k, 0, j)),
            pl.BlockSpec((1, 1, tn), lambda k, j: (k, 0, j)),
        ],
        out_specs=pl.BlockSpec((1, B, tn), lambda k, j: (k, 0, j)),
        out_shape=jax.ShapeDtypeStruct((n_sub, B, N), F32),
        compiler_params=_cparams(("arbitrary", "arbitrary")),
        name="ada_params",
    )(c, w, b)
    return out.reshape(n_sub * B * 3, 1, D)


def _mod_row(k, b, which, B):
    return (k * B + b) * 3 + which


SHIFT, SCALE, GATE = 0, 1, 2


def _modmm_kernel(x_ref, sc_ref, sh_ref, w_ref, o_ref, w_bf):
    @pl.when(jnp.logical_and(pl.program_id(1) == 0, pl.program_id(2) == 0))
    def _():
        w_bf[...] = w_ref[0].astype(BF16)

    h = x_ref[0] * (1.0 + sc_ref[0]) + sh_ref[0]
    o_ref[0] = jnp.dot(h.astype(BF16), w_bf[...], preferred_element_type=F32).astype(o_ref.dtype)


def _mod_matmul(x, mods, k, w_all, layer, tn):
    B, S, D = x.shape
    N = w_all.shape[2]
    tm = MM_TM
    return pl.pallas_call(
        _modmm_kernel,
        grid=(N // tn, B, S // tm),
        in_specs=[
            pl.BlockSpec((1, tm, D), lambda n, b, i: (b, i, 0)),
            pl.BlockSpec((1, 1, D), lambda n, b, i: (_mod_row(k, b, SCALE, B), 0, 0)),
            pl.BlockSpec((1, 1, D), lambda n, b, i: (_mod_row(k, b, SHIFT, B), 0, 0)),
            pl.BlockSpec((1, D, tn), lambda n, b, i: (layer, 0, n)),
        ],
        out_specs=pl.BlockSpec((1, tm, tn), lambda n, b, i: (b, i, n)),
        out_shape=jax.ShapeDtypeStruct((B, S, N), BF16),
        scratch_shapes=[pltpu.VMEM((D, tn), BF16)],
        compiler_params=_cparams(("arbitrary", "arbitrary", "arbitrary")),
        name="mod_matmul",
    )(x, mods, mods, w_all)


SB_PIPE = 4
SB_UNROLL = 4


def _sb_tile_tables(nq):
    near = []
    for i in range(nq):
        near += [(i, i), (i, i - 1) if i else (nq, 0)]
    far = [(i, i - o) for o in range(2, nq) for i in range(o, nq)]
    ti, tj, spans = [], [], []
    for tiles in (near, far):
        padded = [(nq, 0)] * SB_PIPE + tiles + [(nq, 0)] * (SB_PIPE + SB_UNROLL)
        spans.append((len(ti), len(tiles)))
        ti += [t[0] for t in padded]
        tj += [t[1] for t in padded]
    return ti, tj, spans


def _sb_kernel(ti_ref, tj_ref, q_ref, k_ref, v_ref, u_ref, o_ref,
               vT_ref, qT_ref, z_ref, sp_ref, d_ref, a_ref, c_ref, run_ref, acc_ref, alive_ref, n_alive_ref, *, spans):
    S = q_ref.shape[1]
    tq, tk, dh = SB_TQ, SB_TK, SB_HEAD_DIM
    nq = S // tq

    for c in range(S // tk):
        v_c = v_ref[0, c * tk:(c + 1) * tk, :].astype(F32)
        vT_ref[c] = v_c.T.astype(BF16)

    scale = dh ** -0.5 * LOG2_E
    head_rows = lax.broadcasted_iota(jnp.int32, (2 * dh, tq), 0)
    for i in range(nq):
        qT = (q_ref[0, i * tq:(i + 1) * tq, :].astype(F32) * scale).T
        qT_ref[i, 0] = jnp.where(head_rows < dh, qT, 0.0).astype(BF16)
        qT_ref[i, 1] = jnp.where(head_rows >= dh, qT, 0.0).astype(BF16)
    qT_ref[nq] = jnp.zeros_like(qT_ref[nq])
    z_ref[...] = jnp.zeros_like(z_ref)
    sp_ref[...] = jnp.zeros_like(sp_ref)
    d_ref[...] = jnp.zeros_like(d_ref)
    a_ref[...] = jnp.zeros_like(a_ref)
    c_ref[...] = jnp.zeros_like(c_ref)
    for i in range(nq + 1):
        alive_ref[i] = jnp.int32(1)
    n_alive_ref[0] = jnp.int32(nq)
    run_ref[nq] = jnp.zeros_like(run_ref[nq])
    acc_ref[nq] = jnp.zeros_like(acc_ref[nq])

    def iteration(t1, t3, t4, p, diag2, diag3, diag4):
        q = 1 - p
        i1, j1 = ti_ref[t1], tj_ref[t1]
        i3 = ti_ref[t3]
        i4, j4 = ti_ref[t4], tj_ref[t4]
        if diag2 or diag3:
            rows = lax.broadcasted_iota(jnp.int32, (tk, tq), 0)
            cols = lax.broadcasted_iota(jnp.int32, (tk, tq), 1)
            valid = rows < cols
        u = u_ref[...]
        suf = [jnp.dot(u, sp_ref[q, h], preferred_element_type=F32) for h in range(2)]
        vT = vT_ref[j4]
        pv = [jnp.dot(vT[h * dh:(h + 1) * dh, :], a_ref[p, h], preferred_element_type=F32) for h in range(2)]
        k_blk = k_ref[0, pl.ds(pl.multiple_of(j1 * tk, tk), tk), :]
        z_new = [jnp.dot(k_blk, qT_ref[i1, h], preferred_element_type=F32) for h in range(2)]
        for h in range(2):
            z = z_ref[p, h]
            sp = jnp.maximum(z, 0.0) + jnp.log2(1.0 + jnp.exp2(-jnp.abs(z)))
            d_ref[p, h] = z - sp
            if diag2:
                sp = jnp.where(valid, sp, 0.0)
            sp_ref[p, h] = sp.astype(BF16)
        for h in range(2):
            a = jnp.exp2(d_ref[q, h] - suf[h])
            if diag3:
                a = jnp.where(valid, a, 0.0)
            a_ref[q, h] = a.astype(BF16)
            tile_mass = suf[h][0:1, :] + sp_ref[q, h, 0:1, :].astype(F32)
            if diag3:
                run_ref[i3, h, 0:1, :] = tile_mass
            else:
                run = run_ref[i3, h, 0:1, :]
                c_ref[q, h, 0:1, :] = jnp.exp2(-run)
                run_ref[i3, h, 0:1, :] = run + tile_mass
                c_next = jnp.exp2(-(run + tile_mass))
                c_max = c_next if h == 0 else jnp.maximum(c_max, c_next)
        if not diag3:
            live = jnp.logical_or(jnp.max(c_max) > 0.0, i3 == nq).astype(jnp.int32)
            n_alive_ref[0] += live - alive_ref[i3]
            alive_ref[i3] = live
        for h in range(2):
            if diag4:
                acc_ref[i4, h] = pv[h]
            else:
                acc_ref[i4, h] += pv[h] * c_ref[p, h, 0:1, :]
            z_ref[p, h] = z_new[h]

    base, n_real = spans[0]
    n_iter = n_real + SB_PIPE
    assert base % 2 == 0 and SB_PIPE % 2 == 0 and n_iter % SB_UNROLL == 0

    def near_body(t, carry):
        for r in range(SB_UNROLL):
            t1 = base + SB_PIPE + SB_UNROLL * t + r
            even = r % 2 == 0
            iteration(t1, t1 - 3, t1 - 4, r % 2, even, not even, even)
        return carry
    lax.fori_loop(0, n_iter // SB_UNROLL, near_body, 0)

    base, n_real = spans[1]
    first = base + SB_PIPE
    end = first + n_real

    def next_live(t):
        return lax.while_loop(lambda u: alive_ref[ti_ref[u]] == 0, lambda u: u + 1, t)

    def far_body(carry):
        ptr, h1, h2, h3, h4 = carry
        picks = []
        for r in range(SB_UNROLL):
            t1 = next_live(ptr)
            picks.append(t1)
            ptr = t1 + 1
        for r, t1 in enumerate(picks):
            iteration(t1, h3, h4, r % 2, False, False, False)
            h1, h2, h3, h4 = t1, h1, h2, h3
        return ptr, h1, h2, h3, h4

    start = jnp.where(n_alive_ref[0] == 0, end + SB_PIPE, first).astype(jnp.int32)
    lax.while_loop(lambda carry: carry[0] < end + SB_PIPE, far_body,
                   (start, jnp.int32(first - 1), jnp.int32(first - 2), jnp.int32(first - 3), jnp.int32(first - 4)))

    for i in range(nq):
        oT = jnp.concatenate([acc_ref[i, 0], acc_ref[i, 1]], axis=0)
        o_ref[0, i * tq:(i + 1) * tq, :] = oT.T.astype(o_ref.dtype)


def _sb_attention(qkv, u):
    B, S, _ = qkv.shape
    n_pairs = SB_HEADS // 2
    nq = S // SB_TQ
    blk = (1, S, 2 * SB_HEAD_DIM)
    ti, tj, spans = _sb_tile_tables(nq)
    tile = (SB_TK, SB_TQ)
    return pl.pallas_call(
        functools.partial(_sb_kernel, spans=spans),
        grid_spec=pltpu.PrefetchScalarGridSpec(
            num_scalar_prefetch=2,
            grid=(B, n_pairs),
            in_specs=[
                pl.BlockSpec(blk, lambda b, p, *_: (b, 0, p)),
                pl.BlockSpec(blk, lambda b, p, *_: (b, 0, n_pairs + p)),
                pl.BlockSpec(blk, lambda b, p, *_: (b, 0, 2 * n_pairs + p)),
                pl.BlockSpec((SB_TK, SB_TK), lambda b, p, *_: (0, 0)),
            ],
            out_specs=pl.BlockSpec(blk, lambda b, p, *_: (b, 0, p)),
            scratch_shapes=[
                pltpu.VMEM((S // SB_TK, 2 * SB_HEAD_DIM, SB_TK), BF16),
                pltpu.VMEM((nq + 1, 2, 2 * SB_HEAD_DIM, SB_TQ), BF16),
                pltpu.VMEM((2, 2) + tile, F32),
                pltpu.VMEM((2, 2) + tile, BF16),
                pltpu.VMEM((2, 2) + tile, F32),
                pltpu.VMEM((2, 2) + tile, BF16),
                pltpu.VMEM((2, 2, SUBLANES, SB_TQ), F32),
                pltpu.VMEM((nq + 1, 2, SUBLANES, SB_TQ), F32),
                pltpu.VMEM((nq + 1, 2, SB_HEAD_DIM, SB_TQ), F32),
                pltpu.SMEM((nq + 1,), jnp.int32),
                pltpu.SMEM((1,), jnp.int32),
            ],
        ),
        out_shape=jax.ShapeDtypeStruct((B, S, D_MODEL), BF16),
        compiler_params=_cparams(("arbitrary", "arbitrary")),
        name="sb_attention",
    )(jnp.asarray(ti, jnp.int32), jnp.asarray(tj, jnp.int32), qkv, qkv, qkv, u)


def _layer_norm_rows(u, g, b):
    mu = jnp.mean(u, axis=-1, keepdims=True)
    d = u - mu
    var = jnp.mean(d * d, axis=-1, keepdims=True)
    return d * lax.rsqrt(var + LN_EPS) * g + b


def _outln_kernel(a_ref, w_ref, x_ref, gate_ref, g_ref, b_ref, sc2_ref, sh2_ref, rw_ref, rb_ref, lt_ref,
                  o_ref, bucket_ref, rank_ref, counts_ref, w_bf, carry_ref):
    @pl.when(jnp.logical_and(pl.program_id(0) == 0, pl.program_id(1) == 0))
    def _():
        w_bf[...] = w_ref[0].astype(BF16)
        carry_ref[...] = jnp.zeros_like(carry_ref)

    tm = a_ref.shape[1]
    sub = tm // OUT_SPLIT
    nt = (((1,), (1,)), ((), ()))
    rw = rw_ref[...].astype(BF16)
    ys = [jnp.dot(a_ref[0, r * sub:(r + 1) * sub, :], w_bf[...], preferred_element_type=F32) for r in range(OUT_SPLIT)]
    logits = []
    for r in range(OUT_SPLIT):
        rows = slice(r * sub, (r + 1) * sub)
        u = DEEPNORM_ALPHA * x_ref[0, rows, :] + (1.0 + gate_ref[0]) * ys[r]
        xn = _layer_norm_rows(u, g_ref[...], b_ref[...])
        o_ref[0, rows, :] = xn
        h = xn * (1.0 + sc2_ref[0]) + sh2_ref[0]
        logits.append(lax.dot_general(rw, h.astype(BF16), nt, preferred_element_type=F32))
    logits = jnp.concatenate(logits, axis=1) + rb_ref[:, 0:1]
    _route(logits, lt_ref, bucket_ref, rank_ref, counts_ref, carry_ref)


def _out_proj_ln(a, w_all, layer, x, mods, k, g, b, k_moe, router_wT, router_bt, lt):
    B, S, D = x.shape
    K = a.shape[-1]
    tm = PROJ_TM
    per_b = S // tm
    T = B * S
    mod = lambda kk, which: pl.BlockSpec((1, 1, D), lambda bb, i: (_mod_row(kk, bb, which, B), 0, 0))
    const = lambda shape: pl.BlockSpec(shape, lambda bb, i: (0,) * len(shape))
    return pl.pallas_call(
        _outln_kernel,
        grid=(B, per_b),
        in_specs=[
            pl.BlockSpec((1, tm, K), lambda bb, i: (bb, i, 0)),
            pl.BlockSpec((1, K, D), lambda bb, i: (layer, 0, 0)),
            pl.BlockSpec((1, tm, D), lambda bb, i: (bb, i, 0)),
            mod(k, GATE), const((1, D)), const((1, D)),
            mod(k_moe, SCALE), mod(k_moe, SHIFT),
            const((N_EXPERTS, D)), const((N_EXPERTS, LANES)), const((tm, tm)),
        ],
        out_specs=[
            pl.BlockSpec((1, tm, D), lambda bb, i: (bb, i, 0)),
            pl.BlockSpec((1, tm), lambda bb, i: (0, bb * per_b + i)),
            pl.BlockSpec((1, tm), lambda bb, i: (0, bb * per_b + i)),
            const((BUCKET_ROWS, LANES)),
        ],
        out_shape=[
            jax.ShapeDtypeStruct((B, S, D), F32),
            jax.ShapeDtypeStruct((1, T), jnp.int32),
            jax.ShapeDtypeStruct((1, T), jnp.int32),
            jax.ShapeDtypeStruct((BUCKET_ROWS, LANES), jnp.int32),
        ],
        scratch_shapes=[pltpu.VMEM((K, D), BF16), pltpu.VMEM((BUCKET_ROWS, LANES), F32)],
        compiler_params=_cparams(("arbitrary", "arbitrary")),
        name="out_proj_ln",
    )(a, w_all, x, mods, g.reshape(1, D), b.reshape(1, D), mods, mods, router_wT, router_bt, lt)


def _rope_kernel(pos_ref, invf_ref, sign_ref, cc_ref, ss_ref):
    ts = pos_ref.shape[1]
    half_t, half_l = ts // 2, invf_ref.shape[1] // 2
    lane = lax.broadcasted_iota(jnp.int32, (half_t, 2 * half_l), 1)
    low = lane < half_l
    pos = jnp.where(low, pos_ref[0, 0:half_t, :], pos_ref[0, half_t:ts, :]).astype(F32)
    ang = pos * invf_ref[...]
    for fn, ref, sign in ((jnp.cos, cc_ref, None), (jnp.sin, ss_ref, sign_ref[...])):
        val = fn(ang)
        swapped = pltpu.roll(val, half_l, 1)
        first = jnp.where(low, val, swapped)
        second = jnp.where(low, swapped, val)
        if sign is not None:
            first, second = first * sign, second * sign
        ref[0, 0:half_t, :] = first
        ref[0, half_t:ts, :] = second


def _rope_tables(positions):
    B, S = positions.shape
    half = RET_QK_DIM // 2
    inv_freq = 1.0 / (ROPE_BASE ** (jnp.arange(0, RET_QK_DIM, 2, dtype=F32) / RET_QK_DIM))
    invf2 = jnp.concatenate([inv_freq, inv_freq]).reshape(1, RET_QK_DIM)
    sign = jnp.concatenate([-jnp.ones((half,), F32), jnp.ones((half,), F32)]).reshape(1, RET_QK_DIM)
    ts = PROJ_TM
    shp = jax.ShapeDtypeStruct((B, S, RET_QK_DIM), F32)
    return pl.pallas_call(
        _rope_kernel,
        grid=(B, S // ts),
        in_specs=[
            pl.BlockSpec((1, ts, 1), lambda b, i: (b, i, 0)),
            pl.BlockSpec((1, RET_QK_DIM), lambda b, i: (0, 0)),
            pl.BlockSpec((1, RET_QK_DIM), lambda b, i: (0, 0)),
        ],
        out_specs=[pl.BlockSpec((1, ts, RET_QK_DIM), lambda b, i: (b, i, 0))] * 2,
        out_shape=[shp, shp],
        compiler_params=_cparams(("arbitrary", "arbitrary")),
        name="rope_tables",
    )(positions.reshape(B, S, 1), invf2, sign)


def _retention_decay_tables():
    C = RET_CHUNK
    kscale = RET_QK_DIM ** -0.5
    log_gamma = jnp.log(1.0 - jnp.exp2(-5.0 - jnp.arange(RET_HEADS, dtype=F32)))
    n = jnp.arange(C, dtype=F32)
    diff = n[:, None] - n[None, :]
    dmask = jnp.where(diff >= 0, jnp.exp(jnp.maximum(diff, 0.0) * log_gamma[:, None, None]), 0.0)
    xi = jnp.exp((n + 1.0) * log_gamma[:, None])
    zeta = jnp.exp((C - 1.0 - n) * log_gamma[:, None])
    chunk_decay = jnp.exp(C * log_gamma)
    dm = dmask * kscale
    xi_t = jnp.broadcast_to(xi[:, :, None], (RET_HEADS, C, RET_V_DIM))
    zeta_t = jnp.broadcast_to(zeta[:, :, None] * kscale, (RET_HEADS, C, RET_QK_DIM))
    decay_t = jnp.broadcast_to(chunk_decay[:, None, None], (RET_HEADS, RET_QK_DIM, RET_V_DIM))
    return dm, xi_t, zeta_t, decay_t


def _ret_kernel(q_ref, k_ref, v_ref, g_ref, cc_ref, ss_ref, dm_ref, xi_ref, zeta_ref, decay_ref,
                gn_ref, o_ref, state_ref):
    S = q_ref.shape[1]
    C = RET_CHUNK
    half = RET_QK_DIM // 2
    state_ref[...] = jnp.zeros_like(state_ref)

    def prep(c):
        r0 = pl.multiple_of(c * C, C)
        rows = pl.ds(r0, C)
        cc = cc_ref[0, rows, :]
        ss = ss_ref[0, rows, :]
        q = q_ref[0, rows, :].astype(F32)
        k = k_ref[0, rows, :].astype(F32)
        qr = q * cc + pltpu.roll(q, half, 1) * ss
        kr = k * cc + pltpu.roll(k, half, 1) * ss
        kzT = (kr * zeta_ref[0]).T.astype(BF16)
        return rows, qr.astype(BF16), kr.astype(BF16), v_ref[0, rows, :], kzT

    def finish(rows, inner, cross):
        o = inner + cross * xi_ref[0]
        mu = jnp.mean(o, axis=-1, keepdims=True)
        d = o - mu
        var = jnp.mean(d * d, axis=-1, keepdims=True)
        on = d * lax.rsqrt(var + LN_EPS) * gn_ref[...]
        g = g_ref[0, rows, :].astype(F32)
        o_ref[0, rows, :] = (g * jax.nn.sigmoid(g) * on).astype(o_ref.dtype)

    def chunk_group(t, carry):
        nt = (((1,), (1,)), ((), ()))
        ps = [prep(RET_GROUP * t + r) for r in range(RET_GROUP)]
        qks = [lax.dot_general(p[1], p[2], nt, preferred_element_type=F32) for p in ps]
        dss = [jnp.dot(p[4], p[3], preferred_element_type=F32) for p in ps]
        state = state_ref[...]
        crosses = []
        for p, ds in zip(ps, dss):
            crosses.append(jnp.dot(p[1], state.astype(BF16), preferred_element_type=F32))
            state = state * decay_ref[0] + ds
        state_ref[...] = state
        inners = [jnp.dot((qk * dm_ref[0]).astype(BF16), p[3], preferred_element_type=F32) for qk, p in zip(qks, ps)]
        for p, inner, cross in zip(ps, inners, crosses):
            finish(p[0], inner, cross)
        return carry

    lax.fori_loop(0, S // (RET_GROUP * C), chunk_group, 0)


def _retention(proj, cc, ss, tables, gn_g):
    B, S, _ = proj.shape
    dk, dv, H, C = RET_QK_DIM, RET_V_DIM, RET_HEADS, RET_CHUNK
    dm, xi_t, zeta_t, decay_t = tables
    return pl.pallas_call(
        _ret_kernel,
        grid=(B, H),
        in_specs=[
            pl.BlockSpec((1, S, dk), lambda b, h: (b, 0, h)),
            pl.BlockSpec((1, S, dk), lambda b, h: (b, 0, H + h)),
            pl.BlockSpec((1, S, dv), lambda b, h: (b, 0, H + h)),
            pl.BlockSpec((1, S, dv), lambda b, h: (b, 0, 2 * H + h)),
            pl.BlockSpec((1, S, dk), lambda b, h: (b, 0, 0)),
            pl.BlockSpec((1, S, dk), lambda b, h: (b, 0, 0)),
            pl.BlockSpec((1, C, C), lambda b, h: (h, 0, 0)),
            pl.BlockSpec((1, C, dv), lambda b, h: (h, 0, 0)),
            pl.BlockSpec((1, C, dk), lambda b, h: (h, 0, 0)),
            pl.BlockSpec((1, dk, dv), lambda b, h: (h, 0, 0)),
            pl.BlockSpec((1, dv), lambda b, h: (0, h)),
        ],
        out_specs=pl.BlockSpec((1, S, dv), lambda b, h: (b, 0, h)),
        out_shape=jax.ShapeDtypeStruct((B, S, H * dv), BF16),
        scratch_shapes=[pltpu.VMEM((dk, dv), F32)],
        compiler_params=_cparams(("arbitrary", "arbitrary")),
        name="retention",
    )(proj, proj, proj, proj, cc, ss, dm, xi_t, zeta_t, decay_t, gn_g.reshape(1, H * dv))


def _store_slabs(ref, val):
    tm = val.shape[0]
    for j in range(ROW_SLAB):
        ref[pl.ds(j, tm, stride=ROW_SLAB), :] = val[:, j * LANES:(j + 1) * LANES]


def _load_slab_chunks(ref, tm):
    return [ref[pl.ds(j, tm, stride=ROW_SLAB), :] for j in range(ROW_SLAB)]


def _top2_flags(vals):
    n = len(vals)
    rank = []
    for k in range(n):
        r = jnp.zeros_like(vals[0])
        for j in range(n):
            if j < k:
                r = r + (vals[j] >= vals[k]).astype(F32)
            elif j > k:
                r = r + (vals[j] > vals[k]).astype(F32)
        rank.append(r)
    is1 = [r == 0.0 for r in rank]
    is2 = [r == 1.0 for r in rank]
    m1 = sum(jnp.where(f, v, 0.0) for f, v in zip(is1, vals))
    m2 = sum(jnp.where(f, v, 0.0) for f, v in zip(is2, vals))
    return is1, is2, m1, m2


def _route(logits, lt_ref, bucket_ref, rank_ref, counts_ref, carry_ref):
    tm = logits.shape[1]
    rows = [logits[e:e + 1, :] for e in range(N_EXPERTS)]
    gmax = functools.reduce(jnp.maximum, rows)
    groups = []
    for g in range(N_GROUPS):
        vals = rows[g * EXPERTS_PER_GROUP:(g + 1) * EXPERTS_PER_GROUP]
        is1, is2, m1, m2 = _top2_flags(vals)
        score = jnp.exp(m1 - gmax) + jnp.exp(m2 - gmax)
        groups.append(([jnp.logical_or(a, b) for a, b in zip(is1, is2)], score))
    best = groups[0][1]
    gsel = jnp.zeros_like(best)
    for g in range(1, N_GROUPS):
        better = groups[g][1] > best
        best = jnp.where(better, groups[g][1], best)
        gsel = jnp.where(better, float(g), gsel)
    sel = [jnp.zeros_like(best) > 1.0] * EXPERTS_PER_GROUP
    for g in range(N_GROUPS):
        here = gsel == float(g)
        sel = [jnp.logical_or(s, jnp.logical_and(here, f)) for s, f in zip(sel, groups[g][0])]
    pair = jnp.where(sel[0],
                     jnp.where(sel[1], 0.0, jnp.where(sel[2], 1.0, 2.0)),
                     jnp.where(sel[1], jnp.where(sel[2], 3.0, 4.0), 5.0))
    bucket = (gsel * float(PAIRS_PER_GROUP) + pair).astype(jnp.int32)

    brow = lax.broadcasted_iota(jnp.int32, (BUCKET_ROWS, tm), 0)
    onehot = (brow == bucket).astype(F32)
    before = jnp.dot(onehot.astype(BF16), lt_ref[...], preferred_element_type=F32)
    carry = carry_ref[...]
    in_tile = jnp.sum(onehot * before, axis=0, keepdims=True)
    base = jnp.sum(onehot * carry[:, 0:1], axis=0, keepdims=True)
    rank_ref[...] = (base + in_tile).astype(jnp.int32)
    bucket_ref[...] = bucket
    new_carry = carry + jnp.sum(onehot, axis=1, keepdims=True)
    carry_ref[...] = new_carry
    counts_ref[...] = new_carry.astype(jnp.int32)


def _slab(ref, r):
    return ref.at[pl.ds(pl.multiple_of(r * ROW_SLAB, ROW_SLAB), ROW_SLAB)]


def _wait_rows(src_ref, dst_ref, sem, n):
    def body(c, carry):
        for _ in range(DMA_CHUNK):
            pltpu.make_async_copy(_slab(src_ref, 0), _slab(dst_ref, 0), sem).wait()
        return carry
    lax.fori_loop(0, n // DMA_CHUNK, body, 0)


def _scatter_kernel(dest_ref, pad_lo_ref, pad_hi_ref, x_ref, sc_ref, sh_ref, xs_ref, hbuf, sems):
    i = pl.program_id(0)
    last = pl.num_programs(0) - 1
    tm = x_ref.shape[0]
    h = x_ref[...] * (1.0 + sc_ref[0]) + sh_ref[0]

    for s in range(2):
        @pl.when(lax.rem(i, 2) == s)
        def _(s=s):
            buf = hbuf.at[s]
            _store_slabs(buf, h)

            def issue(c, carry):
                for u in range(DMA_CHUNK):
                    r = c * DMA_CHUNK + u
                    pltpu.make_async_copy(_slab(buf, r), _slab(xs_ref, dest_ref[i * tm + r]), sems.at[s]).start(priority=u % 2)
                return carry
            lax.fori_loop(0, tm // DMA_CHUNK, issue, 0)

            @pl.when(i > 0)
            def _():
                _wait_rows(hbuf.at[1 - s], xs_ref, sems.at[1 - s], tm)

            @pl.when(i == last)
            def _():
                _wait_rows(buf, xs_ref, sems.at[s], tm)

                def filler(row0, n_rows, start):
                    cp = pltpu.make_async_copy(
                        buf.at[pl.ds(0, n_rows * ROW_SLAB)],
                        xs_ref.at[pl.ds(pl.multiple_of(row0 * ROW_SLAB, ROW_SLAB), n_rows * ROW_SLAB)],
                        sems.at[s])
                    if start:
                        cp.start()
                    else:
                        cp.wait()

                def fill(start):
                    def bucket(b, carry):
                        pos = pad_lo_ref[b]
                        n = pad_hi_ref[b] - pos
                        for bit in reversed(range(MOE_TM.bit_length() - 1)):
                            take = lax.rem(lax.shift_right_logical(n, bit), 2)

                            @pl.when(take == 1)
                            def _(pos=pos, bit=bit):
                                filler(pos, 1 << bit, start)
                            pos = pos + take * (1 << bit)
                        return carry
                    lax.fori_loop(0, N_BUCKETS, bucket, 0)

                    def tail(t, carry):
                        filler(t * MOE_TM, MOE_TM, start)
                        return carry
                    n_tiles = xs_ref.shape[0] // (MOE_TM * ROW_SLAB)
                    lax.fori_loop(pad_hi_ref[N_BUCKETS - 1] // MOE_TM, n_tiles, tail, 0)
                fill(True)
                fill(False)


def _scatter_rows(x2, mods, k, B, S, plan, n_out):
    T, D = x2.shape
    tm = PROJ_TM
    per_b = S // tm
    return pl.pallas_call(
        _scatter_kernel,
        grid_spec=pltpu.PrefetchScalarGridSpec(
            num_scalar_prefetch=3,
            grid=(T // tm,),
            in_specs=[
                pl.BlockSpec((tm, D), lambda i, *_: (i, 0)),
                pl.BlockSpec((1, 1, D), lambda i, *_: (_mod_row(k, i // per_b, SCALE, B), 0, 0)),
                pl.BlockSpec((1, 1, D), lambda i, *_: (_mod_row(k, i // per_b, SHIFT, B), 0, 0)),
            ],
            out_specs=pl.BlockSpec(memory_space=pl.ANY),
            scratch_shapes=[pltpu.VMEM((2, tm * ROW_SLAB, LANES), F32), pltpu.SemaphoreType.DMA((2,))],
        ),
        out_shape=jax.ShapeDtypeStruct((n_out * ROW_SLAB, LANES), F32),
        compiler_params=_cparams(("arbitrary",)),
        name="scatter_rows",
    )(plan["dest"], plan["pad_lo"], plan["pad_hi"], x2, mods, mods)


def _expert_pair(x, wg_bf, wu_bf, wd_bf):
    gates = [jnp.dot(x, wg_bf[e], preferred_element_type=F32) for e in range(2)]
    ups = [jnp.dot(x, wu_bf[e], preferred_element_type=F32) for e in range(2)]
    outs = []
    for e in range(2):
        hid = gates[e] * jax.nn.sigmoid(gates[e]) * ups[e]
        outs.append(jnp.dot(hid.astype(BF16), wd_bf[e], preferred_element_type=F32))
    return outs


def _router_logit(chunks, rw_ref, rb_ref, e):
    w_row = rw_ref[pl.ds(e, 1), :]
    acc = chunks[0] * w_row[:, 0:LANES]
    for j in range(1, ROW_SLAB):
        acc = acc + chunks[j] * w_row[:, j * LANES:(j + 1) * LANES]
    return jnp.sum(acc, axis=-1, keepdims=True) + rb_ref[pl.ds(e, 1), 0:1]


def _moe_kernel(src_ref, ea_ref, eb_ref, valid_ref, na_ref, nb_ref,
                xs_ref, rw_ref, rb_ref, wg_hbm, wu_hbm, wd_hbm, o_ref,
                wg_st, wu_st, wd_st, wg_bf, wu_bf, wd_bf, sems, *, layer):
    i = pl.program_id(0)
    prev = jnp.maximum(i - 1, 0)

    def fetch(slot, e):
        return [pltpu.make_async_copy(wg_hbm.at[layer, e], wg_st.at[slot], sems.at[slot, 0]),
                pltpu.make_async_copy(wu_hbm.at[layer, e], wu_st.at[slot], sems.at[slot, 1]),
                pltpu.make_async_copy(wd_hbm.at[layer, e], wd_st.at[slot], sems.at[slot, 2])]

    for slot, e_ref, n_ref in ((0, ea_ref, na_ref), (1, eb_ref, nb_ref)):
        @pl.when(i == 0)
        def _(slot=slot, e_ref=e_ref):
            for cp in fetch(slot, e_ref[0]):
                cp.start()

        @pl.when(jnp.logical_or(i == 0, e_ref[i] != e_ref[prev]))
        def _(slot=slot, e_ref=e_ref, n_ref=n_ref):
            for cp in fetch(slot, e_ref[i]):
                cp.wait()
            wg_bf[slot] = wg_st[slot].astype(BF16)
            wu_bf[slot] = wu_st[slot].astype(BF16)
            wd_bf[slot] = wd_st[slot].astype(BF16)

            @pl.when(n_ref[i] >= 0)
            def _():
                for cp in fetch(slot, n_ref[i]):
                    cp.start()

    @pl.when(valid_ref[i] == 1)
    def _():
        chunks = _load_slab_chunks(xs_ref, MOE_TM)
        l_a = _router_logit(chunks, rw_ref, rb_ref, ea_ref[i])
        l_b = _router_logit(chunks, rw_ref, rb_ref, eb_ref[i])
        w_a = jax.nn.sigmoid(l_a - l_b)
        w_b = jax.nn.sigmoid(l_b - l_a)
        x = jnp.concatenate([c.astype(BF16) for c in chunks], axis=-1)
        y_a, y_b = _expert_pair(x, wg_bf, wu_bf, wd_bf)
        y = w_a * y_a + w_b * y_b
        _store_slabs(o_ref, y)

    @pl.when(valid_ref[i] == 0)
    def _():
        o_ref[...] = jnp.zeros_like(o_ref)


def _grouped_experts(xs, plan, router_wT, router_bt, w_gate, w_up, w_down, layer):
    n_tiles = xs.shape[0] // (MOE_TM * ROW_SLAB)
    D, Fd = D_MODEL, D_EXPERT
    row_spec = lambda index: pl.BlockSpec((MOE_TM * ROW_SLAB, LANES), index)
    hbm = pl.BlockSpec(memory_space=pl.ANY)
    return pl.pallas_call(
        functools.partial(_moe_kernel, layer=layer),
        grid_spec=pltpu.PrefetchScalarGridSpec(
            num_scalar_prefetch=6,
            grid=(n_tiles,),
            in_specs=[
                row_spec(lambda i, src, *_: (src[i], 0)),
                pl.BlockSpec((N_EXPERTS, D), lambda i, *_: (0, 0)),
                pl.BlockSpec((N_EXPERTS, LANES), lambda i, *_: (0, 0)),
                hbm, hbm, hbm,
            ],
            out_specs=row_spec(lambda i, *_: (i, 0)),
            scratch_shapes=[
                pltpu.VMEM((2, D, Fd), F32), pltpu.VMEM((2, D, Fd), F32), pltpu.VMEM((2, Fd, D), F32),
                pltpu.VMEM((2, D, Fd), BF16), pltpu.VMEM((2, D, Fd), BF16), pltpu.VMEM((2, Fd, D), BF16),
                pltpu.SemaphoreType.DMA((2, 3)),
            ],
        ),
        out_shape=jax.ShapeDtypeStruct(xs.shape, F32),
        compiler_params=_cparams(("arbitrary",)),
        name="grouped_experts",
    )(plan["tile_src"], plan["tile_ea"], plan["tile_eb"], plan["tile_valid"], plan["next_ea"], plan["next_eb"],
      xs, router_wT, router_bt, w_gate, w_up, w_down)


def _next_change(e):
    n = e.shape[0]
    idx = jnp.arange(n, dtype=jnp.int32)
    change = jnp.concatenate([jnp.ones((1,), jnp.bool_), e[1:] != e[:-1]])
    at = jnp.where(change, idx, n)
    nxt = lax.cummin(at[::-1])[::-1]
    nxt = jnp.concatenate([nxt[1:], jnp.full((1,), n, jnp.int32)])
    return jnp.where(nxt < n, jnp.take(e, jnp.minimum(nxt, n - 1)), -1).astype(jnp.int32)


def _moe_plan(bucket, rank, counts, n_tiles):
    cnt = counts[:N_BUCKETS, 0]
    tiles_b = (cnt + MOE_TM - 1) // MOE_TM
    tile_end = jnp.cumsum(tiles_b)
    tile_start = tile_end - tiles_b
    row_start = tile_start * MOE_TM
    dest = jnp.take(row_start, bucket[0]) + rank[0]
    n_used = tile_end[-1]
    t = jnp.arange(n_tiles, dtype=jnp.int32)
    t_c = jnp.minimum(t, n_used - 1)
    tb = jnp.sum((t_c[:, None] >= tile_end[None, :]).astype(jnp.int32), axis=1)
    pair_lo = jnp.array([0, 0, 0, 1, 1, 2], jnp.int32)
    pair_hi = jnp.array([1, 2, 3, 2, 3, 3], jnp.int32)
    grp = tb // PAIRS_PER_GROUP
    pr = tb % PAIRS_PER_GROUP
    tile_ea = (grp * EXPERTS_PER_GROUP + jnp.take(pair_lo, pr)).astype(jnp.int32)
    tile_eb = (grp * EXPERTS_PER_GROUP + jnp.take(pair_hi, pr)).astype(jnp.int32)
    return {
        "dest": dest.astype(jnp.int32),
        "pad_lo": (row_start + cnt).astype(jnp.int32),
        "pad_hi": (tile_end * MOE_TM).astype(jnp.int32),
        "tile_src": t_c.astype(jnp.int32),
        "tile_ea": tile_ea,
        "tile_eb": tile_eb,
        "next_ea": _next_change(tile_ea),
        "next_eb": _next_change(tile_eb),
        "tile_valid": (t < n_used).astype(jnp.int32),
    }


def _resln_kernel(dest_ref, ys_ref, x_ref, gate_ref, g_ref, b_ref, o_ref, ybuf, sems):
    i = pl.program_id(0)
    n = pl.num_programs(0)
    tm = x_ref.shape[0]

    def issue(tile, s):
        def body(c, carry):
            for u in range(DMA_CHUNK):
                r = c * DMA_CHUNK + u
                pltpu.make_async_copy(_slab(ys_ref, dest_ref[tile * tm + r]), _slab(ybuf.at[s], r), sems.at[s]).start(priority=u % 2)
            return carry
        lax.fori_loop(0, tm // DMA_CHUNK, body, 0)

    @pl.when(i == 0)
    def _():
        issue(0, 0)

    for s in range(2):
        @pl.when(lax.rem(i, 2) == s)
        def _(s=s):
            @pl.when(i + 1 < n)
            def _():
                issue(i + 1, 1 - s)

            _wait_rows(ys_ref, ybuf.at[s], sems.at[s], tm)
            y = jnp.concatenate(_load_slab_chunks(ybuf.at[s], tm), axis=-1)
            u = DEEPNORM_ALPHA * x_ref[...] + (1.0 + gate_ref[0]) * y
            o_ref[...] = _layer_norm_rows(u, g_ref[...], b_ref[...])


def _gather_residual_ln(ys, plan, x2, mods, k, B, S, g, b):
    T, D = x2.shape
    tm = PROJ_TM
    per_b = S // tm
    return pl.pallas_call(
        _resln_kernel,
        grid_spec=pltpu.PrefetchScalarGridSpec(
            num_scalar_prefetch=1,
            grid=(T // tm,),
            in_specs=[
                pl.BlockSpec(memory_space=pl.ANY),
                pl.BlockSpec((tm, D), lambda i, *_: (i, 0)),
                pl.BlockSpec((1, 1, D), lambda i, *_: (_mod_row(k, i // per_b, GATE, B), 0, 0)),
                pl.BlockSpec((1, D), lambda i, *_: (0, 0)),
                pl.BlockSpec((1, D), lambda i, *_: (0, 0)),
            ],
            out_specs=pl.BlockSpec((tm, D), lambda i, *_: (i, 0)),
            scratch_shapes=[pltpu.VMEM((2, tm * ROW_SLAB, LANES), F32), pltpu.SemaphoreType.DMA((2,))],
        ),
        out_shape=jax.ShapeDtypeStruct((T, D), F32),
        compiler_params=_cparams(("arbitrary",)),
        name="gather_residual_ln",
    )(plan["dest"], ys, x2, mods, g.reshape(1, D), b.reshape(1, D))


def _moe_sublayer(x, routing, mods, k, router_wT, router_bt, w_gate, w_up, w_down, layer, g, b):
    B, S, D = x.shape
    T = B * S
    x2 = x.reshape(T, D)
    bucket, rank, counts = routing
    n_tiles = T // MOE_TM + N_BUCKETS
    plan = _moe_plan(bucket, rank, counts, n_tiles)
    xs = _scatter_rows(x2, mods, k, B, S, plan, n_tiles * MOE_TM)
    ys = _grouped_experts(xs, plan, router_wT, router_bt, w_gate, w_up, w_down, layer)
    return _gather_residual_ln(ys, plan, x2, mods, k, B, S, g, b).reshape(B, S, D)


def kernel(x, c, positions, ada_w, ada_b, ln_g, ln_b, sb_w_in, sb_w_out, ret_w_in, ret_gn_g, ret_w_out,
           router_w, router_b, moe_w_gate, moe_w_up, moe_w_down):
    mods = _ada_params(c, ada_w, ada_b)
    cc, ss = _rope_tables(positions)
    tables = _retention_decay_tables()
    u = (jnp.arange(SB_TK)[:, None] < jnp.arange(SB_TK)[None, :]).astype(BF16)
    lt = (jnp.arange(PROJ_TM)[:, None] < jnp.arange(PROJ_TM)[None, :]).astype(BF16)
    router_wT = router_w.T
    router_bt = jnp.broadcast_to(router_b[:, None], (N_EXPERTS, LANES))
    for i in range(DEPTH):
        k_mix, k_moe = 2 * i, 2 * i + 1
        if i % 2 == 0:
            qkv = _mod_matmul(x, mods, k_mix, sb_w_in, i // 2, tn=3 * D_MODEL // 2)
            a = _sb_attention(qkv, u)
            w_out = sb_w_out
        else:
            proj = _mod_matmul(x, mods, k_mix, ret_w_in, i // 2, tn=2 * D_MODEL)
            a = _retention(proj, cc, ss, tables, ret_gn_g[i // 2])
            w_out = ret_w_out
        x, *routing = _out_proj_ln(a, w_out, i // 2, x, mods, k_mix, ln_g[i, 0], ln_b[i, 0],
                                   k_moe, router_wT, router_bt, lt)
        x = _moe_sublayer(x, routing, mods, k_moe, router_wT, router_bt, moe_w_gate, moe_w_up, moe_w_down, i,
                          ln_g[i, 1], ln_b[i, 1])
    return x
```

```python
import functools

import jax
import jax.numpy as jnp
from jax import lax
from jax.experimental import pallas as pl
from jax.experimental.pallas import tpu as pltpu

F32 = jnp.float32
BF16 = jnp.bfloat16

D_MODEL = 1024
DEPTH = 4
SB_HEADS = 16
SB_HEAD_DIM = D_MODEL // SB_HEADS
RET_HEADS = 8
RET_QK_DIM = D_MODEL // RET_HEADS
RET_V_DIM = 2 * RET_QK_DIM
ROPE_BASE = 10000.0
N_EXPERTS = 16
N_GROUPS = 4
EXPERTS_PER_GROUP = N_EXPERTS // N_GROUPS
D_EXPERT = D_MODEL // 2
DEEPNORM_ALPHA = (2 * DEPTH) ** 0.25
LN_EPS = 1e-5
LOG2_E = 1.4426950408889634

LANES = 128
SUBLANES = 8
MXU_DIM = 256
VMEM_LIMIT_BYTES = 56 * 1024 * 1024

PROJ_TM = 512
MM_TM = 1024
OUT_SPLIT = 4
SB_TQ = MXU_DIM
SB_TK = MXU_DIM
RET_CHUNK = 256
RET_GROUP = 8
MOE_TM = 256
PAIRS_PER_GROUP = 6
N_BUCKETS = N_GROUPS * PAIRS_PER_GROUP
BUCKET_ROWS = 32
ROW_SLAB = D_MODEL // LANES
DMA_CHUNK = 16
assert ROW_SLAB == SUBLANES


def _cparams(semantics):
    return pltpu.CompilerParams(dimension_semantics=semantics, vmem_limit_bytes=VMEM_LIMIT_BYTES)


def _split_bf16(a):
    hi = a.astype(BF16)
    lo = (a - hi.astype(F32)).astype(BF16)
    return hi, lo


def _ada_kernel(c_ref, w_ref, b_ref, o_ref):
    c = c_ref[...]
    s = c * jax.nn.sigmoid(c)
    sh, sl = _split_bf16(s)
    wh, wl = _split_bf16(w_ref[0])
    acc = jnp.dot(sh, wh, preferred_element_type=F32)
    acc += jnp.dot(sl, wh, preferred_element_type=F32)
    acc += jnp.dot(sh, wl, preferred_element_type=F32)
    o_ref[0] = acc + b_ref[0]


def _ada_params(c, ada_w, ada_b):
    B, D = c.shape
    n_sub = ada_w.shape[0] * ada_w.shape[1]
    N = ada_w.shape[-1]
    tn = N // 2
    w = ada_w.reshape(n_sub, D, N)
    b = ada_b.reshape(n_sub, 1, N)
    out = pl.pallas_call(
        _ada_kernel,
        grid=(n_sub, N // tn),
        in_specs=[
            pl.BlockSpec((B, D), lambda k, j: (0, 0)),
            pl.BlockSpec((1, D, tn), lambda k, j: (k, 0, j)),
            pl.BlockSpec((1, 1, tn), lambda k, j: (k, 0, j)),
        ],
        out_specs=pl.BlockSpec((1, B, tn), lambda k, j: (k, 0, j)),
        out_shape=jax.ShapeDtypeStruct((n_sub, B, N), F32),
        compiler_params=_cparams(("arbitrary", "arbitrary")),
        name="ada_params",
    )(c, w, b)
    return out.reshape(n_sub * B * 3, 1, D)


def _mod_row(k, b, which, B):
    return (k * B + b) * 3 + which


SHIFT, SCALE, GATE = 0, 1, 2


def _modmm_kernel(x_ref, sc_ref, sh_ref, w_ref, o_ref, w_bf):
    @pl.when(jnp.logical_and(pl.program_id(1) == 0, pl.program_id(2) == 0))
    def _():
        w_bf[...] = w_ref[0].astype(BF16)

    h = x_ref[0] * (1.0 + sc_ref[0]) + sh_ref[0]
    o_ref[0] = jnp.dot(h.astype(BF16), w_bf[...], preferred_element_type=F32).astype(o_ref.dtype)


def _mod_matmul(x, mods, k, w_all, layer, tn):
    B, S, D = x.shape
    N = w_all.shape[2]
    tm = MM_TM
    return pl.pallas_call(
        _modmm_kernel,
        grid=(N // tn, B, S // tm),
        in_specs=[
            pl.BlockSpec((1, tm, D), lambda n, b, i: (b, i, 0)),
            pl.BlockSpec((1, 1, D), lambda n, b, i: (_mod_row(k, b, SCALE, B), 0, 0)),
            pl.BlockSpec((1, 1, D), lambda n, b, i: (_mod_row(k, b, SHIFT, B), 0, 0)),
            pl.BlockSpec((1, D, tn), lambda n, b, i: (layer, 0, n)),
        ],
        out_specs=pl.BlockSpec((1, tm, tn), lambda n, b, i: (b, i, n)),
        out_shape=jax.ShapeDtypeStruct((B, S, N), BF16),
        scratch_shapes=[pltpu.VMEM((D, tn), BF16)],
        compiler_params=_cparams(("arbitrary", "arbitrary", "arbitrary")),
        name="mod_matmul",
    )(x, mods, mods, w_all)


SB_PIPE = 4
SB_UNROLL = 4


def _sb_tile_tables(nq):
    near = []
    for i in range(nq):
        near += [(i, i), (i, i - 1) if i else (nq, 0)]
    far = [(i, i - o) for o in range(2, nq) for i in range(o, nq)]
    ti, tj, spans = [], [], []
    for tiles in (near, far):
        padded = [(nq, 0)] * SB_PIPE + tiles + [(nq, 0)] * (SB_PIPE + SB_UNROLL)
        spans.append((len(ti), len(tiles)))
        ti += [t[0] for t in padded]
        tj += [t[1] for t in padded]
    return ti, tj, spans


def _sb_kernel(ti_ref, tj_ref, q_ref, k_ref, v_ref, u_ref, o_ref,
               vT_ref, qT_ref, z_ref, sp_ref, d_ref, a_ref, c_ref, run_ref, acc_ref, alive_ref, n_alive_ref, *, spans):
    S = q_ref.shape[1]
    tq, tk, dh = SB_TQ, SB_TK, SB_HEAD_DIM
    nq = S // tq

    for c in range(S // tk):
        vT_ref[c] = v_ref[0, c * tk:(c + 1) * tk, :].T

    scale = dh ** -0.5 * LOG2_E
    head_rows = lax.broadcasted_iota(jnp.int32, (2 * dh, tq), 0)
    for i in range(nq):
        qT = (q_ref[0, i * tq:(i + 1) * tq, :].astype(F32) * scale).astype(BF16).T
        qT_ref[i, 0] = jnp.where(head_rows < dh, qT, jnp.zeros_like(qT))
        qT_ref[i, 1] = jnp.where(head_rows >= dh, qT, jnp.zeros_like(qT))
    qT_ref[nq] = jnp.zeros_like(qT_ref[nq])
    z_ref[...] = jnp.zeros_like(z_ref)
    sp_ref[...] = jnp.zeros_like(sp_ref)
    d_ref[...] = jnp.zeros_like(d_ref)
    a_ref[...] = jnp.zeros_like(a_ref)
    c_ref[...] = jnp.zeros_like(c_ref)
    for i in range(nq + 1):
        alive_ref[i] = jnp.int32(1)
    n_alive_ref[0] = jnp.int32(nq)
    run_ref[nq] = jnp.zeros_like(run_ref[nq])
    acc_ref[nq] = jnp.zeros_like(acc_ref[nq])

    def iteration(t1, t3, t4, p, diag2, diag3, diag4):
        q = 1 - p
        i1, j1 = ti_ref[t1], tj_ref[t1]
        i3 = ti_ref[t3]
        i4, j4 = ti_ref[t4], tj_ref[t4]
        if diag2 or diag3:
            rows = lax.broadcasted_iota(jnp.int32, (tk, tq), 0)
            cols = lax.broadcasted_iota(jnp.int32, (tk, tq), 1)
            valid = rows < cols
        u = u_ref[...]
        suf = [jnp.dot(u, sp_ref[q, h], preferred_element_type=F32) for h in range(2)]
        vT = vT_ref[j4]
        pv = [jnp.dot(vT[h * dh:(h + 1) * dh, :], a_ref[p, h], preferred_element_type=F32) for h in range(2)]
        k_blk = k_ref[0, pl.ds(pl.multiple_of(j1 * tk, tk), tk), :]
        z_new = [jnp.dot(k_blk, qT_ref[i1, h], preferred_element_type=F32) for h in range(2)]
        for h in range(2):
            z = z_ref[p, h]
            sp = jnp.maximum(z, 0.0) + jnp.log2(1.0 + jnp.exp2(-jnp.abs(z)))
            d_ref[p, h] = z - sp
            if diag2:
                sp = jnp.where(valid, sp, 0.0)
            sp_ref[p, h] = sp.astype(BF16)
        for h in range(2):
            a = jnp.exp2(d_ref[q, h] - suf[h])
            if diag3:
                a = jnp.where(valid, a, 0.0)
            a_ref[q, h] = a.astype(BF16)
            tile_mass = suf[h][0:1, :] + sp_ref[q, h, 0:1, :].astype(F32)
            if diag3:
                run_ref[i3, h, 0:1, :] = tile_mass
            else:
                run = run_ref[i3, h, 0:1, :]
                c_ref[q, h, 0:1, :] = jnp.exp2(-run)
                run_ref[i3, h, 0:1, :] = run + tile_mass
                c_next = jnp.exp2(-(run + tile_mass))
                c_max = c_next if h == 0 else jnp.maximum(c_max, c_next)
        if not diag3:
            live = jnp.logical_or(jnp.max(c_max) > 0.0, i3 == nq).astype(jnp.int32)
            n_alive_ref[0] += live - alive_ref[i3]
            alive_ref[i3] = live
        for h in range(2):
            if diag4:
                acc_ref[i4, h] = pv[h]
            else:
                acc_ref[i4, h] += pv[h] * c_ref[p, h, 0:1, :]
            z_ref[p, h] = z_new[h]

    base, n_real = spans[0]
    n_iter = n_real + SB_PIPE
    assert base % 2 == 0 and SB_PIPE % 2 == 0 and n_iter % SB_UNROLL == 0

    def near_body(t, carry):
        for r in range(SB_UNROLL):
            t1 = base + SB_PIPE + SB_UNROLL * t + r
            even = r % 2 == 0
            iteration(t1, t1 - 3, t1 - 4, r % 2, even, not even, even)
        return carry
    lax.fori_loop(0, n_iter // SB_UNROLL, near_body, 0)

    base, n_real = spans[1]
    first = base + SB_PIPE
    end = first + n_real

    def next_live(t):
        return lax.while_loop(lambda u: alive_ref[ti_ref[u]] == 0, lambda u: u + 1, t)

    def far_body(carry):
        ptr, h1, h2, h3, h4 = carry
        picks = []
        for r in range(SB_UNROLL):
            t1 = next_live(ptr)
            picks.append(t1)
            ptr = t1 + 1
        for r, t1 in enumerate(picks):
            iteration(t1, h3, h4, r % 2, False, False, False)
            h1, h2, h3, h4 = t1, h1, h2, h3
        return ptr, h1, h2, h3, h4

    start = jnp.where(n_alive_ref[0] == 0, end + SB_PIPE, first).astype(jnp.int32)
    lax.while_loop(lambda carry: carry[0] < end + SB_PIPE, far_body,
                   (start, jnp.int32(first - 1), jnp.int32(first - 2), jnp.int32(first - 3), jnp.int32(first - 4)))

    for i in range(nq):
        oT = jnp.concatenate([acc_ref[i, 0], acc_ref[i, 1]], axis=0)
        o_ref[0, i * tq:(i + 1) * tq, :] = oT.astype(o_ref.dtype).T


def _sb_attention(qkv, u):
    B, S, _ = qkv.shape
    n_pairs = SB_HEADS // 2
    nq = S // SB_TQ
    blk = (1, S, 2 * SB_HEAD_DIM)
    ti, tj, spans = _sb_tile_tables(nq)
    tile = (SB_TK, SB_TQ)
    return pl.pallas_call(
        functools.partial(_sb_kernel, spans=spans),
        grid_spec=pltpu.PrefetchScalarGridSpec(
            num_scalar_prefetch=2,
            grid=(B, n_pairs),
            in_specs=[
                pl.BlockSpec(blk, lambda b, p, *_: (b, 0, p)),
                pl.BlockSpec(blk, lambda b, p, *_: (b, 0, n_pairs + p)),
                pl.BlockSpec(blk, lambda b, p, *_: (b, 0, 2 * n_pairs + p)),
                pl.BlockSpec((SB_TK, SB_TK), lambda b, p, *_: (0, 0)),
            ],
            out_specs=pl.BlockSpec(blk, lambda b, p, *_: (b, 0, p)),
            scratch_shapes=[
                pltpu.VMEM((S // SB_TK, 2 * SB_HEAD_DIM, SB_TK), BF16),
                pltpu.VMEM((nq + 1, 2, 2 * SB_HEAD_DIM, SB_TQ), BF16),
                pltpu.VMEM((2, 2) + tile, F32),
                pltpu.VMEM((2, 2) + tile, BF16),
                pltpu.VMEM((2, 2) + tile, F32),
                pltpu.VMEM((2, 2) + tile, BF16),
                pltpu.VMEM((2, 2, SUBLANES, SB_TQ), F32),
                pltpu.VMEM((nq + 1, 2, SUBLANES, SB_TQ), F32),
                pltpu.VMEM((nq + 1, 2, SB_HEAD_DIM, SB_TQ), F32),
                pltpu.SMEM((nq + 1,), jnp.int32),
                pltpu.SMEM((1,), jnp.int32),
            ],
        ),
        out_shape=jax.ShapeDtypeStruct((B, S, D_MODEL), BF16),
        compiler_params=_cparams(("arbitrary", "arbitrary")),
        name="sb_attention",
    )(jnp.asarray(ti, jnp.int32), jnp.asarray(tj, jnp.int32), qkv, qkv, qkv, u)


def _layer_norm_rows(u, g, b):
    mu = jnp.mean(u, axis=-1, keepdims=True)
    d = u - mu
    var = jnp.mean(d * d, axis=-1, keepdims=True)
    return d * lax.rsqrt(var + LN_EPS) * g + b


def _outln_kernel(a_ref, w_ref, x_ref, gate_ref, g_ref, b_ref, sc2_ref, sh2_ref, rw_ref, rb_ref, lt_ref,
                  o_ref, bucket_ref, rank_ref, counts_ref, w_bf, carry_ref):
    @pl.when(jnp.logical_and(pl.program_id(0) == 0, pl.program_id(1) == 0))
    def _():
        w_bf[...] = w_ref[0].astype(BF16)
        carry_ref[...] = jnp.zeros_like(carry_ref)

    tm = a_ref.shape[1]
    sub = tm // OUT_SPLIT
    nt = (((1,), (1,)), ((), ()))
    rw = rw_ref[...].astype(BF16)
    ys = [jnp.dot(a_ref[0, r * sub:(r + 1) * sub, :], w_bf[...], preferred_element_type=F32) for r in range(OUT_SPLIT)]
    logits = []
    for r in range(OUT_SPLIT):
        rows = slice(r * sub, (r + 1) * sub)
        u = DEEPNORM_ALPHA * x_ref[0, rows, :] + (1.0 + gate_ref[0]) * ys[r]
        xn = _layer_norm_rows(u, g_ref[...], b_ref[...])
        o_ref[0, rows, :] = xn
        h = xn * (1.0 + sc2_ref[0]) + sh2_ref[0]
        logits.append(lax.dot_general(rw, h.astype(BF16), nt, preferred_element_type=F32))
    logits = jnp.concatenate(logits, axis=1) + rb_ref[:, 0:1]
    _route(logits, lt_ref, bucket_ref, rank_ref, counts_ref, carry_ref)


def _out_proj_ln(a, w_all, layer, x, mods, k, g, b, k_moe, router_wT, router_bt, lt):
    B, S, D = x.shape
    K = a.shape[-1]
    tm = PROJ_TM
    per_b = S // tm
    T = B * S
    mod = lambda kk, which: pl.BlockSpec((1, 1, D), lambda bb, i: (_mod_row(kk, bb, which, B), 0, 0))
    const = lambda shape: pl.BlockSpec(shape, lambda bb, i: (0,) * len(shape))
    return pl.pallas_call(
        _outln_kernel,
        grid=(B, per_b),
        in_specs=[
            pl.BlockSpec((1, tm, K), lambda bb, i: (bb, i, 0)),
            pl.BlockSpec((1, K, D), lambda bb, i: (layer, 0, 0)),
            pl.BlockSpec((1, tm, D), lambda bb, i: (bb, i, 0)),
            mod(k, GATE), const((1, D)), const((1, D)),
            mod(k_moe, SCALE), mod(k_moe, SHIFT),
            const((N_EXPERTS, D)), const((N_EXPERTS, LANES)), const((tm, tm)),
        ],
        out_specs=[
            pl.BlockSpec((1, tm, D), lambda bb, i: (bb, i, 0)),
            pl.BlockSpec((1, tm), lambda bb, i: (0, bb * per_b + i)),
            pl.BlockSpec((1, tm), lambda bb, i: (0, bb * per_b + i)),
            const((BUCKET_ROWS, LANES)),
        ],
        out_shape=[
            jax.ShapeDtypeStruct((B, S, D), F32),
            jax.ShapeDtypeStruct((1, T), jnp.int32),
            jax.ShapeDtypeStruct((1, T), jnp.int32),
            jax.ShapeDtypeStruct((BUCKET_ROWS, LANES), jnp.int32),
        ],
        scratch_shapes=[pltpu.VMEM((K, D), BF16), pltpu.VMEM((BUCKET_ROWS, LANES), F32)],
        compiler_params=_cparams(("arbitrary", "arbitrary")),
        name="out_proj_ln",
    )(a, w_all, x, mods, g.reshape(1, D), b.reshape(1, D), mods, mods, router_wT, router_bt, lt)


def _rope_kernel(pos_ref, invf_ref, sign_ref, cc_ref, ss_ref):
    ang = pos_ref[0].astype(F32) * invf_ref[...]
    cc_ref[0] = jnp.cos(ang)
    ss_ref[0] = jnp.sin(ang) * sign_ref[...]


def _rope_tables(positions):
    B, S = positions.shape
    half = RET_QK_DIM // 2
    inv_freq = 1.0 / (ROPE_BASE ** (jnp.arange(0, RET_QK_DIM, 2, dtype=F32) / RET_QK_DIM))
    invf2 = jnp.concatenate([inv_freq, inv_freq]).reshape(1, RET_QK_DIM)
    sign = jnp.concatenate([-jnp.ones((half,), F32), jnp.ones((half,), F32)]).reshape(1, RET_QK_DIM)
    ts = PROJ_TM
    shp = jax.ShapeDtypeStruct((B, S, RET_QK_DIM), F32)
    return pl.pallas_call(
        _rope_kernel,
        grid=(B, S // ts),
        in_specs=[
            pl.BlockSpec((1, ts, 1), lambda b, i: (b, i, 0)),
            pl.BlockSpec((1, RET_QK_DIM), lambda b, i: (0, 0)),
            pl.BlockSpec((1, RET_QK_DIM), lambda b, i: (0, 0)),
        ],
        out_specs=[pl.BlockSpec((1, ts, RET_QK_DIM), lambda b, i: (b, i, 0))] * 2,
        out_shape=[shp, shp],
        compiler_params=_cparams(("arbitrary", "arbitrary")),
        name="rope_tables",
    )(positions.reshape(B, S, 1), invf2, sign)


def _retention_decay_tables():
    C = RET_CHUNK
    kscale = RET_QK_DIM ** -0.5
    log_gamma = jnp.log(1.0 - jnp.exp2(-5.0 - jnp.arange(RET_HEADS, dtype=F32)))
    n = jnp.arange(C, dtype=F32)
    diff = n[:, None] - n[None, :]
    dmask = jnp.where(diff >= 0, jnp.exp(jnp.maximum(diff, 0.0) * log_gamma[:, None, None]), 0.0)
    xi = jnp.exp((n + 1.0) * log_gamma[:, None])
    zeta = jnp.exp((C - 1.0 - n) * log_gamma[:, None])
    chunk_decay = jnp.exp(C * log_gamma)
    dm = dmask * kscale
    xi_t = jnp.broadcast_to(xi[:, :, None], (RET_HEADS, C, RET_V_DIM))
    zeta_t = jnp.broadcast_to(zeta[:, :, None] * kscale, (RET_HEADS, C, RET_QK_DIM))
    decay_t = jnp.broadcast_to(chunk_decay[:, None, None], (RET_HEADS, RET_QK_DIM, RET_V_DIM))
    return dm, xi_t, zeta_t, decay_t


def _ret_kernel(q_ref, k_ref, v_ref, g_ref, cc_ref, ss_ref, dm_ref, xi_ref, zeta_ref, decay_ref,
                gn_ref, o_ref, state_ref):
    S = q_ref.shape[1]
    C = RET_CHUNK
    half = RET_QK_DIM // 2
    state_ref[...] = jnp.zeros_like(state_ref)

    def prep(c):
        r0 = pl.multiple_of(c * C, C)
        rows = pl.ds(r0, C)
        cc = cc_ref[0, rows, :]
        ss = ss_ref[0, rows, :]
        q = q_ref[0, rows, :].astype(F32)
        k = k_ref[0, rows, :].astype(F32)
        qr = q * cc + pltpu.roll(q, half, 1) * ss
        kr = k * cc + pltpu.roll(k, half, 1) * ss
        kzT = (kr * zeta_ref[0]).T.astype(BF16)
        return rows, qr.astype(BF16), kr.astype(BF16), v_ref[0, rows, :], kzT

    def finish(rows, inner, cross):
        o = inner + cross * xi_ref[0]
        mu = jnp.mean(o, axis=-1, keepdims=True)
        d = o - mu
        var = jnp.mean(d * d, axis=-1, keepdims=True)
        on = d * lax.rsqrt(var + LN_EPS) * gn_ref[...]
        g = g_ref[0, rows, :].astype(F32)
        o_ref[0, rows, :] = (g * jax.nn.sigmoid(g) * on).astype(o_ref.dtype)

    def chunk_group(t, carry):
        nt = (((1,), (1,)), ((), ()))
        ps = [prep(RET_GROUP * t + r) for r in range(RET_GROUP)]
        qks = [lax.dot_general(p[1], p[2], nt, preferred_element_type=F32) for p in ps]
        dss = [jnp.dot(p[4], p[3], preferred_element_type=F32) for p in ps]
        state = state_ref[...]
        crosses = []
        for p, ds in zip(ps, dss):
            crosses.append(jnp.dot(p[1], state.astype(BF16), preferred_element_type=F32))
            state = state * decay_ref[0] + ds
        state_ref[...] = state
        inners = [jnp.dot((qk * dm_ref[0]).astype(BF16), p[3], preferred_element_type=F32) for qk, p in zip(qks, ps)]
        for p, inner, cross in zip(ps, inners, crosses):
            finish(p[0], inner, cross)
        return carry

    lax.fori_loop(0, S // (RET_GROUP * C), chunk_group, 0)


def _retention(proj, cc, ss, tables, gn_g):
    B, S, _ = proj.shape
    dk, dv, H, C = RET_QK_DIM, RET_V_DIM, RET_HEADS, RET_CHUNK
    dm, xi_t, zeta_t, decay_t = tables
    return pl.pallas_call(
        _ret_kernel,
        grid=(B, H),
        in_specs=[
            pl.BlockSpec((1, S, dk), lambda b, h: (b, 0, h)),
            pl.BlockSpec((1, S, dk), lambda b, h: (b, 0, H + h)),
            pl.BlockSpec((1, S, dv), lambda b, h: (b, 0, H + h)),
            pl.BlockSpec((1, S, dv), lambda b, h: (b, 0, 2 * H + h)),
            pl.BlockSpec((1, S, dk), lambda b, h: (b, 0, 0)),
            pl.BlockSpec((1, S, dk), lambda b, h: (b, 0, 0)),
            pl.BlockSpec((1, C, C), lambda b, h: (h, 0, 0)),
            pl.BlockSpec((1, C, dv), lambda b, h: (h, 0, 0)),
            pl.BlockSpec((1, C, dk), lambda b, h: (h, 0, 0)),
            pl.BlockSpec((1, dk, dv), lambda b, h: (h, 0, 0)),
            pl.BlockSpec((1, dv), lambda b, h: (0, h)),
        ],
        out_specs=pl.BlockSpec((1, S, dv), lambda b, h: (b, 0, h)),
        out_shape=jax.ShapeDtypeStruct((B, S, H * dv), BF16),
        scratch_shapes=[pltpu.VMEM((dk, dv), F32)],
        compiler_params=_cparams(("arbitrary", "arbitrary")),
        name="retention",
    )(proj, proj, proj, proj, cc, ss, dm, xi_t, zeta_t, decay_t, gn_g.reshape(1, H * dv))


def _store_slabs(ref, val):
    tm = val.shape[0]
    for j in range(ROW_SLAB):
        ref[pl.ds(j, tm, stride=ROW_SLAB), :] = val[:, j * LANES:(j + 1) * LANES]


def _load_slab_chunks(ref, tm):
    return [ref[pl.ds(j, tm, stride=ROW_SLAB), :] for j in range(ROW_SLAB)]


def _top2_flags(vals):
    n = len(vals)
    rank = []
    for k in range(n):
        r = jnp.zeros_like(vals[0])
        for j in range(n):
            if j < k:
                r = r + (vals[j] >= vals[k]).astype(F32)
            elif j > k:
                r = r + (vals[j] > vals[k]).astype(F32)
        rank.append(r)
    is1 = [r == 0.0 for r in rank]
    is2 = [r == 1.0 for r in rank]
    m1 = sum(jnp.where(f, v, 0.0) for f, v in zip(is1, vals))
    m2 = sum(jnp.where(f, v, 0.0) for f, v in zip(is2, vals))
    return is1, is2, m1, m2


def _route(logits, lt_ref, bucket_ref, rank_ref, counts_ref, carry_ref):
    tm = logits.shape[1]
    rows = [logits[e:e + 1, :] for e in range(N_EXPERTS)]
    gmax = functools.reduce(jnp.maximum, rows)
    groups = []
    for g in range(N_GROUPS):
        vals = rows[g * EXPERTS_PER_GROUP:(g + 1) * EXPERTS_PER_GROUP]
        is1, is2, m1, m2 = _top2_flags(vals)
        score = jnp.exp(m1 - gmax) + jnp.exp(m2 - gmax)
        groups.append(([jnp.logical_or(a, b) for a, b in zip(is1, is2)], score))
    best = groups[0][1]
    gsel = jnp.zeros_like(best)
    for g in range(1, N_GROUPS):
        better = groups[g][1] > best
        best = jnp.where(better, groups[g][1], best)
        gsel = jnp.where(better, float(g), gsel)
    sel = [jnp.zeros_like(best) > 1.0] * EXPERTS_PER_GROUP
    for g in range(N_GROUPS):
        here = gsel == float(g)
        sel = [jnp.logical_or(s, jnp.logical_and(here, f)) for s, f in zip(sel, groups[g][0])]
    pair = jnp.where(sel[0],
                     jnp.where(sel[1], 0.0, jnp.where(sel[2], 1.0, 2.0)),
                     jnp.where(sel[1], jnp.where(sel[2], 3.0, 4.0), 5.0))
    bucket = (gsel * float(PAIRS_PER_GROUP) + pair).astype(jnp.int32)

    brow = lax.broadcasted_iota(jnp.int32, (BUCKET_ROWS, tm), 0)
    onehot = (brow == bucket).astype(F32)
    before = jnp.dot(onehot.astype(BF16), lt_ref[...], preferred_element_type=F32)
    carry = carry_ref[...]
    in_tile = jnp.sum(onehot * before, axis=0, keepdims=True)
    base = jnp.sum(onehot * carry[:, 0:1], axis=0, keepdims=True)
    rank_ref[...] = (base + in_tile).astype(jnp.int32)
    bucket_ref[...] = bucket
    new_carry = carry + jnp.sum(onehot, axis=1, keepdims=True)
    carry_ref[...] = new_carry
    counts_ref[...] = new_carry.astype(jnp.int32)


def _slab(ref, r):
    return ref.at[pl.ds(pl.multiple_of(r * ROW_SLAB, ROW_SLAB), ROW_SLAB)]


def _wait_rows(src_ref, dst_ref, sem, n):
    def body(c, carry):
        for _ in range(DMA_CHUNK):
            pltpu.make_async_copy(_slab(src_ref, 0), _slab(dst_ref, 0), sem).wait()
        return carry
    lax.fori_loop(0, n // DMA_CHUNK, body, 0)


def _scatter_kernel(dest_ref, pad_lo_ref, pad_hi_ref, x_ref, sc_ref, sh_ref, xs_ref, hbuf, sems):
    i = pl.program_id(0)
    last = pl.num_programs(0) - 1
    tm = x_ref.shape[0]
    h = x_ref[...] * (1.0 + sc_ref[0]) + sh_ref[0]

    for s in range(2):
        @pl.when(lax.rem(i, 2) == s)
        def _(s=s):
            buf = hbuf.at[s]
            _store_slabs(buf, h)

            def issue(c, carry):
                for u in range(DMA_CHUNK):
                    r = c * DMA_CHUNK + u
                    pltpu.make_async_copy(_slab(buf, r), _slab(xs_ref, dest_ref[i * tm + r]), sems.at[s]).start(priority=u % 2)
                return carry
            lax.fori_loop(0, tm // DMA_CHUNK, issue, 0)

            @pl.when(i > 0)
            def _():
                _wait_rows(hbuf.at[1 - s], xs_ref, sems.at[1 - s], tm)

            @pl.when(i == last)
            def _():
                _wait_rows(buf, xs_ref, sems.at[s], tm)

                def filler(row0, n_rows, start):
                    cp = pltpu.make_async_copy(
                        buf.at[pl.ds(0, n_rows * ROW_SLAB)],
                        xs_ref.at[pl.ds(pl.multiple_of(row0 * ROW_SLAB, ROW_SLAB), n_rows * ROW_SLAB)],
                        sems.at[s])
                    if start:
                        cp.start()
                    else:
                        cp.wait()

                def fill(start):
                    def bucket(b, carry):
                        pos = pad_lo_ref[b]
                        n = pad_hi_ref[b] - pos
                        for bit in reversed(range(MOE_TM.bit_length() - 1)):
                            take = lax.rem(lax.shift_right_logical(n, bit), 2)

                            @pl.when(take == 1)
                            def _(pos=pos, bit=bit):
                                filler(pos, 1 << bit, start)
                            pos = pos + take * (1 << bit)
                        return carry
                    lax.fori_loop(0, N_BUCKETS, bucket, 0)

                    def tail(t, carry):
                        filler(t * MOE_TM, MOE_TM, start)
                        return carry
                    n_tiles = xs_ref.shape[0] // (MOE_TM * ROW_SLAB)
                    lax.fori_loop(pad_hi_ref[N_BUCKETS - 1] // MOE_TM, n_tiles, tail, 0)
                fill(True)
                fill(False)


def _scatter_rows(x2, mods, k, B, S, plan, n_out):
    T, D = x2.shape
    tm = PROJ_TM
    per_b = S // tm
    return pl.pallas_call(
        _scatter_kernel,
        grid_spec=pltpu.PrefetchScalarGridSpec(
            num_scalar_prefetch=3,
            grid=(T // tm,),
            in_specs=[
                pl.BlockSpec((tm, D), lambda i, *_: (i, 0)),
                pl.BlockSpec((1, 1, D), lambda i, *_: (_mod_row(k, i // per_b, SCALE, B), 0, 0)),
                pl.BlockSpec((1, 1, D), lambda i, *_: (_mod_row(k, i // per_b, SHIFT, B), 0, 0)),
            ],
            out_specs=pl.BlockSpec(memory_space=pl.ANY),
            scratch_shapes=[pltpu.VMEM((2, tm * ROW_SLAB, LANES), F32), pltpu.SemaphoreType.DMA((2,))],
        ),
        out_shape=jax.ShapeDtypeStruct((n_out * ROW_SLAB, LANES), F32),
        compiler_params=_cparams(("arbitrary",)),
        name="scatter_rows",
    )(plan["dest"], plan["pad_lo"], plan["pad_hi"], x2, mods, mods)


def _expert_pair(x, wg_bf, wu_bf, wd_bf):
    gates = [jnp.dot(x, wg_bf[e], preferred_element_type=F32) for e in range(2)]
    ups = [jnp.dot(x, wu_bf[e], preferred_element_type=F32) for e in range(2)]
    outs = []
    for e in range(2):
        hid = gates[e] * jax.nn.sigmoid(gates[e]) * ups[e]
        outs.append(jnp.dot(hid.astype(BF16), wd_bf[e], preferred_element_type=F32))
    return outs


def _router_logit(chunks, rw_ref, rb_ref, e):
    w_row = rw_ref[pl.ds(e, 1), :]
    acc = chunks[0] * w_row[:, 0:LANES]
    for j in range(1, ROW_SLAB):
        acc = acc + chunks[j] * w_row[:, j * LANES:(j + 1) * LANES]
    return jnp.sum(acc, axis=-1, keepdims=True) + rb_ref[pl.ds(e, 1), 0:1]


def _moe_kernel(src_ref, ea_ref, eb_ref, valid_ref, na_ref, nb_ref,
                xs_ref, rw_ref, rb_ref, wg_hbm, wu_hbm, wd_hbm, o_ref,
                wg_st, wu_st, wd_st, wg_bf, wu_bf, wd_bf, sems, *, layer):
    i = pl.program_id(0)
    prev = jnp.maximum(i - 1, 0)

    def fetch(slot, e):
        return [pltpu.make_async_copy(wg_hbm.at[layer, e], wg_st.at[slot], sems.at[slot, 0]),
                pltpu.make_async_copy(wu_hbm.at[layer, e], wu_st.at[slot], sems.at[slot, 1]),
                pltpu.make_async_copy(wd_hbm.at[layer, e], wd_st.at[slot], sems.at[slot, 2])]

    for slot, e_ref, n_ref in ((0, ea_ref, na_ref), (1, eb_ref, nb_ref)):
        @pl.when(i == 0)
        def _(slot=slot, e_ref=e_ref):
            for cp in fetch(slot, e_ref[0]):
                cp.start()

        @pl.when(jnp.logical_or(i == 0, e_ref[i] != e_ref[prev]))
        def _(slot=slot, e_ref=e_ref, n_ref=n_ref):
            for cp in fetch(slot, e_ref[i]):
                cp.wait()
            wg_bf[slot] = wg_st[slot].astype(BF16)
            wu_bf[slot] = wu_st[slot].astype(BF16)
            wd_bf[slot] = wd_st[slot].astype(BF16)

            @pl.when(n_ref[i] >= 0)
            def _():
                for cp in fetch(slot, n_ref[i]):
                    cp.start()

    @pl.when(valid_ref[i] == 1)
    def _():
        chunks = _load_slab_chunks(xs_ref, MOE_TM)
        l_a = _router_logit(chunks, rw_ref, rb_ref, ea_ref[i])
        l_b = _router_logit(chunks, rw_ref, rb_ref, eb_ref[i])
        w_a = jax.nn.sigmoid(l_a - l_b)
        w_b = jax.nn.sigmoid(l_b - l_a)
        x = jnp.concatenate([c.astype(BF16) for c in chunks], axis=-1)
        y_a, y_b = _expert_pair(x, wg_bf, wu_bf, wd_bf)
        y = w_a * y_a + w_b * y_b
        _store_slabs(o_ref, y)

    @pl.when(valid_ref[i] == 0)
    def _():
        o_ref[...] = jnp.zeros_like(o_ref)


def _grouped_experts(xs, plan, router_wT, router_bt, w_gate, w_up, w_down, layer):
    n_tiles = xs.shape[0] // (MOE_TM * ROW_SLAB)
    D, Fd = D_MODEL, D_EXPERT
    row_spec = lambda index: pl.BlockSpec((MOE_TM * ROW_SLAB, LANES), index)
    hbm = pl.BlockSpec(memory_space=pl.ANY)
    return pl.pallas_call(
        functools.partial(_moe_kernel, layer=layer),
        grid_spec=pltpu.PrefetchScalarGridSpec(
            num_scalar_prefetch=6,
            grid=(n_tiles,),
            in_specs=[
                row_spec(lambda i, src, *_: (src[i], 0)),
                pl.BlockSpec((N_EXPERTS, D), lambda i, *_: (0, 0)),
                pl.BlockSpec((N_EXPERTS, LANES), lambda i, *_: (0, 0)),
                hbm, hbm, hbm,
            ],
            out_specs=row_spec(lambda i, *_: (i, 0)),
            scratch_shapes=[
                pltpu.VMEM((2, D, Fd), F32), pltpu.VMEM((2, D, Fd), F32), pltpu.VMEM((2, Fd, D), F32),
                pltpu.VMEM((2, D, Fd), BF16), pltpu.VMEM((2, D, Fd), BF16), pltpu.VMEM((2, Fd, D), BF16),
                pltpu.SemaphoreType.DMA((2, 3)),
            ],
        ),
        out_shape=jax.ShapeDtypeStruct(xs.shape, F32),
        compiler_params=_cparams(("arbitrary",)),
        name="grouped_experts",
    )(plan["tile_src"], plan["tile_ea"], plan["tile_eb"], plan["tile_valid"], plan["next_ea"], plan["next_eb"],
      xs, router_wT, router_bt, w_gate, w_up, w_down)


def _next_change(e):
    n = e.shape[0]
    idx = jnp.arange(n, dtype=jnp.int32)
    change = jnp.concatenate([jnp.ones((1,), jnp.bool_), e[1:] != e[:-1]])
    at = jnp.where(change, idx, n)
    nxt = lax.cummin(at[::-1])[::-1]
    nxt = jnp.concatenate([nxt[1:], jnp.full((1,), n, jnp.int32)])
    return jnp.where(nxt < n, jnp.take(e, jnp.minimum(nxt, n - 1)), -1).astype(jnp.int32)


def _moe_plan(bucket, rank, counts, n_tiles):
    cnt = counts[:N_BUCKETS, 0]
    tiles_b = (cnt + MOE_TM - 1) // MOE_TM
    tile_end = jnp.cumsum(tiles_b)
    tile_start = tile_end - tiles_b
    row_start = tile_start * MOE_TM
    dest = jnp.take(row_start, bucket[0]) + rank[0]
    n_used = tile_end[-1]
    t = jnp.arange(n_tiles, dtype=jnp.int32)
    t_c = jnp.minimum(t, n_used - 1)
    tb = jnp.sum((t_c[:, None] >= tile_end[None, :]).astype(jnp.int32), axis=1)
    pair_lo = jnp.array([0, 0, 0, 1, 1, 2], jnp.int32)
    pair_hi = jnp.array([1, 2, 3, 2, 3, 3], jnp.int32)
    grp = tb // PAIRS_PER_GROUP
    pr = tb % PAIRS_PER_GROUP
    tile_ea = (grp * EXPERTS_PER_GROUP + jnp.take(pair_lo, pr)).astype(jnp.int32)
    tile_eb = (grp * EXPERTS_PER_GROUP + jnp.take(pair_hi, pr)).astype(jnp.int32)
    return {
        "dest": dest.astype(jnp.int32),
        "pad_lo": (row_start + cnt).astype(jnp.int32),
        "pad_hi": (tile_end * MOE_TM).astype(jnp.int32),
        "tile_src": t_c.astype(jnp.int32),
        "tile_ea": tile_ea,
        "tile_eb": tile_eb,
        "next_ea": _next_change(tile_ea),
        "next_eb": _next_change(tile_eb),
        "tile_valid": (t < n_used).astype(jnp.int32),
    }


def _resln_kernel(dest_ref, ys_ref, x_ref, gate_ref, g_ref, b_ref, o_ref, ybuf, sems):
    i = pl.program_id(0)
    n = pl.num_programs(0)
    tm = x_ref.shape[0]

    def issue(tile, s):
        def body(c, carry):
            for u in range(DMA_CHUNK):
                r = c * DMA_CHUNK + u
                pltpu.make_async_copy(_slab(ys_ref, dest_ref[tile * tm + r]), _slab(ybuf.at[s], r), sems.at[s]).start(priority=u % 2)
            return carry
        lax.fori_loop(0, tm // DMA_CHUNK, body, 0)

    @pl.when(i == 0)
    def _():
        issue(0, 0)

    for s in range(2):
        @pl.when(lax.rem(i, 2) == s)
        def _(s=s):
            @pl.when(i + 1 < n)
            def _():
                issue(i + 1, 1 - s)

            _wait_rows(ys_ref, ybuf.at[s], sems.at[s], tm)
            y = jnp.concatenate(_load_slab_chunks(ybuf.at[s], tm), axis=-1)
            u = DEEPNORM_ALPHA * x_ref[...] + (1.0 + gate_ref[0]) * y
            o_ref[...] = _layer_norm_rows(u, g_ref[...], b_ref[...])


def _gather_residual_ln(ys, plan, x2, mods, k, B, S, g, b):
    T, D = x2.shape
    tm = PROJ_TM
    per_b = S // tm
    return pl.pallas_call(
        _resln_kernel,
        grid_spec=pltpu.PrefetchScalarGridSpec(
            num_scalar_prefetch=1,
            grid=(T // tm,),
            in_specs=[
                pl.BlockSpec(memory_space=pl.ANY),
                pl.BlockSpec((tm, D), lambda i, *_: (i, 0)),
                pl.BlockSpec((1, 1, D), lambda i, *_: (_mod_row(k, i // per_b, GATE, B), 0, 0)),
                pl.BlockSpec((1, D), lambda i, *_: (0, 0)),
                pl.BlockSpec((1, D), lambda i, *_: (0, 0)),
            ],
            out_specs=pl.BlockSpec((tm, D), lambda i, *_: (i, 0)),
            scratch_shapes=[pltpu.VMEM((2, tm * ROW_SLAB, LANES), F32), pltpu.SemaphoreType.DMA((2,))],
        ),
        out_shape=jax.ShapeDtypeStruct((T, D), F32),
        compiler_params=_cparams(("arbitrary",)),
        name="gather_residual_ln",
    )(plan["dest"], ys, x2, mods, g.reshape(1, D), b.reshape(1, D))


def _moe_sublayer(x, routing, mods, k, router_wT, router_bt, w_gate, w_up, w_down, layer, g, b):
    B, S, D = x.shape
    T = B * S
    x2 = x.reshape(T, D)
    bucket, rank, counts = routing
    n_tiles = T // MOE_TM + N_BUCKETS
    plan = _moe_plan(bucket, rank, counts, n_tiles)
    xs = _scatter_rows(x2, mods, k, B, S, plan, n_tiles * MOE_TM)
    ys = _grouped_experts(xs, plan, router_wT, router_bt, w_gate, w_up, w_down, layer)
    return _gather_residual_ln(ys, plan, x2, mods, k, B, S, g, b).reshape(B, S, D)


def kernel(x, c, positions, ada_w, ada_b, ln_g, ln_b, sb_w_in, sb_w_out, ret_w_in, ret_gn_g, ret_w_out,
           router_w, router_b, moe_w_gate, moe_w_up, moe_w_down):
    mods = _ada_params(c, ada_w, ada_b)
    cc, ss = _rope_tables(positions)
    tables = _retention_decay_tables()
    u = (jnp.arange(SB_TK)[:, None] < jnp.arange(SB_TK)[None, :]).astype(BF16)
    lt = (jnp.arange(PROJ_TM)[:, None] < jnp.arange(PROJ_TM)[None, :]).astype(BF16)
    router_wT = router_w.T
    router_bt = jnp.broadcast_to(router_b[:, None], (N_EXPERTS, LANES))
    for i in range(DEPTH):
        k_mix, k_moe = 2 * i, 2 * i + 1
        if i % 2 == 0:
            qkv = _mod_matmul(x, mods, k_mix, sb_w_in, i // 2, tn=3 * D_MODEL // 2)
            a = _sb_attention(qkv, u)
            w_out = sb_w_out
        else:
            proj = _mod_matmul(x, mods, k_mix, ret_w_in, i // 2, tn=2 * D_MODEL)
            a = _retention(proj, cc, ss, tables, ret_gn_g[i // 2])
            w_out = ret_w_out
        x, *routing = _out_proj_ln(a, w_out, i // 2, x, mods, k_mix, ln_g[i, 0], ln_b[i, 0],
                                   k_moe, router_wT, router_bt, lt)
        x = _moe_sublayer(x, routing, mods, k_moe, router_wT, router_bt, moe_w_gate, moe_w_up, moe_w_down, i,
                          ln_g[i, 1], ln_b[i, 1])
    return x
```

```python
import functools

import jax
import jax.numpy as jnp
from jax import lax
from jax.experimental import pallas as pl
from jax.experimental.pallas import tpu as pltpu

F32 = jnp.float32
BF16 = jnp.bfloat16

D_MODEL = 1024
DEPTH = 4
SB_HEADS = 16
SB_HEAD_DIM = D_MODEL // SB_HEADS
RET_HEADS = 8
RET_QK_DIM = D_MODEL // RET_HEADS
RET_V_DIM = 2 * RET_QK_DIM
ROPE_BASE = 10000.0
N_EXPERTS = 16
N_GROUPS = 4
EXPERTS_PER_GROUP = N_EXPERTS // N_GROUPS
D_EXPERT = D_MODEL // 2
DEEPNORM_ALPHA = (2 * DEPTH) ** 0.25
LN_EPS = 1e-5
LOG2_E = 1.4426950408889634

LANES = 128
SUBLANES = 8
MXU_DIM = 256
VMEM_LIMIT_BYTES = 56 * 1024 * 1024

PROJ_TM = 512
MM_TM = 1024
OUT_SPLIT = 4
SB_TQ = MXU_DIM
SB_TK = MXU_DIM
RET_CHUNK = 256
RET_GROUP = 8
MOE_TM = 256
PAIRS_PER_GROUP = 6
N_BUCKETS = N_GROUPS * PAIRS_PER_GROUP
BUCKET_ROWS = 32
ROW_SLAB = D_MODEL // LANES
DMA_CHUNK = 16
assert ROW_SLAB == SUBLANES


def _cparams(semantics):
    return pltpu.CompilerParams(dimension_semantics=semantics, vmem_limit_bytes=VMEM_LIMIT_BYTES)


def _split_bf16(a):
    hi = a.astype(BF16)
    lo = (a - hi.astype(F32)).astype(BF16)
    return hi, lo


def _ada_kernel(c_ref, w_ref, b_ref, o_ref):
    c = c_ref[...]
    s = c * jax.nn.sigmoid(c)
    sh, sl = _split_bf16(s)
    wh, wl = _split_bf16(w_ref[0])
    acc = jnp.dot(sh, wh, preferred_element_type=F32)
    acc += jnp.dot(sl, wh, preferred_element_type=F32)
    acc += jnp.dot(sh, wl, preferred_element_type=F32)
    o_ref[0] = acc + b_ref[0]


def _ada_params(c, ada_w, ada_b):
    B, D = c.shape
    n_sub = ada_w.shape[0] * ada_w.shape[1]
    N = ada_w.shape[-1]
    tn = N // 2
    w = ada_w.reshape(n_sub, D, N)
    b = ada_b.reshape(n_sub, 1, N)
    out = pl.pallas_call(
        _ada_kernel,
        grid=(n_sub, N // tn),
        in_specs=[
            pl.BlockSpec((B, D), lambda k, j: (0, 0)),
            pl.BlockSpec((1, D, tn), lambda k, j: (k, 0, j)),
            pl.BlockSpec((1, 1, tn), lambda k, j: (k, 0, j)),
        ],
        out_specs=pl.BlockSpec((1, B, tn), lambda k, j: (k, 0, j)),
        out_shape=jax.ShapeDtypeStruct((n_sub, B, N), F32),
        compiler_params=_cparams(("arbitrary", "arbitrary")),
        name="ada_params",
    )(c, w, b)
    return out.reshape(n_sub * B * 3, 1, D)


def _mod_row(k, b, which, B):
    return (k * B + b) * 3 + which


SHIFT, SCALE, GATE = 0, 1, 2


def _modmm_kernel(x_ref, sc_ref, sh_ref, w_ref, o_ref, w_bf):
    @pl.when(jnp.logical_and(pl.program_id(1) == 0, pl.program_id(2) == 0))
    def _():
        w_bf[...] = w_ref[0].astype(BF16)

    h = x_ref[0] * (1.0 + sc_ref[0]) + sh_ref[0]
    o_ref[0] = jnp.dot(h.astype(BF16), w_bf[...], preferred_element_type=F32).astype(o_ref.dtype)


def _mod_matmul(x, mods, k, w_all, layer, tn):
    B, S, D = x.shape
    N = w_all.shape[2]
    tm = MM_TM
    return pl.pallas_call(
        _modmm_kernel,
        grid=(N // tn, B, S // tm),
        in_specs=[
            pl.BlockSpec((1, tm, D), lambda n, b, i: (b, i, 0)),
            pl.BlockSpec((1, 1, D), lambda n, b, i: (_mod_row(k, b, SCALE, B), 0, 0)),
            pl.BlockSpec((1, 1, D), lambda n, b, i: (_mod_row(k, b, SHIFT, B), 0, 0)),
            pl.BlockSpec((1, D, tn), lambda n, b, i: (layer, 0, n)),
        ],
        out_specs=pl.BlockSpec((1, tm, tn), lambda n, b, i: (b, i, n)),
        out_shape=jax.ShapeDtypeStruct((B, S, N), BF16),
        scratch_shapes=[pltpu.VMEM((D, tn), BF16)],
        compiler_params=_cparams(("arbitrary", "arbitrary", "arbitrary")),
        name="mod_matmul",
    )(x, mods, mods, w_all)


SB_PIPE = 4
SB_UNROLL = 4


def _sb_tile_tables(nq):
    near = []
    for i in range(nq):
        near += [(i, i), (i, i - 1) if i else (nq, 0)]
    far = [(i, i - o) for o in range(2, nq) for i in range(o, nq)]
    ti, tj, spans = [], [], []
    for tiles in (near, far):
        padded = [(nq, 0)] * SB_PIPE + tiles + [(nq, 0)] * (SB_PIPE + SB_UNROLL)
        spans.append((len(ti), len(tiles)))
        ti += [t[0] for t in padded]
        tj += [t[1] for t in padded]
    return ti, tj, spans


def _sb_kernel(ti_ref, tj_ref, q_ref, k_ref, v_ref, u_ref, o_ref,
               vT_ref, qT_ref, z_ref, sp_ref, d_ref, a_ref, c_ref, run_ref, acc_ref, alive_ref, n_alive_ref, *, spans):
    S = q_ref.shape[1]
    tq, tk, dh = SB_TQ, SB_TK, SB_HEAD_DIM
    nq = S // tq

    for c in range(S // tk):
        vT_ref[c] = v_ref[0, c * tk:(c + 1) * tk, :].T

    scale = dh ** -0.5 * LOG2_E
    head_rows = lax.broadcasted_iota(jnp.int32, (2 * dh, tq), 0)
    for i in range(nq):
        qT = (q_ref[0, i * tq:(i + 1) * tq, :].astype(F32) * scale).astype(BF16).T
        qT_ref[i, 0] = jnp.where(head_rows < dh, qT, jnp.zeros_like(qT))
        qT_ref[i, 1] = jnp.where(head_rows >= dh, qT, jnp.zeros_like(qT))
    qT_ref[nq] = jnp.zeros_like(qT_ref[nq])
    z_ref[...] = jnp.zeros_like(z_ref)
    sp_ref[...] = jnp.zeros_like(sp_ref)
    d_ref[...] = jnp.zeros_like(d_ref)
    a_ref[...] = jnp.zeros_like(a_ref)
    c_ref[...] = jnp.zeros_like(c_ref)
    for i in range(nq + 1):
        alive_ref[i] = jnp.int32(1)
    n_alive_ref[0] = jnp.int32(nq)
    run_ref[nq] = jnp.zeros_like(run_ref[nq])
    acc_ref[nq] = jnp.zeros_like(acc_ref[nq])

    def iteration(t1, t3, t4, p, diag2, diag3, diag4):
        q = 1 - p
        i1, j1 = ti_ref[t1], tj_ref[t1]
        i3 = ti_ref[t3]
        i4, j4 = ti_ref[t4], tj_ref[t4]
        if diag2 or diag3:
            rows = lax.broadcasted_iota(jnp.int32, (tk, tq), 0)
            cols = lax.broadcasted_iota(jnp.int32, (tk, tq), 1)
            valid = rows < cols
        u = u_ref[...]
        suf = [jnp.dot(u, sp_ref[q, h], preferred_element_type=F32) for h in range(2)]
        vT = vT_ref[j4]
        pv = [jnp.dot(vT[h * dh:(h + 1) * dh, :], a_ref[p, h], preferred_element_type=F32) for h in range(2)]
        k_blk = k_ref[0, pl.ds(pl.multiple_of(j1 * tk, tk), tk), :]
        z_new = [jnp.dot(k_blk, qT_ref[i1, h], preferred_element_type=F32) for h in range(2)]
        for h in range(2):
            z = z_ref[p, h]
            sp = jnp.maximum(z, 0.0) + jnp.log2(1.0 + jnp.exp2(-jnp.abs(z)))
            d_ref[p, h] = z - sp
            if diag2:
                sp = jnp.where(valid, sp, 0.0)
            sp_ref[p, h] = sp.astype(BF16)
        for h in range(2):
            a = jnp.exp2(d_ref[q, h] - suf[h])
            if diag3:
                a = jnp.where(valid, a, 0.0)
            a_ref[q, h] = a.astype(BF16)
            tile_mass = suf[h][0:1, :] + sp_ref[q, h, 0:1, :].astype(F32)
            if diag3:
                run_ref[i3, h, 0:1, :] = tile_mass
            else:
                run = run_ref[i3, h, 0:1, :]
                c_ref[q, h, 0:1, :] = jnp.exp2(-run)
                run_ref[i3, h, 0:1, :] = run + tile_mass
                c_next = jnp.exp2(-(run + tile_mass))
                c_max = c_next if h == 0 else jnp.maximum(c_max, c_next)
        if not diag3:
            live = jnp.logical_or(jnp.max(c_max) > 0.0, i3 == nq).astype(jnp.int32)
            n_alive_ref[0] += live - alive_ref[i3]
            alive_ref[i3] = live
        for h in range(2):
            if diag4:
                acc_ref[i4, h] = pv[h]
            else:
                acc_ref[i4, h] += pv[h] * c_ref[p, h, 0:1, :]
            z_ref[p, h] = z_new[h]

    base, n_real = spans[0]
    n_iter = n_real + SB_PIPE
    assert base % 2 == 0 and SB_PIPE % 2 == 0 and n_iter % SB_UNROLL == 0

    def near_body(t, carry):
        for r in range(SB_UNROLL):
            t1 = base + SB_PIPE + SB_UNROLL * t + r
            even = r % 2 == 0
            iteration(t1, t1 - 3, t1 - 4, r % 2, even, not even, even)
        return carry
    lax.fori_loop(0, n_iter // SB_UNROLL, near_body, 0)

    base, n_real = spans[1]
    first = base + SB_PIPE
    end = first + n_real

    def next_live(t):
        return lax.while_loop(lambda u: alive_ref[ti_ref[u]] == 0, lambda u: u + 1, t)

    def far_body(carry):
        ptr, h1, h2, h3, h4 = carry
        picks = []
        for r in range(SB_UNROLL):
            t1 = next_live(ptr)
            picks.append(t1)
            ptr = t1 + 1
        for r, t1 in enumerate(picks):
            iteration(t1, h3, h4, r % 2, False, False, False)
            h1, h2, h3, h4 = t1, h1, h2, h3
        return ptr, h1, h2, h3, h4

    start = jnp.where(n_alive_ref[0] == 0, end + SB_PIPE, first).astype(jnp.int32)
    lax.while_loop(lambda carry: carry[0] < end + SB_PIPE, far_body,
                   (start, jnp.int32(first - 1), jnp.int32(first - 2), jnp.int32(first - 3), jnp.int32(first - 4)))

    for i in range(nq):
        oT = jnp.concatenate([acc_ref[i, 0], acc_ref[i, 1]], axis=0)
        o_ref[0, i * tq:(i + 1) * tq, :] = oT.astype(o_ref.dtype).T


def _sb_attention(qkv, u):
    B, S, _ = qkv.shape
    n_pairs = SB_HEADS // 2
    nq = S // SB_TQ
    blk = (1, S, 2 * SB_HEAD_DIM)
    ti, tj, spans = _sb_tile_tables(nq)
    tile = (SB_TK, SB_TQ)
    return pl.pallas_call(
        functools.partial(_sb_kernel, spans=spans),
        grid_spec=pltpu.PrefetchScalarGridSpec(
            num_scalar_prefetch=2,
            grid=(B, n_pairs),
            in_specs=[
                pl.BlockSpec(blk, lambda b, p, *_: (b, 0, p)),
                pl.BlockSpec(blk, lambda b, p, *_: (b, 0, n_pairs + p)),
                pl.BlockSpec(blk, lambda b, p, *_: (b, 0, 2 * n_pairs + p)),
                pl.BlockSpec((SB_TK, SB_TK), lambda b, p, *_: (0, 0)),
            ],
            out_specs=pl.BlockSpec(blk, lambda b, p, *_: (b, 0, p)),
            scratch_shapes=[
                pltpu.VMEM((S // SB_TK, 2 * SB_HEAD_DIM, SB_TK), BF16),
                pltpu.VMEM((nq + 1, 2, 2 * SB_HEAD_DIM, SB_TQ), BF16),
                pltpu.VMEM((2, 2) + tile, F32),
                pltpu.VMEM((2, 2) + tile, BF16),
                pltpu.VMEM((2, 2) + tile, F32),
                pltpu.VMEM((2, 2) + tile, BF16),
                pltpu.VMEM((2, 2, SUBLANES, SB_TQ), F32),
                pltpu.VMEM((nq + 1, 2, SUBLANES, SB_TQ), F32),
                pltpu.VMEM((nq + 1, 2, SB_HEAD_DIM, SB_TQ), F32),
                pltpu.SMEM((nq + 1,), jnp.int32),
                pltpu.SMEM((1,), jnp.int32),
            ],
        ),
        out_shape=jax.ShapeDtypeStruct((B, S, D_MODEL), BF16),
        compiler_params=_cparams(("arbitrary", "arbitrary")),
        name="sb_attention",
    )(jnp.asarray(ti, jnp.int32), jnp.asarray(tj, jnp.int32), qkv, qkv, qkv, u)


def _layer_norm_rows(u, g, b):
    mu = jnp.mean(u, axis=-1, keepdims=True)
    d = u - mu
    var = jnp.mean(d * d, axis=-1, keepdims=True)
    return d * lax.rsqrt(var + LN_EPS) * g + b


def _outln_kernel(a_ref, w_ref, x_ref, gate_ref, g_ref, b_ref, sc2_ref, sh2_ref, rw_ref, rb_ref, lt_ref,
                  o_ref, bucket_ref, rank_ref, counts_ref, w_bf, carry_ref):
    @pl.when(jnp.logical_and(pl.program_id(0) == 0, pl.program_id(1) == 0))
    def _():
        w_bf[...] = w_ref[0].astype(BF16)
        carry_ref[...] = jnp.zeros_like(carry_ref)

    tm = a_ref.shape[1]
    sub = tm // OUT_SPLIT
    nt = (((1,), (1,)), ((), ()))
    rw = rw_ref[...].astype(BF16)
    ys = [jnp.dot(a_ref[0, r * sub:(r + 1) * sub, :], w_bf[...], preferred_element_type=F32) for r in range(OUT_SPLIT)]
    logits = []
    for r in range(OUT_SPLIT):
        rows = slice(r * sub, (r + 1) * sub)
        u = DEEPNORM_ALPHA * x_ref[0, rows, :] + (1.0 + gate_ref[0]) * ys[r]
        xn = _layer_norm_rows(u, g_ref[...], b_ref[...])
        o_ref[0, rows, :] = xn
        h = xn * (1.0 + sc2_ref[0]) + sh2_ref[0]
        logits.append(lax.dot_general(rw, h.astype(BF16), nt, preferred_element_type=F32))
    logits = jnp.concatenate(logits, axis=1) + rb_ref[:, 0:1]
    _route(logits, lt_ref, bucket_ref, rank_ref, counts_ref, carry_ref)


def _out_proj_ln(a, w_all, layer, x, mods, k, g, b, k_moe, router_wT, router_bt, lt):
    B, S, D = x.shape
    K = a.shape[-1]
    tm = PROJ_TM
    per_b = S // tm
    T = B * S
    mod = lambda kk, which: pl.BlockSpec((1, 1, D), lambda bb, i: (_mod_row(kk, bb, which, B), 0, 0))
    const = lambda shape: pl.BlockSpec(shape, lambda bb, i: (0,) * len(shape))
    return pl.pallas_call(
        _outln_kernel,
        grid=(B, per_b),
        in_specs=[
            pl.BlockSpec((1, tm, K), lambda bb, i: (bb, i, 0)),
            pl.BlockSpec((1, K, D), lambda bb, i: (layer, 0, 0)),
            pl.BlockSpec((1, tm, D), lambda bb, i: (bb, i, 0)),
            mod(k, GATE), const((1, D)), const((1, D)),
            mod(k_moe, SCALE), mod(k_moe, SHIFT),
            const((N_EXPERTS, D)), const((N_EXPERTS, LANES)), const((tm, tm)),
        ],
        out_specs=[
            pl.BlockSpec((1, tm, D), lambda bb, i: (bb, i, 0)),
            pl.BlockSpec((1, tm), lambda bb, i: (0, bb * per_b + i)),
            pl.BlockSpec((1, tm), lambda bb, i: (0, bb * per_b + i)),
            const((BUCKET_ROWS, LANES)),
        ],
        out_shape=[
            jax.ShapeDtypeStruct((B, S, D), F32),
            jax.ShapeDtypeStruct((1, T), jnp.int32),
            jax.ShapeDtypeStruct((1, T), jnp.int32),
            jax.ShapeDtypeStruct((BUCKET_ROWS, LANES), jnp.int32),
        ],
        scratch_shapes=[pltpu.VMEM((K, D), BF16), pltpu.VMEM((BUCKET_ROWS, LANES), F32)],
        compiler_params=_cparams(("arbitrary", "arbitrary")),
        name="out_proj_ln",
    )(a, w_all, x, mods, g.reshape(1, D), b.reshape(1, D), mods, mods, router_wT, router_bt, lt)


def _rope_kernel(pos_ref, invf_ref, sign_ref, cc_ref, ss_ref):
    ang = pos_ref[0].astype(F32) * invf_ref[...]
    cc_ref[0] = jnp.cos(ang)
    ss_ref[0] = jnp.sin(ang) * sign_ref[...]


def _rope_tables(positions):
    B, S = positions.shape
    half = RET_QK_DIM // 2
    inv_freq = 1.0 / (ROPE_BASE ** (jnp.arange(0, RET_QK_DIM, 2, dtype=F32) / RET_QK_DIM))
    invf2 = jnp.concatenate([inv_freq, inv_freq]).reshape(1, RET_QK_DIM)
    sign = jnp.concatenate([-jnp.ones((half,), F32), jnp.ones((half,), F32)]).reshape(1, RET_QK_DIM)
    ts = PROJ_TM
    shp = jax.ShapeDtypeStruct((B, S, RET_QK_DIM), F32)
    return pl.pallas_call(
        _rope_kernel,
        grid=(B, S // ts),
        in_specs=[
            pl.BlockSpec((1, ts, 1), lambda b, i: (b, i, 0)),
            pl.BlockSpec((1, RET_QK_DIM), lambda b, i: (0, 0)),
            pl.BlockSpec((1, RET_QK_DIM), lambda b, i: (0, 0)),
        ],
        out_specs=[pl.BlockSpec((1, ts, RET_QK_DIM), lambda b, i: (b, i, 0))] * 2,
        out_shape=[shp, shp],
        compiler_params=_cparams(("arbitrary", "arbitrary")),
        name="rope_tables",
    )(positions.reshape(B, S, 1), invf2, sign)


def _retention_decay_tables():
    C = RET_CHUNK
    kscale = RET_QK_DIM ** -0.5
    log_gamma = jnp.log(1.0 - jnp.exp2(-5.0 - jnp.arange(RET_HEADS, dtype=F32)))
    n = jnp.arange(C, dtype=F32)
    diff = n[:, None] - n[None, :]
    dmask = jnp.where(diff >= 0, jnp.exp(jnp.maximum(diff, 0.0) * log_gamma[:, None, None]), 0.0)
    xi = jnp.exp((n + 1.0) * log_gamma[:, None])
    zeta = jnp.exp((C - 1.0 - n) * log_gamma[:, None])
    chunk_decay = jnp.exp(C * log_gamma)
    dm = dmask * kscale
    xi_t = jnp.broadcast_to(xi[:, :, None], (RET_HEADS, C, RET_V_DIM))
    zeta_t = jnp.broadcast_to(zeta[:, :, None] * kscale, (RET_HEADS, C, RET_QK_DIM))
    decay_t = jnp.broadcast_to(chunk_decay[:, None, None], (RET_HEADS, RET_QK_DIM, RET_V_DIM))
    return dm, xi_t, zeta_t, decay_t


def _ret_kernel(q_ref, k_ref, v_ref, g_ref, cc_ref, ss_ref, dm_ref, xi_ref, zeta_ref, decay_ref,
                gn_ref, o_ref, state_ref):
    S = q_ref.shape[1]
    C = RET_CHUNK
    half = RET_QK_DIM // 2
    state_ref[...] = jnp.zeros_like(state_ref)

    def prep(c):
        r0 = pl.multiple_of(c * C, C)
        rows = pl.ds(r0, C)
        cc = cc_ref[0, rows, :]
        ss = ss_ref[0, rows, :]
        q = q_ref[0, rows, :].astype(F32)
        k = k_ref[0, rows, :].astype(F32)
        qr = q * cc + pltpu.roll(q, half, 1) * ss
        kr = k * cc + pltpu.roll(k, half, 1) * ss
        kzT = (kr * zeta_ref[0]).T.astype(BF16)
        return rows, qr.astype(BF16), kr.astype(BF16), v_ref[0, rows, :], kzT

    def finish(rows, inner, cross):
        o = inner + cross * xi_ref[0]
        mu = jnp.mean(o, axis=-1, keepdims=True)
        d = o - mu
        var = jnp.mean(d * d, axis=-1, keepdims=True)
        on = d * lax.rsqrt(var + LN_EPS) * gn_ref[...]
        g = g_ref[0, rows, :].astype(F32)
        o_ref[0, rows, :] = (g * jax.nn.sigmoid(g) * on).astype(o_ref.dtype)

    def chunk_group(t, carry):
        nt = (((1,), (1,)), ((), ()))
        ps = [prep(RET_GROUP * t + r) for r in range(RET_GROUP)]
        qks = [lax.dot_general(p[1], p[2], nt, preferred_element_type=F32) for p in ps]
        dss = [jnp.dot(p[4], p[3], preferred_element_type=F32) for p in ps]
        state = state_ref[...]
        crosses = []
        for p, ds in zip(ps, dss):
            crosses.append(jnp.dot(p[1], state.astype(BF16), preferred_element_type=F32))
            state = state * decay_ref[0] + ds
        state_ref[...] = state
        inners = [jnp.dot((qk * dm_ref[0]).astype(BF16), p[3], preferred_element_type=F32) for qk, p in zip(qks, ps)]
        for p, inner, cross in zip(ps, inners, crosses):
            finish(p[0], inner, cross)
        return carry

    lax.fori_loop(0, S // (RET_GROUP * C), chunk_group, 0)


def _retention(proj, cc, ss, tables, gn_g):
    B, S, _ = proj.shape
    dk, dv, H, C = RET_QK_DIM, RET_V_DIM, RET_HEADS, RET_CHUNK
    dm, xi_t, zeta_t, decay_t = tables
    return pl.pallas_call(
        _ret_kernel,
        grid=(B, H),
        in_specs=[
            pl.BlockSpec((1, S, dk), lambda b, h: (b, 0, h)),
            pl.BlockSpec((1, S, dk), lambda b, h: (b, 0, H + h)),
            pl.BlockSpec((1, S, dv), lambda b, h: (b, 0, H + h)),
            pl.BlockSpec((1, S, dv), lambda b, h: (b, 0, 2 * H + h)),
            pl.BlockSpec((1, S, dk), lambda b, h: (b, 0, 0)),
            pl.BlockSpec((1, S, dk), lambda b, h: (b, 0, 0)),
            pl.BlockSpec((1, C, C), lambda b, h: (h, 0, 0)),
            pl.BlockSpec((1, C, dv), lambda b, h: (h, 0, 0)),
            pl.BlockSpec((1, C, dk), lambda b, h: (h, 0, 0)),
            pl.BlockSpec((1, dk, dv), lambda b, h: (h, 0, 0)),
            pl.BlockSpec((1, dv), lambda b, h: (0, h)),
        ],
        out_specs=pl.BlockSpec((1, S, dv), lambda b, h: (b, 0, h)),
        out_shape=jax.ShapeDtypeStruct((B, S, H * dv), BF16),
        scratch_shapes=[pltpu.VMEM((dk, dv), F32)],
        compiler_params=_cparams(("arbitrary", "arbitrary")),
        name="retention",
    )(proj, proj, proj, proj, cc, ss, dm, xi_t, zeta_t, decay_t, gn_g.reshape(1, H * dv))


def _store_slabs(ref, val):
    tm = val.shape[0]
    for j in range(ROW_SLAB):
        ref[pl.ds(j, tm, stride=ROW_SLAB), :] = val[:, j * LANES:(j + 1) * LANES]


def _load_slab_chunks(ref, tm):
    return [ref[pl.ds(j, tm, stride=ROW_SLAB), :] for j in range(ROW_SLAB)]


def _top2_flags(vals):
    n = len(vals)
    rank = []
    for k in range(n):
        r = jnp.zeros_like(vals[0])
        for j in range(n):
            if j < k:
                r = r + (vals[j] >= vals[k]).astype(F32)
            elif j > k:
                r = r + (vals[j] > vals[k]).astype(F32)
        rank.append(r)
    is1 = [r == 0.0 for r in rank]
    is2 = [r == 1.0 for r in rank]
    m1 = sum(jnp.where(f, v, 0.0) for f, v in zip(is1, vals))
    m2 = sum(jnp.where(f, v, 0.0) for f, v in zip(is2, vals))
    return is1, is2, m1, m2


def _route(logits, lt_ref, bucket_ref, rank_ref, counts_ref, carry_ref):
    tm = logits.shape[1]
    rows = [logits[e:e + 1, :] for e in range(N_EXPERTS)]
    gmax = functools.reduce(jnp.maximum, rows)
    groups = []
    for g in range(N_GROUPS):
        vals = rows[g * EXPERTS_PER_GROUP:(g + 1) * EXPERTS_PER_GROUP]
        is1, is2, m1, m2 = _top2_flags(vals)
        score = jnp.exp(m1 - gmax) + jnp.exp(m2 - gmax)
        groups.append(([jnp.logical_or(a, b) for a, b in zip(is1, is2)], score))
    best = groups[0][1]
    gsel = jnp.zeros_like(best)
    for g in range(1, N_GROUPS):
        better = groups[g][1] > best
        best = jnp.where(better, groups[g][1], best)
        gsel = jnp.where(better, float(g), gsel)
    sel = [jnp.zeros_like(best) > 1.0] * EXPERTS_PER_GROUP
    for g in range(N_GROUPS):
        here = gsel == float(g)
        sel = [jnp.logical_or(s, jnp.logical_and(here, f)) for s, f in zip(sel, groups[g][0])]
    pair = jnp.where(sel[0],
                     jnp.where(sel[1], 0.0, jnp.where(sel[2], 1.0, 2.0)),
                     jnp.where(sel[1], jnp.where(sel[2], 3.0, 4.0), 5.0))
    bucket = (gsel * float(PAIRS_PER_GROUP) + pair).astype(jnp.int32)

    brow = lax.broadcasted_iota(jnp.int32, (BUCKET_ROWS, tm), 0)
    onehot = (brow == bucket).astype(F32)
    before = jnp.dot(onehot.astype(BF16), lt_ref[...], preferred_element_type=F32)
    carry = carry_ref[...]
    in_tile = jnp.sum(onehot * before, axis=0, keepdims=True)
    base = jnp.sum(onehot * carry[:, 0:1], axis=0, keepdims=True)
    rank_ref[...] = (base + in_tile).astype(jnp.int32)
    bucket_ref[...] = bucket
    new_carry = carry + jnp.sum(onehot, axis=1, keepdims=True)
    carry_ref[...] = new_carry
    counts_ref[...] = new_carry.astype(jnp.int32)


def _slab(ref, r):
    return ref.at[pl.ds(pl.multiple_of(r * ROW_SLAB, ROW_SLAB), ROW_SLAB)]


def _wait_rows(src_ref, dst_ref, sem, n):
    def body(c, carry):
        for _ in range(DMA_CHUNK):
            pltpu.make_async_copy(_slab(src_ref, 0), _slab(dst_ref, 0), sem).wait()
        return carry
    lax.fori_loop(0, n // DMA_CHUNK, body, 0)


def _scatter_kernel(dest_ref, pad_lo_ref, pad_hi_ref, x_ref, sc_ref, sh_ref, xs_ref, hbuf, sems):
    i = pl.program_id(0)
    last = pl.num_programs(0) - 1
    tm = x_ref.shape[0]
    h = x_ref[...] * (1.0 + sc_ref[0]) + sh_ref[0]

    for s in range(2):
        @pl.when(lax.rem(i, 2) == s)
        def _(s=s):
            buf = hbuf.at[s]
            _store_slabs(buf, h)

            def issue(c, carry):
                for u in range(DMA_CHUNK):
                    r = c * DMA_CHUNK + u
                    pltpu.make_async_copy(_slab(buf, r), _slab(xs_ref, dest_ref[i * tm + r]), sems.at[s]).start(priority=u % 2)
                return carry
            lax.fori_loop(0, tm // DMA_CHUNK, issue, 0)

            @pl.when(i > 0)
            def _():
                _wait_rows(hbuf.at[1 - s], xs_ref, sems.at[1 - s], tm)

            @pl.when(i == last)
            def _():
                _wait_rows(buf, xs_ref, sems.at[s], tm)

                def filler(row0, n_rows, start):
                    cp = pltpu.make_async_copy(
                        buf.at[pl.ds(0, n_rows * ROW_SLAB)],
                        xs_ref.at[pl.ds(pl.multiple_of(row0 * ROW_SLAB, ROW_SLAB), n_rows * ROW_SLAB)],
                        sems.at[s])
                    if start:
                        cp.start()
                    else:
                        cp.wait()

                def fill(start):
                    def bucket(b, carry):
                        pos = pad_lo_ref[b]
                        n = pad_hi_ref[b] - pos
                        for bit in reversed(range(MOE_TM.bit_length() - 1)):
                            take = lax.rem(lax.shift_right_logical(n, bit), 2)

                            @pl.when(take == 1)
                            def _(pos=pos, bit=bit):
                                filler(pos, 1 << bit, start)
                            pos = pos + take * (1 << bit)
                        return carry
                    lax.fori_loop(0, N_BUCKETS, bucket, 0)

                    def tail(t, carry):
                        filler(t * MOE_TM, MOE_TM, start)
                        return carry
                    n_tiles = xs_ref.shape[0] // (MOE_TM * ROW_SLAB)
                    lax.fori_loop(pad_hi_ref[N_BUCKETS - 1] // MOE_TM, n_tiles, tail, 0)
                fill(True)
                fill(False)


def _scatter_rows(x2, mods, k, B, S, plan, n_out):
    T, D = x2.shape
    tm = PROJ_TM
    per_b = S // tm
    return pl.pallas_call(
        _scatter_kernel,
        grid_spec=pltpu.PrefetchScalarGridSpec(
            num_scalar_prefetch=3,
            grid=(T // tm,),
            in_specs=[
                pl.BlockSpec((tm, D), lambda i, *_: (i, 0)),
                pl.BlockSpec((1, 1, D), lambda i, *_: (_mod_row(k, i // per_b, SCALE, B), 0, 0)),
                pl.BlockSpec((1, 1, D), lambda i, *_: (_mod_row(k, i // per_b, SHIFT, B), 0, 0)),
            ],
            out_specs=pl.BlockSpec(memory_space=pl.ANY),
            scratch_shapes=[pltpu.VMEM((2, tm * ROW_SLAB, LANES), F32), pltpu.SemaphoreType.DMA((2,))],
        ),
        out_shape=jax.ShapeDtypeStruct((n_out * ROW_SLAB, LANES), F32),
        compiler_params=_cparams(("arbitrary",)),
        name="scatter_rows",
    )(plan["dest"], plan["pad_lo"], plan["pad_hi"], x2, mods, mods)


def _expert_pair(x, wg_bf, wu_bf, wd_bf):
    gates = [jnp.dot(x, wg_bf[e], preferred_element_type=F32) for e in range(2)]
    ups = [jnp.dot(x, wu_bf[e], preferred_element_type=F32) for e in range(2)]
    outs = []
    for e in range(2):
        hid = gates[e] * jax.nn.sigmoid(gates[e]) * ups[e]
        outs.append(jnp.dot(hid.astype(BF16), wd_bf[e], preferred_element_type=F32))
    return outs


def _router_logit(chunks, rw_ref, rb_ref, e):
    w_row = rw_ref[pl.ds(e, 1), :]
    acc = chunks[0] * w_row[:, 0:LANES]
    for j in range(1, ROW_SLAB):
        acc = acc + chunks[j] * w_row[:, j * LANES:(j + 1) * LANES]
    return jnp.sum(acc, axis=-1, keepdims=True) + rb_ref[pl.ds(e, 1), 0:1]


def _moe_kernel(src_ref, ea_ref, eb_ref, valid_ref, na_ref, nb_ref,
                xs_ref, rw_ref, rb_ref, wg_hbm, wu_hbm, wd_hbm, o_ref,
                wg_st, wu_st, wd_st, wg_bf, wu_bf, wd_bf, sems, *, layer):
    i = pl.program_id(0)
    prev = jnp.maximum(i - 1, 0)

    def fetch(slot, e):
        return [pltpu.make_async_copy(wg_hbm.at[layer, e], wg_st.at[slot], sems.at[slot, 0]),
                pltpu.make_async_copy(wu_hbm.at[layer, e], wu_st.at[slot], sems.at[slot, 1]),
                pltpu.make_async_copy(wd_hbm.at[layer, e], wd_st.at[slot], sems.at[slot, 2])]

    for slot, e_ref, n_ref in ((0, ea_ref, na_ref), (1, eb_ref, nb_ref)):
        @pl.when(i == 0)
        def _(slot=slot, e_ref=e_ref):
            for cp in fetch(slot, e_ref[0]):
                cp.start()

        @pl.when(jnp.logical_or(i == 0, e_ref[i] != e_ref[prev]))
        def _(slot=slot, e_ref=e_ref, n_ref=n_ref):
            for cp in fetch(slot, e_ref[i]):
                cp.wait()
            wg_bf[slot] = wg_st[slot].astype(BF16)
            wu_bf[slot] = wu_st[slot].astype(BF16)
            wd_bf[slot] = wd_st[slot].astype(BF16)

            @pl.when(n_ref[i] >= 0)
            def _():
                for cp in fetch(slot, n_ref[i]):
                    cp.start()

    @pl.when(valid_ref[i] == 1)
    def _():
        chunks = _load_slab_chunks(xs_ref, MOE_TM)
        l_a = _router_logit(chunks, rw_ref, rb_ref, ea_ref[i])
        l_b = _router_logit(chunks, rw_ref, rb_ref, eb_ref[i])
        w_a = jax.nn.sigmoid(l_a - l_b)
        w_b = jax.nn.sigmoid(l_b - l_a)
        x = jnp.concatenate([c.astype(BF16) for c in chunks], axis=-1)
        y_a, y_b = _expert_pair(x, wg_bf, wu_bf, wd_bf)
        y = w_a * y_a + w_b * y_b
        _store_slabs(o_ref, y)

    @pl.when(valid_ref[i] == 0)
    def _():
        o_ref[...] = jnp.zeros_like(o_ref)


def _grouped_experts(xs, plan, router_wT, router_bt, w_gate, w_up, w_down, layer):
    n_tiles = xs.shape[0] // (MOE_TM * ROW_SLAB)
    D, Fd = D_MODEL, D_EXPERT
    row_spec = lambda index: pl.BlockSpec((MOE_TM * ROW_SLAB, LANES), index)
    hbm = pl.BlockSpec(memory_space=pl.ANY)
    return pl.pallas_call(
        functools.partial(_moe_kernel, layer=layer),
        grid_spec=pltpu.PrefetchScalarGridSpec(
            num_scalar_prefetch=6,
            grid=(n_tiles,),
            in_specs=[
                row_spec(lambda i, src, *_: (src[i], 0)),
                pl.BlockSpec((N_EXPERTS, D), lambda i, *_: (0, 0)),
                pl.BlockSpec((N_EXPERTS, LANES), lambda i, *_: (0, 0)),
                hbm, hbm, hbm,
            ],
            out_specs=row_spec(lambda i, *_: (i, 0)),
            scratch_shapes=[
                pltpu.VMEM((2, D, Fd), F32), pltpu.VMEM((2, D, Fd), F32), pltpu.VMEM((2, Fd, D), F32),
                pltpu.VMEM((2, D, Fd), BF16), pltpu.VMEM((2, D, Fd), BF16), pltpu.VMEM((2, Fd, D), BF16),
                pltpu.SemaphoreType.DMA((2, 3)),
            ],
        ),
        out_shape=jax.ShapeDtypeStruct(xs.shape, F32),
        compiler_params=_cparams(("arbitrary",)),
        name="grouped_experts",
    )(plan["tile_src"], plan["tile_ea"], plan["tile_eb"], plan["tile_valid"], plan["next_ea"], plan["next_eb"],
      xs, router_wT, router_bt, w_gate, w_up, w_down)


def _next_change(e):
    n = e.shape[0]
    idx = jnp.arange(n, dtype=jnp.int32)
    change = jnp.concatenate([jnp.ones((1,), jnp.bool_), e[1:] != e[:-1]])
    at = jnp.where(change, idx, n)
    nxt = lax.cummin(at[::-1])[::-1]
    nxt = jnp.concatenate([nxt[1:], jnp.full((1,), n, jnp.int32)])
    return jnp.where(nxt < n, jnp.take(e, jnp.minimum(nxt, n - 1)), -1).astype(jnp.int32)


def _moe_plan(bucket, rank, counts, n_tiles):
    cnt = counts[:N_BUCKETS, 0]
    tiles_b = (cnt + MOE_TM - 1) // MOE_TM
    tile_end = jnp.cumsum(tiles_b)
    tile_start = tile_end - tiles_b
    row_start = tile_start * MOE_TM
    dest = jnp.take(row_start, bucket[0]) + rank[0]
    n_used = tile_end[-1]
    t = jnp.arange(n_tiles, dtype=jnp.int32)
    t_c = jnp.minimum(t, n_used - 1)
    tb = jnp.sum((t_c[:, None] >= tile_end[None, :]).astype(jnp.int32), axis=1)
    pair_lo = jnp.array([0, 0, 0, 1, 1, 2], jnp.int32)
    pair_hi = jnp.array([1, 2, 3, 2, 3, 3], jnp.int32)
    grp = tb // PAIRS_PER_GROUP
    pr = tb % PAIRS_PER_GROUP
    tile_ea = (grp * EXPERTS_PER_GROUP + jnp.take(pair_lo, pr)).astype(jnp.int32)
    tile_eb = (grp * EXPERTS_PER_GROUP + jnp.take(pair_hi, pr)).astype(jnp.int32)
    return {
        "dest": dest.astype(jnp.int32),
        "pad_lo": (row_start + cnt).astype(jnp.int32),
        "pad_hi": (tile_end * MOE_TM).astype(jnp.int32),
        "tile_src": t_c.astype(jnp.int32),
        "tile_ea": tile_ea,
        "tile_eb": tile_eb,
        "next_ea": _next_change(tile_ea),
        "next_eb": _next_change(tile_eb),
        "tile_valid": (t < n_used).astype(jnp.int32),
    }


def _resln_kernel(dest_ref, ys_ref, x_ref, gate_ref, g_ref, b_ref, o_ref, ybuf, sems):
    i = pl.program_id(0)
    n = pl.num_programs(0)
    tm = x_ref.shape[0]

    def issue(tile, s):
        def body(c, carry):
            for u in range(DMA_CHUNK):
                r = c * DMA_CHUNK + u
                pltpu.make_async_copy(_slab(ys_ref, dest_ref[tile * tm + r]), _slab(ybuf.at[s], r), sems.at[s]).start(priority=u % 2)
            return carry
        lax.fori_loop(0, tm // DMA_CHUNK, body, 0)

    @pl.when(i == 0)
    def _():
        issue(0, 0)

        @pl.when(n > 1)
        def _():
            issue(1, 1)

    for s in range(3):
        @pl.when(lax.rem(i, 3) == s)
        def _(s=s):
            @pl.when(i + 2 < n)
            def _():
                issue(i + 2, (s + 2) % 3)

            _wait_rows(ys_ref, ybuf.at[s], sems.at[s], tm)
            y = jnp.concatenate(_load_slab_chunks(ybuf.at[s], tm), axis=-1)
            u = DEEPNORM_ALPHA * x_ref[...] + (1.0 + gate_ref[0]) * y
            o_ref[...] = _layer_norm_rows(u, g_ref[...], b_ref[...])


def _gather_residual_ln(ys, plan, x2, mods, k, B, S, g, b):
    T, D = x2.shape
    tm = PROJ_TM
    per_b = S // tm
    return pl.pallas_call(
        _resln_kernel,
        grid_spec=pltpu.PrefetchScalarGridSpec(
            num_scalar_prefetch=1,
            grid=(T // tm,),
            in_specs=[
                pl.BlockSpec(memory_space=pl.ANY),
                pl.BlockSpec((tm, D), lambda i, *_: (i, 0)),
                pl.BlockSpec((1, 1, D), lambda i, *_: (_mod_row(k, i // per_b, GATE, B), 0, 0)),
                pl.BlockSpec((1, D), lambda i, *_: (0, 0)),
                pl.BlockSpec((1, D), lambda i, *_: (0, 0)),
            ],
            out_specs=pl.BlockSpec((tm, D), lambda i, *_: (i, 0)),
            scratch_shapes=[pltpu.VMEM((3, tm * ROW_SLAB, LANES), F32), pltpu.SemaphoreType.DMA((3,))],
        ),
        out_shape=jax.ShapeDtypeStruct((T, D), F32),
        compiler_params=_cparams(("arbitrary",)),
        name="gather_residual_ln",
    )(plan["dest"], ys, x2, mods, g.reshape(1, D), b.reshape(1, D))


def _moe_sublayer(x, routing, mods, k, router_wT, router_bt, w_gate, w_up, w_down, layer, g, b):
    B, S, D = x.shape
    T = B * S
    x2 = x.reshape(T, D)
    bucket, rank, counts = routing
    n_tiles = T // MOE_TM + N_BUCKETS
    plan = _moe_plan(bucket, rank, counts, n_tiles)
    xs = _scatter_rows(x2, mods, k, B, S, plan, n_tiles * MOE_TM)
    ys = _grouped_experts(xs, plan, router_wT, router_bt, w_gate, w_up, w_down, layer)
    return _gather_residual_ln(ys, plan, x2, mods, k, B, S, g, b).reshape(B, S, D)


def kernel(x, c, positions, ada_w, ada_b, ln_g, ln_b, sb_w_in, sb_w_out, ret_w_in, ret_gn_g, ret_w_out,
           router_w, router_b, moe_w_gate, moe_w_up, moe_w_down):
    mods = _ada_params(c, ada_w, ada_b)
    cc, ss = _rope_tables(positions)
    tables = _retention_decay_tables()
    u = (jnp.arange(SB_TK)[:, None] < jnp.arange(SB_TK)[None, :]).astype(BF16)
    lt = (jnp.arange(PROJ_TM)[:, None] < jnp.arange(PROJ_TM)[None, :]).astype(BF16)
    router_wT = router_w.T
    router_bt = jnp.broadcast_to(router_b[:, None], (N_EXPERTS, LANES))
    for i in range(DEPTH):
        k_mix, k_moe = 2 * i, 2 * i + 1
        if i % 2 == 0:
            qkv = _mod_matmul(x, mods, k_mix, sb_w_in, i // 2, tn=3 * D_MODEL // 2)
            a = _sb_attention(qkv, u)
            w_out = sb_w_out
        else:
            proj = _mod_matmul(x, mods, k_mix, ret_w_in, i // 2, tn=2 * D_MODEL)
            a = _retention(proj, cc, ss, tables, ret_gn_g[i // 2])
            w_out = ret_w_out
        x, *routing = _out_proj_ln(a, w_out, i // 2, x, mods, k_mix, ln_g[i, 0], ln_b[i, 0],
                                   k_moe, router_wT, router_bt, lt)
        x = _moe_sublayer(x, routing, mods, k_moe, router_wT, router_bt, moe_w_gate, moe_w_up, moe_w_down, i,
                          ln_g[i, 1], ln_b[i, 1])
    return x
```
